```python
import jax, jax.numpy as jnp
from jax import lax
import numpy as np

D_MODEL = 2048
BATCH = 8
SEQ = 2048
DEPTH = 1

D_MIX = D_MODEL
GM_WIDTH = D_MIX // 2
GM_GROUPS = 8
GM_DG = GM_WIDTH // GM_GROUPS
CHUNK = 128
SB_WIDTH = D_MIX - GM_WIDTH
SB_HEADS = 8
SB_HEAD_DIM = SB_WIDTH // SB_HEADS
Q_BLOCK = 128
D_FF = -(-(8 * D_MODEL) // (3 * 256)) * 256
N_IN = 2 * GM_WIDTH + 3 * SB_WIDTH
N_MOD = 6
EPS = 1e-6

kernel_name = "hybrid_gmlp_stickbreaking_adaln_block"


def rmsnorm(x, g):
    xf = x.astype(jnp.float32)
    y = xf * lax.rsqrt(jnp.mean(xf * xf, axis=-1, keepdims=True) + EPS)
    return (y * g.astype(jnp.float32)).astype(x.dtype)


def group_rmsnorm(x, g, groups):
    xf = x.astype(jnp.float32).reshape(*x.shape[:-1], groups, -1)
    y = xf * lax.rsqrt(jnp.mean(xf * xf, axis=-1, keepdims=True) + EPS)
    return (y.reshape(x.shape) * g.astype(jnp.float32)).astype(x.dtype)


def group_layernorm(x, g, groups):
    xf = x.astype(jnp.float32).reshape(*x.shape[:-1], groups, -1)
    mu = jnp.mean(xf, axis=-1, keepdims=True)
    xc = xf - mu
    y = xc * lax.rsqrt(jnp.mean(xc * xc, axis=-1, keepdims=True) + EPS)
    return (y.reshape(x.shape) * g.astype(jnp.float32)).astype(x.dtype)


def chunked_spatial_gating(z, v_norm_g, w_s, b_s):
    B, S, _ = z.shape
    u, v = z[..., :GM_WIDTH], z[..., GM_WIDTH:]
    v = group_layernorm(v, v_norm_g, GM_GROUPS)
    v = v.reshape(B, S // CHUNK, CHUNK, GM_GROUPS, GM_DG)
    causal = jnp.tril(jnp.ones((CHUNK, CHUNK), dtype=bool))
    w = jnp.where(causal[None], w_s, 0).astype(v.dtype)
    mixed = jnp.einsum('gts,bnsgd->bntgd', w, v) + b_s.T.astype(v.dtype)[None, None, :, :, None]
    return u * mixed.reshape(B, S, GM_WIDTH)


def stick_breaking_attention(q, k, v):
    B, S, H, Dh = q.shape
    scale = Dh ** -0.5
    outs = []
    for i in range(S // Q_BLOCK):
        start, end = i * Q_BLOCK, (i + 1) * Q_BLOCK
        qs, ks, vs = q[:, start:end], k[:, :end], v[:, :end]
        z = jnp.einsum('bqhd,bkhd->bhqk', qs, ks).astype(jnp.float32) * scale
        t_pos = start + jnp.arange(Q_BLOCK)[:, None]
        s_pos = jnp.arange(end)[None, :]
        mask = s_pos < t_pos
        log_beta = jax.nn.log_sigmoid(z)
        log_1m = jnp.where(mask, jax.nn.log_sigmoid(-z), 0.0)
        tail = lax.cumsum(log_1m, axis=3, reverse=True) - log_1m
        a = jnp.where(mask, jnp.exp(log_beta + tail), 0.0)
        outs.append(jnp.einsum('bhqk,bkhd->bqhd', a.astype(vs.dtype), vs))
    return jnp.concatenate(outs, axis=1)


def _fwd_setup_inputs(seed: int = 0) -> dict:
    key = jax.random.key(seed)
    ks = jax.random.split(key, 16)
    f32 = jnp.float32
    n = lambda k, shape: jax.random.normal(k, shape, dtype=f32)
    return {
        "x": n(ks[0], (BATCH, SEQ, D_MODEL)),
        "c": n(ks[1], (BATCH, D_MODEL)),
        "w_ada": n(ks[2], (DEPTH, D_MODEL, N_MOD * D_MODEL)) * (0.5 * D_MODEL ** -0.5),
        "b_ada": n(ks[3], (DEPTH, N_MOD * D_MODEL)) * 0.01,
        "norm1_g": 1.0 + 0.01 * n(ks[4], (DEPTH, D_MODEL)),
        "w_in": n(ks[5], (DEPTH, D_MODEL, N_IN)) * D_MODEL ** -0.5,
        "v_norm_g": 1.0 + 0.01 * n(ks[6], (DEPTH, GM_WIDTH)),
        "w_spatial": n(ks[7], (DEPTH, GM_GROUPS, CHUNK, CHUNK)) * CHUNK ** -0.5,
        "b_spatial": 1.0 + 0.01 * n(ks[8], (DEPTH, GM_GROUPS, CHUNK)),
        "out_norm_g": 1.0 + 0.01 * n(ks[9], (DEPTH, D_MIX)),
        "w_out": n(ks[10], (DEPTH, D_MIX, D_MODEL)) * D_MIX ** -0.5,
        "norm2_g": 1.0 + 0.01 * n(ks[11], (DEPTH, D_MODEL)),
        "w_gate": n(ks[12], (DEPTH, D_MODEL, D_FF)) * D_MODEL ** -0.5,
        "w_up": n(ks[13], (DEPTH, D_MODEL, D_FF)) * D_MODEL ** -0.5,
        "w_down": n(ks[14], (DEPTH, D_FF, D_MODEL)) * D_FF ** -0.5,
        "final_g": 1.0 + 0.01 * n(ks[15], (D_MODEL,)),
    }


def _fwd_reference(x, c, w_ada, b_ada, norm1_g, w_in, v_norm_g, w_spatial, b_spatial,
              out_norm_g, w_out, norm2_g, w_gate, w_up, w_down, final_g):
    B, S, _ = x.shape
    c_act = jax.nn.silu(c)
    for l in range(DEPTH):
        mod = c_act @ w_ada[l] + b_ada[l]
        shift1, scale1, gate1, shift2, scale2, gate2 = [m[:, None, :] for m in jnp.split(mod, N_MOD, axis=-1)]

        h = rmsnorm(x, norm1_g[l]) * (1.0 + scale1) + shift1
        proj = h @ w_in[l]
        z_gm = jax.nn.gelu(proj[..., :2 * GM_WIDTH], approximate=False)
        o_gm = chunked_spatial_gating(z_gm, v_norm_g[l], w_spatial[l], b_spatial[l])
        qkv = proj[..., 2 * GM_WIDTH:].reshape(B, S, 3, SB_HEADS, SB_HEAD_DIM)
        o_sb = stick_breaking_attention(qkv[:, :, 0], qkv[:, :, 1], qkv[:, :, 2]).reshape(B, S, SB_WIDTH)
        o = jnp.concatenate([o_gm, o_sb], axis=-1)
        o = group_rmsnorm(o, out_norm_g[l], GM_GROUPS + SB_HEADS)
        x = x + gate1 * (o @ w_out[l])

        h = rmsnorm(x, norm2_g[l]) * (1.0 + scale2) + shift2
        f = (jax.nn.silu(h @ w_gate[l]) * (h @ w_up[l])) @ w_down[l]
        x = x + gate2 * f
    return rmsnorm(x, final_g)


import jax as _jax
import jax.numpy as _jnp

TWIN_FORMAT = 'train_step'
FWD_PARAMS = ['x', 'c', 'w_ada', 'b_ada', 'norm1_g', 'w_in', 'v_norm_g', 'w_spatial', 'b_spatial', 'out_norm_g', 'w_out', 'norm2_g', 'w_gate', 'w_up', 'w_down', 'final_g']
TWIN_WEIGHTS = ['w_ada', 'b_ada', 'norm1_g', 'w_in', 'v_norm_g', 'w_spatial', 'b_spatial', 'out_norm_g', 'w_out', 'norm2_g', 'w_gate', 'w_up', 'w_down', 'final_g']
TWIN_DIFF_INPUT = 'x'
TWIN_INPUTS = ['x', 'c', 'w_ada', 'b_ada', 'norm1_g', 'w_in', 'v_norm_g', 'w_spatial', 'b_spatial', 'out_norm_g', 'w_out', 'norm2_g', 'w_gate', 'w_up', 'w_down', 'final_g', 'loss_target', 'm_w_ada', 'm_b_ada', 'm_norm1_g', 'm_w_in', 'm_v_norm_g', 'm_w_spatial', 'm_b_spatial', 'm_out_norm_g', 'm_w_out', 'm_norm2_g', 'm_w_gate', 'm_w_up', 'm_w_down', 'm_final_g', 'v_w_ada', 'v_b_ada', 'v_norm1_g', 'v_w_in', 'v_v_norm_g', 'v_w_spatial', 'v_b_spatial', 'v_out_norm_g', 'v_w_out', 'v_norm2_g', 'v_w_gate', 'v_w_up', 'v_w_down', 'v_final_g']
TWIN_OUTPUTS = ['loss', 'grad_x', 'grad_w_ada', 'grad_b_ada', 'grad_norm1_g', 'grad_w_in', 'grad_v_norm_g', 'grad_w_spatial', 'grad_b_spatial', 'grad_out_norm_g', 'grad_w_out', 'grad_norm2_g', 'grad_w_gate', 'grad_w_up', 'grad_w_down', 'grad_final_g', 'delta_w_ada', 'delta_b_ada', 'delta_norm1_g', 'delta_w_in', 'delta_v_norm_g', 'delta_w_spatial', 'delta_b_spatial', 'delta_out_norm_g', 'delta_w_out', 'delta_norm2_g', 'delta_w_gate', 'delta_w_up', 'delta_w_down', 'delta_final_g', 'new_m_w_ada', 'new_m_b_ada', 'new_m_norm1_g', 'new_m_w_in', 'new_m_v_norm_g', 'new_m_w_spatial', 'new_m_b_spatial', 'new_m_out_norm_g', 'new_m_w_out', 'new_m_norm2_g', 'new_m_w_gate', 'new_m_w_up', 'new_m_w_down', 'new_m_final_g', 'new_v_w_ada', 'new_v_b_ada', 'new_v_norm1_g', 'new_v_w_in', 'new_v_v_norm_g', 'new_v_w_spatial', 'new_v_b_spatial', 'new_v_out_norm_g', 'new_v_w_out', 'new_v_norm2_g', 'new_v_w_gate', 'new_v_w_up', 'new_v_w_down', 'new_v_final_g']
TWIN_LEAF_KINDS = {'loss': 'loss', 'grad_x': 'grad_x', 'grad_w_ada': 'grad_w', 'grad_b_ada': 'grad_w', 'grad_norm1_g': 'grad_w', 'grad_w_in': 'grad_w', 'grad_v_norm_g': 'grad_w', 'grad_w_spatial': 'grad_w', 'grad_b_spatial': 'grad_w', 'grad_out_norm_g': 'grad_w', 'grad_w_out': 'grad_w', 'grad_norm2_g': 'grad_w', 'grad_w_gate': 'grad_w', 'grad_w_up': 'grad_w', 'grad_w_down': 'grad_w', 'grad_final_g': 'grad_w', 'delta_w_ada': 'delta_w', 'delta_b_ada': 'delta_w', 'delta_norm1_g': 'delta_w', 'delta_w_in': 'delta_w', 'delta_v_norm_g': 'delta_w', 'delta_w_spatial': 'delta_w', 'delta_b_spatial': 'delta_w', 'delta_out_norm_g': 'delta_w', 'delta_w_out': 'delta_w', 'delta_norm2_g': 'delta_w', 'delta_w_gate': 'delta_w', 'delta_w_up': 'delta_w', 'delta_w_down': 'delta_w', 'delta_final_g': 'delta_w', 'new_m_w_ada': 'new_m', 'new_m_b_ada': 'new_m', 'new_m_norm1_g': 'new_m', 'new_m_w_in': 'new_m', 'new_m_v_norm_g': 'new_m', 'new_m_w_spatial': 'new_m', 'new_m_b_spatial': 'new_m', 'new_m_out_norm_g': 'new_m', 'new_m_w_out': 'new_m', 'new_m_norm2_g': 'new_m', 'new_m_w_gate': 'new_m', 'new_m_w_up': 'new_m', 'new_m_w_down': 'new_m', 'new_m_final_g': 'new_m', 'new_v_w_ada': 'new_v', 'new_v_b_ada': 'new_v', 'new_v_norm1_g': 'new_v', 'new_v_w_in': 'new_v', 'new_v_v_norm_g': 'new_v', 'new_v_w_spatial': 'new_v', 'new_v_b_spatial': 'new_v', 'new_v_out_norm_g': 'new_v', 'new_v_w_out': 'new_v', 'new_v_norm2_g': 'new_v', 'new_v_w_gate': 'new_v', 'new_v_w_up': 'new_v', 'new_v_w_down': 'new_v', 'new_v_final_g': 'new_v'}


def _forward(args):
    return _fwd_reference(*[args[k] for k in FWD_PARAMS])


def _output_shape():
    out = _jax.eval_shape(lambda: _forward(_fwd_setup_inputs(0)))
    return out.shape, out.dtype

N_MICROBATCH = 1
ADAM_LR = 0.001
ADAM_B1 = 0.9
ADAM_B2 = 0.999
ADAM_EPS = 1e-08
ADAM_WD = 0.01
ADAM_STEP = 10
PER_EXAMPLE_BATCH_AXIS = {'x': 0, 'c': 0, 'loss_target': 0}
SHARED_INPUTS = []
_WEIGHT_DTYPES = {'w_ada': _jnp.float32, 'b_ada': _jnp.float32, 'norm1_g': _jnp.float32, 'w_in': _jnp.float32, 'v_norm_g': _jnp.float32, 'w_spatial': _jnp.float32, 'b_spatial': _jnp.float32, 'out_norm_g': _jnp.float32, 'w_out': _jnp.float32, 'norm2_g': _jnp.float32, 'w_gate': _jnp.float32, 'w_up': _jnp.float32, 'w_down': _jnp.float32, 'final_g': _jnp.float32}
MOMENT_SCALE = {'w_ada': 2.095451e-02, 'b_ada': 3.555652e-02, 'norm1_g': 2.170937e-02, 'w_in': 1.432364e-02, 'v_norm_g': 1.031704e-02, 'w_spatial': 1.023704e-02, 'b_spatial': 8.556235e-03, 'out_norm_g': 1.832744e-02, 'w_out': 1.844523e-02, 'norm2_g': 1.764964e-02, 'w_gate': 7.975720e-03, 'w_up': 7.716784e-03, 'w_down': 1.277650e-02, 'final_g': 8.011106e+00}


def _to_microbatches(a, axis):
    t = _jnp.moveaxis(a, axis, 0)
    t = t.reshape((N_MICROBATCH, t.shape[0] // N_MICROBATCH) + t.shape[1:])
    return _jnp.moveaxis(t, 1, axis + 1)


def setup_inputs(seed: int = 0) -> dict:
    inp = _fwd_setup_inputs(seed)
    key = _jax.random.fold_in(_jax.random.key(seed), 7919)
    shape, _ = _output_shape()
    out = dict(inp)
    out["loss_target"] = _jax.random.normal(_jax.random.fold_in(key, 0), shape, _jnp.float32)
    for i, name in enumerate(TWIN_WEIGHTS):
        w = inp[name].astype(_jnp.float32)
        if MOMENT_SCALE is None:
            s = _jnp.sqrt(_jnp.mean(_jnp.square(w)) + 1e-30)
        else:
            s = MOMENT_SCALE[name]
        km, kv = _jax.random.split(_jax.random.fold_in(key, i + 1))
        out[name] = w
        out["m_" + name] = s * _jax.random.normal(km, w.shape, _jnp.float32)
        out["v_" + name] = (s * s) * _jax.random.uniform(kv, w.shape, _jnp.float32, 0.5, 1.5)
    if N_MICROBATCH > 1:
        for name, axis in PER_EXAMPLE_BATCH_AXIS.items():
            out[name] = _to_microbatches(out[name], axis)
    return {'x': out['x'], 'c': out['c'], 'w_ada': out['w_ada'], 'b_ada': out['b_ada'], 'norm1_g': out['norm1_g'], 'w_in': out['w_in'], 'v_norm_g': out['v_norm_g'], 'w_spatial': out['w_spatial'], 'b_spatial': out['b_spatial'], 'out_norm_g': out['out_norm_g'], 'w_out': out['w_out'], 'norm2_g': out['norm2_g'], 'w_gate': out['w_gate'], 'w_up': out['w_up'], 'w_down': out['w_down'], 'final_g': out['final_g'], 'loss_target': out['loss_target'], 'm_w_ada': out['m_w_ada'], 'm_b_ada': out['m_b_ada'], 'm_norm1_g': out['m_norm1_g'], 'm_w_in': out['m_w_in'], 'm_v_norm_g': out['m_v_norm_g'], 'm_w_spatial': out['m_w_spatial'], 'm_b_spatial': out['m_b_spatial'], 'm_out_norm_g': out['m_out_norm_g'], 'm_w_out': out['m_w_out'], 'm_norm2_g': out['m_norm2_g'], 'm_w_gate': out['m_w_gate'], 'm_w_up': out['m_w_up'], 'm_w_down': out['m_w_down'], 'm_final_g': out['m_final_g'], 'v_w_ada': out['v_w_ada'], 'v_b_ada': out['v_b_ada'], 'v_norm1_g': out['v_norm1_g'], 'v_w_in': out['v_w_in'], 'v_v_norm_g': out['v_v_norm_g'], 'v_w_spatial': out['v_w_spatial'], 'v_b_spatial': out['v_b_spatial'], 'v_out_norm_g': out['v_out_norm_g'], 'v_w_out': out['v_w_out'], 'v_norm2_g': out['v_norm2_g'], 'v_w_gate': out['v_w_gate'], 'v_w_up': out['v_w_up'], 'v_w_down': out['v_w_down'], 'v_final_g': out['v_final_g']}


def _loss(weights, diff, rest, loss_target):
    with _jax.named_scope("forward"):
        args = {**rest, TWIN_DIFF_INPUT: diff, **{k: w.astype(_WEIGHT_DTYPES[k]) for k, w in weights.items()}}
        y = _forward(args)
    with _jax.named_scope("loss_head"):
        err = _jnp.square(y.astype(_jnp.float32) - loss_target)
        return 0.5 * _jnp.sum(_jnp.mean(err, axis=-1)) if err.ndim else 0.5 * err


def _adamw(w, g, m, v):
    m = ADAM_B1 * m + (1.0 - ADAM_B1) * g
    v = ADAM_B2 * v + (1.0 - ADAM_B2) * _jnp.square(g)
    m_hat = m / (1.0 - ADAM_B1 ** ADAM_STEP)
    v_hat = v / (1.0 - ADAM_B2 ** ADAM_STEP)
    delta = -ADAM_LR * (m_hat / (_jnp.sqrt(v_hat) + ADAM_EPS) + ADAM_WD * w)
    return delta, m, v


def reference(x, c, w_ada, b_ada, norm1_g, w_in, v_norm_g, w_spatial, b_spatial, out_norm_g, w_out, norm2_g, w_gate, w_up, w_down, final_g, loss_target, m_w_ada, m_b_ada, m_norm1_g, m_w_in, m_v_norm_g, m_w_spatial, m_b_spatial, m_out_norm_g, m_w_out, m_norm2_g, m_w_gate, m_w_up, m_w_down, m_final_g, v_w_ada, v_b_ada, v_norm1_g, v_w_in, v_v_norm_g, v_w_spatial, v_b_spatial, v_out_norm_g, v_w_out, v_norm2_g, v_w_gate, v_w_up, v_w_down, v_final_g):
    given = dict(x=x, c=c, w_ada=w_ada, b_ada=b_ada, norm1_g=norm1_g, w_in=w_in, v_norm_g=v_norm_g, w_spatial=w_spatial, b_spatial=b_spatial, out_norm_g=out_norm_g, w_out=w_out, norm2_g=norm2_g, w_gate=w_gate, w_up=w_up, w_down=w_down, final_g=final_g, loss_target=loss_target, m_w_ada=m_w_ada, m_b_ada=m_b_ada, m_norm1_g=m_norm1_g, m_w_in=m_w_in, m_v_norm_g=m_v_norm_g, m_w_spatial=m_w_spatial, m_b_spatial=m_b_spatial, m_out_norm_g=m_out_norm_g, m_w_out=m_w_out, m_norm2_g=m_norm2_g, m_w_gate=m_w_gate, m_w_up=m_w_up, m_w_down=m_w_down, m_final_g=m_final_g, v_w_ada=v_w_ada, v_b_ada=v_b_ada, v_norm1_g=v_norm1_g, v_w_in=v_w_in, v_v_norm_g=v_v_norm_g, v_w_spatial=v_w_spatial, v_b_spatial=v_b_spatial, v_out_norm_g=v_out_norm_g, v_w_out=v_w_out, v_norm2_g=v_norm2_g, v_w_gate=v_w_gate, v_w_up=v_w_up, v_w_down=v_w_down, v_final_g=v_final_g)
    weights = {n: given[n] for n in TWIN_WEIGHTS}
    shared = {n: given[n] for n in SHARED_INPUTS}
    per_example = {n: given[n] for n in ['x', 'c']}
    grad_fn = _jax.value_and_grad(_loss, argnums=(0, 1))

    def one_microbatch(ex, loss_target):
        ex = dict(ex)
        diff = ex.pop(TWIN_DIFF_INPUT)
        return grad_fn(weights, diff, {**shared, **ex}, loss_target)

    if N_MICROBATCH == 1:
        loss, (grad_w, grad_x) = one_microbatch(per_example, given["loss_target"])
    else:
        def body(carry, xs):
            loss_sum, grad_sum = carry
            l_k, (gw_k, gx_k) = one_microbatch(xs[0], xs[1])
            with _jax.named_scope("update"):
                return (loss_sum + l_k, _jax.tree.map(_jnp.add, grad_sum, gw_k)), gx_k

        init = (_jnp.zeros((), _jnp.float32), _jax.tree.map(_jnp.zeros_like, weights))
        (loss, grad_w), grad_x = _jax.lax.scan(body, init, (per_example, given["loss_target"]))
    with _jax.named_scope("update"):
        delta_w, new_m, new_v = {}, {}, {}
        for n in TWIN_WEIGHTS:
            delta_w[n], new_m[n], new_v[n] = _adamw(weights[n], grad_w[n], given["m_" + n], given["v_" + n])
    return (loss, grad_x, *[grad_w[n] for n in TWIN_WEIGHTS], *[delta_w[n] for n in TWIN_WEIGHTS],
            *[new_m[n] for n in TWIN_WEIGHTS], *[new_v[n] for n in TWIN_WEIGHTS])
```

```python
import functools

import jax
import jax.numpy as jnp
from jax import lax
from jax.experimental import pallas as pl
from jax.experimental.pallas import tpu as pltpu

F32, BF16 = jnp.float32, jnp.bfloat16
NDEV = 8
GRP = 128
EPS = 1e-6
VMEM_BYTES = 64 * 2 ** 20
VMEM_LIMIT = VMEM_BYTES - 8 * 2 ** 20
ADAM_LR, ADAM_B1, ADAM_B2, ADAM_EPS, ADAM_WD, ADAM_STEP = 0.001, 0.9, 0.999, 1e-08, 0.01, 10
MESH = pl.DeviceIdType.MESH
NN = (((1,), (0,)), ((), ()))
NT = (((1,), (1,)), ((), ()))
TN = (((0,), (0,)), ((), ()))


def _params(n_axes):
    return pltpu.CompilerParams(dimension_semantics=("arbitrary",) * n_axes, vmem_limit_bytes=VMEM_LIMIT)


def _tile(n, cap, mult):
    best = None
    for t in range(mult, min(n, cap) + 1, mult):
        if n % t == 0:
            best = t
    assert best is not None, (n, cap, mult)
    return best


def _dot(a, b, dims):
    return lax.dot_general(a, b, dims, preferred_element_type=F32)


def _exchange(src, per_peer, name):
    blk = src.shape[1:] if per_peer else src.shape

    def body(src_ref, out_ref, send_sems, recv_sems, local_sem):
        x, y, c = lax.axis_index("x"), lax.axis_index("y"), lax.axis_index("c")
        me = 4 * x + 2 * y + c
        local = pltpu.make_async_copy(src_ref.at[me] if per_peer else src_ref, out_ref.at[me], local_sem)
        local.start()
        sends, recvs = [], []
        for k in range(1, NDEV):
            px, py, pc = x ^ (k >> 2), y ^ ((k >> 1) & 1), c ^ (k & 1)
            p = 4 * px + 2 * py + pc
            mine = src_ref.at[p] if per_peer else src_ref
            sends.append(pltpu.make_async_remote_copy(
                src_ref=mine, dst_ref=out_ref.at[me], send_sem=send_sems.at[k - 1], recv_sem=recv_sems.at[k - 1],
                device_id=(px, py, pc), device_id_type=MESH))
            recvs.append(pltpu.make_async_remote_copy(
                src_ref=mine, dst_ref=out_ref.at[p], send_sem=send_sems.at[k - 1], recv_sem=recv_sems.at[k - 1],
                device_id=(px, py, pc), device_id_type=MESH))
        for cp in sends:
            cp.start()
        for cp in recvs:
            cp.wait_recv()
        for cp in sends:
            cp.wait_send()
        local.wait()

    return pl.pallas_call(
        body, name=name,
        out_shape=jax.ShapeDtypeStruct((NDEV,) + tuple(blk), src.dtype),
        in_specs=[pl.BlockSpec(memory_space=pl.ANY)],
        out_specs=pl.BlockSpec(memory_space=pl.ANY),
        scratch_shapes=[pltpu.SemaphoreType.DMA((NDEV - 1,)), pltpu.SemaphoreType.DMA((NDEV - 1,)),
                        pltpu.SemaphoreType.DMA],
    )(src)


def _mm(a, b, a_spec, b_spec, out_sds, o_spec, grid, dims, acc, name):
    nk = grid[1]

    def body(a_ref, b_ref, o_ref, *scratch):
        prod = _dot(a_ref[...], b_ref[...], dims)
        if not acc:
            o_ref[...] = prod.astype(o_ref.dtype)
            return
        acc_ref = scratch[0]
        k = pl.program_id(1)

        @pl.when(k == 0)
        def _():
            acc_ref[...] = prod

        @pl.when(k > 0)
        def _():
            acc_ref[...] += prod

        @pl.when(k == nk - 1)
        def _():
            o_ref[...] = acc_ref[...].astype(o_ref.dtype)

    blk = tuple(d for d in o_spec.block_shape if d is not None)
    return pl.pallas_call(
        body, name=name, grid=grid, out_shape=out_sds, in_specs=[a_spec, b_spec], out_specs=o_spec,
        scratch_shapes=[pltpu.VMEM(blk, F32)] if acc else [],
        compiler_params=_params(2),
    )(a, b)


def _row_spec(tm, d):
    return pl.BlockSpec((tm, d), lambda i: (i, 0))


def _vec_spec(d):
    return pl.BlockSpec((1, d), lambda i: (0, 0))


def _prenorm_fwd(x, g, scale, shift, name):
    s, d = x.shape
    tm = _tile(s, 256, 16)

    def body(x_ref, g_ref, sc_ref, sh_ref, h_ref):
        xv = x_ref[...]
        rstd = lax.rsqrt(jnp.mean(xv * xv, axis=-1, keepdims=True) + EPS)
        h_ref[...] = ((xv * rstd * g_ref[...]) * (1.0 + sc_ref[...]) + sh_ref[...]).astype(BF16)

    return pl.pallas_call(
        body, name=name, grid=(s // tm,), out_shape=jax.ShapeDtypeStruct((s, d), BF16),
        in_specs=[_row_spec(tm, d), _vec_spec(d), _vec_spec(d), _vec_spec(d)], out_specs=_row_spec(tm, d),
        compiler_params=_params(1),
    )(x, g, scale, shift)


def _residual(x, gate, p, name):
    s, d = x.shape
    tm = _tile(s, 256, 8)

    def body(x_ref, g_ref, p_ref, o_ref):
        o_ref[...] = x_ref[...] + g_ref[...] * p_ref[...]

    return pl.pallas_call(
        body, name=name, grid=(s // tm,), out_shape=jax.ShapeDtypeStruct((s, d), F32),
        in_specs=[_row_spec(tm, d), _vec_spec(d), _row_spec(tm, d)], out_specs=_row_spec(tm, d),
        compiler_params=_params(1),
    )(x, gate, p)


def _final_loss(x2, final_g, target, gate2):
    s, d = x2.shape
    tm = _tile(s, 256, 16)

    def body(x_ref, g_ref, t_ref, gate_ref, dx_ref, df_ref, st_ref):
        @pl.when(pl.program_id(0) == 0)
        def _():
            st_ref[...] = jnp.zeros_like(st_ref)

        xv, gf = x_ref[...], g_ref[...]
        rstd = lax.rsqrt(jnp.mean(xv * xv, axis=-1, keepdims=True) + EPS)
        xhat = xv * rstd
        err = xhat * gf - t_ref[...]
        dy = err * (1.0 / d)
        gdy = dy * gf
        dx = rstd * (gdy - xhat * jnp.mean(gdy * xhat, axis=-1, keepdims=True))
        dx_ref[...] = dx
        df_ref[...] = (gate_ref[...] * dx).astype(BF16)
        st_ref[0:1, :] += jnp.sum(dy * xhat, axis=0, keepdims=True)
        st_ref[1:2, :] += jnp.sum(err * err, axis=0, keepdims=True)

    return pl.pallas_call(
        body, name="final_loss", grid=(s // tm,),
        out_shape=(jax.ShapeDtypeStruct((s, d), F32), jax.ShapeDtypeStruct((s, d), BF16),
                   jax.ShapeDtypeStruct((8, d), F32)),
        in_specs=[_row_spec(tm, d), _vec_spec(d), _row_spec(tm, d), _vec_spec(d)],
        out_specs=(_row_spec(tm, d), _row_spec(tm, d), pl.BlockSpec((8, d), lambda i: (0, 0))),
        compiler_params=_params(1),
    )(x2, final_g, target, gate2)


def _prenorm_bwd(xin, dh, dres, pf, g, scale, gate_next, name):
    s, d = xin.shape
    tm = _tile(s, 256, 16)

    def body(x_ref, dh_ref, dr_ref, pf_ref, g_ref, sc_ref, gn_ref, dx_ref, dn_ref, st_ref):
        @pl.when(pl.program_id(0) == 0)
        def _():
            st_ref[...] = jnp.zeros_like(st_ref)

        xv, dhv, drv, gv = x_ref[...], dh_ref[...], dr_ref[...], g_ref[...]
        one_sc = 1.0 + sc_ref[...]
        rstd = lax.rsqrt(jnp.mean(xv * xv, axis=-1, keepdims=True) + EPS)
        xhat = xv * rstd
        dxhat = dhv * (gv * one_sc)
        dx = drv + rstd * (dxhat - xhat * jnp.mean(dxhat * xhat, axis=-1, keepdims=True))
        dx_ref[...] = dx
        dn_ref[...] = (gn_ref[...] * dx).astype(BF16)
        dhx = dhv * xhat
        st_ref[0:1, :] += jnp.sum(dhv, axis=0, keepdims=True)
        st_ref[1:2, :] += jnp.sum(dhx, axis=0, keepdims=True) * gv
        st_ref[2:3, :] += jnp.sum(dhx, axis=0, keepdims=True) * one_sc
        st_ref[3:4, :] += jnp.sum(drv * pf_ref[...], axis=0, keepdims=True)

    return pl.pallas_call(
        body, name=name, grid=(s // tm,),
        out_shape=(jax.ShapeDtypeStruct((s, d), F32), jax.ShapeDtypeStruct((s, d), BF16),
                   jax.ShapeDtypeStruct((8, d), F32)),
        in_specs=[_row_spec(tm, d)] * 4 + [_vec_spec(d)] * 3,
        out_specs=(_row_spec(tm, d), _row_spec(tm, d), pl.BlockSpec((8, d), lambda i: (0, 0))),
        compiler_params=_params(1),
    )(xin, dh, dres, pf, g, scale, gate_next)


def _ada_fwd(c_all, w_loc):
    nb, d = c_all.shape
    n = w_loc.shape[1]
    tn = _tile(n, 512, 128) if n % 128 == 0 else n

    def body(c_ref, w_ref, o_ref):
        cv = c_ref[...]
        o_ref[...] = jnp.dot(cv * jax.nn.sigmoid(cv), w_ref[...], preferred_element_type=F32,
                             precision=lax.Precision.HIGHEST)

    return pl.pallas_call(
        body, name="ada_fwd", grid=(n // tn,), out_shape=jax.ShapeDtypeStruct((nb, n), F32),
        in_specs=[pl.BlockSpec((nb, d), lambda j: (0, 0)), pl.BlockSpec((d, tn), lambda j: (0, j))],
        out_specs=pl.BlockSpec((nb, tn), lambda j: (0, j)), compiler_params=_params(1),
    )(c_all, w_loc)


def _ada_bwd(c_all, dmod_cols):
    nb, d = c_all.shape
    n = dmod_cols.shape[1]
    tn = _tile(n, 512, 128) if n % 128 == 0 else n

    def body(c_ref, dm_ref, o_ref):
        cv = c_ref[...]
        o_ref[...] = lax.dot_general(cv * jax.nn.sigmoid(cv), dm_ref[...], TN, preferred_element_type=F32,
                                     precision=lax.Precision.HIGHEST)

    return pl.pallas_call(
        body, name="ada_bwd", grid=(n // tn,), out_shape=jax.ShapeDtypeStruct((d, n), F32),
        in_specs=[pl.BlockSpec((nb, d), lambda j: (0, 0)), pl.BlockSpec((nb, tn), lambda j: (0, j))],
        out_specs=pl.BlockSpec((d, tn), lambda j: (0, j)), compiler_params=_params(1),
    )(c_all, dmod_cols)


_INV_SQRT2 = 0.7071067811865476
_INV_SQRT2PI = 0.3989422804014327


def _gelu(x):
    return 0.5 * x * (1.0 + lax.erf(x * _INV_SQRT2))


def _gelu_grad(x):
    return 0.5 * (1.0 + lax.erf(x * _INV_SQRT2)) + x * jnp.exp(-0.5 * x * x) * _INV_SQRT2PI


def _gm_group_fwd(up, vp, gv, wt, bcol):
    u = _gelu(up)
    va = _gelu(vp)
    xc = va - jnp.mean(va, axis=-1, keepdims=True)
    rstd_v = lax.rsqrt(jnp.mean(xc * xc, axis=-1, keepdims=True) + EPS)
    yv = xc * rstd_v
    vn = (yv * gv).astype(BF16)
    mixed = _dot(wt, vn, NN) + bcol
    return u, rstd_v, yv, vn, mixed, u * mixed


def _tril_bf16(w):
    row = lax.broadcasted_iota(jnp.int32, w.shape, 0)
    col = lax.broadcasted_iota(jnp.int32, w.shape, 1)
    return jnp.where(col <= row, w, 0.0).astype(BF16)


def _gmlp_fwd(proj, v_norm_g, w_spatial, b_cols, gamma, gm):
    s = proj.shape[0]
    ng = gm // GRP

    def body(p_ref, gv_ref, w_ref, b_ref, gam_ref, o_ref):
        for g in range(ng):
            lo = g * GRP
            wt = _tril_bf16(w_ref[g])
            *_, o = _gm_group_fwd(p_ref[:, lo:lo + GRP], p_ref[:, gm + lo:gm + lo + GRP], gv_ref[:, lo:lo + GRP],
                                  wt, b_ref[:, g:g + 1])
            rstd_o = lax.rsqrt(jnp.mean(o * o, axis=-1, keepdims=True) + EPS)
            o_ref[:, lo:lo + GRP] = (o * rstd_o * gam_ref[:, lo:lo + GRP]).astype(BF16)

    return pl.pallas_call(
        body, name="gmlp_fwd", grid=(s // GRP,), out_shape=jax.ShapeDtypeStruct((s, gm), BF16),
        in_specs=[pl.BlockSpec((GRP, 2 * gm), lambda n: (n, 0)), _vec_spec(gm),
                  pl.BlockSpec((ng, GRP, GRP), lambda n: (0, 0, 0)), pl.BlockSpec((GRP, GRP), lambda n: (0, 0)),
                  _vec_spec(gm)],
        out_specs=pl.BlockSpec((GRP, gm), lambda n: (n, 0)), compiler_params=_params(1),
    )(proj, v_norm_g, w_spatial, b_cols, gamma)


def _gmlp_bwd(proj, don, v_norm_g, w_spatial, b_cols, gamma, gm):
    s = proj.shape[0]
    ng = gm // GRP

    def body(p_ref, don_ref, gv_ref, w_ref, b_ref, gam_ref, dp_ref, dw_ref, db_ref, dgv_ref, dgam_ref):
        @pl.when(pl.program_id(0) == 0)
        def _():
            dw_ref[...] = jnp.zeros_like(dw_ref)
            db_ref[...] = jnp.zeros_like(db_ref)
            dgv_ref[...] = jnp.zeros_like(dgv_ref)
            dgam_ref[...] = jnp.zeros_like(dgam_ref)

        lane = lax.broadcasted_iota(jnp.int32, (GRP, GRP), 1)
        row = lax.broadcasted_iota(jnp.int32, (GRP, GRP), 0)
        for g in range(ng):
            lo = g * GRP
            up, vp = p_ref[:, lo:lo + GRP], p_ref[:, gm + lo:gm + lo + GRP]
            gv, gam = gv_ref[:, lo:lo + GRP], gam_ref[:, lo:lo + GRP]
            wt = _tril_bf16(w_ref[g])
            u, rstd_v, yv, vn, mixed, o = _gm_group_fwd(up, vp, gv, wt, b_ref[:, g:g + 1])
            rstd_o = lax.rsqrt(jnp.mean(o * o, axis=-1, keepdims=True) + EPS)
            ohat = o * rstd_o
            dn = don_ref[:, lo:lo + GRP]
            dgam_ref[:, lo:lo + GRP] += jnp.sum(dn * ohat, axis=0, keepdims=True)
            dohat = dn * gam
            do = rstd_o * (dohat - ohat * jnp.mean(dohat * ohat, axis=-1, keepdims=True))
            du = do * mixed
            dmixed = do * u
            dmb = dmixed.astype(BF16)
            db_ref[...] += jnp.where(lane == g, jnp.sum(dmixed, axis=-1, keepdims=True), 0.0)
            dw_ref[g] += jnp.where(lane <= row, _dot(dmb, vn, NT), 0.0)
            dvn = _dot(wt, dmb, TN)
            dgv_ref[:, lo:lo + GRP] += jnp.sum(dvn * yv, axis=0, keepdims=True)
            dyv = dvn * gv
            dva = rstd_v * (dyv - jnp.mean(dyv, axis=-1, keepdims=True)
                            - yv * jnp.mean(dyv * yv, axis=-1, keepdims=True))
            dp_ref[:, lo:lo + GRP] = (du * _gelu_grad(up)).astype(BF16)
            dp_ref[:, gm + lo:gm + lo + GRP] = (dva * _gelu_grad(vp)).astype(BF16)

    const2 = lambda n: (0, 0)
    return pl.pallas_call(
        body, name="gmlp_bwd", grid=(s // GRP,),
        out_shape=(jax.ShapeDtypeStruct((s, 2 * gm), BF16), jax.ShapeDtypeStruct((ng, GRP, GRP), F32),
                   jax.ShapeDtypeStruct((GRP, GRP), F32), jax.ShapeDtypeStruct((1, gm), F32),
                   jax.ShapeDtypeStruct((1, gm), F32)),
        in_specs=[pl.BlockSpec((GRP, 2 * gm), lambda n: (n, 0)), pl.BlockSpec((GRP, gm), lambda n: (n, 0)),
                  _vec_spec(gm), pl.BlockSpec((ng, GRP, GRP), lambda n: (0, 0, 0)),
                  pl.BlockSpec((GRP, GRP), const2), _vec_spec(gm)],
        out_specs=(pl.BlockSpec((GRP, 2 * gm), lambda n: (n, 0)), pl.BlockSpec((ng, GRP, GRP), lambda n: (0, 0, 0)),
                   pl.BlockSpec((GRP, GRP), const2), _vec_spec(gm), _vec_spec(gm)),
        compiler_params=_params(1),
    )(proj, don, v_norm_g, w_spatial, b_cols, gamma)


BLK = 256


def _log_sigmoid(z):
    return jnp.minimum(z, 0.0) - jnp.log1p(jnp.exp(-jnp.abs(z)))


def _split_dot(x, tri, passes):
    n = x.shape[0]
    parts, rest = [], x
    for _ in range(passes):
        hi = rest.astype(BF16)
        parts.append(hi)
        rest = rest - hi.astype(F32)
    res = _dot(jnp.concatenate(parts, axis=0), tri, NN)
    out = res[0:n]
    for k in range(1, passes):
        out = out + res[k * n:(k + 1) * n]
    return out


def _sb_fwd(proj, gamma, gm, sb):
    s = proj.shape[0]
    nh, nq = sb // GRP, s // BLK
    qc, kc, vc = 2 * gm // GRP, (2 * gm + sb) // GRP, (2 * gm + 2 * sb) // GRP
    scale = GRP ** -0.5

    def body(q_ref, k_ref, v_ref, gam_ref, on_ref, o_ref, mt_ref, kb, vb):
        i = pl.program_id(1)

        @pl.when(i == 0)
        def _():
            kb[...] = k_ref[...].astype(BF16)
            vb[...] = v_ref[...].astype(BF16)

        qb = q_ref[...].astype(BF16)
        row = lax.broadcasted_iota(jnp.int32, (BLK, BLK), 0)
        col = lax.broadcasted_iota(jnp.int32, (BLK, BLK), 1)
        later = (row > col).astype(BF16)

        def step(jj, carry):
            tail, acc = carry
            j = i - jj
            off = pl.multiple_of(j * BLK, BLK)
            z = _dot(qb, kb[pl.ds(off, BLK), :], NT) * scale
            lb = _log_sigmoid(z)
            mask = col + j * BLK < row + i * BLK
            l1m = jnp.where(mask, lb - z, 0.0)
            a = jnp.where(mask, jnp.exp(lb + _split_dot(l1m, later, 3) + tail), 0.0)
            acc = acc + _dot(a.astype(BF16), vb[pl.ds(off, BLK), :], NN)
            return tail + jnp.sum(l1m, axis=-1, keepdims=True), acc

        tail, acc = lax.fori_loop(0, i + 1, step, (jnp.zeros((BLK, 1), F32), jnp.zeros((BLK, GRP), F32)))
        rstd = lax.rsqrt(jnp.mean(acc * acc, axis=-1, keepdims=True) + EPS)
        on_ref[...] = (acc * rstd * gam_ref[...]).astype(BF16)
        o_ref[...] = acc
        mt_ref[...] = tail

    return pl.pallas_call(
        body, name="sb_fwd", grid=(nh, nq),
        out_shape=(jax.ShapeDtypeStruct((s, sb), BF16), jax.ShapeDtypeStruct((s, sb), F32),
                   jax.ShapeDtypeStruct((nh, s, 1), F32)),
        in_specs=[pl.BlockSpec((BLK, GRP), lambda h, i: (i, qc + h)), pl.BlockSpec((s, GRP), lambda h, i: (0, kc + h)),
                  pl.BlockSpec((s, GRP), lambda h, i: (0, vc + h)),
                  pl.BlockSpec((1, GRP), lambda h, i: (0, gm // GRP + h))],
        out_specs=(pl.BlockSpec((BLK, GRP), lambda h, i: (i, h)), pl.BlockSpec((BLK, GRP), lambda h, i: (i, h)),
                   pl.BlockSpec((None, BLK, 1), lambda h, i: (h, i, 0))),
        scratch_shapes=[pltpu.VMEM((s, GRP), BF16), pltpu.VMEM((s, GRP), BF16)],
        compiler_params=_params(2),
    )(proj, proj, proj, gamma)


def _sb_bwd(proj, o_raw, mtot, don, gamma, gm, sb):
    s = proj.shape[0]
    nh, nq = sb // GRP, s // BLK
    qc, kc, vc = 2 * gm // GRP, (2 * gm + sb) // GRP, (2 * gm + 2 * sb) // GRP
    scale = GRP ** -0.5

    def body(q_ref, k_ref, v_ref, o_ref, mt_ref, don_ref, gam_ref, dq_ref, dk_ref, dv_ref, dgam_ref,
             kb, vb, dk_acc, dv_acc):
        i = pl.program_id(1)

        @pl.when(i == 0)
        def _():
            kb[...] = k_ref[...].astype(BF16)
            vb[...] = v_ref[...].astype(BF16)
            dk_acc[...] = jnp.zeros_like(dk_acc)
            dv_acc[...] = jnp.zeros_like(dv_acc)
            dgam_ref[...] = jnp.zeros_like(dgam_ref)

        o, dn, gam = o_ref[...], don_ref[...], gam_ref[...]
        rstd = lax.rsqrt(jnp.mean(o * o, axis=-1, keepdims=True) + EPS)
        ohat = o * rstd
        dgam_ref[...] += jnp.sum(dn * ohat, axis=0, keepdims=True)
        dohat = dn * gam
        dob = (rstd * (dohat - ohat * jnp.mean(dohat * ohat, axis=-1, keepdims=True))).astype(BF16)

        qb = q_ref[...].astype(BF16)
        mt = mt_ref[...]
        row = lax.broadcasted_iota(jnp.int32, (BLK, BLK), 0)
        col = lax.broadcasted_iota(jnp.int32, (BLK, BLK), 1)
        upto = (row <= col).astype(BF16)
        before = (row < col).astype(BF16)

        def step(j, carry):
            m_pre, e_pre, dq = carry
            off = pl.multiple_of(j * BLK, BLK)
            kj, vj = kb[pl.ds(off, BLK), :], vb[pl.ds(off, BLK), :]
            z = _dot(qb, kj, NT) * scale
            lb = _log_sigmoid(z)
            mask = col + j * BLK < row + i * BLK
            l1m = jnp.where(mask, lb - z, 0.0)
            tail = mt - m_pre - _split_dot(l1m, upto, 3)
            a = jnp.where(mask, jnp.exp(lb + tail), 0.0)
            de = a * _dot(dob, vj, NT)
            dv_acc[pl.ds(off, BLK), :] += _dot(a.astype(BF16), dob, TN)
            dl1m = e_pre + _split_dot(de, before, 2)
            sig = jnp.exp(lb)
            dz = (de * (1.0 - sig) - jnp.where(mask, dl1m * sig, 0.0)) * scale
            dzb = dz.astype(BF16)
            dk_acc[pl.ds(off, BLK), :] += _dot(dzb, qb, TN)
            return (m_pre + jnp.sum(l1m, axis=-1, keepdims=True), e_pre + jnp.sum(de, axis=-1, keepdims=True),
                    dq + _dot(dzb, kj, NN))

        zero = jnp.zeros((BLK, 1), F32)
        _, _, dq = lax.fori_loop(0, i + 1, step, (zero, zero, jnp.zeros((BLK, GRP), F32)))
        dq_ref[...] = dq.astype(BF16)

        @pl.when(i == nq - 1)
        def _():
            dk_ref[...] = dk_acc[...].astype(BF16)
            dv_ref[...] = dv_acc[...].astype(BF16)

    blk_q = lambda h, i: (i, h)
    whole = lambda h, i: (0, h)
    return pl.pallas_call(
        body, name="sb_bwd", grid=(nh, nq),
        out_shape=(jax.ShapeDtypeStruct((s, sb), BF16),) * 3 + (jax.ShapeDtypeStruct((1, sb), F32),),
        in_specs=[pl.BlockSpec((BLK, GRP), lambda h, i: (i, qc + h)), pl.BlockSpec((s, GRP), lambda h, i: (0, kc + h)),
                  pl.BlockSpec((s, GRP), lambda h, i: (0, vc + h)), pl.BlockSpec((BLK, GRP), blk_q),
                  pl.BlockSpec((None, BLK, 1), lambda h, i: (h, i, 0)),
                  pl.BlockSpec((BLK, GRP), lambda h, i: (i, gm // GRP + h)),
                  pl.BlockSpec((1, GRP), lambda h, i: (0, gm // GRP + h))],
        out_specs=(pl.BlockSpec((BLK, GRP), blk_q), pl.BlockSpec((s, GRP), whole), pl.BlockSpec((s, GRP), whole),
                   pl.BlockSpec((1, GRP), whole)),
        scratch_shapes=[pltpu.VMEM((s, GRP), BF16), pltpu.VMEM((s, GRP), BF16),
                        pltpu.VMEM((s, GRP), F32), pltpu.VMEM((s, GRP), F32)],
        compiler_params=_params(2),
    )(proj, proj, proj, o_raw, mtot, don, gamma)


def _ffn_fwd(h2, wg, wu, wd):
    s, d = h2.shape
    nb, _, fb = wg.shape
    tm = _tile(s, 512, 16)

    def body(h_ref, wg_ref, wu_ref, wd_ref, f_ref, g_ref, u_ref, acc):
        j = pl.program_id(1)
        hv = h_ref[...]
        g = _dot(hv, wg_ref[...], NN)
        u = _dot(hv, wu_ref[...], NN)
        g_ref[...] = g.astype(BF16)
        u_ref[...] = u.astype(BF16)
        part = _dot((g * jax.nn.sigmoid(g) * u).astype(BF16), wd_ref[...], NN)

        @pl.when(j == 0)
        def _():
            acc[...] = part

        @pl.when(j > 0)
        def _():
            acc[...] += part

        @pl.when(j == nb - 1)
        def _():
            f_ref[...] = acc[...]

    rows = pl.BlockSpec((tm, d), lambda i, j: (i, 0))
    wcol = pl.BlockSpec((None, d, fb), lambda i, j: (j, 0, 0))
    hid = pl.BlockSpec((None, tm, fb), lambda i, j: (j, i, 0))
    return pl.pallas_call(
        body, name="ffn_fwd", grid=(s // tm, nb),
        out_shape=(jax.ShapeDtypeStruct((s, d), F32), jax.ShapeDtypeStruct((nb, s, fb), BF16),
                   jax.ShapeDtypeStruct((nb, s, fb), BF16)),
        in_specs=[rows, wcol, wcol, pl.BlockSpec((None, fb, d), lambda i, j: (j, 0, 0))],
        out_specs=(rows, hid, hid), scratch_shapes=[pltpu.VMEM((tm, d), F32)],
        compiler_params=_params(2),
    )(h2, wg, wu, wd)


def _ffn_bwd(df, g_pre, u_pre, wg, wu, wd):
    s, d = df.shape
    nb, _, fb = wg.shape
    tm = _tile(s, 256, 16)

    def body(df_ref, g_ref, u_ref, wg_ref, wu_ref, wd_ref, dh_ref, a_ref, dg_ref, du_ref, acc):
        j = pl.program_id(1)
        g, u = g_ref[...].astype(F32), u_ref[...].astype(F32)
        da = _dot(df_ref[...], wd_ref[...], NT)
        sg = jax.nn.sigmoid(g)
        silu = g * sg
        a_ref[...] = (silu * u).astype(BF16)
        dg = (da * u * (sg * (1.0 + g * (1.0 - sg)))).astype(BF16)
        du = (da * silu).astype(BF16)
        dg_ref[...] = dg
        du_ref[...] = du
        part = _dot(dg, wg_ref[...], NT) + _dot(du, wu_ref[...], NT)

        @pl.when(j == 0)
        def _():
            acc[...] = part

        @pl.when(j > 0)
        def _():
            acc[...] += part

        @pl.when(j == nb - 1)
        def _():
            dh_ref[...] = acc[...]

    rows = pl.BlockSpec((tm, d), lambda i, j: (i, 0))
    wcol = pl.BlockSpec((None, d, fb), lambda i, j: (j, 0, 0))
    hid = pl.BlockSpec((None, tm, fb), lambda i, j: (j, i, 0))
    hid_sds = jax.ShapeDtypeStruct((nb, s, fb), BF16)
    return pl.pallas_call(
        body, name="ffn_bwd", grid=(s // tm, nb),
        out_shape=(jax.ShapeDtypeStruct((s, d), F32), hid_sds, hid_sds, hid_sds),
        in_specs=[rows, hid, hid, wcol, wcol, pl.BlockSpec((None, fb, d), lambda i, j: (j, 0, 0))],
        out_specs=(rows, hid, hid, hid), scratch_shapes=[pltpu.VMEM((tm, d), F32)],
        compiler_params=_params(2),
    )(df, g_pre, u_pre, wg, wu, wd)


def _reduce_adamw(parts, w, m, v, name):
    npart, r, c = parts.shape
    tr = _tile(r, max(16, 65536 // c), 16)
    c1, c2 = 1.0 - ADAM_B1 ** ADAM_STEP, 1.0 - ADAM_B2 ** ADAM_STEP

    def body(p_ref, w_ref, m_ref, v_ref, g_ref, d_ref, nm_ref, nv_ref):
        g = p_ref[0].astype(F32)
        for k in range(1, npart):
            g = g + p_ref[k].astype(F32)
        nm = ADAM_B1 * m_ref[...] + (1.0 - ADAM_B1) * g
        nv = ADAM_B2 * v_ref[...] + (1.0 - ADAM_B2) * (g * g)
        g_ref[...] = g
        nm_ref[...] = nm
        nv_ref[...] = nv
        d_ref[...] = -ADAM_LR * ((nm / c1) / (jnp.sqrt(nv / c2) + ADAM_EPS) + ADAM_WD * w_ref[...])

    blk = pl.BlockSpec((tr, c), lambda i: (i, 0))
    sds = jax.ShapeDtypeStruct((r, c), F32)
    return pl.pallas_call(
        body, name=name, grid=(r // tr,), out_shape=(sds,) * 4,
        in_specs=[pl.BlockSpec((npart, tr, c), lambda i: (0, i, 0)), blk, blk, blk], out_specs=(blk,) * 4,
        compiler_params=_params(1),
    )(parts, w, m, v)


def _pack(vecs):
    rows = jnp.concatenate([a.reshape(-1, GRP) for a in vecs], axis=0)
    pad = -rows.shape[0] % 64
    return jnp.pad(rows, ((0, pad), (0, 0)))


def _unpack(rows, shapes):
    out, at = [], 0
    for shp in shapes:
        n = 1
        for k in shp:
            n *= k
        out.append(rows[at:at + n // GRP].reshape(shp))
        at += n // GRP
    return out


def kernel(x, c, w_ada, b_ada, norm1_g, w_in, v_norm_g, w_spatial, b_spatial, out_norm_g, w_out, norm2_g, w_gate, w_up, w_down, final_g, loss_target, m_w_ada, m_b_ada, m_norm1_g, m_w_in, m_v_norm_g, m_w_spatial, m_b_spatial, m_out_norm_g, m_w_out, m_norm2_g, m_w_gate, m_w_up, m_w_down, m_final_g, v_w_ada, v_b_ada, v_norm1_g, v_w_in, v_v_norm_g, v_w_spatial, v_b_spatial, v_out_norm_g, v_w_out, v_norm2_g, v_w_gate, v_w_up, v_w_down, v_final_g):
    s, d = x.shape[1], x.shape[2]
    gm = v_norm_g.shape[1]
    sb = d - gm
    n_in, ffb, ob = w_in.shape[2], w_gate.shape[2], w_out.shape[1]
    xs, tgt = x[0], loss_target[0]
    me = 4 * lax.axis_index("x") + 2 * lax.axis_index("y") + lax.axis_index("c")

    c_all = _exchange(c, False, "gather_c")[:, 0, :]
    mod_cols = _ada_fwd(c_all, w_ada[0])
    mod = _exchange(mod_cols[:, None, :], True, "scatter_mod").reshape(1, 6 * d) + b_ada
    shift1, scale1, gate1, shift2, scale2, gate2 = [mod[:, k * d:(k + 1) * d] for k in range(6)]

    w_in_all = _exchange(w_in[0].astype(BF16), False, "gather_w_in")
    w_out_all = _exchange(w_out[0].astype(BF16), False, "gather_w_out")
    w_gate_all = _exchange(w_gate[0].astype(BF16), False, "gather_w_gate")
    w_up_all = _exchange(w_up[0].astype(BF16), False, "gather_w_up")
    w_down_all = _exchange(w_down[0].astype(BF16), False, "gather_w_down")

    tm = _tile(s, 512, 16)
    nt = s // tm
    rows_d = pl.BlockSpec((tm, d), lambda i, j: (i, 0))
    h1 = _prenorm_fwd(xs, norm1_g, scale1, shift1, "prenorm1")
    proj = _mm(h1, w_in_all, rows_d, pl.BlockSpec((None, d, n_in), lambda i, j: (j, 0, 0)),
               jax.ShapeDtypeStruct((s, NDEV * n_in), F32), pl.BlockSpec((tm, n_in), lambda i, j: (i, j)),
               (nt, NDEV), NN, False, "proj")
    b_cols = jnp.pad(b_spatial[0].T, ((0, 0), (0, GRP - b_spatial.shape[1])))
    on_gm = _gmlp_fwd(proj, v_norm_g, w_spatial[0], b_cols, out_norm_g, gm)
    on_sb, o_sb, mtot = _sb_fwd(proj, out_norm_g, gm, sb)
    o_n = jnp.concatenate([on_gm, on_sb], axis=1)
    p_out = _mm(o_n, w_out_all, pl.BlockSpec((tm, ob), lambda i, j: (i, j)),
                pl.BlockSpec((None, ob, d), lambda i, j: (j, 0, 0)), jax.ShapeDtypeStruct((s, d), F32), rows_d,
                (nt, NDEV), NN, True, "out_proj")
    x1 = _residual(xs, gate1, p_out, "residual1")
    h2 = _prenorm_fwd(x1, norm2_g, scale2, shift2, "prenorm2")
    f_out, g_pre, u_pre = _ffn_fwd(h2, w_gate_all, w_up_all, w_down_all)
    x2 = _residual(x1, gate2, f_out, "residual2")

    dx2, df, st_f = _final_loss(x2, final_g.reshape(1, d), tgt, gate2)
    loss = lax.psum(0.5 * jnp.sum(st_f[1]) / d, ("x", "y", "c"))
    dh2, act, dg, du = _ffn_bwd(df, g_pre, u_pre, w_gate_all, w_up_all, w_down_all)
    hid = pl.BlockSpec((None, tm, ffb), lambda j, i: (j, i, 0))
    rows_t = pl.BlockSpec((tm, d), lambda j, i: (i, 0))
    col_sds = jax.ShapeDtypeStruct((NDEV, d, ffb), BF16)
    col_out = pl.BlockSpec((None, d, ffb), lambda j, i: (j, 0, 0))
    gw_gate = _mm(h2, dg, rows_t, hid, col_sds, col_out, (NDEV, nt), TN, True, "grad_w_gate")
    gw_up = _mm(h2, du, rows_t, hid, col_sds, col_out, (NDEV, nt), TN, True, "grad_w_up")
    gw_down = _mm(act, df, hid, rows_t, jax.ShapeDtypeStruct((NDEV, ffb, d), BF16),
                  pl.BlockSpec((None, ffb, d), lambda j, i: (j, 0, 0)), (NDEV, nt), TN, True, "grad_w_down")
    dx1, dp, st2 = _prenorm_bwd(x1, dh2, dx2, f_out, norm2_g, scale2, gate1, "prenorm2_bwd")
    don = _mm(dp, w_out_all, rows_d, pl.BlockSpec((None, ob, d), lambda i, j: (j, 0, 0)),
              jax.ShapeDtypeStruct((s, d), F32), pl.BlockSpec((tm, ob), lambda i, j: (i, j)), (nt, NDEV), NT, False,
              "out_proj_bwd")
    gw_out = _mm(o_n, dp, pl.BlockSpec((tm, ob), lambda j, i: (i, j)), rows_t,
                 jax.ShapeDtypeStruct((NDEV, ob, d), BF16), pl.BlockSpec((None, ob, d), lambda j, i: (j, 0, 0)),
                 (NDEV, nt), TN, True, "grad_w_out")
    dproj_gm, dw_sp, db_cols, dgv, dgam_gm = _gmlp_bwd(proj, don, v_norm_g, w_spatial[0], b_cols, out_norm_g, gm)
    dq, dk, dv, dgam_sb = _sb_bwd(proj, o_sb, mtot, don, out_norm_g, gm, sb)
    dproj = jnp.concatenate([dproj_gm, dq, dk, dv], axis=1)
    gw_in = _mm(h1, dproj, rows_t, pl.BlockSpec((tm, n_in), lambda j, i: (i, j)),
                jax.ShapeDtypeStruct((NDEV, d, n_in), BF16), pl.BlockSpec((None, d, n_in), lambda j, i: (j, 0, 0)),
                (NDEV, nt), TN, True, "grad_w_in")
    dh1 = _mm(dproj, w_in_all, pl.BlockSpec((tm, n_in), lambda i, j: (i, j)),
              pl.BlockSpec((None, d, n_in), lambda i, j: (j, 0, 0)), jax.ShapeDtypeStruct((s, d), F32), rows_d,
              (nt, NDEV), NT, True, "in_proj_bwd")
    grad_x, _, st1 = _prenorm_bwd(xs, dh1, dx1, p_out, norm1_g, scale1, gate1, "prenorm1_bwd")

    dmod = jnp.concatenate([st1[0], st1[1], st1[3], st2[0], st2[1], st2[3]])
    small = [st1[2], dgv, dw_sp, db_cols[:, :b_spatial.shape[1]].T, jnp.concatenate([dgam_gm, dgam_sb], axis=1),
             st2[2], st_f[0], dmod]
    small_w = [norm1_g, v_norm_g, w_spatial, b_spatial, out_norm_g, norm2_g, final_g, b_ada]
    small_m = [m_norm1_g, m_v_norm_g, m_w_spatial, m_b_spatial, m_out_norm_g, m_norm2_g, m_final_g, m_b_ada]
    small_v = [v_norm1_g, v_v_norm_g, v_w_spatial, v_b_spatial, v_out_norm_g, v_norm2_g, v_final_g, v_b_ada]
    small_all = _exchange(_pack(small), False, "gather_small")
    sm = _reduce_adamw(small_all, _pack(small_w), _pack(small_m), _pack(small_v), "adamw_small")
    shapes = [a.shape for a in small_w]
    sm_g, sm_d, sm_m, sm_v = [_unpack(t, shapes) for t in sm]

    at = sum(a.size for a in small_w[:-1]) // GRP
    dmod_all = small_all[:, at:at + 6 * d // GRP, :].reshape(NDEV, 6 * d)
    n_ada = w_ada.shape[2]
    dmod_cols = lax.dynamic_slice(dmod_all, (0, me * n_ada), (NDEV, n_ada))
    g_ada = _ada_bwd(c_all, dmod_cols)
    big = {"w_ada": _reduce_adamw(g_ada[None], w_ada[0], m_w_ada[0], v_w_ada[0], "adamw_w_ada")}

    for nm, part, w, m, v in (("w_in", gw_in, w_in, m_w_in, v_w_in), ("w_out", gw_out, w_out, m_w_out, v_w_out),
                              ("w_gate", gw_gate, w_gate, m_w_gate, v_w_gate), ("w_up", gw_up, w_up, m_w_up, v_w_up),
                              ("w_down", gw_down, w_down, m_w_down, v_w_down)):
        big[nm] = _reduce_adamw(_exchange(part, True, "scatter_grad_" + nm), w[0], m[0], v[0], "adamw_" + nm)

    names = ["w_ada", "b_ada", "norm1_g", "w_in", "v_norm_g", "w_spatial", "b_spatial", "out_norm_g", "w_out",
             "norm2_g", "w_gate", "w_up", "w_down", "final_g"]
    small_at = {"norm1_g": 0, "v_norm_g": 1, "w_spatial": 2, "b_spatial": 3, "out_norm_g": 4, "norm2_g": 5,
                "final_g": 6, "b_ada": 7}
    outs = [[], [], [], []]
    for nm in names:
        for k in range(4):
            outs[k].append(big[nm][k][None] if nm in big else (sm_g, sm_d, sm_m, sm_v)[k][small_at[nm]])
    return (loss, grad_x[None], *outs[0], *outs[1], *outs[2], *outs[3])
```

```python
import functools

import jax
import jax.numpy as jnp
from jax import lax
from jax.experimental import pallas as pl
from jax.experimental.pallas import tpu as pltpu

F32, BF16 = jnp.float32, jnp.bfloat16
NDEV = 8
GRP = 128
EPS = 1e-6
VMEM_BYTES = 64 * 2 ** 20
VMEM_LIMIT = VMEM_BYTES - 8 * 2 ** 20
ADAM_LR, ADAM_B1, ADAM_B2, ADAM_EPS, ADAM_WD, ADAM_STEP = 0.001, 0.9, 0.999, 1e-08, 0.01, 10
MESH = pl.DeviceIdType.MESH
NN = (((1,), (0,)), ((), ()))
NT = (((1,), (1,)), ((), ()))
TN = (((0,), (0,)), ((), ()))


def _params(n_axes):
    return pltpu.CompilerParams(dimension_semantics=("arbitrary",) * n_axes, vmem_limit_bytes=VMEM_LIMIT)


def _tile(n, cap, mult):
    best = None
    for t in range(mult, min(n, cap) + 1, mult):
        if n % t == 0:
            best = t
    assert best is not None, (n, cap, mult)
    return best


def _dot(a, b, dims):
    return lax.dot_general(a, b, dims, preferred_element_type=F32)


def _exchange(src, per_peer, name):
    blk = src.shape[1:] if per_peer else src.shape

    def body(src_ref, out_ref, send_sems, recv_sems, local_sem):
        x, y, c = lax.axis_index("x"), lax.axis_index("y"), lax.axis_index("c")
        me = 4 * x + 2 * y + c
        local = pltpu.make_async_copy(src_ref.at[me] if per_peer else src_ref, out_ref.at[me], local_sem)
        local.start()
        sends, recvs = [], []
        for k in range(1, NDEV):
            px, py, pc = x ^ (k >> 2), y ^ ((k >> 1) & 1), c ^ (k & 1)
            p = 4 * px + 2 * py + pc
            mine = src_ref.at[p] if per_peer else src_ref
            sends.append(pltpu.make_async_remote_copy(
                src_ref=mine, dst_ref=out_ref.at[me], send_sem=send_sems.at[k - 1], recv_sem=recv_sems.at[k - 1],
                device_id=(px, py, pc), device_id_type=MESH))
            recvs.append(pltpu.make_async_remote_copy(
                src_ref=mine, dst_ref=out_ref.at[p], send_sem=send_sems.at[k - 1], recv_sem=recv_sems.at[k - 1],
                device_id=(px, py, pc), device_id_type=MESH))
        for cp in sends:
            cp.start()
        for cp in recvs:
            cp.wait_recv()
        for cp in sends:
            cp.wait_send()
        local.wait()

    return pl.pallas_call(
        body, name=name,
        out_shape=jax.ShapeDtypeStruct((NDEV,) + tuple(blk), src.dtype),
        in_specs=[pl.BlockSpec(memory_space=pl.ANY)],
        out_specs=pl.BlockSpec(memory_space=pl.ANY),
        scratch_shapes=[pltpu.SemaphoreType.DMA((NDEV - 1,)), pltpu.SemaphoreType.DMA((NDEV - 1,)),
                        pltpu.SemaphoreType.DMA],
    )(src)


_HBM = pl.BlockSpec(memory_space=pltpu.HBM)
_SEM = pl.BlockSpec(memory_space=pltpu.SEMAPHORE)
_ANY = pl.BlockSpec(memory_space=pl.ANY)
_EFFECT = pltpu.SideEffectType.DATAFLOW_SIDE_EFFECTING


def _peers():
    x, y, c = lax.axis_index("x"), lax.axis_index("y"), lax.axis_index("c")
    out = []
    for k in range(1, NDEV):
        px, py, pc = x ^ (k >> 2), y ^ ((k >> 1) & 1), c ^ (k & 1)
        out.append((k, (px, py, pc), 4 * px + 2 * py + pc))
    return 4 * x + 2 * y + c, out


def _exchange_start(src, per_peer, after, name):
    blk = src.shape[1:] if per_peer else src.shape
    me = 4 * lax.axis_index("x") + 2 * lax.axis_index("y") + lax.axis_index("c")
    own = lax.dynamic_index_in_dim(src, me, 0, keepdims=True) if per_peer else src[None]
    land = lax.dynamic_update_slice(lax.empty((NDEV,) + tuple(blk), src.dtype), own, (me,) + (0,) * len(blk))

    def body(src_ref, land_ref, after_ref, send_sems, recv_sems, src_thru, land_thru, token):
        my, peers = _peers()
        for k, coords, p in peers:
            pltpu.make_async_remote_copy(
                src_ref=src_ref.at[p] if per_peer else src_ref, dst_ref=land_ref.at[my],
                send_sem=send_sems.at[k - 1], recv_sem=recv_sems.at[k - 1], device_id=coords, device_id_type=MESH).start()
        token[...] = jnp.zeros_like(token)

    res = pl.pallas_call(
        body, name=name,
        out_shape=(pltpu.SemaphoreType.DMA((NDEV - 1,)), pltpu.SemaphoreType.DMA((NDEV - 1,)),
                   pltpu.HBM(src.shape, src.dtype), pltpu.HBM(land.shape, land.dtype), jax.ShapeDtypeStruct((8, GRP), F32)),
        in_specs=(_HBM, _HBM, _ANY), out_specs=(_SEM, _SEM, _HBM, _HBM, pl.BlockSpec(memory_space=pltpu.VMEM)),
        input_output_aliases={0: 2, 1: 3}, compiler_params=pltpu.CompilerParams(has_side_effects=_EFFECT),
    )(pltpu.with_memory_space_constraint(src, pltpu.HBM), pltpu.with_memory_space_constraint(land, pltpu.HBM), after)
    return (per_peer,) + tuple(res[:4]), res[4]


def _exchange_wait(handles, after, name):
    per_peer, send_sems, recv_sems, src_thru, land_thru = handles

    def body(src_ref, land_ref, send_sems, recv_sems, after_ref, src_dead, got_ref):
        _, peers = _peers()
        for k, coords, p in peers:
            cp = pltpu.make_async_remote_copy(
                src_ref=src_ref.at[p] if per_peer else src_ref, dst_ref=land_ref.at[p],
                send_sem=send_sems.at[k - 1], recv_sem=recv_sems.at[k - 1], device_id=coords, device_id_type=MESH)
            cp.wait_send()
            cp.wait_recv()

    return pl.pallas_call(
        body, name=name,
        out_shape=(pltpu.HBM(src_thru.shape, src_thru.dtype), pltpu.HBM(land_thru.shape, land_thru.dtype)),
        in_specs=(_HBM, _HBM, _SEM, _SEM, _ANY), out_specs=(_HBM, _HBM), input_output_aliases={0: 0, 1: 1},
        compiler_params=pltpu.CompilerParams(has_side_effects=_EFFECT),
    )(src_thru, land_thru, send_sems, recv_sems, after)[1]


def _mm(a, b, a_spec, b_spec, out_sds, o_spec, grid, dims, acc, name, after=None):
    nk = grid[1]
    extra = () if after is None else (after,)

    def body(a_ref, b_ref, *rest):
        o_ref, *scratch = rest[len(extra):]
        prod = _dot(a_ref[...], b_ref[...], dims)
        if not acc:
            o_ref[...] = prod.astype(o_ref.dtype)
            return
        acc_ref = scratch[0]
        k = pl.program_id(1)

        @pl.when(k == 0)
        def _():
            acc_ref[...] = prod

        @pl.when(k > 0)
        def _():
            acc_ref[...] += prod

        @pl.when(k == nk - 1)
        def _():
            o_ref[...] = acc_ref[...].astype(o_ref.dtype)

    blk = tuple(d for d in o_spec.block_shape if d is not None)
    return pl.pallas_call(
        body, name=name, grid=grid, out_shape=out_sds, in_specs=[a_spec, b_spec] + [_ANY] * len(extra), out_specs=o_spec,
        scratch_shapes=[pltpu.VMEM(blk, F32)] if acc else [],
        compiler_params=_params(2),
    )(a, b, *extra)


def _row_spec(tm, d):
    return pl.BlockSpec((tm, d), lambda i: (i, 0))


def _vec_spec(d):
    return pl.BlockSpec((1, d), lambda i: (0, 0))


def _prenorm_fwd(x, g, scale, shift, name):
    s, d = x.shape
    tm = _tile(s, 256, 16)

    def body(x_ref, g_ref, sc_ref, sh_ref, h_ref):
        xv = x_ref[...]
        rstd = lax.rsqrt(jnp.mean(xv * xv, axis=-1, keepdims=True) + EPS)
        h_ref[...] = ((xv * rstd * g_ref[...]) * (1.0 + sc_ref[...]) + sh_ref[...]).astype(BF16)

    return pl.pallas_call(
        body, name=name, grid=(s // tm,), out_shape=jax.ShapeDtypeStruct((s, d), BF16),
        in_specs=[_row_spec(tm, d), _vec_spec(d), _vec_spec(d), _vec_spec(d)], out_specs=_row_spec(tm, d),
        compiler_params=_params(1),
    )(x, g, scale, shift)


def _residual(x, gate, p, name):
    s, d = x.shape
    tm = _tile(s, 256, 8)

    def body(x_ref, g_ref, p_ref, o_ref):
        o_ref[...] = x_ref[...] + g_ref[...] * p_ref[...]

    return pl.pallas_call(
        body, name=name, grid=(s // tm,), out_shape=jax.ShapeDtypeStruct((s, d), F32),
        in_specs=[_row_spec(tm, d), _vec_spec(d), _row_spec(tm, d)], out_specs=_row_spec(tm, d),
        compiler_params=_params(1),
    )(x, gate, p)


def _final_loss(x2, final_g, target, gate2):
    s, d = x2.shape
    tm = _tile(s, 256, 16)

    def body(x_ref, g_ref, t_ref, gate_ref, dx_ref, df_ref, st_ref):
        @pl.when(pl.program_id(0) == 0)
        def _():
            st_ref[...] = jnp.zeros_like(st_ref)

        xv, gf = x_ref[...], g_ref[...]
        rstd = lax.rsqrt(jnp.mean(xv * xv, axis=-1, keepdims=True) + EPS)
        xhat = xv * rstd
        err = xhat * gf - t_ref[...]
        dy = err * (1.0 / d)
        gdy = dy * gf
        dx = rstd * (gdy - xhat * jnp.mean(gdy * xhat, axis=-1, keepdims=True))
        dx_ref[...] = dx
        df_ref[...] = (gate_ref[...] * dx).astype(BF16)
        st_ref[0:1, :] += jnp.sum(dy * xhat, axis=0, keepdims=True)
        st_ref[1:2, :] += jnp.sum(err * err, axis=0, keepdims=True)

    return pl.pallas_call(
        body, name="final_loss", grid=(s // tm,),
        out_shape=(jax.ShapeDtypeStruct((s, d), F32), jax.ShapeDtypeStruct((s, d), BF16),
                   jax.ShapeDtypeStruct((8, d), F32)),
        in_specs=[_row_spec(tm, d), _vec_spec(d), _row_spec(tm, d), _vec_spec(d)],
        out_specs=(_row_spec(tm, d), _row_spec(tm, d), pl.BlockSpec((8, d), lambda i: (0, 0))),
        compiler_params=_params(1),
    )(x2, final_g, target, gate2)


def _prenorm_bwd(xin, dh, dres, pf, g, scale, gate_next, name, after):
    s, d = xin.shape
    tm = _tile(s, 256, 16)

    def body(x_ref, dh_ref, dr_ref, pf_ref, g_ref, sc_ref, gn_ref, after_ref, dx_ref, dn_ref, st_ref):
        @pl.when(pl.program_id(0) == 0)
        def _():
            st_ref[...] = jnp.zeros_like(st_ref)

        xv, dhv, drv, gv = x_ref[...], dh_ref[...], dr_ref[...], g_ref[...]
        one_sc = 1.0 + sc_ref[...]
        rstd = lax.rsqrt(jnp.mean(xv * xv, axis=-1, keepdims=True) + EPS)
        xhat = xv * rstd
        dxhat = dhv * (gv * one_sc)
        dx = drv + rstd * (dxhat - xhat * jnp.mean(dxhat * xhat, axis=-1, keepdims=True))
        dx_ref[...] = dx
        dn_ref[...] = (gn_ref[...] * dx).astype(BF16)
        dhx = dhv * xhat
        st_ref[0:1, :] += jnp.sum(dhv, axis=0, keepdims=True)
        st_ref[1:2, :] += jnp.sum(dhx, axis=0, keepdims=True) * gv
        st_ref[2:3, :] += jnp.sum(dhx, axis=0, keepdims=True) * one_sc
        st_ref[3:4, :] += jnp.sum(drv * pf_ref[...], axis=0, keepdims=True)

    return pl.pallas_call(
        body, name=name, grid=(s // tm,),
        out_shape=(jax.ShapeDtypeStruct((s, d), F32), jax.ShapeDtypeStruct((s, d), BF16),
                   jax.ShapeDtypeStruct((8, d), F32)),
        in_specs=[_row_spec(tm, d)] * 4 + [_vec_spec(d)] * 3 + [_ANY],
        out_specs=(_row_spec(tm, d), _row_spec(tm, d), pl.BlockSpec((8, d), lambda i: (0, 0))),
        compiler_params=_params(1),
    )(xin, dh, dres, pf, g, scale, gate_next, after)


def _ada_fwd(c_all, w_loc):
    nb, d = c_all.shape
    n = w_loc.shape[1]
    tn = _tile(n, 512, 128) if n % 128 == 0 else n

    def body(c_ref, w_ref, o_ref):
        cv = c_ref[...]
        o_ref[...] = jnp.dot(cv * jax.nn.sigmoid(cv), w_ref[...], preferred_element_type=F32,
                             precision=lax.Precision.HIGHEST)

    return pl.pallas_call(
        body, name="ada_fwd", grid=(n // tn,), out_shape=jax.ShapeDtypeStruct((nb, n), F32),
        in_specs=[pl.BlockSpec((nb, d), lambda j: (0, 0)), pl.BlockSpec((d, tn), lambda j: (0, j))],
        out_specs=pl.BlockSpec((nb, tn), lambda j: (0, j)), compiler_params=_params(1),
    )(c_all, w_loc)


def _ada_bwd(c_all, dmod_cols):
    nb, d = c_all.shape
    n = dmod_cols.shape[1]
    tn = _tile(n, 512, 128) if n % 128 == 0 else n

    def body(c_ref, dm_ref, o_ref):
        cv = c_ref[...]
        o_ref[...] = lax.dot_general(cv * jax.nn.sigmoid(cv), dm_ref[...], TN, preferred_element_type=F32,
                                     precision=lax.Precision.HIGHEST)

    return pl.pallas_call(
        body, name="ada_bwd", grid=(n // tn,), out_shape=jax.ShapeDtypeStruct((d, n), F32),
        in_specs=[pl.BlockSpec((nb, d), lambda j: (0, 0)), pl.BlockSpec((nb, tn), lambda j: (0, j))],
        out_specs=pl.BlockSpec((d, tn), lambda j: (0, j)), compiler_params=_params(1),
    )(c_all, dmod_cols)


_INV_SQRT2 = 0.7071067811865476
_INV_SQRT2PI = 0.3989422804014327


def _gelu(x):
    return 0.5 * x * (1.0 + lax.erf(x * _INV_SQRT2))


def _gelu_grad(x):
    return 0.5 * (1.0 + lax.erf(x * _INV_SQRT2)) + x * jnp.exp(-0.5 * x * x) * _INV_SQRT2PI


def _gm_group_fwd(up, vp, gv, wt, bcol):
    u = _gelu(up)
    va = _gelu(vp)
    xc = va - jnp.mean(va, axis=-1, keepdims=True)
    rstd_v = lax.rsqrt(jnp.mean(xc * xc, axis=-1, keepdims=True) + EPS)
    yv = xc * rstd_v
    vn = (yv * gv).astype(BF16)
    mixed = _dot(wt, vn, NN) + bcol
    return u, rstd_v, yv, vn, mixed, u * mixed


def _tril_bf16(w):
    row = lax.broadcasted_iota(jnp.int32, w.shape, 0)
    col = lax.broadcasted_iota(jnp.int32, w.shape, 1)
    return jnp.where(col <= row, w, 0.0).astype(BF16)


def _gmlp_fwd(proj, v_norm_g, w_spatial, b_cols, gamma, gm):
    s = proj.shape[0]
    ng = gm // GRP

    def body(p_ref, gv_ref, w_ref, b_ref, gam_ref, o_ref):
        for g in range(ng):
            lo = g * GRP
            wt = _tril_bf16(w_ref[g])
            *_, o = _gm_group_fwd(p_ref[:, lo:lo + GRP], p_ref[:, gm + lo:gm + lo + GRP], gv_ref[:, lo:lo + GRP],
                                  wt, b_ref[:, g:g + 1])
            rstd_o = lax.rsqrt(jnp.mean(o * o, axis=-1, keepdims=True) + EPS)
            o_ref[:, lo:lo + GRP] = (o * rstd_o * gam_ref[:, lo:lo + GRP]).astype(BF16)

    return pl.pallas_call(
        body, name="gmlp_fwd", grid=(s // GRP,), out_shape=jax.ShapeDtypeStruct((s, gm), BF16),
        in_specs=[pl.BlockSpec((GRP, 2 * gm), lambda n: (n, 0)), _vec_spec(gm),
                  pl.BlockSpec((ng, GRP, GRP), lambda n: (0, 0, 0)), pl.BlockSpec((GRP, GRP), lambda n: (0, 0)),
                  _vec_spec(gm)],
        out_specs=pl.BlockSpec((GRP, gm), lambda n: (n, 0)), compiler_params=_params(1),
    )(proj, v_norm_g, w_spatial, b_cols, gamma)


def _gmlp_bwd(proj, don, v_norm_g, w_spatial, b_cols, gamma, gm, after):
    s = proj.shape[0]
    ng = gm // GRP

    def body(p_ref, don_ref, gv_ref, w_ref, b_ref, gam_ref, after_ref, dp_ref, dw_ref, db_ref, dgv_ref, dgam_ref):
        @pl.when(pl.program_id(0) == 0)
        def _():
            dw_ref[...] = jnp.zeros_like(dw_ref)
            db_ref[...] = jnp.zeros_like(db_ref)
            dgv_ref[...] = jnp.zeros_like(dgv_ref)
            dgam_ref[...] = jnp.zeros_like(dgam_ref)

        lane = lax.broadcasted_iota(jnp.int32, (GRP, GRP), 1)
        row = lax.broadcasted_iota(jnp.int32, (GRP, GRP), 0)
        for g in range(ng):
            lo = g * GRP
            up, vp = p_ref[:, lo:lo + GRP], p_ref[:, gm + lo:gm + lo + GRP]
            gv, gam = gv_ref[:, lo:lo + GRP], gam_ref[:, lo:lo + GRP]
            wt = _tril_bf16(w_ref[g])
            u, rstd_v, yv, vn, mixed, o = _gm_group_fwd(up, vp, gv, wt, b_ref[:, g:g + 1])
            rstd_o = lax.rsqrt(jnp.mean(o * o, axis=-1, keepdims=True) + EPS)
            ohat = o * rstd_o
            dn = don_ref[:, lo:lo + GRP]
            dgam_ref[:, lo:lo + GRP] += jnp.sum(dn * ohat, axis=0, keepdims=True)
            dohat = dn * gam
            do = rstd_o * (dohat - ohat * jnp.mean(dohat * ohat, axis=-1, keepdims=True))
            du = do * mixed
            dmixed = do * u
            dmb = dmixed.astype(BF16)
            db_ref[...] += jnp.where(lane == g, jnp.sum(dmixed, axis=-1, keepdims=True), 0.0)
            dw_ref[g] += jnp.where(lane <= row, _dot(dmb, vn, NT), 0.0)
            dvn = _dot(wt, dmb, TN)
            dgv_ref[:, lo:lo + GRP] += jnp.sum(dvn * yv, axis=0, keepdims=True)
            dyv = dvn * gv
            dva = rstd_v * (dyv - jnp.mean(dyv, axis=-1, keepdims=True)
                            - yv * jnp.mean(dyv * yv, axis=-1, keepdims=True))
            dp_ref[:, lo:lo + GRP] = (du * _gelu_grad(up)).astype(BF16)
            dp_ref[:, gm + lo:gm + lo + GRP] = (dva * _gelu_grad(vp)).astype(BF16)

    const2 = lambda n: (0, 0)
    return pl.pallas_call(
        body, name="gmlp_bwd", grid=(s // GRP,),
        out_shape=(jax.ShapeDtypeStruct((s, 2 * gm), BF16), jax.ShapeDtypeStruct((ng, GRP, GRP), F32),
                   jax.ShapeDtypeStruct((GRP, GRP), F32), jax.ShapeDtypeStruct((1, gm), F32),
                   jax.ShapeDtypeStruct((1, gm), F32)),
        in_specs=[pl.BlockSpec((GRP, 2 * gm), lambda n: (n, 0)), pl.BlockSpec((GRP, gm), lambda n: (n, 0)),
                  _vec_spec(gm), pl.BlockSpec((ng, GRP, GRP), lambda n: (0, 0, 0)),
                  pl.BlockSpec((GRP, GRP), const2), _vec_spec(gm), _ANY],
        out_specs=(pl.BlockSpec((GRP, 2 * gm), lambda n: (n, 0)), pl.BlockSpec((ng, GRP, GRP), lambda n: (0, 0, 0)),
                   pl.BlockSpec((GRP, GRP), const2), _vec_spec(gm), _vec_spec(gm)),
        compiler_params=_params(1),
    )(proj, don, v_norm_g, w_spatial, b_cols, gamma, after)


BLK = 256


def _log_sigmoid(z):
    return jnp.minimum(z, 0.0) - jnp.log1p(jnp.exp(-jnp.abs(z)))


def _split_dot(x, tri, passes):
    n = x.shape[0]
    parts, rest = [], x
    for _ in range(passes):
        hi = rest.astype(BF16)
        parts.append(hi)
        rest = rest - hi.astype(F32)
    res = _dot(jnp.concatenate(parts, axis=0), tri, NN)
    out = res[0:n]
    for k in range(1, passes):
        out = out + res[k * n:(k + 1) * n]
    return out


def _sb_fwd(proj, gamma, gm, sb):
    s = proj.shape[0]
    nh, nq = sb // GRP, s // BLK
    qc, kc, vc = 2 * gm // GRP, (2 * gm + sb) // GRP, (2 * gm + 2 * sb) // GRP
    scale = GRP ** -0.5

    def body(q_ref, k_ref, v_ref, gam_ref, on_ref, o_ref, mt_ref, kb, vb):
        i = pl.program_id(1)

        @pl.when(i == 0)
        def _():
            kb[...] = k_ref[...].astype(BF16)
            vb[...] = v_ref[...].astype(BF16)

        qb = q_ref[...].astype(BF16)
        row = lax.broadcasted_iota(jnp.int32, (BLK, BLK), 0)
        col = lax.broadcasted_iota(jnp.int32, (BLK, BLK), 1)
        later = (row > col).astype(BF16)

        def step(jj, carry):
            tail, acc = carry
            j = i - jj
            off = pl.multiple_of(j * BLK, BLK)
            z = _dot(qb, kb[pl.ds(off, BLK), :], NT) * scale
            lb = _log_sigmoid(z)
            mask = col + j * BLK < row + i * BLK
            l1m = jnp.where(mask, lb - z, 0.0)
            a = jnp.where(mask, jnp.exp(lb + _split_dot(l1m, later, 3) + tail), 0.0)
            acc = acc + _dot(a.astype(BF16), vb[pl.ds(off, BLK), :], NN)
            return tail + jnp.sum(l1m, axis=-1, keepdims=True), acc

        tail, acc = lax.fori_loop(0, i + 1, step, (jnp.zeros((BLK, 1), F32), jnp.zeros((BLK, GRP), F32)))
        rstd = lax.rsqrt(jnp.mean(acc * acc, axis=-1, keepdims=True) + EPS)
        on_ref[...] = (acc * rstd * gam_ref[...]).astype(BF16)
        o_ref[...] = acc
        mt_ref[...] = tail

    return pl.pallas_call(
        body, name="sb_fwd", grid=(nh, nq),
        out_shape=(jax.ShapeDtypeStruct((s, sb), BF16), jax.ShapeDtypeStruct((s, sb), F32),
                   jax.ShapeDtypeStruct((nh, s, 1), F32)),
        in_specs=[pl.BlockSpec((BLK, GRP), lambda h, i: (i, qc + h)), pl.BlockSpec((s, GRP), lambda h, i: (0, kc + h)),
                  pl.BlockSpec((s, GRP), lambda h, i: (0, vc + h)),
                  pl.BlockSpec((1, GRP), lambda h, i: (0, gm // GRP + h))],
        out_specs=(pl.BlockSpec((BLK, GRP), lambda h, i: (i, h)), pl.BlockSpec((BLK, GRP), lambda h, i: (i, h)),
                   pl.BlockSpec((None, BLK, 1), lambda h, i: (h, i, 0))),
        scratch_shapes=[pltpu.VMEM((s, GRP), BF16), pltpu.VMEM((s, GRP), BF16)],
        compiler_params=_params(2),
    )(proj, proj, proj, gamma)


def _sb_bwd(proj, o_raw, mtot, don, gamma, gm, sb):
    s = proj.shape[0]
    nh, nq = sb // GRP, s // BLK
    qc, kc, vc = 2 * gm // GRP, (2 * gm + sb) // GRP, (2 * gm + 2 * sb) // GRP
    scale = GRP ** -0.5

    def body(q_ref, k_ref, v_ref, o_ref, mt_ref, don_ref, gam_ref, dq_ref, dk_ref, dv_ref, dgam_ref,
             kb, vb, dk_acc, dv_acc):
        i = pl.program_id(1)

        @pl.when(i == 0)
        def _():
            kb[...] = k_ref[...].astype(BF16)
            vb[...] = v_ref[...].astype(BF16)
            dk_acc[...] = jnp.zeros_like(dk_acc)
            dv_acc[...] = jnp.zeros_like(dv_acc)
            dgam_ref[...] = jnp.zeros_like(dgam_ref)

        o, dn, gam = o_ref[...], don_ref[...], gam_ref[...]
        rstd = lax.rsqrt(jnp.mean(o * o, axis=-1, keepdims=True) + EPS)
        ohat = o * rstd
        dgam_ref[...] += jnp.sum(dn * ohat, axis=0, keepdims=True)
        dohat = dn * gam
        dob = (rstd * (dohat - ohat * jnp.mean(dohat * ohat, axis=-1, keepdims=True))).astype(BF16)

        qb = q_ref[...].astype(BF16)
        mt = mt_ref[...]
        row = lax.broadcasted_iota(jnp.int32, (BLK, BLK), 0)
        col = lax.broadcasted_iota(jnp.int32, (BLK, BLK), 1)
        upto = (row <= col).astype(BF16)
        before = (row < col).astype(BF16)

        def step(j, carry):
            m_pre, e_pre, dq = carry
            off = pl.multiple_of(j * BLK, BLK)
            kj, vj = kb[pl.ds(off, BLK), :], vb[pl.ds(off, BLK), :]
            z = _dot(qb, kj, NT) * scale
            lb = _log_sigmoid(z)
            mask = col + j * BLK < row + i * BLK
            l1m = jnp.where(mask, lb - z, 0.0)
            tail = mt - m_pre - _split_dot(l1m, upto, 3)
            a = jnp.where(mask, jnp.exp(lb + tail), 0.0)
            de = a * _dot(dob, vj, NT)
            dv_acc[pl.ds(off, BLK), :] += _dot(a.astype(BF16), dob, TN)
            dl1m = e_pre + _split_dot(de, before, 2)
            sig = jnp.exp(lb)
            dz = (de * (1.0 - sig) - jnp.where(mask, dl1m * sig, 0.0)) * scale
            dzb = dz.astype(BF16)
            dk_acc[pl.ds(off, BLK), :] += _dot(dzb, qb, TN)
            return (m_pre + jnp.sum(l1m, axis=-1, keepdims=True), e_pre + jnp.sum(de, axis=-1, keepdims=True),
                    dq + _dot(dzb, kj, NN))

        zero = jnp.zeros((BLK, 1), F32)
        _, _, dq = lax.fori_loop(0, i + 1, step, (zero, zero, jnp.zeros((BLK, GRP), F32)))
        dq_ref[...] = dq.astype(BF16)

        @pl.when(i == nq - 1)
        def _():
            dk_ref[...] = dk_acc[...].astype(BF16)
            dv_ref[...] = dv_acc[...].astype(BF16)

    blk_q = lambda h, i: (i, h)
    whole = lambda h, i: (0, h)
    return pl.pallas_call(
        body, name="sb_bwd", grid=(nh, nq),
        out_shape=(jax.ShapeDtypeStruct((s, sb), BF16),) * 3 + (jax.ShapeDtypeStruct((1, sb), F32),),
        in_specs=[pl.BlockSpec((BLK, GRP), lambda h, i: (i, qc + h)), pl.BlockSpec((s, GRP), lambda h, i: (0, kc + h)),
                  pl.BlockSpec((s, GRP), lambda h, i: (0, vc + h)), pl.BlockSpec((BLK, GRP), blk_q),
                  pl.BlockSpec((None, BLK, 1), lambda h, i: (h, i, 0)),
                  pl.BlockSpec((BLK, GRP), lambda h, i: (i, gm // GRP + h)),
                  pl.BlockSpec((1, GRP), lambda h, i: (0, gm // GRP + h))],
        out_specs=(pl.BlockSpec((BLK, GRP), blk_q), pl.BlockSpec((s, GRP), whole), pl.BlockSpec((s, GRP), whole),
                   pl.BlockSpec((1, GRP), whole)),
        scratch_shapes=[pltpu.VMEM((s, GRP), BF16), pltpu.VMEM((s, GRP), BF16),
                        pltpu.VMEM((s, GRP), F32), pltpu.VMEM((s, GRP), F32)],
        compiler_params=_params(2),
    )(proj, proj, proj, o_raw, mtot, don, gamma)


def _ffn_fwd(h2, wg, wu, wd):
    s, d = h2.shape
    nb, _, fb = wg.shape
    tm = _tile(s, 512, 16)

    def body(h_ref, wg_ref, wu_ref, wd_ref, f_ref, g_ref, u_ref, acc):
        j = pl.program_id(1)
        hv = h_ref[...]
        g = _dot(hv, wg_ref[...], NN)
        u = _dot(hv, wu_ref[...], NN)
        g_ref[...] = g.astype(BF16)
        u_ref[...] = u.astype(BF16)
        part = _dot((g * jax.nn.sigmoid(g) * u).astype(BF16), wd_ref[...], NN)

        @pl.when(j == 0)
        def _():
            acc[...] = part

        @pl.when(j > 0)
        def _():
            acc[...] += part

        @pl.when(j == nb - 1)
        def _():
            f_ref[...] = acc[...]

    rows = pl.BlockSpec((tm, d), lambda i, j: (i, 0))
    wcol = pl.BlockSpec((None, d, fb), lambda i, j: (j, 0, 0))
    hid = pl.BlockSpec((None, tm, fb), lambda i, j: (j, i, 0))
    return pl.pallas_call(
        body, name="ffn_fwd", grid=(s // tm, nb),
        out_shape=(jax.ShapeDtypeStruct((s, d), F32), jax.ShapeDtypeStruct((nb, s, fb), BF16),
                   jax.ShapeDtypeStruct((nb, s, fb), BF16)),
        in_specs=[rows, wcol, wcol, pl.BlockSpec((None, fb, d), lambda i, j: (j, 0, 0))],
        out_specs=(rows, hid, hid), scratch_shapes=[pltpu.VMEM((tm, d), F32)],
        compiler_params=_params(2),
    )(h2, wg, wu, wd)


def _ffn_bwd(df, g_pre, u_pre, wg, wu, wd):
    s, d = df.shape
    nb, _, fb = wg.shape
    tm = _tile(s, 256, 16)

    def body(df_ref, g_ref, u_ref, wg_ref, wu_ref, wd_ref, dh_ref, a_ref, dg_ref, du_ref, acc):
        j = pl.program_id(1)
        g, u = g_ref[...].astype(F32), u_ref[...].astype(F32)
        da = _dot(df_ref[...], wd_ref[...], NT)
        sg = jax.nn.sigmoid(g)
        silu = g * sg
        a_ref[...] = (silu * u).astype(BF16)
        dg = (da * u * (sg * (1.0 + g * (1.0 - sg)))).astype(BF16)
        du = (da * silu).astype(BF16)
        dg_ref[...] = dg
        du_ref[...] = du
        part = _dot(dg, wg_ref[...], NT) + _dot(du, wu_ref[...], NT)

        @pl.when(j == 0)
        def _():
            acc[...] = part

        @pl.when(j > 0)
        def _():
            acc[...] += part

        @pl.when(j == nb - 1)
        def _():
            dh_ref[...] = acc[...]

    rows = pl.BlockSpec((tm, d), lambda i, j: (i, 0))
    wcol = pl.BlockSpec((None, d, fb), lambda i, j: (j, 0, 0))
    hid = pl.BlockSpec((None, tm, fb), lambda i, j: (j, i, 0))
    hid_sds = jax.ShapeDtypeStruct((nb, s, fb), BF16)
    return pl.pallas_call(
        body, name="ffn_bwd", grid=(s // tm, nb),
        out_shape=(jax.ShapeDtypeStruct((s, d), F32), hid_sds, hid_sds, hid_sds),
        in_specs=[rows, hid, hid, wcol, wcol, pl.BlockSpec((None, fb, d), lambda i, j: (j, 0, 0))],
        out_specs=(rows, hid, hid, hid), scratch_shapes=[pltpu.VMEM((tm, d), F32)],
        compiler_params=_params(2),
    )(df, g_pre, u_pre, wg, wu, wd)


def _reduce_adamw(parts, w, m, v, name, after):
    npart, r, c = parts.shape
    tr = _tile(r, max(16, 65536 // c), 16)
    c1, c2 = 1.0 - ADAM_B1 ** ADAM_STEP, 1.0 - ADAM_B2 ** ADAM_STEP

    def body(p_ref, w_ref, m_ref, v_ref, after_ref, g_ref, d_ref, nm_ref, nv_ref):
        g = p_ref[0].astype(F32)
        for k in range(1, npart):
            g = g + p_ref[k].astype(F32)
        nm = ADAM_B1 * m_ref[...] + (1.0 - ADAM_B1) * g
        nv = ADAM_B2 * v_ref[...] + (1.0 - ADAM_B2) * (g * g)
        g_ref[...] = g
        nm_ref[...] = nm
        nv_ref[...] = nv
        d_ref[...] = -ADAM_LR * ((nm / c1) / (jnp.sqrt(nv / c2) + ADAM_EPS) + ADAM_WD * w_ref[...])

    blk = pl.BlockSpec((tr, c), lambda i: (i, 0))
    sds = jax.ShapeDtypeStruct((r, c), F32)
    return pl.pallas_call(
        body, name=name, grid=(r // tr,), out_shape=(sds,) * 4,
        in_specs=[pl.BlockSpec((npart, tr, c), lambda i: (0, i, 0)), blk, blk, blk, _ANY], out_specs=(blk,) * 4,
        compiler_params=_params(1),
    )(parts, w, m, v, after)


def _pack(vecs):
    rows = jnp.concatenate([a.reshape(-1, GRP) for a in vecs], axis=0)
    pad = -rows.shape[0] % 64
    return jnp.pad(rows, ((0, pad), (0, 0)))


def _unpack(rows, shapes):
    out, at = [], 0
    for shp in shapes:
        n = 1
        for k in shp:
            n *= k
        out.append(rows[at:at + n // GRP].reshape(shp))
        at += n // GRP
    return out


def kernel(x, c, w_ada, b_ada, norm1_g, w_in, v_norm_g, w_spatial, b_spatial, out_norm_g, w_out, norm2_g, w_gate, w_up, w_down, final_g, loss_target, m_w_ada, m_b_ada, m_norm1_g, m_w_in, m_v_norm_g, m_w_spatial, m_b_spatial, m_out_norm_g, m_w_out, m_norm2_g, m_w_gate, m_w_up, m_w_down, m_final_g, v_w_ada, v_b_ada, v_norm1_g, v_w_in, v_v_norm_g, v_w_spatial, v_b_spatial, v_out_norm_g, v_w_out, v_norm2_g, v_w_gate, v_w_up, v_w_down, v_final_g):
    s, d = x.shape[1], x.shape[2]
    gm = v_norm_g.shape[1]
    sb = d - gm
    n_in, ffb, ob = w_in.shape[2], w_gate.shape[2], w_out.shape[1]
    xs, tgt = x[0], loss_target[0]
    me = 4 * lax.axis_index("x") + 2 * lax.axis_index("y") + lax.axis_index("c")

    c_all = _exchange(c, False, "gather_c")[:, 0, :]
    mod_cols = _ada_fwd(c_all, w_ada[0])
    mod = _exchange(mod_cols[:, None, :], True, "scatter_mod").reshape(1, 6 * d) + b_ada
    shift1, scale1, gate1, shift2, scale2, gate2 = [mod[:, k * d:(k + 1) * d] for k in range(6)]

    tok, gathers = mod, {}
    for nm, w in (("w_in", w_in), ("w_out", w_out), ("w_gate", w_gate), ("w_up", w_up), ("w_down", w_down)):
        gathers[nm], tok = _exchange_start(w[0].astype(BF16), False, tok, "gather_start_" + nm)
    shift1 = shift1 + tok[0:1, 0:1]

    tm = _tile(s, 512, 16)
    nt = s // tm
    rows_d = pl.BlockSpec((tm, d), lambda i, j: (i, 0))
    h1 = _prenorm_fwd(xs, norm1_g, scale1, shift1, "prenorm1")
    w_in_all = _exchange_wait(gathers["w_in"], h1, "gather_wait_w_in")
    proj = _mm(h1, w_in_all, rows_d, pl.BlockSpec((None, d, n_in), lambda i, j: (j, 0, 0)),
               jax.ShapeDtypeStruct((s, NDEV * n_in), F32), pl.BlockSpec((tm, n_in), lambda i, j: (i, j)),
               (nt, NDEV), NN, False, "proj")
    b_cols = jnp.pad(b_spatial[0].T, ((0, 0), (0, GRP - b_spatial.shape[1])))
    on_gm = _gmlp_fwd(proj, v_norm_g, w_spatial[0], b_cols, out_norm_g, gm)
    on_sb, o_sb, mtot = _sb_fwd(proj, out_norm_g, gm, sb)
    o_n = jnp.concatenate([on_gm, on_sb], axis=1)
    w_out_all = _exchange_wait(gathers["w_out"], o_n, "gather_wait_w_out")
    p_out = _mm(o_n, w_out_all, pl.BlockSpec((tm, ob), lambda i, j: (i, j)),
                pl.BlockSpec((None, ob, d), lambda i, j: (j, 0, 0)), jax.ShapeDtypeStruct((s, d), F32), rows_d,
                (nt, NDEV), NN, True, "out_proj")
    x1 = _residual(xs, gate1, p_out, "residual1")
    h2 = _prenorm_fwd(x1, norm2_g, scale2, shift2, "prenorm2")
    w_gate_all = _exchange_wait(gathers["w_gate"], h2, "gather_wait_w_gate")
    w_up_all = _exchange_wait(gathers["w_up"], h2, "gather_wait_w_up")
    w_down_all = _exchange_wait(gathers["w_down"], h2, "gather_wait_w_down")
    f_out, g_pre, u_pre = _ffn_fwd(h2, w_gate_all, w_up_all, w_down_all)
    x2 = _residual(x1, gate2, f_out, "residual2")

    dx2, df, st_f = _final_loss(x2, final_g.reshape(1, d), tgt, gate2)
    loss = lax.psum(0.5 * jnp.sum(st_f[1]) / d, ("x", "y", "c"))
    dh2, act, dg, du = _ffn_bwd(df, g_pre, u_pre, w_gate_all, w_up_all, w_down_all)
    hid = pl.BlockSpec((None, tm, ffb), lambda j, i: (j, i, 0))
    rows_t = pl.BlockSpec((tm, d), lambda j, i: (i, 0))
    col_sds = jax.ShapeDtypeStruct((NDEV, d, ffb), BF16)
    col_out = pl.BlockSpec((None, d, ffb), lambda j, i: (j, 0, 0))
    scatters = {}
    gw_gate = _mm(h2, dg, rows_t, hid, col_sds, col_out, (NDEV, nt), TN, True, "grad_w_gate")
    scatters["w_gate"], tok = _exchange_start(gw_gate, True, tok, "scatter_start_w_gate")
    gw_up = _mm(h2, du, rows_t, hid, col_sds, col_out, (NDEV, nt), TN, True, "grad_w_up", tok)
    scatters["w_up"], tok = _exchange_start(gw_up, True, tok, "scatter_start_w_up")
    gw_down = _mm(act, df, hid, rows_t, jax.ShapeDtypeStruct((NDEV, ffb, d), BF16),
                  pl.BlockSpec((None, ffb, d), lambda j, i: (j, 0, 0)), (NDEV, nt), TN, True, "grad_w_down", tok)
    scatters["w_down"], tok = _exchange_start(gw_down, True, tok, "scatter_start_w_down")
    dx1, dp, st2 = _prenorm_bwd(x1, dh2, dx2, f_out, norm2_g, scale2, gate1, "prenorm2_bwd", tok)
    don = _mm(dp, w_out_all, rows_d, pl.BlockSpec((None, ob, d), lambda i, j: (j, 0, 0)),
              jax.ShapeDtypeStruct((s, d), F32), pl.BlockSpec((tm, ob), lambda i, j: (i, j)), (nt, NDEV), NT, False,
              "out_proj_bwd")
    gw_out = _mm(o_n, dp, pl.BlockSpec((tm, ob), lambda j, i: (i, j)), rows_t,
                 jax.ShapeDtypeStruct((NDEV, ob, d), BF16), pl.BlockSpec((None, ob, d), lambda j, i: (j, 0, 0)),
                 (NDEV, nt), TN, True, "grad_w_out", don)
    scatters["w_out"], tok = _exchange_start(gw_out, True, tok, "scatter_start_w_out")
    dproj_gm, dw_sp, db_cols, dgv, dgam_gm = _gmlp_bwd(proj, don, v_norm_g, w_spatial[0], b_cols, out_norm_g, gm, tok)
    dq, dk, dv, dgam_sb = _sb_bwd(proj, o_sb, mtot, don, out_norm_g, gm, sb)
    dproj = jnp.concatenate([dproj_gm, dq, dk, dv], axis=1)
    dh1 = _mm(dproj, w_in_all, pl.BlockSpec((tm, n_in), lambda i, j: (i, j)),
              pl.BlockSpec((None, d, n_in), lambda i, j: (j, 0, 0)), jax.ShapeDtypeStruct((s, d), F32), rows_d,
              (nt, NDEV), NT, True, "in_proj_bwd")
    grad_x, _, st1 = _prenorm_bwd(xs, dh1, dx1, p_out, norm1_g, scale1, gate1, "prenorm1_bwd", tok)

    dmod = jnp.concatenate([st1[0], st1[1], st1[3], st2[0], st2[1], st2[3]])
    small = [st1[2], dgv, dw_sp, db_cols[:, :b_spatial.shape[1]].T, jnp.concatenate([dgam_gm, dgam_sb], axis=1),
             st2[2], st_f[0], dmod]
    small_w = [norm1_g, v_norm_g, w_spatial, b_spatial, out_norm_g, norm2_g, final_g, b_ada]
    small_m = [m_norm1_g, m_v_norm_g, m_w_spatial, m_b_spatial, m_out_norm_g, m_norm2_g, m_final_g, m_b_ada]
    small_v = [v_norm1_g, v_v_norm_g, v_w_spatial, v_b_spatial, v_out_norm_g, v_norm2_g, v_final_g, v_b_ada]
    small_all = _exchange(_pack(small), False, "gather_small")
    gw_in = _mm(h1, dproj, rows_t, pl.BlockSpec((tm, n_in), lambda j, i: (i, j)),
                jax.ShapeDtypeStruct((NDEV, d, n_in), BF16), pl.BlockSpec((None, d, n_in), lambda j, i: (j, 0, 0)),
                (NDEV, nt), TN, True, "grad_w_in", small_all)
    scatters["w_in"], tok = _exchange_start(gw_in, True, tok, "scatter_start_w_in")
    sm = _reduce_adamw(small_all, _pack(small_w), _pack(small_m), _pack(small_v), "adamw_small", tok)
    shapes = [a.shape for a in small_w]
    sm_g, sm_d, sm_m, sm_v = [_unpack(t, shapes) for t in sm]

    at = sum(a.size for a in small_w[:-1]) // GRP
    dmod_all = small_all[:, at:at + 6 * d // GRP, :].reshape(NDEV, 6 * d)
    n_ada = w_ada.shape[2]
    dmod_cols = lax.dynamic_slice(dmod_all, (0, me * n_ada), (NDEV, n_ada))
    g_ada = _ada_bwd(c_all, dmod_cols)
    big = {"w_ada": _reduce_adamw(g_ada[None], w_ada[0], m_w_ada[0], v_w_ada[0], "adamw_w_ada", sm[0])}

    prev = big["w_ada"][1]
    for nm, w, m, v in (("w_gate", w_gate, m_w_gate, v_w_gate), ("w_up", w_up, m_w_up, v_w_up),
                        ("w_down", w_down, m_w_down, v_w_down), ("w_out", w_out, m_w_out, v_w_out),
                        ("w_in", w_in, m_w_in, v_w_in)):
        parts = _exchange_wait(scatters[nm], prev, "scatter_wait_" + nm)
        big[nm] = _reduce_adamw(parts, w[0], m[0], v[0], "adamw_" + nm, tok)
        prev = big[nm][1]

    names = ["w_ada", "b_ada", "norm1_g", "w_in", "v_norm_g", "w_spatial", "b_spatial", "out_norm_g", "w_out",
             "norm2_g", "w_gate", "w_up", "w_down", "final_g"]
    small_at = {"norm1_g": 0, "v_norm_g": 1, "w_spatial": 2, "b_spatial": 3, "out_norm_g": 4, "norm2_g": 5,
                "final_g": 6, "b_ada": 7}
    outs = [[], [], [], []]
    for nm in names:
        for k in range(4):
            outs[k].append(big[nm][k][None] if nm in big else (sm_g, sm_d, sm_m, sm_v)[k][small_at[nm]])
    return (loss, grad_x[None], *outs[0], *outs[1], *outs[2], *outs[3])
```

```python
import functools

import jax
import jax.numpy as jnp
from jax import lax
from jax.experimental import pallas as pl
from jax.experimental.pallas import tpu as pltpu

F32, BF16 = jnp.float32, jnp.bfloat16
NDEV = 8
GRP = 128
EPS = 1e-6
VMEM_BYTES = 64 * 2 ** 20
VMEM_LIMIT = VMEM_BYTES - 8 * 2 ** 20
ADAM_LR, ADAM_B1, ADAM_B2, ADAM_EPS, ADAM_WD, ADAM_STEP = 0.001, 0.9, 0.999, 1e-08, 0.01, 10
MESH = pl.DeviceIdType.MESH
NN = (((1,), (0,)), ((), ()))
NT = (((1,), (1,)), ((), ()))
TN = (((0,), (0,)), ((), ()))


def _params(n_axes):
    return pltpu.CompilerParams(dimension_semantics=("arbitrary",) * n_axes, vmem_limit_bytes=VMEM_LIMIT)


def _tile(n, cap, mult):
    best = None
    for t in range(mult, min(n, cap) + 1, mult):
        if n % t == 0:
            best = t
    assert best is not None, (n, cap, mult)
    return best


def _dot(a, b, dims):
    return lax.dot_general(a, b, dims, preferred_element_type=F32)


def _exchange(src, per_peer, name):
    blk = src.shape[1:] if per_peer else src.shape

    def body(src_ref, out_ref, send_sems, recv_sems, local_sem):
        x, y, c = lax.axis_index("x"), lax.axis_index("y"), lax.axis_index("c")
        me = 4 * x + 2 * y + c
        local = pltpu.make_async_copy(src_ref.at[me] if per_peer else src_ref, out_ref.at[me], local_sem)
        local.start()
        sends, recvs = [], []
        for k in range(1, NDEV):
            px, py, pc = x ^ (k >> 2), y ^ ((k >> 1) & 1), c ^ (k & 1)
            p = 4 * px + 2 * py + pc
            mine = src_ref.at[p] if per_peer else src_ref
            sends.append(pltpu.make_async_remote_copy(
                src_ref=mine, dst_ref=out_ref.at[me], send_sem=send_sems.at[k - 1], recv_sem=recv_sems.at[k - 1],
                device_id=(px, py, pc), device_id_type=MESH))
            recvs.append(pltpu.make_async_remote_copy(
                src_ref=mine, dst_ref=out_ref.at[p], send_sem=send_sems.at[k - 1], recv_sem=recv_sems.at[k - 1],
                device_id=(px, py, pc), device_id_type=MESH))
        for cp in sends:
            cp.start()
        for cp in recvs:
            cp.wait_recv()
        for cp in sends:
            cp.wait_send()
        local.wait()

    return pl.pallas_call(
        body, name=name,
        out_shape=jax.ShapeDtypeStruct((NDEV,) + tuple(blk), src.dtype),
        in_specs=[pl.BlockSpec(memory_space=pl.ANY)],
        out_specs=pl.BlockSpec(memory_space=pl.ANY),
        scratch_shapes=[pltpu.SemaphoreType.DMA((NDEV - 1,)), pltpu.SemaphoreType.DMA((NDEV - 1,)),
                        pltpu.SemaphoreType.DMA],
    )(src)


_HBM = pl.BlockSpec(memory_space=pltpu.HBM)
_SEM = pl.BlockSpec(memory_space=pltpu.SEMAPHORE)
_ANY = pl.BlockSpec(memory_space=pl.ANY)
_EFFECT = pltpu.SideEffectType.DATAFLOW_SIDE_EFFECTING


def _peers():
    x, y, c = lax.axis_index("x"), lax.axis_index("y"), lax.axis_index("c")
    out = []
    for k in range(1, NDEV):
        px, py, pc = x ^ (k >> 2), y ^ ((k >> 1) & 1), c ^ (k & 1)
        out.append((k, (px, py, pc), 4 * px + 2 * py + pc))
    return 4 * x + 2 * y + c, out


def _exchange_start(src, per_peer, after, name):
    blk = src.shape[1:] if per_peer else src.shape
    me = 4 * lax.axis_index("x") + 2 * lax.axis_index("y") + lax.axis_index("c")
    own = lax.dynamic_index_in_dim(src, me, 0, keepdims=True) if per_peer else src[None]
    land = lax.dynamic_update_slice(lax.empty((NDEV,) + tuple(blk), src.dtype), own, (me,) + (0,) * len(blk))

    def body(src_ref, land_ref, after_ref, send_sems, recv_sems, src_thru, land_thru, token):
        my, peers = _peers()
        for k, coords, p in peers:
            pltpu.make_async_remote_copy(
                src_ref=src_ref.at[p] if per_peer else src_ref, dst_ref=land_ref.at[my],
                send_sem=send_sems.at[k - 1], recv_sem=recv_sems.at[k - 1], device_id=coords, device_id_type=MESH).start()
        token[...] = jnp.zeros_like(token)

    res = pl.pallas_call(
        body, name=name,
        out_shape=(pltpu.SemaphoreType.DMA((NDEV - 1,)), pltpu.SemaphoreType.DMA((NDEV - 1,)),
                   pltpu.HBM(src.shape, src.dtype), pltpu.HBM(land.shape, land.dtype), jax.ShapeDtypeStruct((8, GRP), F32)),
        in_specs=(_HBM, _HBM, _ANY), out_specs=(_SEM, _SEM, _HBM, _HBM, pl.BlockSpec(memory_space=pltpu.VMEM)),
        input_output_aliases={0: 2, 1: 3}, compiler_params=pltpu.CompilerParams(has_side_effects=_EFFECT),
    )(pltpu.with_memory_space_constraint(src, pltpu.HBM), pltpu.with_memory_space_constraint(land, pltpu.HBM), after)
    return (per_peer,) + tuple(res[:4]), res[4]


def _exchange_wait(handles, after, name):
    per_peer, send_sems, recv_sems, src_thru, land_thru = handles

    def body(src_ref, land_ref, send_sems, recv_sems, after_ref, src_dead, got_ref):
        _, peers = _peers()
        for k, coords, p in peers:
            cp = pltpu.make_async_remote_copy(
                src_ref=src_ref.at[p] if per_peer else src_ref, dst_ref=land_ref.at[p],
                send_sem=send_sems.at[k - 1], recv_sem=recv_sems.at[k - 1], device_id=coords, device_id_type=MESH)
            cp.wait_send()
            cp.wait_recv()

    return pl.pallas_call(
        body, name=name,
        out_shape=(pltpu.HBM(src_thru.shape, src_thru.dtype), pltpu.HBM(land_thru.shape, land_thru.dtype)),
        in_specs=(_HBM, _HBM, _SEM, _SEM, _ANY), out_specs=(_HBM, _HBM), input_output_aliases={0: 0, 1: 1},
        compiler_params=pltpu.CompilerParams(has_side_effects=_EFFECT),
    )(src_thru, land_thru, send_sems, recv_sems, after)[1]


def _chip_peers():
    x, y, c = lax.axis_index("x"), lax.axis_index("y"), lax.axis_index("c")
    chips = [(x, 1 - y), (1 - x, y), (1 - x, 1 - y)]
    return 4 * x + 2 * y + c, (x, y, 1 - c), [((px, py, c), 4 * px + 2 * py + c) for px, py in chips]


def _gather_start(src, after, name):
    me = 4 * lax.axis_index("x") + 2 * lax.axis_index("y") + lax.axis_index("c")
    land = lax.dynamic_update_slice(lax.empty((NDEV,) + src.shape, src.dtype), src[None], (me,) + (0,) * src.ndim)

    def body(src_ref, land_ref, after_ref, send_sems, recv_sems, src_thru, land_thru, token):
        my, sibling, chips = _chip_peers()
        for k, to in enumerate([sibling] + [coords for coords, _ in chips]):
            pltpu.make_async_remote_copy(src_ref=src_ref, dst_ref=land_ref.at[my], send_sem=send_sems.at[k],
                                         recv_sem=recv_sems.at[k], device_id=to, device_id_type=MESH).start()
        token[...] = jnp.zeros_like(token)

    res = pl.pallas_call(
        body, name=name,
        out_shape=(pltpu.SemaphoreType.DMA((4,)), pltpu.SemaphoreType.DMA((4,)), pltpu.HBM(src.shape, src.dtype),
                   pltpu.HBM(land.shape, land.dtype), jax.ShapeDtypeStruct((8, GRP), F32)),
        in_specs=(_HBM, _HBM, _ANY), out_specs=(_SEM, _SEM, _HBM, _HBM, pl.BlockSpec(memory_space=pltpu.VMEM)),
        input_output_aliases={0: 2, 1: 3}, compiler_params=pltpu.CompilerParams(has_side_effects=_EFFECT),
    )(pltpu.with_memory_space_constraint(src, pltpu.HBM), pltpu.with_memory_space_constraint(land, pltpu.HBM), after)
    return tuple(res[:4]), res[4]


def _gather_forward(handles, after, name):
    send_sems, recv_sems, src_thru, land_thru = handles

    def body(src_ref, land_ref, send_sems, recv_sems, after_ref, send2, recv2, land_out):
        my, sibling, chips = _chip_peers()
        for k, (to, p) in enumerate([(sibling, my ^ 1)] + chips):
            first = pltpu.make_async_remote_copy(src_ref=src_ref, dst_ref=land_ref.at[p], send_sem=send_sems.at[k],
                                                 recv_sem=recv_sems.at[k], device_id=to, device_id_type=MESH)
            first.wait_send()
            first.wait_recv()
        for k, (_, p) in enumerate(chips):
            pltpu.make_async_remote_copy(src_ref=land_ref.at[p], dst_ref=land_ref.at[p], send_sem=send2.at[k],
                                         recv_sem=recv2.at[k], device_id=sibling, device_id_type=MESH).start()

    res = pl.pallas_call(
        body, name=name,
        out_shape=(pltpu.SemaphoreType.DMA((3,)), pltpu.SemaphoreType.DMA((3,)), pltpu.HBM(land_thru.shape, land_thru.dtype)),
        in_specs=(_HBM, _HBM, _SEM, _SEM, _ANY), out_specs=(_SEM, _SEM, _HBM), input_output_aliases={1: 2},
        compiler_params=pltpu.CompilerParams(has_side_effects=_EFFECT),
    )(src_thru, land_thru, send_sems, recv_sems, after)
    return tuple(res)


def _gather_wait(handles, after, name):
    send2, recv2, land_thru = handles

    def body(land_ref, send2, recv2, after_ref, got_ref):
        _, sibling, chips = _chip_peers()
        for k, (_, p) in enumerate(chips):
            cp = pltpu.make_async_remote_copy(src_ref=land_ref.at[p], dst_ref=land_ref.at[p ^ 1], send_sem=send2.at[k],
                                              recv_sem=recv2.at[k], device_id=sibling, device_id_type=MESH)
            cp.wait_send()
            cp.wait_recv()

    return pl.pallas_call(
        body, name=name, out_shape=pltpu.HBM(land_thru.shape, land_thru.dtype),
        in_specs=(_HBM, _SEM, _SEM, _ANY), out_specs=_HBM, input_output_aliases={0: 0},
        compiler_params=pltpu.CompilerParams(has_side_effects=_EFFECT),
    )(land_thru, send2, recv2, after)


def _mm(a, b, a_spec, b_spec, out_sds, o_spec, grid, dims, acc, name, after=None):
    nk = grid[1]
    extra = () if after is None else (after,)

    def body(a_ref, b_ref, *rest):
        o_ref, *scratch = rest[len(extra):]
        prod = _dot(a_ref[...], b_ref[...], dims)
        if not acc:
            o_ref[...] = prod.astype(o_ref.dtype)
            return
        acc_ref = scratch[0]
        k = pl.program_id(1)

        @pl.when(k == 0)
        def _():
            acc_ref[...] = prod

        @pl.when(k > 0)
        def _():
            acc_ref[...] += prod

        @pl.when(k == nk - 1)
        def _():
            o_ref[...] = acc_ref[...].astype(o_ref.dtype)

    blk = tuple(d for d in o_spec.block_shape if d is not None)
    return pl.pallas_call(
        body, name=name, grid=grid, out_shape=out_sds, in_specs=[a_spec, b_spec] + [_ANY] * len(extra), out_specs=o_spec,
        scratch_shapes=[pltpu.VMEM(blk, F32)] if acc else [],
        compiler_params=_params(2),
    )(a, b, *extra)


def _row_spec(tm, d):
    return pl.BlockSpec((tm, d), lambda i: (i, 0))


def _vec_spec(d):
    return pl.BlockSpec((1, d), lambda i: (0, 0))


def _prenorm_fwd(x, g, scale, shift, name):
    s, d = x.shape
    tm = _tile(s, 256, 16)

    def body(x_ref, g_ref, sc_ref, sh_ref, h_ref):
        xv = x_ref[...]
        rstd = lax.rsqrt(jnp.mean(xv * xv, axis=-1, keepdims=True) + EPS)
        h_ref[...] = ((xv * rstd * g_ref[...]) * (1.0 + sc_ref[...]) + sh_ref[...]).astype(BF16)

    return pl.pallas_call(
        body, name=name, grid=(s // tm,), out_shape=jax.ShapeDtypeStruct((s, d), BF16),
        in_specs=[_row_spec(tm, d), _vec_spec(d), _vec_spec(d), _vec_spec(d)], out_specs=_row_spec(tm, d),
        compiler_params=_params(1),
    )(x, g, scale, shift)


def _residual(x, gate, p, name):
    s, d = x.shape
    tm = _tile(s, 256, 8)

    def body(x_ref, g_ref, p_ref, o_ref):
        o_ref[...] = x_ref[...] + g_ref[...] * p_ref[...]

    return pl.pallas_call(
        body, name=name, grid=(s // tm,), out_shape=jax.ShapeDtypeStruct((s, d), F32),
        in_specs=[_row_spec(tm, d), _vec_spec(d), _row_spec(tm, d)], out_specs=_row_spec(tm, d),
        compiler_params=_params(1),
    )(x, gate, p)


def _final_loss(x2, final_g, target, gate2):
    s, d = x2.shape
    tm = _tile(s, 256, 16)

    def body(x_ref, g_ref, t_ref, gate_ref, dx_ref, df_ref, st_ref):
        @pl.when(pl.program_id(0) == 0)
        def _():
            st_ref[...] = jnp.zeros_like(st_ref)

        xv, gf = x_ref[...], g_ref[...]
        rstd = lax.rsqrt(jnp.mean(xv * xv, axis=-1, keepdims=True) + EPS)
        xhat = xv * rstd
        err = xhat * gf - t_ref[...]
        dy = err * (1.0 / d)
        gdy = dy * gf
        dx = rstd * (gdy - xhat * jnp.mean(gdy * xhat, axis=-1, keepdims=True))
        dx_ref[...] = dx
        df_ref[...] = (gate_ref[...] * dx).astype(BF16)
        st_ref[0:1, :] += jnp.sum(dy * xhat, axis=0, keepdims=True)
        st_ref[1:2, :] += jnp.sum(err * err, axis=0, keepdims=True)

    return pl.pallas_call(
        body, name="final_loss", grid=(s // tm,),
        out_shape=(jax.ShapeDtypeStruct((s, d), F32), jax.ShapeDtypeStruct((s, d), BF16),
                   jax.ShapeDtypeStruct((8, d), F32)),
        in_specs=[_row_spec(tm, d), _vec_spec(d), _row_spec(tm, d), _vec_spec(d)],
        out_specs=(_row_spec(tm, d), _row_spec(tm, d), pl.BlockSpec((8, d), lambda i: (0, 0))),
        compiler_params=_params(1),
    )(x2, final_g, target, gate2)


def _prenorm_bwd(xin, dh, dres, pf, g, scale, gate_next, name, after):
    s, d = xin.shape
    tm = _tile(s, 256, 16)

    def body(x_ref, dh_ref, dr_ref, pf_ref, g_ref, sc_ref, gn_ref, after_ref, dx_ref, dn_ref, st_ref):
        @pl.when(pl.program_id(0) == 0)
        def _():
            st_ref[...] = jnp.zeros_like(st_ref)

        xv, dhv, drv, gv = x_ref[...], dh_ref[...], dr_ref[...], g_ref[...]
        one_sc = 1.0 + sc_ref[...]
        rstd = lax.rsqrt(jnp.mean(xv * xv, axis=-1, keepdims=True) + EPS)
        xhat = xv * rstd
        dxhat = dhv * (gv * one_sc)
        dx = drv + rstd * (dxhat - xhat * jnp.mean(dxhat * xhat, axis=-1, keepdims=True))
        dx_ref[...] = dx
        dn_ref[...] = (gn_ref[...] * dx).astype(BF16)
        dhx = dhv * xhat
        st_ref[0:1, :] += jnp.sum(dhv, axis=0, keepdims=True)
        st_ref[1:2, :] += jnp.sum(dhx, axis=0, keepdims=True) * gv
        st_ref[2:3, :] += jnp.sum(dhx, axis=0, keepdims=True) * one_sc
        st_ref[3:4, :] += jnp.sum(drv * pf_ref[...], axis=0, keepdims=True)

    return pl.pallas_call(
        body, name=name, grid=(s // tm,),
        out_shape=(jax.ShapeDtypeStruct((s, d), F32), jax.ShapeDtypeStruct((s, d), BF16),
                   jax.ShapeDtypeStruct((8, d), F32)),
        in_specs=[_row_spec(tm, d)] * 4 + [_vec_spec(d)] * 3 + [_ANY],
        out_specs=(_row_spec(tm, d), _row_spec(tm, d), pl.BlockSpec((8, d), lambda i: (0, 0))),
        compiler_params=_params(1),
    )(xin, dh, dres, pf, g, scale, gate_next, after)


def _ada_fwd(c_all, w_loc):
    nb, d = c_all.shape
    n = w_loc.shape[1]
    tn = _tile(n, 512, 128) if n % 128 == 0 else n

    def body(c_ref, w_ref, o_ref):
        cv = c_ref[...]
        o_ref[...] = jnp.dot(cv * jax.nn.sigmoid(cv), w_ref[...], preferred_element_type=F32,
                             precision=lax.Precision.HIGHEST)

    return pl.pallas_call(
        body, name="ada_fwd", grid=(n // tn,), out_shape=jax.ShapeDtypeStruct((nb, n), F32),
        in_specs=[pl.BlockSpec((nb, d), lambda j: (0, 0)), pl.BlockSpec((d, tn), lambda j: (0, j))],
        out_specs=pl.BlockSpec((nb, tn), lambda j: (0, j)), compiler_params=_params(1),
    )(c_all, w_loc)


def _ada_bwd(c_all, dmod_cols):
    nb, d = c_all.shape
    n = dmod_cols.shape[1]
    tn = _tile(n, 512, 128) if n % 128 == 0 else n

    def body(c_ref, dm_ref, o_ref):
        cv = c_ref[...]
        o_ref[...] = lax.dot_general(cv * jax.nn.sigmoid(cv), dm_ref[...], TN, preferred_element_type=F32,
                                     precision=lax.Precision.HIGHEST)

    return pl.pallas_call(
        body, name="ada_bwd", grid=(n // tn,), out_shape=jax.ShapeDtypeStruct((d, n), F32),
        in_specs=[pl.BlockSpec((nb, d), lambda j: (0, 0)), pl.BlockSpec((nb, tn), lambda j: (0, j))],
        out_specs=pl.BlockSpec((d, tn), lambda j: (0, j)), compiler_params=_params(1),
    )(c_all, dmod_cols)


_INV_SQRT2 = 0.7071067811865476
_INV_SQRT2PI = 0.3989422804014327


def _gelu(x):
    return 0.5 * x * (1.0 + lax.erf(x * _INV_SQRT2))


def _gelu_grad(x):
    return 0.5 * (1.0 + lax.erf(x * _INV_SQRT2)) + x * jnp.exp(-0.5 * x * x) * _INV_SQRT2PI


def _gm_group_fwd(up, vp, gv, wt, bcol):
    u = _gelu(up)
    va = _gelu(vp)
    xc = va - jnp.mean(va, axis=-1, keepdims=True)
    rstd_v = lax.rsqrt(jnp.mean(xc * xc, axis=-1, keepdims=True) + EPS)
    yv = xc * rstd_v
    vn = (yv * gv).astype(BF16)
    mixed = _dot(wt, vn, NN) + bcol
    return u, rstd_v, yv, vn, mixed, u * mixed


def _tril_bf16(w):
    row = lax.broadcasted_iota(jnp.int32, w.shape, 0)
    col = lax.broadcasted_iota(jnp.int32, w.shape, 1)
    return jnp.where(col <= row, w, 0.0).astype(BF16)


def _gmlp_fwd(proj, v_norm_g, w_spatial, b_cols, gamma, gm):
    s = proj.shape[0]
    ng = gm // GRP

    def body(p_ref, gv_ref, w_ref, b_ref, gam_ref, o_ref):
        for g in range(ng):
            lo = g * GRP
            wt = _tril_bf16(w_ref[g])
            *_, o = _gm_group_fwd(p_ref[:, lo:lo + GRP], p_ref[:, gm + lo:gm + lo + GRP], gv_ref[:, lo:lo + GRP],
                                  wt, b_ref[:, g:g + 1])
            rstd_o = lax.rsqrt(jnp.mean(o * o, axis=-1, keepdims=True) + EPS)
            o_ref[:, lo:lo + GRP] = (o * rstd_o * gam_ref[:, lo:lo + GRP]).astype(BF16)

    return pl.pallas_call(
        body, name="gmlp_fwd", grid=(s // GRP,), out_shape=jax.ShapeDtypeStruct((s, gm), BF16),
        in_specs=[pl.BlockSpec((GRP, 2 * gm), lambda n: (n, 0)), _vec_spec(gm),
                  pl.BlockSpec((ng, GRP, GRP), lambda n: (0, 0, 0)), pl.BlockSpec((GRP, GRP), lambda n: (0, 0)),
                  _vec_spec(gm)],
        out_specs=pl.BlockSpec((GRP, gm), lambda n: (n, 0)), compiler_params=_params(1),
    )(proj, v_norm_g, w_spatial, b_cols, gamma)


def _gmlp_bwd(proj, don, v_norm_g, w_spatial, b_cols, gamma, gm, after):
    s = proj.shape[0]
    ng = gm // GRP

    def body(p_ref, don_ref, gv_ref, w_ref, b_ref, gam_ref, after_ref, dp_ref, dw_ref, db_ref, dgv_ref, dgam_ref):
        @pl.when(pl.program_id(0) == 0)
        def _():
            dw_ref[...] = jnp.zeros_like(dw_ref)
            db_ref[...] = jnp.zeros_like(db_ref)
            dgv_ref[...] = jnp.zeros_like(dgv_ref)
            dgam_ref[...] = jnp.zeros_like(dgam_ref)

        lane = lax.broadcasted_iota(jnp.int32, (GRP, GRP), 1)
        row = lax.broadcasted_iota(jnp.int32, (GRP, GRP), 0)
        for g in range(ng):
            lo = g * GRP
            up, vp = p_ref[:, lo:lo + GRP], p_ref[:, gm + lo:gm + lo + GRP]
            gv, gam = gv_ref[:, lo:lo + GRP], gam_ref[:, lo:lo + GRP]
            wt = _tril_bf16(w_ref[g])
            u, rstd_v, yv, vn, mixed, o = _gm_group_fwd(up, vp, gv, wt, b_ref[:, g:g + 1])
            rstd_o = lax.rsqrt(jnp.mean(o * o, axis=-1, keepdims=True) + EPS)
            ohat = o * rstd_o
            dn = don_ref[:, lo:lo + GRP]
            dgam_ref[:, lo:lo + GRP] += jnp.sum(dn * ohat, axis=0, keepdims=True)
            dohat = dn * gam
            do = rstd_o * (dohat - ohat * jnp.mean(dohat * ohat, axis=-1, keepdims=True))
            du = do * mixed
            dmixed = do * u
            dmb = dmixed.astype(BF16)
            db_ref[...] += jnp.where(lane == g, jnp.sum(dmixed, axis=-1, keepdims=True), 0.0)
            dw_ref[g] += jnp.where(lane <= row, _dot(dmb, vn, NT), 0.0)
            dvn = _dot(wt, dmb, TN)
            dgv_ref[:, lo:lo + GRP] += jnp.sum(dvn * yv, axis=0, keepdims=True)
            dyv = dvn * gv
            dva = rstd_v * (dyv - jnp.mean(dyv, axis=-1, keepdims=True)
                            - yv * jnp.mean(dyv * yv, axis=-1, keepdims=True))
            dp_ref[:, lo:lo + GRP] = (du * _gelu_grad(up)).astype(BF16)
            dp_ref[:, gm + lo:gm + lo + GRP] = (dva * _gelu_grad(vp)).astype(BF16)

    const2 = lambda n: (0, 0)
    return pl.pallas_call(
        body, name="gmlp_bwd", grid=(s // GRP,),
        out_shape=(jax.ShapeDtypeStruct((s, 2 * gm), BF16), jax.ShapeDtypeStruct((ng, GRP, GRP), F32),
                   jax.ShapeDtypeStruct((GRP, GRP), F32), jax.ShapeDtypeStruct((1, gm), F32),
                   jax.ShapeDtypeStruct((1, gm), F32)),
        in_specs=[pl.BlockSpec((GRP, 2 * gm), lambda n: (n, 0)), pl.BlockSpec((GRP, gm), lambda n: (n, 0)),
                  _vec_spec(gm), pl.BlockSpec((ng, GRP, GRP), lambda n: (0, 0, 0)),
                  pl.BlockSpec((GRP, GRP), const2), _vec_spec(gm), _ANY],
        out_specs=(pl.BlockSpec((GRP, 2 * gm), lambda n: (n, 0)), pl.BlockSpec((ng, GRP, GRP), lambda n: (0, 0, 0)),
                   pl.BlockSpec((GRP, GRP), const2), _vec_spec(gm), _vec_spec(gm)),
        compiler_params=_params(1),
    )(proj, don, v_norm_g, w_spatial, b_cols, gamma, after)


BLK = 256


def _log_sigmoid(z):
    return jnp.minimum(z, 0.0) - jnp.log1p(jnp.exp(-jnp.abs(z)))


def _split_dot(x, tri, passes):
    n = x.shape[0]
    parts, rest = [], x
    for _ in range(passes):
        hi = rest.astype(BF16)
        parts.append(hi)
        rest = rest - hi.astype(F32)
    res = _dot(jnp.concatenate(parts, axis=0), tri, NN)
    out = res[0:n]
    for k in range(1, passes):
        out = out + res[k * n:(k + 1) * n]
    return out


def _sb_fwd(proj, gamma, gm, sb):
    s = proj.shape[0]
    nh, nq = sb // GRP, s // BLK
    qc, kc, vc = 2 * gm // GRP, (2 * gm + sb) // GRP, (2 * gm + 2 * sb) // GRP
    scale = GRP ** -0.5

    def body(q_ref, k_ref, v_ref, gam_ref, on_ref, o_ref, mt_ref, kb, vb):
        i = pl.program_id(1)

        @pl.when(i == 0)
        def _():
            kb[...] = k_ref[...].astype(BF16)
            vb[...] = v_ref[...].astype(BF16)

        qb = q_ref[...].astype(BF16)
        row = lax.broadcasted_iota(jnp.int32, (BLK, BLK), 0)
        col = lax.broadcasted_iota(jnp.int32, (BLK, BLK), 1)
        later = (row > col).astype(BF16)

        def step(jj, carry):
            tail, acc = carry
            j = i - jj
            off = pl.multiple_of(j * BLK, BLK)
            z = _dot(qb, kb[pl.ds(off, BLK), :], NT) * scale
            lb = _log_sigmoid(z)
            mask = col + j * BLK < row + i * BLK
            l1m = jnp.where(mask, lb - z, 0.0)
            a = jnp.where(mask, jnp.exp(lb + _split_dot(l1m, later, 3) + tail), 0.0)
            acc = acc + _dot(a.astype(BF16), vb[pl.ds(off, BLK), :], NN)
            return tail + jnp.sum(l1m, axis=-1, keepdims=True), acc

        tail, acc = lax.fori_loop(0, i + 1, step, (jnp.zeros((BLK, 1), F32), jnp.zeros((BLK, GRP), F32)))
        rstd = lax.rsqrt(jnp.mean(acc * acc, axis=-1, keepdims=True) + EPS)
        on_ref[...] = (acc * rstd * gam_ref[...]).astype(BF16)
        o_ref[...] = acc
        mt_ref[...] = tail

    return pl.pallas_call(
        body, name="sb_fwd", grid=(nh, nq),
        out_shape=(jax.ShapeDtypeStruct((s, sb), BF16), jax.ShapeDtypeStruct((s, sb), F32),
                   jax.ShapeDtypeStruct((nh, s, 1), F32)),
        in_specs=[pl.BlockSpec((BLK, GRP), lambda h, i: (i, qc + h)), pl.BlockSpec((s, GRP), lambda h, i: (0, kc + h)),
                  pl.BlockSpec((s, GRP), lambda h, i: (0, vc + h)),
                  pl.BlockSpec((1, GRP), lambda h, i: (0, gm // GRP + h))],
        out_specs=(pl.BlockSpec((BLK, GRP), lambda h, i: (i, h)), pl.BlockSpec((BLK, GRP), lambda h, i: (i, h)),
                   pl.BlockSpec((None, BLK, 1), lambda h, i: (h, i, 0))),
        scratch_shapes=[pltpu.VMEM((s, GRP), BF16), pltpu.VMEM((s, GRP), BF16)],
        compiler_params=_params(2),
    )(proj, proj, proj, gamma)


def _sb_bwd(proj, o_raw, mtot, don, gamma, gm, sb):
    s = proj.shape[0]
    nh, nq = sb // GRP, s // BLK
    qc, kc, vc = 2 * gm // GRP, (2 * gm + sb) // GRP, (2 * gm + 2 * sb) // GRP
    scale = GRP ** -0.5

    def body(q_ref, k_ref, v_ref, o_ref, mt_ref, don_ref, gam_ref, dq_ref, dk_ref, dv_ref, dgam_ref,
             kb, vb, dk_acc, dv_acc):
        i = pl.program_id(1)

        @pl.when(i == 0)
        def _():
            kb[...] = k_ref[...].astype(BF16)
            vb[...] = v_ref[...].astype(BF16)
            dk_acc[...] = jnp.zeros_like(dk_acc)
            dv_acc[...] = jnp.zeros_like(dv_acc)
            dgam_ref[...] = jnp.zeros_like(dgam_ref)

        o, dn, gam = o_ref[...], don_ref[...], gam_ref[...]
        rstd = lax.rsqrt(jnp.mean(o * o, axis=-1, keepdims=True) + EPS)
        ohat = o * rstd
        dgam_ref[...] += jnp.sum(dn * ohat, axis=0, keepdims=True)
        dohat = dn * gam
        dob = (rstd * (dohat - ohat * jnp.mean(dohat * ohat, axis=-1, keepdims=True))).astype(BF16)

        qb = q_ref[...].astype(BF16)
        mt = mt_ref[...]
        row = lax.broadcasted_iota(jnp.int32, (BLK, BLK), 0)
        col = lax.broadcasted_iota(jnp.int32, (BLK, BLK), 1)
        upto = (row <= col).astype(BF16)
        before = (row < col).astype(BF16)

        def step(j, carry):
            m_pre, e_pre, dq = carry
            off = pl.multiple_of(j * BLK, BLK)
            kj, vj = kb[pl.ds(off, BLK), :], vb[pl.ds(off, BLK), :]
            z = _dot(qb, kj, NT) * scale
            lb = _log_sigmoid(z)
            mask = col + j * BLK < row + i * BLK
            l1m = jnp.where(mask, lb - z, 0.0)
            tail = mt - m_pre - _split_dot(l1m, upto, 3)
            a = jnp.where(mask, jnp.exp(lb + tail), 0.0)
            de = a * _dot(dob, vj, NT)
            dv_acc[pl.ds(off, BLK), :] += _dot(a.astype(BF16), dob, TN)
            dl1m = e_pre + _split_dot(de, before, 2)
            sig = jnp.exp(lb)
            dz = (de * (1.0 - sig) - jnp.where(mask, dl1m * sig, 0.0)) * scale
            dzb = dz.astype(BF16)
            dk_acc[pl.ds(off, BLK), :] += _dot(dzb, qb, TN)
            return (m_pre + jnp.sum(l1m, axis=-1, keepdims=True), e_pre + jnp.sum(de, axis=-1, keepdims=True),
                    dq + _dot(dzb, kj, NN))

        zero = jnp.zeros((BLK, 1), F32)
        _, _, dq = lax.fori_loop(0, i + 1, step, (zero, zero, jnp.zeros((BLK, GRP), F32)))
        dq_ref[...] = dq.astype(BF16)

        @pl.when(i == nq - 1)
        def _():
            dk_ref[...] = dk_acc[...].astype(BF16)
            dv_ref[...] = dv_acc[...].astype(BF16)

    blk_q = lambda h, i: (i, h)
    whole = lambda h, i: (0, h)
    return pl.pallas_call(
        body, name="sb_bwd", grid=(nh, nq),
        out_shape=(jax.ShapeDtypeStruct((s, sb), BF16),) * 3 + (jax.ShapeDtypeStruct((1, sb), F32),),
        in_specs=[pl.BlockSpec((BLK, GRP), lambda h, i: (i, qc + h)), pl.BlockSpec((s, GRP), lambda h, i: (0, kc + h)),
                  pl.BlockSpec((s, GRP), lambda h, i: (0, vc + h)), pl.BlockSpec((BLK, GRP), blk_q),
                  pl.BlockSpec((None, BLK, 1), lambda h, i: (h, i, 0)),
                  pl.BlockSpec((BLK, GRP), lambda h, i: (i, gm // GRP + h)),
                  pl.BlockSpec((1, GRP), lambda h, i: (0, gm // GRP + h))],
        out_specs=(pl.BlockSpec((BLK, GRP), blk_q), pl.BlockSpec((s, GRP), whole), pl.BlockSpec((s, GRP), whole),
                   pl.BlockSpec((1, GRP), whole)),
        scratch_shapes=[pltpu.VMEM((s, GRP), BF16), pltpu.VMEM((s, GRP), BF16),
                        pltpu.VMEM((s, GRP), F32), pltpu.VMEM((s, GRP), F32)],
        compiler_params=_params(2),
    )(proj, proj, proj, o_raw, mtot, don, gamma)


def _ffn_fwd(h2, wg, wu, wd):
    s, d = h2.shape
    nb, _, fb = wg.shape
    tm = _tile(s, 512, 16)

    def body(h_ref, wg_ref, wu_ref, wd_ref, f_ref, g_ref, u_ref, acc):
        j = pl.program_id(1)
        hv = h_ref[...]
        g = _dot(hv, wg_ref[...], NN)
        u = _dot(hv, wu_ref[...], NN)
        g_ref[...] = g.astype(BF16)
        u_ref[...] = u.astype(BF16)
        part = _dot((g * jax.nn.sigmoid(g) * u).astype(BF16), wd_ref[...], NN)

        @pl.when(j == 0)
        def _():
            acc[...] = part

        @pl.when(j > 0)
        def _():
            acc[...] += part

        @pl.when(j == nb - 1)
        def _():
            f_ref[...] = acc[...]

    rows = pl.BlockSpec((tm, d), lambda i, j: (i, 0))
    wcol = pl.BlockSpec((None, d, fb), lambda i, j: (j, 0, 0))
    hid = pl.BlockSpec((None, tm, fb), lambda i, j: (j, i, 0))
    return pl.pallas_call(
        body, name="ffn_fwd", grid=(s // tm, nb),
        out_shape=(jax.ShapeDtypeStruct((s, d), F32), jax.ShapeDtypeStruct((nb, s, fb), BF16),
                   jax.ShapeDtypeStruct((nb, s, fb), BF16)),
        in_specs=[rows, wcol, wcol, pl.BlockSpec((None, fb, d), lambda i, j: (j, 0, 0))],
        out_specs=(rows, hid, hid), scratch_shapes=[pltpu.VMEM((tm, d), F32)],
        compiler_params=_params(2),
    )(h2, wg, wu, wd)


def _ffn_bwd(df, g_pre, u_pre, wg, wu, wd):
    s, d = df.shape
    nb, _, fb = wg.shape
    tm = _tile(s, 256, 16)

    def body(df_ref, g_ref, u_ref, wg_ref, wu_ref, wd_ref, dh_ref, a_ref, dg_ref, du_ref, acc):
        j = pl.program_id(1)
        g, u = g_ref[...].astype(F32), u_ref[...].astype(F32)
        da = _dot(df_ref[...], wd_ref[...], NT)
        sg = jax.nn.sigmoid(g)
        silu = g * sg
        a_ref[...] = (silu * u).astype(BF16)
        dg = (da * u * (sg * (1.0 + g * (1.0 - sg)))).astype(BF16)
        du = (da * silu).astype(BF16)
        dg_ref[...] = dg
        du_ref[...] = du
        part = _dot(dg, wg_ref[...], NT) + _dot(du, wu_ref[...], NT)

        @pl.when(j == 0)
        def _():
            acc[...] = part

        @pl.when(j > 0)
        def _():
            acc[...] += part

        @pl.when(j == nb - 1)
        def _():
            dh_ref[...] = acc[...]

    rows = pl.BlockSpec((tm, d), lambda i, j: (i, 0))
    wcol = pl.BlockSpec((None, d, fb), lambda i, j: (j, 0, 0))
    hid = pl.BlockSpec((None, tm, fb), lambda i, j: (j, i, 0))
    hid_sds = jax.ShapeDtypeStruct((nb, s, fb), BF16)
    return pl.pallas_call(
        body, name="ffn_bwd", grid=(s // tm, nb),
        out_shape=(jax.ShapeDtypeStruct((s, d), F32), hid_sds, hid_sds, hid_sds),
        in_specs=[rows, hid, hid, wcol, wcol, pl.BlockSpec((None, fb, d), lambda i, j: (j, 0, 0))],
        out_specs=(rows, hid, hid, hid), scratch_shapes=[pltpu.VMEM((tm, d), F32)],
        compiler_params=_params(2),
    )(df, g_pre, u_pre, wg, wu, wd)


def _reduce_adamw(parts, w, m, v, name, after):
    npart, r, c = parts.shape
    tr = _tile(r, max(16, 65536 // c), 16)
    c1, c2 = 1.0 - ADAM_B1 ** ADAM_STEP, 1.0 - ADAM_B2 ** ADAM_STEP

    def body(p_ref, w_ref, m_ref, v_ref, after_ref, g_ref, d_ref, nm_ref, nv_ref):
        g = p_ref[0].astype(F32)
        for k in range(1, npart):
            g = g + p_ref[k].astype(F32)
        nm = ADAM_B1 * m_ref[...] + (1.0 - ADAM_B1) * g
        nv = ADAM_B2 * v_ref[...] + (1.0 - ADAM_B2) * (g * g)
        g_ref[...] = g
        nm_ref[...] = nm
        nv_ref[...] = nv
        d_ref[...] = -ADAM_LR * ((nm / c1) / (jnp.sqrt(nv / c2) + ADAM_EPS) + ADAM_WD * w_ref[...])

    blk = pl.BlockSpec((tr, c), lambda i: (i, 0))
    sds = jax.ShapeDtypeStruct((r, c), F32)
    return pl.pallas_call(
        body, name=name, grid=(r // tr,), out_shape=(sds,) * 4,
        in_specs=[pl.BlockSpec((npart, tr, c), lambda i: (0, i, 0)), blk, blk, blk, _ANY], out_specs=(blk,) * 4,
        compiler_params=_params(1),
    )(parts, w, m, v, after)


def _pack(vecs):
    rows = jnp.concatenate([a.reshape(-1, GRP) for a in vecs], axis=0)
    pad = -rows.shape[0] % 64
    return jnp.pad(rows, ((0, pad), (0, 0)))


def _unpack(rows, shapes):
    out, at = [], 0
    for shp in shapes:
        n = 1
        for k in shp:
            n *= k
        out.append(rows[at:at + n // GRP].reshape(shp))
        at += n // GRP
    return out


def kernel(x, c, w_ada, b_ada, norm1_g, w_in, v_norm_g, w_spatial, b_spatial, out_norm_g, w_out, norm2_g, w_gate, w_up, w_down, final_g, loss_target, m_w_ada, m_b_ada, m_norm1_g, m_w_in, m_v_norm_g, m_w_spatial, m_b_spatial, m_out_norm_g, m_w_out, m_norm2_g, m_w_gate, m_w_up, m_w_down, m_final_g, v_w_ada, v_b_ada, v_norm1_g, v_w_in, v_v_norm_g, v_w_spatial, v_b_spatial, v_out_norm_g, v_w_out, v_norm2_g, v_w_gate, v_w_up, v_w_down, v_final_g):
    s, d = x.shape[1], x.shape[2]
    gm = v_norm_g.shape[1]
    sb = d - gm
    n_in, ffb, ob = w_in.shape[2], w_gate.shape[2], w_out.shape[1]
    xs, tgt = x[0], loss_target[0]
    me = 4 * lax.axis_index("x") + 2 * lax.axis_index("y") + lax.axis_index("c")

    c_all = _exchange(c, False, "gather_c")[:, 0, :]
    mod_cols = _ada_fwd(c_all, w_ada[0])
    mod = _exchange(mod_cols[:, None, :], True, "scatter_mod").reshape(1, 6 * d) + b_ada
    shift1, scale1, gate1, shift2, scale2, gate2 = [mod[:, k * d:(k + 1) * d] for k in range(6)]

    tok, gathers = mod, {}
    for nm, w in (("w_in", w_in), ("w_out", w_out), ("w_gate", w_gate), ("w_up", w_up), ("w_down", w_down)):
        gathers[nm], tok = _gather_start(w[0].astype(BF16), tok, "gather_start_" + nm)
    shift1 = shift1 + tok[0:1, 0:1]

    tm = _tile(s, 512, 16)
    nt = s // tm
    rows_d = pl.BlockSpec((tm, d), lambda i, j: (i, 0))
    h1 = _prenorm_fwd(xs, norm1_g, scale1, shift1, "prenorm1")
    w_in_all = _gather_wait(_gather_forward(gathers["w_in"], h1, "gather_fwd_w_in"), h1, "gather_wait_w_in")
    proj = _mm(h1, w_in_all, rows_d, pl.BlockSpec((None, d, n_in), lambda i, j: (j, 0, 0)),
               jax.ShapeDtypeStruct((s, NDEV * n_in), F32), pl.BlockSpec((tm, n_in), lambda i, j: (i, j)),
               (nt, NDEV), NN, False, "proj")
    fwd_w_out = _gather_forward(gathers["w_out"], proj, "gather_fwd_w_out")
    b_cols = jnp.pad(b_spatial[0].T, ((0, 0), (0, GRP - b_spatial.shape[1])))
    on_sb, o_sb, mtot = _sb_fwd(proj, out_norm_g, gm, sb)
    fwd_w_gate = _gather_forward(gathers["w_gate"], on_sb, "gather_fwd_w_gate")
    on_gm = _gmlp_fwd(proj, v_norm_g, w_spatial[0], b_cols, out_norm_g, gm)
    fwd_w_up = _gather_forward(gathers["w_up"], on_gm, "gather_fwd_w_up")
    o_n = jnp.concatenate([on_gm, on_sb], axis=1)
    w_out_all = _gather_wait(fwd_w_out, o_n, "gather_wait_w_out")
    p_out = _mm(o_n, w_out_all, pl.BlockSpec((tm, ob), lambda i, j: (i, j)),
                pl.BlockSpec((None, ob, d), lambda i, j: (j, 0, 0)), jax.ShapeDtypeStruct((s, d), F32), rows_d,
                (nt, NDEV), NN, True, "out_proj")
    x1 = _residual(xs, gate1, p_out, "residual1")
    h2 = _prenorm_fwd(x1, norm2_g, scale2, shift2, "prenorm2")
    fwd_w_down = _gather_forward(gathers["w_down"], h2, "gather_fwd_w_down")
    w_gate_all = _gather_wait(fwd_w_gate, h2, "gather_wait_w_gate")
    w_up_all = _gather_wait(fwd_w_up, h2, "gather_wait_w_up")
    w_down_all = _gather_wait(fwd_w_down, h2, "gather_wait_w_down")
    f_out, g_pre, u_pre = _ffn_fwd(h2, w_gate_all, w_up_all, w_down_all)
    x2 = _residual(x1, gate2, f_out, "residual2")

    dx2, df, st_f = _final_loss(x2, final_g.reshape(1, d), tgt, gate2)
    loss = lax.psum(0.5 * jnp.sum(st_f[1]) / d, ("x", "y", "c"))
    dh2, act, dg, du = _ffn_bwd(df, g_pre, u_pre, w_gate_all, w_up_all, w_down_all)
    hid = pl.BlockSpec((None, tm, ffb), lambda j, i: (j, i, 0))
    rows_t = pl.BlockSpec((tm, d), lambda j, i: (i, 0))
    col_sds = jax.ShapeDtypeStruct((NDEV, d, ffb), BF16)
    col_out = pl.BlockSpec((None, d, ffb), lambda j, i: (j, 0, 0))
    scatters = {}
    gw_gate = _mm(h2, dg, rows_t, hid, col_sds, col_out, (NDEV, nt), TN, True, "grad_w_gate")
    scatters["w_gate"], tok = _exchange_start(gw_gate, True, tok, "scatter_start_w_gate")
    gw_up = _mm(h2, du, rows_t, hid, col_sds, col_out, (NDEV, nt), TN, True, "grad_w_up", tok)
    scatters["w_up"], tok = _exchange_start(gw_up, True, tok, "scatter_start_w_up")
    gw_down = _mm(act, df, hid, rows_t, jax.ShapeDtypeStruct((NDEV, ffb, d), BF16),
                  pl.BlockSpec((None, ffb, d), lambda j, i: (j, 0, 0)), (NDEV, nt), TN, True, "grad_w_down", tok)
    scatters["w_down"], tok = _exchange_start(gw_down, True, tok, "scatter_start_w_down")
    dx1, dp, st2 = _prenorm_bwd(x1, dh2, dx2, f_out, norm2_g, scale2, gate1, "prenorm2_bwd", tok)
    don = _mm(dp, w_out_all, rows_d, pl.BlockSpec((None, ob, d), lambda i, j: (j, 0, 0)),
              jax.ShapeDtypeStruct((s, d), F32), pl.BlockSpec((tm, ob), lambda i, j: (i, j)), (nt, NDEV), NT, False,
              "out_proj_bwd")
    gw_out = _mm(o_n, dp, pl.BlockSpec((tm, ob), lambda j, i: (i, j)), rows_t,
                 jax.ShapeDtypeStruct((NDEV, ob, d), BF16), pl.BlockSpec((None, ob, d), lambda j, i: (j, 0, 0)),
                 (NDEV, nt), TN, True, "grad_w_out", don)
    scatters["w_out"], tok = _exchange_start(gw_out, True, tok, "scatter_start_w_out")
    dproj_gm, dw_sp, db_cols, dgv, dgam_gm = _gmlp_bwd(proj, don, v_norm_g, w_spatial[0], b_cols, out_norm_g, gm, tok)
    dq, dk, dv, dgam_sb = _sb_bwd(proj, o_sb, mtot, don, out_norm_g, gm, sb)
    dproj = jnp.concatenate([dproj_gm, dq, dk, dv], axis=1)
    dh1 = _mm(dproj, w_in_all, pl.BlockSpec((tm, n_in), lambda i, j: (i, j)),
              pl.BlockSpec((None, d, n_in), lambda i, j: (j, 0, 0)), jax.ShapeDtypeStruct((s, d), F32), rows_d,
              (nt, NDEV), NT, True, "in_proj_bwd")
    grad_x, _, st1 = _prenorm_bwd(xs, dh1, dx1, p_out, norm1_g, scale1, gate1, "prenorm1_bwd", tok)

    dmod = jnp.concatenate([st1[0], st1[1], st1[3], st2[0], st2[1], st2[3]])
    small = [st1[2], dgv, dw_sp, db_cols[:, :b_spatial.shape[1]].T, jnp.concatenate([dgam_gm, dgam_sb], axis=1),
             st2[2], st_f[0], dmod]
    small_w = [norm1_g, v_norm_g, w_spatial, b_spatial, out_norm_g, norm2_g, final_g, b_ada]
    small_m = [m_norm1_g, m_v_norm_g, m_w_spatial, m_b_spatial, m_out_norm_g, m_norm2_g, m_final_g, m_b_ada]
    small_v = [v_norm1_g, v_v_norm_g, v_w_spatial, v_b_spatial, v_out_norm_g, v_norm2_g, v_final_g, v_b_ada]
    small_all = _exchange(_pack(small), False, "gather_small")
    gw_in = _mm(h1, dproj, rows_t, pl.BlockSpec((tm, n_in), lambda j, i: (i, j)),
                jax.ShapeDtypeStruct((NDEV, d, n_in), BF16), pl.BlockSpec((None, d, n_in), lambda j, i: (j, 0, 0)),
                (NDEV, nt), TN, True, "grad_w_in", small_all)
    scatters["w_in"], tok = _exchange_start(gw_in, True, tok, "scatter_start_w_in")
    sm = _reduce_adamw(small_all, _pack(small_w), _pack(small_m), _pack(small_v), "adamw_small", tok)
    shapes = [a.shape for a in small_w]
    sm_g, sm_d, sm_m, sm_v = [_unpack(t, shapes) for t in sm]

    at = sum(a.size for a in small_w[:-1]) // GRP
    dmod_all = small_all[:, at:at + 6 * d // GRP, :].reshape(NDEV, 6 * d)
    n_ada = w_ada.shape[2]
    dmod_cols = lax.dynamic_slice(dmod_all, (0, me * n_ada), (NDEV, n_ada))
    g_ada = _ada_bwd(c_all, dmod_cols)
    big = {"w_ada": _reduce_adamw(g_ada[None], w_ada[0], m_w_ada[0], v_w_ada[0], "adamw_w_ada", sm[0])}

    prev = big["w_ada"][1]
    for nm, w, m, v in (("w_gate", w_gate, m_w_gate, v_w_gate), ("w_up", w_up, m_w_up, v_w_up),
                        ("w_down", w_down, m_w_down, v_w_down), ("w_out", w_out, m_w_out, v_w_out),
                        ("w_in", w_in, m_w_in, v_w_in)):
        parts = _exchange_wait(scatters[nm], prev, "scatter_wait_" + nm)
        big[nm] = _reduce_adamw(parts, w[0], m[0], v[0], "adamw_" + nm, tok)
        prev = big[nm][1]

    names = ["w_ada", "b_ada", "norm1_g", "w_in", "v_norm_g", "w_spatial", "b_spatial", "out_norm_g", "w_out",
             "norm2_g", "w_gate", "w_up", "w_down", "final_g"]
    small_at = {"norm1_g": 0, "v_norm_g": 1, "w_spatial": 2, "b_spatial": 3, "out_norm_g": 4, "norm2_g": 5,
                "final_g": 6, "b_ada": 7}
    outs = [[], [], [], []]
    for nm in names:
        for k in range(4):
            outs[k].append(big[nm][k][None] if nm in big else (sm_g, sm_d, sm_m, sm_v)[k][small_at[nm]])
    return (loss, grad_x[None], *outs[0], *outs[1], *outs[2], *outs[3])
```

```python
import functools

import jax
import jax.numpy as jnp
from jax import lax
from jax.experimental import pallas as pl
from jax.experimental.pallas import tpu as pltpu

F32, BF16 = jnp.float32, jnp.bfloat16
NDEV = 8
GRP = 128
EPS = 1e-6
VMEM_BYTES = 64 * 2 ** 20
VMEM_LIMIT = VMEM_BYTES - 8 * 2 ** 20
ADAM_LR, ADAM_B1, ADAM_B2, ADAM_EPS, ADAM_WD, ADAM_STEP = 0.001, 0.9, 0.999, 1e-08, 0.01, 10
MESH = pl.DeviceIdType.MESH
NN = (((1,), (0,)), ((), ()))
NT = (((1,), (1,)), ((), ()))
TN = (((0,), (0,)), ((), ()))


def _params(n_axes):
    return pltpu.CompilerParams(dimension_semantics=("arbitrary",) * n_axes, vmem_limit_bytes=VMEM_LIMIT)


def _tile(n, cap, mult):
    best = None
    for t in range(mult, min(n, cap) + 1, mult):
        if n % t == 0:
            best = t
    assert best is not None, (n, cap, mult)
    return best


def _dot(a, b, dims):
    return lax.dot_general(a, b, dims, preferred_element_type=F32)


def _exchange(src, per_peer, name):
    blk = src.shape[1:] if per_peer else src.shape

    def body(src_ref, out_ref, send_sems, recv_sems, local_sem):
        x, y, c = lax.axis_index("x"), lax.axis_index("y"), lax.axis_index("c")
        me = 4 * x + 2 * y + c
        local = pltpu.make_async_copy(src_ref.at[me] if per_peer else src_ref, out_ref.at[me], local_sem)
        local.start()
        sends, recvs = [], []
        for k in range(1, NDEV):
            px, py, pc = x ^ (k >> 2), y ^ ((k >> 1) & 1), c ^ (k & 1)
            p = 4 * px + 2 * py + pc
            mine = src_ref.at[p] if per_peer else src_ref
            sends.append(pltpu.make_async_remote_copy(
                src_ref=mine, dst_ref=out_ref.at[me], send_sem=send_sems.at[k - 1], recv_sem=recv_sems.at[k - 1],
                device_id=(px, py, pc), device_id_type=MESH))
            recvs.append(pltpu.make_async_remote_copy(
                src_ref=mine, dst_ref=out_ref.at[p], send_sem=send_sems.at[k - 1], recv_sem=recv_sems.at[k - 1],
                device_id=(px, py, pc), device_id_type=MESH))
        for cp in sends:
            cp.start()
        for cp in recvs:
            cp.wait_recv()
        for cp in sends:
            cp.wait_send()
        local.wait()

    return pl.pallas_call(
        body, name=name,
        out_shape=jax.ShapeDtypeStruct((NDEV,) + tuple(blk), src.dtype),
        in_specs=[pl.BlockSpec(memory_space=pl.ANY)],
        out_specs=pl.BlockSpec(memory_space=pl.ANY),
        scratch_shapes=[pltpu.SemaphoreType.DMA((NDEV - 1,)), pltpu.SemaphoreType.DMA((NDEV - 1,)),
                        pltpu.SemaphoreType.DMA],
    )(src)


_HBM = pl.BlockSpec(memory_space=pltpu.HBM)
_SEM = pl.BlockSpec(memory_space=pltpu.SEMAPHORE)
_ANY = pl.BlockSpec(memory_space=pl.ANY)
_EFFECT = pltpu.SideEffectType.DATAFLOW_SIDE_EFFECTING


def _peers():
    x, y, c = lax.axis_index("x"), lax.axis_index("y"), lax.axis_index("c")
    out = []
    for k in range(1, NDEV):
        px, py, pc = x ^ (k >> 2), y ^ ((k >> 1) & 1), c ^ (k & 1)
        out.append((k, (px, py, pc), 4 * px + 2 * py + pc))
    return 4 * x + 2 * y + c, out


def _exchange_start(src, per_peer, after, name):
    blk = src.shape[1:] if per_peer else src.shape
    me = 4 * lax.axis_index("x") + 2 * lax.axis_index("y") + lax.axis_index("c")
    own = lax.dynamic_index_in_dim(src, me, 0, keepdims=True) if per_peer else src[None]
    land = lax.dynamic_update_slice(lax.empty((NDEV,) + tuple(blk), src.dtype), own, (me,) + (0,) * len(blk))

    def body(src_ref, land_ref, after_ref, send_sems, recv_sems, src_thru, land_thru, token):
        my, peers = _peers()
        for k, coords, p in peers:
            pltpu.make_async_remote_copy(
                src_ref=src_ref.at[p] if per_peer else src_ref, dst_ref=land_ref.at[my],
                send_sem=send_sems.at[k - 1], recv_sem=recv_sems.at[k - 1], device_id=coords, device_id_type=MESH).start()
        token[...] = jnp.zeros_like(token)

    res = pl.pallas_call(
        body, name=name,
        out_shape=(pltpu.SemaphoreType.DMA((NDEV - 1,)), pltpu.SemaphoreType.DMA((NDEV - 1,)),
                   pltpu.HBM(src.shape, src.dtype), pltpu.HBM(land.shape, land.dtype), jax.ShapeDtypeStruct((8, GRP), F32)),
        in_specs=(_HBM, _HBM, _ANY), out_specs=(_SEM, _SEM, _HBM, _HBM, pl.BlockSpec(memory_space=pltpu.VMEM)),
        input_output_aliases={0: 2, 1: 3}, compiler_params=pltpu.CompilerParams(has_side_effects=_EFFECT),
    )(pltpu.with_memory_space_constraint(src, pltpu.HBM), pltpu.with_memory_space_constraint(land, pltpu.HBM), after)
    return (per_peer,) + tuple(res[:4]), res[4]


def _exchange_wait(handles, after, name):
    per_peer, send_sems, recv_sems, src_thru, land_thru = handles

    def body(src_ref, land_ref, send_sems, recv_sems, after_ref, src_dead, got_ref):
        _, peers = _peers()
        for k, coords, p in peers:
            cp = pltpu.make_async_remote_copy(
                src_ref=src_ref.at[p] if per_peer else src_ref, dst_ref=land_ref.at[p],
                send_sem=send_sems.at[k - 1], recv_sem=recv_sems.at[k - 1], device_id=coords, device_id_type=MESH)
            cp.wait_send()
            cp.wait_recv()

    return pl.pallas_call(
        body, name=name,
        out_shape=(pltpu.HBM(src_thru.shape, src_thru.dtype), pltpu.HBM(land_thru.shape, land_thru.dtype)),
        in_specs=(_HBM, _HBM, _SEM, _SEM, _ANY), out_specs=(_HBM, _HBM), input_output_aliases={0: 0, 1: 1},
        compiler_params=pltpu.CompilerParams(has_side_effects=_EFFECT),
    )(src_thru, land_thru, send_sems, recv_sems, after)[1]


def _chip_peers():
    x, y, c = lax.axis_index("x"), lax.axis_index("y"), lax.axis_index("c")
    chips = [(x, 1 - y), (1 - x, y), (1 - x, 1 - y)]
    return 4 * x + 2 * y + c, (x, y, 1 - c), [((px, py, c), 4 * px + 2 * py + c) for px, py in chips]


def _gather_start(src, after, name):
    me = 4 * lax.axis_index("x") + 2 * lax.axis_index("y") + lax.axis_index("c")
    land = lax.dynamic_update_slice(lax.empty((NDEV,) + src.shape, src.dtype), src[None], (me,) + (0,) * src.ndim)

    def body(src_ref, land_ref, after_ref, send_sems, recv_sems, src_thru, land_thru, token):
        my, sibling, chips = _chip_peers()
        for k, to in enumerate([sibling] + [coords for coords, _ in chips]):
            pltpu.make_async_remote_copy(src_ref=src_ref, dst_ref=land_ref.at[my], send_sem=send_sems.at[k],
                                         recv_sem=recv_sems.at[k], device_id=to, device_id_type=MESH).start()
        token[...] = jnp.zeros_like(token)

    res = pl.pallas_call(
        body, name=name,
        out_shape=(pltpu.SemaphoreType.DMA((4,)), pltpu.SemaphoreType.DMA((4,)), pltpu.HBM(src.shape, src.dtype),
                   pltpu.HBM(land.shape, land.dtype), jax.ShapeDtypeStruct((8, GRP), F32)),
        in_specs=(_HBM, _HBM, _ANY), out_specs=(_SEM, _SEM, _HBM, _HBM, pl.BlockSpec(memory_space=pltpu.VMEM)),
        input_output_aliases={0: 2, 1: 3}, compiler_params=pltpu.CompilerParams(has_side_effects=_EFFECT),
    )(pltpu.with_memory_space_constraint(src, pltpu.HBM), pltpu.with_memory_space_constraint(land, pltpu.HBM), after)
    return tuple(res[:4]), res[4]


def _gather_forward(handles, after, name):
    send_sems, recv_sems, src_thru, land_thru = handles

    def body(src_ref, land_ref, send_sems, recv_sems, after_ref, send2, recv2, land_out, token):
        my, sibling, chips = _chip_peers()
        for k, (to, p) in enumerate([(sibling, my ^ 1)] + chips):
            first = pltpu.make_async_remote_copy(src_ref=src_ref, dst_ref=land_ref.at[p], send_sem=send_sems.at[k],
                                                 recv_sem=recv_sems.at[k], device_id=to, device_id_type=MESH)
            first.wait_send()
            first.wait_recv()
        for k, (_, p) in enumerate(chips):
            pltpu.make_async_remote_copy(src_ref=land_ref.at[p], dst_ref=land_ref.at[p], send_sem=send2.at[k],
                                         recv_sem=recv2.at[k], device_id=sibling, device_id_type=MESH).start()
        token[...] = jnp.zeros_like(token)

    res = pl.pallas_call(
        body, name=name,
        out_shape=(pltpu.SemaphoreType.DMA((3,)), pltpu.SemaphoreType.DMA((3,)), pltpu.HBM(land_thru.shape, land_thru.dtype),
                   jax.ShapeDtypeStruct((8, GRP), F32)),
        in_specs=(_HBM, _HBM, _SEM, _SEM, _ANY), out_specs=(_SEM, _SEM, _HBM, pl.BlockSpec(memory_space=pltpu.VMEM)),
        input_output_aliases={1: 2}, compiler_params=pltpu.CompilerParams(has_side_effects=_EFFECT),
    )(src_thru, land_thru, send_sems, recv_sems, after)
    return tuple(res[:3]), res[3]


def _gather_wait(handles, after, name):
    send2, recv2, land_thru = handles

    def body(land_ref, send2, recv2, after_ref, got_ref):
        _, sibling, chips = _chip_peers()
        for k, (_, p) in enumerate(chips):
            cp = pltpu.make_async_remote_copy(src_ref=land_ref.at[p], dst_ref=land_ref.at[p ^ 1], send_sem=send2.at[k],
                                              recv_sem=recv2.at[k], device_id=sibling, device_id_type=MESH)
            cp.wait_send()
            cp.wait_recv()

    return pl.pallas_call(
        body, name=name, out_shape=pltpu.HBM(land_thru.shape, land_thru.dtype),
        in_specs=(_HBM, _SEM, _SEM, _ANY), out_specs=_HBM, input_output_aliases={0: 0},
        compiler_params=pltpu.CompilerParams(has_side_effects=_EFFECT),
    )(land_thru, send2, recv2, after)


def _mm(a, b, a_spec, b_spec, out_sds, o_spec, grid, dims, acc, name, after=None):
    nk = grid[-1]
    extra = () if after is None else (after,)

    def body(a_ref, b_ref, *rest):
        o_ref, *scratch = rest[len(extra):]
        prod = _dot(a_ref[...], b_ref[...], dims)
        if not acc:
            o_ref[...] = prod.astype(o_ref.dtype)
            return
        acc_ref = scratch[0]
        k = pl.program_id(1)

        @pl.when(k == 0)
        def _():
            acc_ref[...] = prod

        @pl.when(k > 0)
        def _():
            acc_ref[...] += prod

        @pl.when(k == nk - 1)
        def _():
            o_ref[...] = acc_ref[...].astype(o_ref.dtype)

    blk = tuple(d for d in o_spec.block_shape if d is not None)
    return pl.pallas_call(
        body, name=name, grid=grid, out_shape=out_sds, in_specs=[a_spec, b_spec] + [_ANY] * len(extra), out_specs=o_spec,
        scratch_shapes=[pltpu.VMEM(blk, F32)] if acc else [],
        compiler_params=_params(len(grid)),
    )(a, b, *extra)


def _row_spec(tm, d):
    return pl.BlockSpec((tm, d), lambda i: (i, 0))


def _vec_spec(d):
    return pl.BlockSpec((1, d), lambda i: (0, 0))


def _prenorm_fwd(x, g, scale, shift, name):
    s, d = x.shape
    tm = _tile(s, 256, 16)

    def body(x_ref, g_ref, sc_ref, sh_ref, h_ref):
        xv = x_ref[...]
        rstd = lax.rsqrt(jnp.mean(xv * xv, axis=-1, keepdims=True) + EPS)
        h_ref[...] = ((xv * rstd * g_ref[...]) * (1.0 + sc_ref[...]) + sh_ref[...]).astype(BF16)

    return pl.pallas_call(
        body, name=name, grid=(s // tm,), out_shape=jax.ShapeDtypeStruct((s, d), BF16),
        in_specs=[_row_spec(tm, d), _vec_spec(d), _vec_spec(d), _vec_spec(d)], out_specs=_row_spec(tm, d),
        compiler_params=_params(1),
    )(x, g, scale, shift)


def _residual(x, gate, p, name):
    s, d = x.shape
    tm = _tile(s, 256, 8)

    def body(x_ref, g_ref, p_ref, o_ref):
        o_ref[...] = x_ref[...] + g_ref[...] * p_ref[...]

    return pl.pallas_call(
        body, name=name, grid=(s // tm,), out_shape=jax.ShapeDtypeStruct((s, d), F32),
        in_specs=[_row_spec(tm, d), _vec_spec(d), _row_spec(tm, d)], out_specs=_row_spec(tm, d),
        compiler_params=_params(1),
    )(x, gate, p)


def _final_loss(x2, final_g, target, gate2):
    s, d = x2.shape
    tm = _tile(s, 256, 16)

    def body(x_ref, g_ref, t_ref, gate_ref, dx_ref, df_ref, st_ref):
        @pl.when(pl.program_id(0) == 0)
        def _():
            st_ref[...] = jnp.zeros_like(st_ref)

        xv, gf = x_ref[...], g_ref[...]
        rstd = lax.rsqrt(jnp.mean(xv * xv, axis=-1, keepdims=True) + EPS)
        xhat = xv * rstd
        err = xhat * gf - t_ref[...]
        dy = err * (1.0 / d)
        gdy = dy * gf
        dx = rstd * (gdy - xhat * jnp.mean(gdy * xhat, axis=-1, keepdims=True))
        dx_ref[...] = dx
        df_ref[...] = (gate_ref[...] * dx).astype(BF16)
        st_ref[0:1, :] += jnp.sum(dy * xhat, axis=0, keepdims=True)
        st_ref[1:2, :] += jnp.sum(err * err, axis=0, keepdims=True)

    return pl.pallas_call(
        body, name="final_loss", grid=(s // tm,),
        out_shape=(jax.ShapeDtypeStruct((s, d), F32), jax.ShapeDtypeStruct((s, d), BF16),
                   jax.ShapeDtypeStruct((8, d), F32)),
        in_specs=[_row_spec(tm, d), _vec_spec(d), _row_spec(tm, d), _vec_spec(d)],
        out_specs=(_row_spec(tm, d), _row_spec(tm, d), pl.BlockSpec((8, d), lambda i: (0, 0))),
        compiler_params=_params(1),
    )(x2, final_g, target, gate2)


def _prenorm_bwd(xin, dh, dres, pf, g, scale, gate_next, name, after):
    s, d = xin.shape
    tm = _tile(s, 256, 16)

    def body(x_ref, dh_ref, dr_ref, pf_ref, g_ref, sc_ref, gn_ref, after_ref, dx_ref, dn_ref, st_ref):
        @pl.when(pl.program_id(0) == 0)
        def _():
            st_ref[...] = jnp.zeros_like(st_ref)

        xv, dhv, drv, gv = x_ref[...], dh_ref[...], dr_ref[...], g_ref[...]
        one_sc = 1.0 + sc_ref[...]
        rstd = lax.rsqrt(jnp.mean(xv * xv, axis=-1, keepdims=True) + EPS)
        xhat = xv * rstd
        dxhat = dhv * (gv * one_sc)
        dx = drv + rstd * (dxhat - xhat * jnp.mean(dxhat * xhat, axis=-1, keepdims=True))
        dx_ref[...] = dx
        dn_ref[...] = (gn_ref[...] * dx).astype(BF16)
        dhx = dhv * xhat
        st_ref[0:1, :] += jnp.sum(dhv, axis=0, keepdims=True)
        st_ref[1:2, :] += jnp.sum(dhx, axis=0, keepdims=True) * gv
        st_ref[2:3, :] += jnp.sum(dhx, axis=0, keepdims=True) * one_sc
        st_ref[3:4, :] += jnp.sum(drv * pf_ref[...], axis=0, keepdims=True)

    return pl.pallas_call(
        body, name=name, grid=(s // tm,),
        out_shape=(jax.ShapeDtypeStruct((s, d), F32), jax.ShapeDtypeStruct((s, d), BF16),
                   jax.ShapeDtypeStruct((8, d), F32)),
        in_specs=[_row_spec(tm, d)] * 4 + [_vec_spec(d)] * 3 + [_ANY],
        out_specs=(_row_spec(tm, d), _row_spec(tm, d), pl.BlockSpec((8, d), lambda i: (0, 0))),
        compiler_params=_params(1),
    )(xin, dh, dres, pf, g, scale, gate_next, after)


def _ada_fwd(c_all, w_loc):
    nb, d = c_all.shape
    n = w_loc.shape[1]
    tn = _tile(n, 512, 128) if n % 128 == 0 else n

    def body(c_ref, w_ref, o_ref):
        cv = c_ref[...]
        o_ref[...] = jnp.dot(cv * jax.nn.sigmoid(cv), w_ref[...], preferred_element_type=F32,
                             precision=lax.Precision.HIGHEST)

    return pl.pallas_call(
        body, name="ada_fwd", grid=(n // tn,), out_shape=jax.ShapeDtypeStruct((nb, n), F32),
        in_specs=[pl.BlockSpec((nb, d), lambda j: (0, 0)), pl.BlockSpec((d, tn), lambda j: (0, j))],
        out_specs=pl.BlockSpec((nb, tn), lambda j: (0, j)), compiler_params=_params(1),
    )(c_all, w_loc)


def _ada_bwd(c_all, dmod_cols):
    nb, d = c_all.shape
    n = dmod_cols.shape[1]
    tn = _tile(n, 512, 128) if n % 128 == 0 else n

    def body(c_ref, dm_ref, o_ref):
        cv = c_ref[...]
        o_ref[...] = lax.dot_general(cv * jax.nn.sigmoid(cv), dm_ref[...], TN, preferred_element_type=F32,
                                     precision=lax.Precision.HIGHEST)

    return pl.pallas_call(
        body, name="ada_bwd", grid=(n // tn,), out_shape=jax.ShapeDtypeStruct((d, n), F32),
        in_specs=[pl.BlockSpec((nb, d), lambda j: (0, 0)), pl.BlockSpec((nb, tn), lambda j: (0, j))],
        out_specs=pl.BlockSpec((d, tn), lambda j: (0, j)), compiler_params=_params(1),
    )(c_all, dmod_cols)


_INV_SQRT2 = 0.7071067811865476
_INV_SQRT2PI = 0.3989422804014327


def _gelu(x):
    return 0.5 * x * (1.0 + lax.erf(x * _INV_SQRT2))


def _gelu_grad(x):
    return 0.5 * (1.0 + lax.erf(x * _INV_SQRT2)) + x * jnp.exp(-0.5 * x * x) * _INV_SQRT2PI


def _gm_group_fwd(up, vp, gv, wt, bcol):
    u = _gelu(up)
    va = _gelu(vp)
    xc = va - jnp.mean(va, axis=-1, keepdims=True)
    rstd_v = lax.rsqrt(jnp.mean(xc * xc, axis=-1, keepdims=True) + EPS)
    yv = xc * rstd_v
    vn = (yv * gv).astype(BF16)
    mixed = _dot(wt, vn, NN) + bcol
    return u, rstd_v, yv, vn, mixed, u * mixed


def _tril_bf16(w):
    row = lax.broadcasted_iota(jnp.int32, w.shape, 0)
    col = lax.broadcasted_iota(jnp.int32, w.shape, 1)
    return jnp.where(col <= row, w, 0.0).astype(BF16)


def _gmlp_fwd(proj, v_norm_g, w_spatial, b_cols, gamma, gm, after):
    s = proj.shape[0]
    ng = gm // GRP

    def body(p_ref, gv_ref, w_ref, b_ref, gam_ref, after_ref, o_ref):
        for g in range(ng):
            lo = g * GRP
            wt = _tril_bf16(w_ref[g])
            *_, o = _gm_group_fwd(p_ref[:, lo:lo + GRP], p_ref[:, gm + lo:gm + lo + GRP], gv_ref[:, lo:lo + GRP],
                                  wt, b_ref[:, g:g + 1])
            rstd_o = lax.rsqrt(jnp.mean(o * o, axis=-1, keepdims=True) + EPS)
            o_ref[:, lo:lo + GRP] = (o * rstd_o * gam_ref[:, lo:lo + GRP]).astype(BF16)

    return pl.pallas_call(
        body, name="gmlp_fwd", grid=(s // GRP,), out_shape=jax.ShapeDtypeStruct((s, gm), BF16),
        in_specs=[pl.BlockSpec((GRP, 2 * gm), lambda n: (n, 0)), _vec_spec(gm),
                  pl.BlockSpec((ng, GRP, GRP), lambda n: (0, 0, 0)), pl.BlockSpec((GRP, GRP), lambda n: (0, 0)),
                  _vec_spec(gm), _ANY],
        out_specs=pl.BlockSpec((GRP, gm), lambda n: (n, 0)), compiler_params=_params(1),
    )(proj, v_norm_g, w_spatial, b_cols, gamma, after)


def _gmlp_bwd(proj, don, v_norm_g, w_spatial, b_cols, gamma, gm, after):
    s = proj.shape[0]
    ng = gm // GRP

    def body(p_ref, don_ref, gv_ref, w_ref, b_ref, gam_ref, after_ref, dp_ref, dw_ref, db_ref, dgv_ref, dgam_ref):
        @pl.when(pl.program_id(0) == 0)
        def _():
            dw_ref[...] = jnp.zeros_like(dw_ref)
            db_ref[...] = jnp.zeros_like(db_ref)
            dgv_ref[...] = jnp.zeros_like(dgv_ref)
            dgam_ref[...] = jnp.zeros_like(dgam_ref)

        lane = lax.broadcasted_iota(jnp.int32, (GRP, GRP), 1)
        row = lax.broadcasted_iota(jnp.int32, (GRP, GRP), 0)
        for g in range(ng):
            lo = g * GRP
            up, vp = p_ref[:, lo:lo + GRP], p_ref[:, gm + lo:gm + lo + GRP]
            gv, gam = gv_ref[:, lo:lo + GRP], gam_ref[:, lo:lo + GRP]
            wt = _tril_bf16(w_ref[g])
            u, rstd_v, yv, vn, mixed, o = _gm_group_fwd(up, vp, gv, wt, b_ref[:, g:g + 1])
            rstd_o = lax.rsqrt(jnp.mean(o * o, axis=-1, keepdims=True) + EPS)
            ohat = o * rstd_o
            dn = don_ref[:, lo:lo + GRP]
            dgam_ref[:, lo:lo + GRP] += jnp.sum(dn * ohat, axis=0, keepdims=True)
            dohat = dn * gam
            do = rstd_o * (dohat - ohat * jnp.mean(dohat * ohat, axis=-1, keepdims=True))
            du = do * mixed
            dmixed = do * u
            dmb = dmixed.astype(BF16)
            db_ref[...] += jnp.where(lane == g, jnp.sum(dmixed, axis=-1, keepdims=True), 0.0)
            dw_ref[g] += jnp.where(lane <= row, _dot(dmb, vn, NT), 0.0)
            dvn = _dot(wt, dmb, TN)
            dgv_ref[:, lo:lo + GRP] += jnp.sum(dvn * yv, axis=0, keepdims=True)
            dyv = dvn * gv
            dva = rstd_v * (dyv - jnp.mean(dyv, axis=-1, keepdims=True)
                            - yv * jnp.mean(dyv * yv, axis=-1, keepdims=True))
            dp_ref[:, lo:lo + GRP] = (du * _gelu_grad(up)).astype(BF16)
            dp_ref[:, gm + lo:gm + lo + GRP] = (dva * _gelu_grad(vp)).astype(BF16)

    const2 = lambda n: (0, 0)
    return pl.pallas_call(
        body, name="gmlp_bwd", grid=(s // GRP,),
        out_shape=(jax.ShapeDtypeStruct((s, 2 * gm), BF16), jax.ShapeDtypeStruct((ng, GRP, GRP), F32),
                   jax.ShapeDtypeStruct((GRP, GRP), F32), jax.ShapeDtypeStruct((1, gm), F32),
                   jax.ShapeDtypeStruct((1, gm), F32)),
        in_specs=[pl.BlockSpec((GRP, 2 * gm), lambda n: (n, 0)), pl.BlockSpec((GRP, gm), lambda n: (n, 0)),
                  _vec_spec(gm), pl.BlockSpec((ng, GRP, GRP), lambda n: (0, 0, 0)),
                  pl.BlockSpec((GRP, GRP), const2), _vec_spec(gm), _ANY],
        out_specs=(pl.BlockSpec((GRP, 2 * gm), lambda n: (n, 0)), pl.BlockSpec((ng, GRP, GRP), lambda n: (0, 0, 0)),
                   pl.BlockSpec((GRP, GRP), const2), _vec_spec(gm), _vec_spec(gm)),
        compiler_params=_params(1),
    )(proj, don, v_norm_g, w_spatial, b_cols, gamma, after)


BLK = 256


def _log_sigmoid(z):
    return jnp.minimum(z, 0.0) - jnp.log1p(jnp.exp(-jnp.abs(z)))


def _split_dot(x, tri, passes):
    n = x.shape[0]
    parts, rest = [], x
    for _ in range(passes):
        hi = rest.astype(BF16)
        parts.append(hi)
        rest = rest - hi.astype(F32)
    res = _dot(jnp.concatenate(parts, axis=0), tri, NN)
    out = res[0:n]
    for k in range(1, passes):
        out = out + res[k * n:(k + 1) * n]
    return out


HEADS_PER_STEP = 2


def _sb_fwd(proj, gamma, gm, sb, after):
    s = proj.shape[0]
    hp = HEADS_PER_STEP
    w = hp * GRP
    nhp, nq = sb // w, s // BLK
    qc, kc, vc, gc = 2 * gm // w, (2 * gm + sb) // w, (2 * gm + 2 * sb) // w, gm // w
    scale = GRP ** -0.5

    def body(q_ref, k_ref, v_ref, gam_ref, after_ref, on_ref, o_ref, mt_ref, kb, vb):
        i = pl.program_id(1)

        @pl.when(i == 0)
        def _():
            kb[...] = k_ref[...].astype(BF16)
            vb[...] = v_ref[...].astype(BF16)

        qb = q_ref[...].astype(BF16)
        row = lax.broadcasted_iota(jnp.int32, (BLK, BLK), 0)
        col = lax.broadcasted_iota(jnp.int32, (BLK, BLK), 1)
        later = (row > col).astype(BF16)

        def block(j, carry, diag):
            off = pl.multiple_of(j * BLK, BLK)
            kj, vj = kb[pl.ds(off, BLK), :], vb[pl.ds(off, BLK), :]
            out = []
            for h in range(hp):
                tail, acc = carry[h]
                sl = slice(h * GRP, (h + 1) * GRP)
                z = _dot(qb[:, sl], kj[:, sl], NT) * scale
                lb = _log_sigmoid(z)
                l1m = lb - z
                if diag:
                    l1m = jnp.where(col < row, l1m, 0.0)
                a = jnp.exp(lb + _split_dot(l1m, later, 2) + tail)
                if diag:
                    a = jnp.where(col < row, a, 0.0)
                out.append((tail + jnp.sum(l1m, axis=-1, keepdims=True), acc + _dot(a.astype(BF16), vj[:, sl], NN)))
            return tuple(out)

        init = tuple((jnp.zeros((BLK, 1), F32), jnp.zeros((BLK, GRP), F32)) for _ in range(hp))
        carry = block(i, init, True)
        carry = lax.fori_loop(0, i, lambda jj, c: block(i - 1 - jj, c, False), carry)
        for h in range(hp):
            tail, acc = carry[h]
            sl = slice(h * GRP, (h + 1) * GRP)
            rstd = lax.rsqrt(jnp.mean(acc * acc, axis=-1, keepdims=True) + EPS)
            on_ref[:, sl] = (acc * rstd * gam_ref[:, sl]).astype(BF16)
            o_ref[:, sl] = acc
            mt_ref[h] = tail

    return pl.pallas_call(
        body, name="sb_fwd", grid=(nhp, nq),
        out_shape=(jax.ShapeDtypeStruct((s, sb), BF16), jax.ShapeDtypeStruct((s, sb), F32),
                   jax.ShapeDtypeStruct((sb // GRP, s, 1), F32)),
        in_specs=[pl.BlockSpec((BLK, w), lambda h, i: (i, qc + h)), pl.BlockSpec((s, w), lambda h, i: (0, kc + h)),
                  pl.BlockSpec((s, w), lambda h, i: (0, vc + h)), pl.BlockSpec((1, w), lambda h, i: (0, gc + h)), _ANY],
        out_specs=(pl.BlockSpec((BLK, w), lambda h, i: (i, h)), pl.BlockSpec((BLK, w), lambda h, i: (i, h)),
                   pl.BlockSpec((hp, BLK, 1), lambda h, i: (h, i, 0))),
        scratch_shapes=[pltpu.VMEM((s, w), BF16), pltpu.VMEM((s, w), BF16)],
        compiler_params=_params(2),
    )(proj, proj, proj, gamma, after)


def _sb_bwd(proj, o_raw, mtot, don, gamma, gm, sb):
    s = proj.shape[0]
    hp = HEADS_PER_STEP
    w = hp * GRP
    nhp, nq = sb // w, s // BLK
    qc, kc, vc, gc = 2 * gm // w, (2 * gm + sb) // w, (2 * gm + 2 * sb) // w, gm // w
    scale = GRP ** -0.5

    def body(q_ref, k_ref, v_ref, o_ref, mt_ref, don_ref, gam_ref, dq_ref, dk_ref, dv_ref, dgam_ref,
             kb, vb, dk_acc, dv_acc):
        i = pl.program_id(1)

        @pl.when(i == 0)
        def _():
            kb[...] = k_ref[...].astype(BF16)
            vb[...] = v_ref[...].astype(BF16)
            dk_acc[...] = jnp.zeros_like(dk_acc)
            dv_acc[...] = jnp.zeros_like(dv_acc)
            dgam_ref[...] = jnp.zeros_like(dgam_ref)

        dobs = []
        for h in range(hp):
            sl = slice(h * GRP, (h + 1) * GRP)
            o, dn = o_ref[:, sl], don_ref[:, sl]
            rstd = lax.rsqrt(jnp.mean(o * o, axis=-1, keepdims=True) + EPS)
            ohat = o * rstd
            dgam_ref[:, sl] += jnp.sum(dn * ohat, axis=0, keepdims=True)
            dohat = dn * gam_ref[:, sl]
            dobs.append((rstd * (dohat - ohat * jnp.mean(dohat * ohat, axis=-1, keepdims=True))).astype(BF16))

        qb = q_ref[...].astype(BF16)
        row = lax.broadcasted_iota(jnp.int32, (BLK, BLK), 0)
        col = lax.broadcasted_iota(jnp.int32, (BLK, BLK), 1)
        upto = (row <= col).astype(BF16)
        before = (row < col).astype(BF16)

        def block(j, carry, diag):
            off = pl.multiple_of(j * BLK, BLK)
            kj, vj = kb[pl.ds(off, BLK), :], vb[pl.ds(off, BLK), :]
            out = []
            for h in range(hp):
                m_pre, e_pre, dq = carry[h]
                sl = slice(h * GRP, (h + 1) * GRP)
                qh, kh, dob = qb[:, sl], kj[:, sl], dobs[h]
                z = _dot(qh, kh, NT) * scale
                lb = _log_sigmoid(z)
                l1m = lb - z
                if diag:
                    l1m = jnp.where(col < row, l1m, 0.0)
                a = jnp.exp(lb + (mt_ref[h] - m_pre) - _split_dot(l1m, upto, 2))
                if diag:
                    a = jnp.where(col < row, a, 0.0)
                de = a * _dot(dob, vj[:, sl], NT)
                dv_acc[pl.ds(off, BLK), sl] += _dot(a.astype(BF16), dob, TN)
                dl1m = e_pre + _split_dot(de, before, 2)
                if diag:
                    dl1m = jnp.where(col < row, dl1m, 0.0)
                sig = jnp.exp(lb)
                dzb = ((de * (1.0 - sig) - dl1m * sig) * scale).astype(BF16)
                dk_acc[pl.ds(off, BLK), sl] += _dot(dzb, qh, TN)
                out.append((m_pre + jnp.sum(l1m, axis=-1, keepdims=True), e_pre + jnp.sum(de, axis=-1, keepdims=True),
                            dq + _dot(dzb, kh, NN)))
            return tuple(out)

        zero = jnp.zeros((BLK, 1), F32)
        carry = tuple((zero, zero, jnp.zeros((BLK, GRP), F32)) for _ in range(hp))
        carry = lax.fori_loop(0, i, lambda j, c: block(j, c, False), carry)
        carry = block(i, carry, True)
        for h in range(hp):
            dq_ref[:, h * GRP:(h + 1) * GRP] = carry[h][2].astype(BF16)

        @pl.when(i == nq - 1)
        def _():
            dk_ref[...] = dk_acc[...].astype(BF16)
            dv_ref[...] = dv_acc[...].astype(BF16)

    blk_q = lambda h, i: (i, h)
    whole = lambda h, i: (0, h)
    return pl.pallas_call(
        body, name="sb_bwd", grid=(nhp, nq),
        out_shape=(jax.ShapeDtypeStruct((s, sb), BF16),) * 3 + (jax.ShapeDtypeStruct((1, sb), F32),),
        in_specs=[pl.BlockSpec((BLK, w), lambda h, i: (i, qc + h)), pl.BlockSpec((s, w), lambda h, i: (0, kc + h)),
                  pl.BlockSpec((s, w), lambda h, i: (0, vc + h)), pl.BlockSpec((BLK, w), blk_q),
                  pl.BlockSpec((hp, BLK, 1), lambda h, i: (h, i, 0)),
                  pl.BlockSpec((BLK, w), lambda h, i: (i, gc + h)),
                  pl.BlockSpec((1, w), lambda h, i: (0, gc + h))],
        out_specs=(pl.BlockSpec((BLK, w), blk_q), pl.BlockSpec((s, w), whole), pl.BlockSpec((s, w), whole),
                   pl.BlockSpec((1, w), whole)),
        scratch_shapes=[pltpu.VMEM((s, w), BF16), pltpu.VMEM((s, w), BF16),
                        pltpu.VMEM((s, w), F32), pltpu.VMEM((s, w), F32)],
        compiler_params=_params(2),
    )(proj, proj, proj, o_raw, mtot, don, gamma)


def _sb_fwd_one_head(proj, gamma, gm, sb):
    s = proj.shape[0]
    nh, nq = sb // GRP, s // BLK
    qc, kc, vc = 2 * gm // GRP, (2 * gm + sb) // GRP, (2 * gm + 2 * sb) // GRP
    scale = GRP ** -0.5

    def body(q_ref, k_ref, v_ref, gam_ref, on_ref, o_ref, mt_ref, kb, vb):
        i = pl.program_id(1)

        @pl.when(i == 0)
        def _():
            kb[...] = k_ref[...].astype(BF16)
            vb[...] = v_ref[...].astype(BF16)

        qb = q_ref[...].astype(BF16)
        row = lax.broadcasted_iota(jnp.int32, (BLK, BLK), 0)
        col = lax.broadcasted_iota(jnp.int32, (BLK, BLK), 1)
        later = (row > col).astype(BF16)

        def step(jj, carry):
            tail, acc = carry
            j = i - jj
            off = pl.multiple_of(j * BLK, BLK)
            z = _dot(qb, kb[pl.ds(off, BLK), :], NT) * scale
            lb = _log_sigmoid(z)
            mask = col + j * BLK < row + i * BLK
            l1m = jnp.where(mask, lb - z, 0.0)
            a = jnp.where(mask, jnp.exp(lb + _split_dot(l1m, later, 3) + tail), 0.0)
            acc = acc + _dot(a.astype(BF16), vb[pl.ds(off, BLK), :], NN)
            return tail + jnp.sum(l1m, axis=-1, keepdims=True), acc

        tail, acc = lax.fori_loop(0, i + 1, step, (jnp.zeros((BLK, 1), F32), jnp.zeros((BLK, GRP), F32)))
        rstd = lax.rsqrt(jnp.mean(acc * acc, axis=-1, keepdims=True) + EPS)
        on_ref[...] = (acc * rstd * gam_ref[...]).astype(BF16)
        o_ref[...] = acc
        mt_ref[...] = tail

    return pl.pallas_call(
        body, name="sb_fwd", grid=(nh, nq),
        out_shape=(jax.ShapeDtypeStruct((s, sb), BF16), jax.ShapeDtypeStruct((s, sb), F32),
                   jax.ShapeDtypeStruct((nh, s, 1), F32)),
        in_specs=[pl.BlockSpec((BLK, GRP), lambda h, i: (i, qc + h)), pl.BlockSpec((s, GRP), lambda h, i: (0, kc + h)),
                  pl.BlockSpec((s, GRP), lambda h, i: (0, vc + h)),
                  pl.BlockSpec((1, GRP), lambda h, i: (0, gm // GRP + h))],
        out_specs=(pl.BlockSpec((BLK, GRP), lambda h, i: (i, h)), pl.BlockSpec((BLK, GRP), lambda h, i: (i, h)),
                   pl.BlockSpec((None, BLK, 1), lambda h, i: (h, i, 0))),
        scratch_shapes=[pltpu.VMEM((s, GRP), BF16), pltpu.VMEM((s, GRP), BF16)],
        compiler_params=_params(2),
    )(proj, proj, proj, gamma)


def _sb_bwd_one_head(proj, o_raw, mtot, don, gamma, gm, sb):
    s = proj.shape[0]
    nh, nq = sb // GRP, s // BLK
    qc, kc, vc = 2 * gm // GRP, (2 * gm + sb) // GRP, (2 * gm + 2 * sb) // GRP
    scale = GRP ** -0.5

    def body(q_ref, k_ref, v_ref, o_ref, mt_ref, don_ref, gam_ref, dq_ref, dk_ref, dv_ref, dgam_ref,
             kb, vb, dk_acc, dv_acc):
        i = pl.program_id(1)

        @pl.when(i == 0)
        def _():
            kb[...] = k_ref[...].astype(BF16)
            vb[...] = v_ref[...].astype(BF16)
            dk_acc[...] = jnp.zeros_like(dk_acc)
            dv_acc[...] = jnp.zeros_like(dv_acc)
            dgam_ref[...] = jnp.zeros_like(dgam_ref)

        o, dn, gam = o_ref[...], don_ref[...], gam_ref[...]
        rstd = lax.rsqrt(jnp.mean(o * o, axis=-1, keepdims=True) + EPS)
        ohat = o * rstd
        dgam_ref[...] += jnp.sum(dn * ohat, axis=0, keepdims=True)
        dohat = dn * gam
        dob = (rstd * (dohat - ohat * jnp.mean(dohat * ohat, axis=-1, keepdims=True))).astype(BF16)

        qb = q_ref[...].astype(BF16)
        mt = mt_ref[...]
        row = lax.broadcasted_iota(jnp.int32, (BLK, BLK), 0)
        col = lax.broadcasted_iota(jnp.int32, (BLK, BLK), 1)
        upto = (row <= col).astype(BF16)
        before = (row < col).astype(BF16)

        def step(j, carry):
            m_pre, e_pre, dq = carry
            off = pl.multiple_of(j * BLK, BLK)
            kj, vj = kb[pl.ds(off, BLK), :], vb[pl.ds(off, BLK), :]
            z = _dot(qb, kj, NT) * scale
            lb = _log_sigmoid(z)
            mask = col + j * BLK < row + i * BLK
            l1m = jnp.where(mask, lb - z, 0.0)
            tail = mt - m_pre - _split_dot(l1m, upto, 3)
            a = jnp.where(mask, jnp.exp(lb + tail), 0.0)
            de = a * _dot(dob, vj, NT)
            dv_acc[pl.ds(off, BLK), :] += _dot(a.astype(BF16), dob, TN)
            dl1m = e_pre + _split_dot(de, before, 2)
            sig = jnp.exp(lb)
            dz = (de * (1.0 - sig) - jnp.where(mask, dl1m * sig, 0.0)) * scale
            dzb = dz.astype(BF16)
            dk_acc[pl.ds(off, BLK), :] += _dot(dzb, qb, TN)
            return (m_pre + jnp.sum(l1m, axis=-1, keepdims=True), e_pre + jnp.sum(de, axis=-1, keepdims=True),
                    dq + _dot(dzb, kj, NN))

        zero = jnp.zeros((BLK, 1), F32)
        _, _, dq = lax.fori_loop(0, i + 1, step, (zero, zero, jnp.zeros((BLK, GRP), F32)))
        dq_ref[...] = dq.astype(BF16)

        @pl.when(i == nq - 1)
        def _():
            dk_ref[...] = dk_acc[...].astype(BF16)
            dv_ref[...] = dv_acc[...].astype(BF16)

    blk_q = lambda h, i: (i, h)
    whole = lambda h, i: (0, h)
    return pl.pallas_call(
        body, name="sb_bwd", grid=(nh, nq),
        out_shape=(jax.ShapeDtypeStruct((s, sb), BF16),) * 3 + (jax.ShapeDtypeStruct((1, sb), F32),),
        in_specs=[pl.BlockSpec((BLK, GRP), lambda h, i: (i, qc + h)), pl.BlockSpec((s, GRP), lambda h, i: (0, kc + h)),
                  pl.BlockSpec((s, GRP), lambda h, i: (0, vc + h)), pl.BlockSpec((BLK, GRP), blk_q),
                  pl.BlockSpec((None, BLK, 1), lambda h, i: (h, i, 0)),
                  pl.BlockSpec((BLK, GRP), lambda h, i: (i, gm // GRP + h)),
                  pl.BlockSpec((1, GRP), lambda h, i: (0, gm // GRP + h))],
        out_specs=(pl.BlockSpec((BLK, GRP), blk_q), pl.BlockSpec((s, GRP), whole), pl.BlockSpec((s, GRP), whole),
                   pl.BlockSpec((1, GRP), whole)),
        scratch_shapes=[pltpu.VMEM((s, GRP), BF16), pltpu.VMEM((s, GRP), BF16),
                        pltpu.VMEM((s, GRP), F32), pltpu.VMEM((s, GRP), F32)],
        compiler_params=_params(2),
    )(proj, proj, proj, o_raw, mtot, don, gamma)


def _ffn_fwd(h2, wg, wu, wd):
    s, d = h2.shape
    nb, _, fb = wg.shape
    tm = _tile(s, 512, 16)

    def body(h_ref, wg_ref, wu_ref, wd_ref, f_ref, g_ref, u_ref):
        j = pl.program_id(1)
        hv = h_ref[...]
        g = _dot(hv, wg_ref[...], NN)
        u = _dot(hv, wu_ref[...], NN)
        g_ref[...] = g.astype(BF16)
        u_ref[...] = u.astype(BF16)
        part = _dot((g * jax.nn.sigmoid(g) * u).astype(BF16), wd_ref[...], NN)

        @pl.when(j == 0)
        def _():
            f_ref[...] = part

        @pl.when(j > 0)
        def _():
            f_ref[...] += part

    rows = pl.BlockSpec((tm, d), lambda i, j: (i, 0))
    wcol = pl.BlockSpec((None, d, fb), lambda i, j: (j, 0, 0))
    hid = pl.BlockSpec((None, tm, fb), lambda i, j: (j, i, 0))
    return pl.pallas_call(
        body, name="ffn_fwd", grid=(s // tm, nb),
        out_shape=(jax.ShapeDtypeStruct((s, d), F32), jax.ShapeDtypeStruct((nb, s, fb), BF16),
                   jax.ShapeDtypeStruct((nb, s, fb), BF16)),
        in_specs=[rows, wcol, wcol, pl.BlockSpec((None, fb, d), lambda i, j: (j, 0, 0))],
        out_specs=(rows, hid, hid), compiler_params=_params(2),
    )(h2, wg, wu, wd)


def _ffn_bwd(df, g_pre, u_pre, wg, wu, wd):
    s, d = df.shape
    nb, _, fb = wg.shape
    tm = _tile(s, 512, 16)

    def body(df_ref, g_ref, u_ref, wg_ref, wu_ref, wd_ref, dh_ref, a_ref, dg_ref, du_ref):
        j = pl.program_id(1)
        g, u = g_ref[...].astype(F32), u_ref[...].astype(F32)
        da = _dot(df_ref[...], wd_ref[...], NT)
        sg = jax.nn.sigmoid(g)
        silu = g * sg
        a_ref[...] = (silu * u).astype(BF16)
        dg = (da * u * (sg * (1.0 + g * (1.0 - sg)))).astype(BF16)
        du = (da * silu).astype(BF16)
        dg_ref[...] = dg
        du_ref[...] = du
        part = _dot(dg, wg_ref[...], NT) + _dot(du, wu_ref[...], NT)

        @pl.when(j == 0)
        def _():
            dh_ref[...] = part

        @pl.when(j > 0)
        def _():
            dh_ref[...] += part

    rows = pl.BlockSpec((tm, d), lambda i, j: (i, 0))
    wcol = pl.BlockSpec((None, d, fb), lambda i, j: (j, 0, 0))
    hid = pl.BlockSpec((None, tm, fb), lambda i, j: (j, i, 0))
    hid_sds = jax.ShapeDtypeStruct((nb, s, fb), BF16)
    return pl.pallas_call(
        body, name="ffn_bwd", grid=(s // tm, nb),
        out_shape=(jax.ShapeDtypeStruct((s, d), F32), hid_sds, hid_sds, hid_sds),
        in_specs=[rows, hid, hid, wcol, wcol, pl.BlockSpec((None, fb, d), lambda i, j: (j, 0, 0))],
        out_specs=(rows, hid, hid, hid), compiler_params=_params(2),
    )(df, g_pre, u_pre, wg, wu, wd)


def _reduce_adamw(parts, w, m, v, name, after):
    npart, r, c = parts.shape
    tr = _tile(r, max(16, 262144 // c), 16)
    c1, c2 = 1.0 - ADAM_B1 ** ADAM_STEP, 1.0 - ADAM_B2 ** ADAM_STEP

    def body(p_ref, w_ref, m_ref, v_ref, after_ref, g_ref, d_ref, nm_ref, nv_ref):
        g = p_ref[0].astype(F32)
        for k in range(1, npart):
            g = g + p_ref[k].astype(F32)
        nm = ADAM_B1 * m_ref[...] + (1.0 - ADAM_B1) * g
        nv = ADAM_B2 * v_ref[...] + (1.0 - ADAM_B2) * (g * g)
        g_ref[...] = g
        nm_ref[...] = nm
        nv_ref[...] = nv
        d_ref[...] = -ADAM_LR * ((nm / c1) / (jnp.sqrt(nv / c2) + ADAM_EPS) + ADAM_WD * w_ref[...])

    blk = pl.BlockSpec((tr, c), lambda i: (i, 0))
    sds = jax.ShapeDtypeStruct((r, c), F32)
    return pl.pallas_call(
        body, name=name, grid=(r // tr,), out_shape=(sds,) * 4,
        in_specs=[pl.BlockSpec((npart, tr, c), lambda i: (0, i, 0)), blk, blk, blk, _ANY], out_specs=(blk,) * 4,
        compiler_params=_params(1),
    )(parts, w, m, v, after)


def _pack(vecs):
    rows = jnp.concatenate([a.reshape(-1, GRP) for a in vecs], axis=0)
    pad = -rows.shape[0] % 64
    return jnp.pad(rows, ((0, pad), (0, 0)))


def _unpack(rows, shapes):
    out, at = [], 0
    for shp in shapes:
        n = 1
        for k in shp:
            n *= k
        out.append(rows[at:at + n // GRP].reshape(shp))
        at += n // GRP
    return out


def kernel(x, c, w_ada, b_ada, norm1_g, w_in, v_norm_g, w_spatial, b_spatial, out_norm_g, w_out, norm2_g, w_gate, w_up, w_down, final_g, loss_target, m_w_ada, m_b_ada, m_norm1_g, m_w_in, m_v_norm_g, m_w_spatial, m_b_spatial, m_out_norm_g, m_w_out, m_norm2_g, m_w_gate, m_w_up, m_w_down, m_final_g, v_w_ada, v_b_ada, v_norm1_g, v_w_in, v_v_norm_g, v_w_spatial, v_b_spatial, v_out_norm_g, v_w_out, v_norm2_g, v_w_gate, v_w_up, v_w_down, v_final_g):
    s, d = x.shape[1], x.shape[2]
    gm = v_norm_g.shape[1]
    sb = d - gm
    n_in, ffb, ob = w_in.shape[2], w_gate.shape[2], w_out.shape[1]
    xs, tgt = x[0], loss_target[0]
    me = 4 * lax.axis_index("x") + 2 * lax.axis_index("y") + lax.axis_index("c")

    c_all = _exchange(c, False, "gather_c")[:, 0, :]
    mod_cols = _ada_fwd(c_all, w_ada[0])
    mod = _exchange(mod_cols[:, None, :], True, "scatter_mod").reshape(1, 6 * d) + b_ada
    shift1, scale1, gate1, shift2, scale2, gate2 = [mod[:, k * d:(k + 1) * d] for k in range(6)]

    tok, gathers = mod, {}
    for nm, w in (("w_in", w_in), ("w_out", w_out), ("w_gate", w_gate), ("w_up", w_up), ("w_down", w_down)):
        gathers[nm], tok = _gather_start(w[0].astype(BF16), tok, "gather_start_" + nm)
    shift1 = shift1 + tok[0:1, 0:1]

    tm = _tile(s, 512, 16)
    nt = s // tm
    rows_d = pl.BlockSpec((tm, d), lambda i, j: (i, 0))
    h1 = _prenorm_fwd(xs, norm1_g, scale1, shift1, "prenorm1")
    w_in_all = _gather_wait(_gather_forward(gathers["w_in"], h1, "gather_fwd_w_in")[0], h1, "gather_wait_w_in")
    proj = _mm(h1, w_in_all, rows_d, pl.BlockSpec((None, d, n_in), lambda i, j: (j, 0, 0)),
               jax.ShapeDtypeStruct((s, NDEV * n_in), F32), pl.BlockSpec((tm, n_in), lambda i, j: (i, j)),
               (nt, NDEV), NN, False, "proj")
    fwd_w_out, tok = _gather_forward(gathers["w_out"], proj, "gather_fwd_w_out")
    b_cols = jnp.pad(b_spatial[0].T, ((0, 0), (0, GRP - b_spatial.shape[1])))
    on_sb, o_sb, mtot = _sb_fwd(proj, out_norm_g, gm, sb, tok)
    fwd_w_gate, tok = _gather_forward(gathers["w_gate"], on_sb, "gather_fwd_w_gate")
    on_gm = _gmlp_fwd(proj, v_norm_g, w_spatial[0], b_cols, out_norm_g, gm, tok)
    fwd_w_up, tok = _gather_forward(gathers["w_up"], on_gm, "gather_fwd_w_up")
    o_n = jnp.concatenate([on_gm, on_sb], axis=1)
    w_out_all = _gather_wait(fwd_w_out, tok, "gather_wait_w_out").reshape(d, d)
    rows_1 = pl.BlockSpec((tm, d), lambda i: (i, 0))
    whole_1 = pl.BlockSpec((d, d), lambda i: (0, 0))
    p_out = _mm(o_n, w_out_all, rows_1, whole_1, jax.ShapeDtypeStruct((s, d), F32), rows_1, (nt,), NN, False,
                "out_proj")
    x1 = _residual(xs, gate1, p_out, "residual1")
    h2 = _prenorm_fwd(x1, norm2_g, scale2, shift2, "prenorm2")
    fwd_w_down, tok = _gather_forward(gathers["w_down"], h2, "gather_fwd_w_down")
    w_gate_all = _gather_wait(fwd_w_gate, tok, "gather_wait_w_gate")
    w_up_all = _gather_wait(fwd_w_up, h2, "gather_wait_w_up")
    w_down_all = _gather_wait(fwd_w_down, h2, "gather_wait_w_down")
    f_out, g_pre, u_pre = _ffn_fwd(h2, w_gate_all, w_up_all, w_down_all)
    x2 = _residual(x1, gate2, f_out, "residual2")

    dx2, df, st_f = _final_loss(x2, final_g.reshape(1, d), tgt, gate2)
    dh2, act, dg, du = _ffn_bwd(df, g_pre, u_pre, w_gate_all, w_up_all, w_down_all)
    hid = pl.BlockSpec((None, s, ffb), lambda j: (j, 0, 0))
    all_d = pl.BlockSpec((s, d), lambda j: (0, 0))
    col_sds = jax.ShapeDtypeStruct((NDEV, d, ffb), BF16)
    col_out = pl.BlockSpec((None, d, ffb), lambda j: (j, 0, 0))
    scatters = {}
    gw_gate = _mm(h2, dg, all_d, hid, col_sds, col_out, (NDEV,), TN, False, "grad_w_gate")
    scatters["w_gate"], tok = _exchange_start(gw_gate, True, tok, "scatter_start_w_gate")
    gw_up = _mm(h2, du, all_d, hid, col_sds, col_out, (NDEV,), TN, False, "grad_w_up", tok)
    scatters["w_up"], tok = _exchange_start(gw_up, True, tok, "scatter_start_w_up")
    gw_down = _mm(act, df, hid, all_d, jax.ShapeDtypeStruct((NDEV, ffb, d), BF16),
                  pl.BlockSpec((None, ffb, d), lambda j: (j, 0, 0)), (NDEV,), TN, False, "grad_w_down", tok)
    scatters["w_down"], tok = _exchange_start(gw_down, True, tok, "scatter_start_w_down")
    dx1, dp, st2 = _prenorm_bwd(x1, dh2, dx2, f_out, norm2_g, scale2, gate1, "prenorm2_bwd", tok)
    don = _mm(dp, w_out_all, rows_1, whole_1, jax.ShapeDtypeStruct((s, d), F32), rows_1, (nt,), NT, False,
              "out_proj_bwd")
    gw_out = _mm(o_n, dp, pl.BlockSpec((s, ob), lambda j: (0, j)), all_d,
                 jax.ShapeDtypeStruct((NDEV, ob, d), BF16), pl.BlockSpec((None, ob, d), lambda j: (j, 0, 0)),
                 (NDEV,), TN, False, "grad_w_out", don)
    scatters["w_out"], tok = _exchange_start(gw_out, True, tok, "scatter_start_w_out")
    dproj_gm, dw_sp, db_cols, dgv, dgam_gm = _gmlp_bwd(proj, don, v_norm_g, w_spatial[0], b_cols, out_norm_g, gm, tok)
    dq, dk, dv, dgam_sb = _sb_bwd(proj, o_sb, mtot, don, out_norm_g, gm, sb)
    dproj = jnp.concatenate([dproj_gm, dq, dk, dv], axis=1)
    dh1 = _mm(dproj, w_in_all, pl.BlockSpec((tm, n_in), lambda i, j: (i, j)),
              pl.BlockSpec((None, d, n_in), lambda i, j: (j, 0, 0)), jax.ShapeDtypeStruct((s, d), F32), rows_d,
              (nt, NDEV), NT, True, "in_proj_bwd")
    grad_x, _, st1 = _prenorm_bwd(xs, dh1, dx1, p_out, norm1_g, scale1, gate1, "prenorm1_bwd", tok)

    dmod = jnp.concatenate([st1[0], st1[1], st1[3], st2[0], st2[1], st2[3]])
    loss_row = jnp.pad((0.5 * jnp.sum(st_f[1]) / d).reshape(1, 1), ((0, 0), (0, GRP - 1)))
    small = [st1[2], dgv, dw_sp, db_cols[:, :b_spatial.shape[1]].T, jnp.concatenate([dgam_gm, dgam_sb], axis=1),
             st2[2], st_f[0], dmod, loss_row]
    zero_row = jnp.zeros((1, GRP), F32)
    small_w = [norm1_g, v_norm_g, w_spatial, b_spatial, out_norm_g, norm2_g, final_g, b_ada, zero_row]
    small_m = [m_norm1_g, m_v_norm_g, m_w_spatial, m_b_spatial, m_out_norm_g, m_norm2_g, m_final_g, m_b_ada, zero_row]
    small_v = [v_norm1_g, v_v_norm_g, v_w_spatial, v_b_spatial, v_out_norm_g, v_norm2_g, v_final_g, v_b_ada, zero_row]
    small_handles, tok = _exchange_start(_pack(small), False, tok, "gather_small_start")
    gw_in = _mm(h1, dproj, all_d, pl.BlockSpec((s, n_in), lambda j: (0, j)),
                jax.ShapeDtypeStruct((NDEV, d, n_in), BF16), pl.BlockSpec((None, d, n_in), lambda j: (j, 0, 0)),
                (NDEV,), TN, False, "grad_w_in", tok)
    scatters["w_in"], tok = _exchange_start(gw_in, True, tok, "scatter_start_w_in")
    small_all = _exchange_wait(small_handles, tok, "gather_small_wait")
    sm = _reduce_adamw(small_all, _pack(small_w), _pack(small_m), _pack(small_v), "adamw_small", tok)
    shapes = [a.shape for a in small_w]
    sm_g, sm_d, sm_m, sm_v = [_unpack(t, shapes) for t in sm]

    loss = sm_g[8][0, 0]
    at = sum(a.size for a in small_w[:7]) // GRP
    dmod_all = small_all[:, at:at + 6 * d // GRP, :].reshape(NDEV, 6 * d)
    n_ada = w_ada.shape[2]
    dmod_cols = lax.dynamic_slice(dmod_all, (0, me * n_ada), (NDEV, n_ada))
    g_ada = _ada_bwd(c_all, dmod_cols)
    big = {"w_ada": _reduce_adamw(g_ada[None], w_ada[0], m_w_ada[0], v_w_ada[0], "adamw_w_ada", sm[0])}

    prev = big["w_ada"][1]
    for nm, w, m, v in (("w_gate", w_gate, m_w_gate, v_w_gate), ("w_up", w_up, m_w_up, v_w_up),
                        ("w_down", w_down, m_w_down, v_w_down), ("w_out", w_out, m_w_out, v_w_out),
                        ("w_in", w_in, m_w_in, v_w_in)):
        parts = _exchange_wait(scatters[nm], prev, "scatter_wait_" + nm)
        big[nm] = _reduce_adamw(parts, w[0], m[0], v[0], "adamw_" + nm, tok)
        prev = big[nm][1]

    names = ["w_ada", "b_ada", "norm1_g", "w_in", "v_norm_g", "w_spatial", "b_spatial", "out_norm_g", "w_out",
             "norm2_g", "w_gate", "w_up", "w_down", "final_g"]
    small_at = {"norm1_g": 0, "v_norm_g": 1, "w_spatial": 2, "b_spatial": 3, "out_norm_g": 4, "norm2_g": 5,
                "final_g": 6, "b_ada": 7}
    outs = [[], [], [], []]
    for nm in names:
        for k in range(4):
            outs[k].append(big[nm][k][None] if nm in big else (sm_g, sm_d, sm_m, sm_v)[k][small_at[nm]])
    return (loss, grad_x[None], *outs[0], *outs[1], *outs[2], *outs[3])
```

```python
import functools

import jax
import jax.numpy as jnp
from jax import lax
from jax.experimental import pallas as pl
from jax.experimental.pallas import tpu as pltpu

F32, BF16 = jnp.float32, jnp.bfloat16
NDEV = 8
GRP = 128
EPS = 1e-6
VMEM_BYTES = 64 * 2 ** 20
VMEM_LIMIT = VMEM_BYTES - 8 * 2 ** 20
ADAM_LR, ADAM_B1, ADAM_B2, ADAM_EPS, ADAM_WD, ADAM_STEP = 0.001, 0.9, 0.999, 1e-08, 0.01, 10
MESH = pl.DeviceIdType.MESH
NN = (((1,), (0,)), ((), ()))
NT = (((1,), (1,)), ((), ()))
TN = (((0,), (0,)), ((), ()))


def _params(n_axes):
    return pltpu.CompilerParams(dimension_semantics=("arbitrary",) * n_axes, vmem_limit_bytes=VMEM_LIMIT)


def _tile(n, cap, mult):
    best = None
    for t in range(mult, min(n, cap) + 1, mult):
        if n % t == 0:
            best = t
    assert best is not None, (n, cap, mult)
    return best


def _dot(a, b, dims):
    return lax.dot_general(a, b, dims, preferred_element_type=F32)


def _exchange(src, per_peer, name):
    blk = src.shape[1:] if per_peer else src.shape

    def body(src_ref, out_ref, send_sems, recv_sems, local_sem):
        x, y, c = lax.axis_index("x"), lax.axis_index("y"), lax.axis_index("c")
        me = 4 * x + 2 * y + c
        local = pltpu.make_async_copy(src_ref.at[me] if per_peer else src_ref, out_ref.at[me], local_sem)
        local.start()
        sends, recvs = [], []
        for k in range(1, NDEV):
            px, py, pc = x ^ (k >> 2), y ^ ((k >> 1) & 1), c ^ (k & 1)
            p = 4 * px + 2 * py + pc
            mine = src_ref.at[p] if per_peer else src_ref
            sends.append(pltpu.make_async_remote_copy(
                src_ref=mine, dst_ref=out_ref.at[me], send_sem=send_sems.at[k - 1], recv_sem=recv_sems.at[k - 1],
                device_id=(px, py, pc), device_id_type=MESH))
            recvs.append(pltpu.make_async_remote_copy(
                src_ref=mine, dst_ref=out_ref.at[p], send_sem=send_sems.at[k - 1], recv_sem=recv_sems.at[k - 1],
                device_id=(px, py, pc), device_id_type=MESH))
        for cp in sends:
            cp.start()
        for cp in recvs:
            cp.wait_recv()
        for cp in sends:
            cp.wait_send()
        local.wait()

    return pl.pallas_call(
        body, name=name,
        out_shape=jax.ShapeDtypeStruct((NDEV,) + tuple(blk), src.dtype),
        in_specs=[pl.BlockSpec(memory_space=pl.ANY)],
        out_specs=pl.BlockSpec(memory_space=pl.ANY),
        scratch_shapes=[pltpu.SemaphoreType.DMA((NDEV - 1,)), pltpu.SemaphoreType.DMA((NDEV - 1,)),
                        pltpu.SemaphoreType.DMA],
    )(src)


_HBM = pl.BlockSpec(memory_space=pltpu.HBM)
_SEM = pl.BlockSpec(memory_space=pltpu.SEMAPHORE)
_ANY = pl.BlockSpec(memory_space=pl.ANY)
_EFFECT = pltpu.SideEffectType.DATAFLOW_SIDE_EFFECTING


def _peers():
    x, y, c = lax.axis_index("x"), lax.axis_index("y"), lax.axis_index("c")
    out = []
    for k in range(1, NDEV):
        px, py, pc = x ^ (k >> 2), y ^ ((k >> 1) & 1), c ^ (k & 1)
        out.append((k, (px, py, pc), 4 * px + 2 * py + pc))
    return 4 * x + 2 * y + c, out


def _exchange_start(src, per_peer, after, name):
    blk = src.shape[1:] if per_peer else src.shape
    me = 4 * lax.axis_index("x") + 2 * lax.axis_index("y") + lax.axis_index("c")
    own = lax.dynamic_index_in_dim(src, me, 0, keepdims=True) if per_peer else src[None]
    land = lax.dynamic_update_slice(lax.empty((NDEV,) + tuple(blk), src.dtype), own, (me,) + (0,) * len(blk))

    def body(src_ref, land_ref, after_ref, send_sems, recv_sems, src_thru, land_thru, token):
        my, peers = _peers()
        for k, coords, p in peers:
            pltpu.make_async_remote_copy(
                src_ref=src_ref.at[p] if per_peer else src_ref, dst_ref=land_ref.at[my],
                send_sem=send_sems.at[k - 1], recv_sem=recv_sems.at[k - 1], device_id=coords, device_id_type=MESH).start()
        token[...] = jnp.zeros_like(token)

    res = pl.pallas_call(
        body, name=name,
        out_shape=(pltpu.SemaphoreType.DMA((NDEV - 1,)), pltpu.SemaphoreType.DMA((NDEV - 1,)),
                   pltpu.HBM(src.shape, src.dtype), pltpu.HBM(land.shape, land.dtype), jax.ShapeDtypeStruct((8, GRP), F32)),
        in_specs=(_HBM, _HBM, _ANY), out_specs=(_SEM, _SEM, _HBM, _HBM, pl.BlockSpec(memory_space=pltpu.VMEM)),
        input_output_aliases={0: 2, 1: 3}, compiler_params=pltpu.CompilerParams(has_side_effects=_EFFECT),
    )(pltpu.with_memory_space_constraint(src, pltpu.HBM), pltpu.with_memory_space_constraint(land, pltpu.HBM), after)
    return (per_peer,) + tuple(res[:4]), res[4]


def _exchange_wait(handles, after, name):
    per_peer, send_sems, recv_sems, src_thru, land_thru = handles

    def body(src_ref, land_ref, send_sems, recv_sems, after_ref, src_dead, got_ref):
        _, peers = _peers()
        for k, coords, p in peers:
            cp = pltpu.make_async_remote_copy(
                src_ref=src_ref.at[p] if per_peer else src_ref, dst_ref=land_ref.at[p],
                send_sem=send_sems.at[k - 1], recv_sem=recv_sems.at[k - 1], device_id=coords, device_id_type=MESH)
            cp.wait_send()
            cp.wait_recv()

    return pl.pallas_call(
        body, name=name,
        out_shape=(pltpu.HBM(src_thru.shape, src_thru.dtype), pltpu.HBM(land_thru.shape, land_thru.dtype)),
        in_specs=(_HBM, _HBM, _SEM, _SEM, _ANY), out_specs=(_HBM, _HBM), input_output_aliases={0: 0, 1: 1},
        compiler_params=pltpu.CompilerParams(has_side_effects=_EFFECT),
    )(src_thru, land_thru, send_sems, recv_sems, after)[1]


def _chip_peers():
    x, y, c = lax.axis_index("x"), lax.axis_index("y"), lax.axis_index("c")
    chips = [(x, 1 - y), (1 - x, y), (1 - x, 1 - y)]
    return 4 * x + 2 * y + c, (x, y, 1 - c), [((px, py, c), 4 * px + 2 * py + c) for px, py in chips]


def _gather_start(src, after, name):
    me = 4 * lax.axis_index("x") + 2 * lax.axis_index("y") + lax.axis_index("c")
    land = lax.dynamic_update_slice(lax.empty((NDEV,) + src.shape, src.dtype), src[None], (me,) + (0,) * src.ndim)

    def body(src_ref, land_ref, after_ref, send_sems, recv_sems, src_thru, land_thru, token):
        my, sibling, chips = _chip_peers()
        for k, to in enumerate([sibling] + [coords for coords, _ in chips]):
            pltpu.make_async_remote_copy(src_ref=src_ref, dst_ref=land_ref.at[my], send_sem=send_sems.at[k],
                                         recv_sem=recv_sems.at[k], device_id=to, device_id_type=MESH).start()
        token[...] = jnp.zeros_like(token)

    res = pl.pallas_call(
        body, name=name,
        out_shape=(pltpu.SemaphoreType.DMA((4,)), pltpu.SemaphoreType.DMA((4,)), pltpu.HBM(src.shape, src.dtype),
                   pltpu.HBM(land.shape, land.dtype), jax.ShapeDtypeStruct((8, GRP), F32)),
        in_specs=(_HBM, _HBM, _ANY), out_specs=(_SEM, _SEM, _HBM, _HBM, pl.BlockSpec(memory_space=pltpu.VMEM)),
        input_output_aliases={0: 2, 1: 3}, compiler_params=pltpu.CompilerParams(has_side_effects=_EFFECT),
    )(pltpu.with_memory_space_constraint(src, pltpu.HBM), pltpu.with_memory_space_constraint(land, pltpu.HBM), after)
    return tuple(res[:4]), res[4]


def _gather_forward(handles, after, name):
    send_sems, recv_sems, src_thru, land_thru = handles

    def body(src_ref, land_ref, send_sems, recv_sems, after_ref, send2, recv2, land_out, token):
        my, sibling, chips = _chip_peers()
        for k, (to, p) in enumerate([(sibling, my ^ 1)] + chips):
            first = pltpu.make_async_remote_copy(src_ref=src_ref, dst_ref=land_ref.at[p], send_sem=send_sems.at[k],
                                                 recv_sem=recv_sems.at[k], device_id=to, device_id_type=MESH)
            first.wait_send()
            first.wait_recv()
        for k, (_, p) in enumerate(chips):
            pltpu.make_async_remote_copy(src_ref=land_ref.at[p], dst_ref=land_ref.at[p], send_sem=send2.at[k],
                                         recv_sem=recv2.at[k], device_id=sibling, device_id_type=MESH).start()
        token[...] = jnp.zeros_like(token)

    res = pl.pallas_call(
        body, name=name,
        out_shape=(pltpu.SemaphoreType.DMA((3,)), pltpu.SemaphoreType.DMA((3,)), pltpu.HBM(land_thru.shape, land_thru.dtype),
                   jax.ShapeDtypeStruct((8, GRP), F32)),
        in_specs=(_HBM, _HBM, _SEM, _SEM, _ANY), out_specs=(_SEM, _SEM, _HBM, pl.BlockSpec(memory_space=pltpu.VMEM)),
        input_output_aliases={1: 2}, compiler_params=pltpu.CompilerParams(has_side_effects=_EFFECT),
    )(src_thru, land_thru, send_sems, recv_sems, after)
    return tuple(res[:3]), res[3]


def _gather_wait(handles, after, name):
    send2, recv2, land_thru = handles

    def body(land_ref, send2, recv2, after_ref, got_ref):
        _, sibling, chips = _chip_peers()
        for k, (_, p) in enumerate(chips):
            cp = pltpu.make_async_remote_copy(src_ref=land_ref.at[p], dst_ref=land_ref.at[p ^ 1], send_sem=send2.at[k],
                                              recv_sem=recv2.at[k], device_id=sibling, device_id_type=MESH)
            cp.wait_send()
            cp.wait_recv()

    return pl.pallas_call(
        body, name=name, out_shape=pltpu.HBM(land_thru.shape, land_thru.dtype),
        in_specs=(_HBM, _SEM, _SEM, _ANY), out_specs=_HBM, input_output_aliases={0: 0},
        compiler_params=pltpu.CompilerParams(has_side_effects=_EFFECT),
    )(land_thru, send2, recv2, after)


NCHIP = NDEV // 2


def _scatter_start1(parts, after, name):
    n = len(parts)
    lands = [lax.empty((NCHIP,) + p.shape[1:], p.dtype) for p in parts]

    def body(*refs):
        part_refs, land_refs = refs[:n], refs[n:2 * n]
        send_sems, recv_sems, token = refs[2 * n + 1], refs[2 * n + 2], refs[-1]
        _, sibling, _ = _chip_peers()
        c = lax.axis_index("c")
        for a in range(n):
            for q in range(NCHIP):
                pltpu.make_async_remote_copy(
                    src_ref=part_refs[a].at[2 * q + 1 - c], dst_ref=land_refs[a].at[q], send_sem=send_sems.at[a * NCHIP + q],
                    recv_sem=recv_sems.at[a * NCHIP + q], device_id=sibling, device_id_type=MESH).start()
        token[...] = jnp.zeros_like(token)

    hbm = lambda arrs: tuple(pltpu.HBM(a.shape, a.dtype) for a in arrs)
    res = pl.pallas_call(
        body, name=name,
        out_shape=(pltpu.SemaphoreType.DMA((n * NCHIP,)), pltpu.SemaphoreType.DMA((n * NCHIP,))) + hbm(parts) + hbm(lands)
        + (jax.ShapeDtypeStruct((8, GRP), F32),),
        in_specs=(_HBM,) * (2 * n) + (_ANY,),
        out_specs=(_SEM, _SEM) + (_HBM,) * (2 * n) + (pl.BlockSpec(memory_space=pltpu.VMEM),),
        input_output_aliases={k: 2 + k for k in range(2 * n)}, compiler_params=pltpu.CompilerParams(has_side_effects=_EFFECT),
    )(*[pltpu.with_memory_space_constraint(a, pltpu.HBM) for a in list(parts) + lands], after)
    return (n,) + tuple(res[:-1]), res[-1]


def _scatter_wait1(handles, after, name):
    n, send_sems, recv_sems = handles[:3]
    thru = handles[3:]

    def body(*refs):
        part_refs, land_refs, send_sems, recv_sems = refs[:n], refs[n:2 * n], refs[2 * n], refs[2 * n + 1]
        _, sibling, _ = _chip_peers()
        for a in range(n):
            for q in range(NCHIP):
                cp = pltpu.make_async_remote_copy(
                    src_ref=part_refs[a].at[q], dst_ref=land_refs[a].at[q], send_sem=send_sems.at[a * NCHIP + q],
                    recv_sem=recv_sems.at[a * NCHIP + q], device_id=sibling, device_id_type=MESH)
                cp.wait_send()
                cp.wait_recv()

    res = pl.pallas_call(
        body, name=name, out_shape=tuple(pltpu.HBM(a.shape, a.dtype) for a in thru),
        in_specs=(_HBM,) * (2 * n) + (_SEM, _SEM, _ANY), out_specs=(_HBM,) * (2 * n),
        input_output_aliases={k: k for k in range(2 * n)}, compiler_params=pltpu.CompilerParams(has_side_effects=_EFFECT),
    )(*thru, send_sems, recv_sems, after)
    return res[:n], res[n:]


def _chip_sum(part, land, name):
    _, r, c = part.shape
    tr = _tile(r, max(16, 2 ** 21 // c), 16)

    def body(core_ref, p_ref, l_ref, o_ref):
        o_ref[...] = (p_ref[...].astype(F32) + l_ref[...].astype(F32)).astype(o_ref.dtype)

    blk = pl.BlockSpec((None, tr, c), lambda q, i, core: (q, i, 0))
    return pl.pallas_call(
        body, name=name, out_shape=jax.ShapeDtypeStruct(land.shape, land.dtype),
        grid_spec=pltpu.PrefetchScalarGridSpec(
            num_scalar_prefetch=1, grid=(NCHIP, r // tr),
            in_specs=[pl.BlockSpec((None, None, tr, c), lambda q, i, core: (q, core[0], i, 0)), blk], out_specs=blk),
        compiler_params=_params(2),
    )(lax.axis_index("c").astype(jnp.int32).reshape(1), part.reshape(NCHIP, 2, r, c), land)


def _scatter_start2(sums, after, name):
    n = len(sums)
    chip = 2 * lax.axis_index("x") + lax.axis_index("y")
    lands = [lax.dynamic_update_slice(lax.empty(s_.shape, s_.dtype), lax.dynamic_index_in_dim(s_, chip, 0, keepdims=True),
                                      (chip,) + (0,) * (s_.ndim - 1)) for s_ in sums]

    def body(*refs):
        sum_refs, land_refs = refs[:n], refs[n:2 * n]
        send_sems, recv_sems, token = refs[2 * n + 1], refs[2 * n + 2], refs[-1]
        my, _, chips = _chip_peers()
        for a in range(n):
            for k, (to, p) in enumerate(chips):
                pltpu.make_async_remote_copy(
                    src_ref=sum_refs[a].at[p // 2], dst_ref=land_refs[a].at[my // 2], send_sem=send_sems.at[a * 3 + k],
                    recv_sem=recv_sems.at[a * 3 + k], device_id=to, device_id_type=MESH).start()
        token[...] = jnp.zeros_like(token)

    hbm = lambda arrs: tuple(pltpu.HBM(a.shape, a.dtype) for a in arrs)
    res = pl.pallas_call(
        body, name=name,
        out_shape=(pltpu.SemaphoreType.DMA((n * 3,)), pltpu.SemaphoreType.DMA((n * 3,))) + hbm(sums) + hbm(lands)
        + (jax.ShapeDtypeStruct((8, GRP), F32),),
        in_specs=(_HBM,) * (2 * n) + (_ANY,),
        out_specs=(_SEM, _SEM) + (_HBM,) * (2 * n) + (pl.BlockSpec(memory_space=pltpu.VMEM),),
        input_output_aliases={k: 2 + k for k in range(2 * n)}, compiler_params=pltpu.CompilerParams(has_side_effects=_EFFECT),
    )(*[pltpu.with_memory_space_constraint(a, pltpu.HBM) for a in list(sums) + lands], after)
    return (n,) + tuple(res[:-1]), res[-1]


def _scatter_wait2(handles, after, name):
    n, send_sems, recv_sems = handles[:3]
    thru = handles[3:]

    def body(*refs):
        sum_refs, land_refs, send_sems, recv_sems = refs[:n], refs[n:2 * n], refs[2 * n], refs[2 * n + 1]
        _, _, chips = _chip_peers()
        for a in range(n):
            for k, (to, p) in enumerate(chips):
                cp = pltpu.make_async_remote_copy(
                    src_ref=sum_refs[a].at[p // 2], dst_ref=land_refs[a].at[p // 2], send_sem=send_sems.at[a * 3 + k],
                    recv_sem=recv_sems.at[a * 3 + k], device_id=to, device_id_type=MESH)
                cp.wait_send()
                cp.wait_recv()

    res = pl.pallas_call(
        body, name=name, out_shape=tuple(pltpu.HBM(a.shape, a.dtype) for a in thru),
        in_specs=(_HBM,) * (2 * n) + (_SEM, _SEM, _ANY), out_specs=(_HBM,) * (2 * n),
        input_output_aliases={k: k for k in range(2 * n)}, compiler_params=pltpu.CompilerParams(has_side_effects=_EFFECT),
    )(*thru, send_sems, recv_sems, after)
    return res[n:]


def _mm(a, b, a_spec, b_spec, out_sds, o_spec, grid, dims, acc, name, after=None):
    nk = grid[-1]
    extra = () if after is None else (after,)

    def body(a_ref, b_ref, *rest):
        o_ref, *scratch = rest[len(extra):]
        prod = _dot(a_ref[...], b_ref[...], dims)
        if not acc:
            o_ref[...] = prod.astype(o_ref.dtype)
            return
        acc_ref = scratch[0]
        k = pl.program_id(1)

        @pl.when(k == 0)
        def _():
            acc_ref[...] = prod

        @pl.when(k > 0)
        def _():
            acc_ref[...] += prod

        @pl.when(k == nk - 1)
        def _():
            o_ref[...] = acc_ref[...].astype(o_ref.dtype)

    blk = tuple(d for d in o_spec.block_shape if d is not None)
    return pl.pallas_call(
        body, name=name, grid=grid, out_shape=out_sds, in_specs=[a_spec, b_spec] + [_ANY] * len(extra), out_specs=o_spec,
        scratch_shapes=[pltpu.VMEM(blk, F32)] if acc else [],
        compiler_params=_params(len(grid)),
    )(a, b, *extra)


def _row_spec(tm, d):
    return pl.BlockSpec((tm, d), lambda i: (i, 0))


def _vec_spec(d):
    return pl.BlockSpec((1, d), lambda i: (0, 0))


def _prenorm_fwd(x, g, scale, shift, name):
    s, d = x.shape
    tm = _tile(s, 256, 16)

    def body(x_ref, g_ref, sc_ref, sh_ref, h_ref):
        xv = x_ref[...]
        rstd = lax.rsqrt(jnp.mean(xv * xv, axis=-1, keepdims=True) + EPS)
        h_ref[...] = ((xv * rstd * g_ref[...]) * (1.0 + sc_ref[...]) + sh_ref[...]).astype(BF16)

    return pl.pallas_call(
        body, name=name, grid=(s // tm,), out_shape=jax.ShapeDtypeStruct((s, d), BF16),
        in_specs=[_row_spec(tm, d), _vec_spec(d), _vec_spec(d), _vec_spec(d)], out_specs=_row_spec(tm, d),
        compiler_params=_params(1),
    )(x, g, scale, shift)


def _residual(x, gate, p, name):
    s, d = x.shape
    tm = _tile(s, 256, 8)

    def body(x_ref, g_ref, p_ref, o_ref):
        o_ref[...] = x_ref[...] + g_ref[...] * p_ref[...]

    return pl.pallas_call(
        body, name=name, grid=(s // tm,), out_shape=jax.ShapeDtypeStruct((s, d), F32),
        in_specs=[_row_spec(tm, d), _vec_spec(d), _row_spec(tm, d)], out_specs=_row_spec(tm, d),
        compiler_params=_params(1),
    )(x, gate, p)


def _final_loss(x2, final_g, target, gate2):
    s, d = x2.shape
    tm = _tile(s, 256, 16)

    def body(x_ref, g_ref, t_ref, gate_ref, dx_ref, df_ref, st_ref):
        @pl.when(pl.program_id(0) == 0)
        def _():
            st_ref[...] = jnp.zeros_like(st_ref)

        xv, gf = x_ref[...], g_ref[...]
        rstd = lax.rsqrt(jnp.mean(xv * xv, axis=-1, keepdims=True) + EPS)
        xhat = xv * rstd
        err = xhat * gf - t_ref[...]
        dy = err * (1.0 / d)
        gdy = dy * gf
        dx = rstd * (gdy - xhat * jnp.mean(gdy * xhat, axis=-1, keepdims=True))
        dx_ref[...] = dx
        df_ref[...] = (gate_ref[...] * dx).astype(BF16)
        st_ref[0:1, :] += jnp.sum(dy * xhat, axis=0, keepdims=True)
        st_ref[1:2, :] += jnp.sum(err * err, axis=0, keepdims=True)

    return pl.pallas_call(
        body, name="final_loss", grid=(s // tm,),
        out_shape=(jax.ShapeDtypeStruct((s, d), F32), jax.ShapeDtypeStruct((s, d), BF16),
                   jax.ShapeDtypeStruct((8, d), F32)),
        in_specs=[_row_spec(tm, d), _vec_spec(d), _row_spec(tm, d), _vec_spec(d)],
        out_specs=(_row_spec(tm, d), _row_spec(tm, d), pl.BlockSpec((8, d), lambda i: (0, 0))),
        compiler_params=_params(1),
    )(x2, final_g, target, gate2)


def _prenorm_bwd(xin, dh, dres, pf, g, scale, gate_next, name, after):
    s, d = xin.shape
    tm = _tile(s, 256, 16)

    def body(x_ref, dh_ref, dr_ref, pf_ref, g_ref, sc_ref, gn_ref, after_ref, dx_ref, dn_ref, st_ref):
        @pl.when(pl.program_id(0) == 0)
        def _():
            st_ref[...] = jnp.zeros_like(st_ref)

        xv, dhv, drv, gv = x_ref[...], dh_ref[...], dr_ref[...], g_ref[...]
        one_sc = 1.0 + sc_ref[...]
        rstd = lax.rsqrt(jnp.mean(xv * xv, axis=-1, keepdims=True) + EPS)
        xhat = xv * rstd
        dxhat = dhv * (gv * one_sc)
        dx = drv + rstd * (dxhat - xhat * jnp.mean(dxhat * xhat, axis=-1, keepdims=True))
        dx_ref[...] = dx
        dn_ref[...] = (gn_ref[...] * dx).astype(BF16)
        dhx = dhv * xhat
        st_ref[0:1, :] += jnp.sum(dhv, axis=0, keepdims=True)
        st_ref[1:2, :] += jnp.sum(dhx, axis=0, keepdims=True) * gv
        st_ref[2:3, :] += jnp.sum(dhx, axis=0, keepdims=True) * one_sc
        st_ref[3:4, :] += jnp.sum(drv * pf_ref[...], axis=0, keepdims=True)

    return pl.pallas_call(
        body, name=name, grid=(s // tm,),
        out_shape=(jax.ShapeDtypeStruct((s, d), F32), jax.ShapeDtypeStruct((s, d), BF16),
                   jax.ShapeDtypeStruct((8, d), F32)),
        in_specs=[_row_spec(tm, d)] * 4 + [_vec_spec(d)] * 3 + [_ANY],
        out_specs=(_row_spec(tm, d), _row_spec(tm, d), pl.BlockSpec((8, d), lambda i: (0, 0))),
        compiler_params=_params(1),
    )(xin, dh, dres, pf, g, scale, gate_next, after)


def _ada_fwd(c_all, w_loc):
    nb, d = c_all.shape
    n = w_loc.shape[1]
    tn = _tile(n, 512, 128) if n % 128 == 0 else n

    def body(c_ref, w_ref, o_ref):
        cv = c_ref[...]
        o_ref[...] = jnp.dot(cv * jax.nn.sigmoid(cv), w_ref[...], preferred_element_type=F32,
                             precision=lax.Precision.HIGHEST)

    return pl.pallas_call(
        body, name="ada_fwd", grid=(n // tn,), out_shape=jax.ShapeDtypeStruct((nb, n), F32),
        in_specs=[pl.BlockSpec((nb, d), lambda j: (0, 0)), pl.BlockSpec((d, tn), lambda j: (0, j))],
        out_specs=pl.BlockSpec((nb, tn), lambda j: (0, j)), compiler_params=_params(1),
    )(c_all, w_loc)


def _ada_bwd(c_all, dmod_cols):
    nb, d = c_all.shape
    n = dmod_cols.shape[1]
    tn = _tile(n, 512, 128) if n % 128 == 0 else n

    def body(c_ref, dm_ref, o_ref):
        cv = c_ref[...]
        o_ref[...] = lax.dot_general(cv * jax.nn.sigmoid(cv), dm_ref[...], TN, preferred_element_type=F32,
                                     precision=lax.Precision.HIGHEST)

    return pl.pallas_call(
        body, name="ada_bwd", grid=(n // tn,), out_shape=jax.ShapeDtypeStruct((d, n), F32),
        in_specs=[pl.BlockSpec((nb, d), lambda j: (0, 0)), pl.BlockSpec((nb, tn), lambda j: (0, j))],
        out_specs=pl.BlockSpec((d, tn), lambda j: (0, j)), compiler_params=_params(1),
    )(c_all, dmod_cols)


_INV_SQRT2 = 0.7071067811865476
_INV_SQRT2PI = 0.3989422804014327


def _gelu(x):
    return 0.5 * x * (1.0 + lax.erf(x * _INV_SQRT2))


def _gelu_grad(x):
    return 0.5 * (1.0 + lax.erf(x * _INV_SQRT2)) + x * jnp.exp(-0.5 * x * x) * _INV_SQRT2PI


def _gm_group_fwd(up, vp, gv, wt, bcol):
    u = _gelu(up)
    va = _gelu(vp)
    xc = va - jnp.mean(va, axis=-1, keepdims=True)
    rstd_v = lax.rsqrt(jnp.mean(xc * xc, axis=-1, keepdims=True) + EPS)
    yv = xc * rstd_v
    vn = (yv * gv).astype(BF16)
    mixed = _dot(wt, vn, NN) + bcol
    return u, rstd_v, yv, vn, mixed, u * mixed


def _tril_bf16(w):
    row = lax.broadcasted_iota(jnp.int32, w.shape, 0)
    col = lax.broadcasted_iota(jnp.int32, w.shape, 1)
    return jnp.where(col <= row, w, 0.0).astype(BF16)


def _gmlp_fwd(proj, v_norm_g, w_spatial, b_cols, gamma, gm, after):
    s = proj.shape[0]
    ng = gm // GRP

    def body(p_ref, gv_ref, w_ref, b_ref, gam_ref, after_ref, o_ref):
        for g in range(ng):
            lo = g * GRP
            wt = _tril_bf16(w_ref[g])
            *_, o = _gm_group_fwd(p_ref[:, lo:lo + GRP], p_ref[:, gm + lo:gm + lo + GRP], gv_ref[:, lo:lo + GRP],
                                  wt, b_ref[:, g:g + 1])
            rstd_o = lax.rsqrt(jnp.mean(o * o, axis=-1, keepdims=True) + EPS)
            o_ref[:, lo:lo + GRP] = (o * rstd_o * gam_ref[:, lo:lo + GRP]).astype(BF16)

    return pl.pallas_call(
        body, name="gmlp_fwd", grid=(s // GRP,), out_shape=jax.ShapeDtypeStruct((s, gm), BF16),
        in_specs=[pl.BlockSpec((GRP, 2 * gm), lambda n: (n, 0)), _vec_spec(gm),
                  pl.BlockSpec((ng, GRP, GRP), lambda n: (0, 0, 0)), pl.BlockSpec((GRP, GRP), lambda n: (0, 0)),
                  _vec_spec(gm), _ANY],
        out_specs=pl.BlockSpec((GRP, gm), lambda n: (n, 0)), compiler_params=_params(1),
    )(proj, v_norm_g, w_spatial, b_cols, gamma, after)


def _gmlp_bwd(proj, don, v_norm_g, w_spatial, b_cols, gamma, gm, after):
    s = proj.shape[0]
    ng = gm // GRP

    def body(p_ref, don_ref, gv_ref, w_ref, b_ref, gam_ref, after_ref, dp_ref, dw_ref, db_ref, dgv_ref, dgam_ref):
        @pl.when(pl.program_id(0) == 0)
        def _():
            dw_ref[...] = jnp.zeros_like(dw_ref)
            db_ref[...] = jnp.zeros_like(db_ref)
            dgv_ref[...] = jnp.zeros_like(dgv_ref)
            dgam_ref[...] = jnp.zeros_like(dgam_ref)

        lane = lax.broadcasted_iota(jnp.int32, (GRP, GRP), 1)
        row = lax.broadcasted_iota(jnp.int32, (GRP, GRP), 0)
        for g in range(ng):
            lo = g * GRP
            up, vp = p_ref[:, lo:lo + GRP], p_ref[:, gm + lo:gm + lo + GRP]
            gv, gam = gv_ref[:, lo:lo + GRP], gam_ref[:, lo:lo + GRP]
            wt = _tril_bf16(w_ref[g])
            u, rstd_v, yv, vn, mixed, o = _gm_group_fwd(up, vp, gv, wt, b_ref[:, g:g + 1])
            rstd_o = lax.rsqrt(jnp.mean(o * o, axis=-1, keepdims=True) + EPS)
            ohat = o * rstd_o
            dn = don_ref[:, lo:lo + GRP]
            dgam_ref[:, lo:lo + GRP] += jnp.sum(dn * ohat, axis=0, keepdims=True)
            dohat = dn * gam
            do = rstd_o * (dohat - ohat * jnp.mean(dohat * ohat, axis=-1, keepdims=True))
            du = do * mixed
            dmixed = do * u
            dmb = dmixed.astype(BF16)
            db_ref[...] += jnp.where(lane == g, jnp.sum(dmixed, axis=-1, keepdims=True), 0.0)
            dw_ref[g] += jnp.where(lane <= row, _dot(dmb, vn, NT), 0.0)
            dvn = _dot(wt, dmb, TN)
            dgv_ref[:, lo:lo + GRP] += jnp.sum(dvn * yv, axis=0, keepdims=True)
            dyv = dvn * gv
            dva = rstd_v * (dyv - jnp.mean(dyv, axis=-1, keepdims=True)
                            - yv * jnp.mean(dyv * yv, axis=-1, keepdims=True))
            dp_ref[:, lo:lo + GRP] = (du * _gelu_grad(up)).astype(BF16)
            dp_ref[:, gm + lo:gm + lo + GRP] = (dva * _gelu_grad(vp)).astype(BF16)

    const2 = lambda n: (0, 0)
    return pl.pallas_call(
        body, name="gmlp_bwd", grid=(s // GRP,),
        out_shape=(jax.ShapeDtypeStruct((s, 2 * gm), BF16), jax.ShapeDtypeStruct((ng, GRP, GRP), F32),
                   jax.ShapeDtypeStruct((GRP, GRP), F32), jax.ShapeDtypeStruct((1, gm), F32),
                   jax.ShapeDtypeStruct((1, gm), F32)),
        in_specs=[pl.BlockSpec((GRP, 2 * gm), lambda n: (n, 0)), pl.BlockSpec((GRP, gm), lambda n: (n, 0)),
                  _vec_spec(gm), pl.BlockSpec((ng, GRP, GRP), lambda n: (0, 0, 0)),
                  pl.BlockSpec((GRP, GRP), const2), _vec_spec(gm), _ANY],
        out_specs=(pl.BlockSpec((GRP, 2 * gm), lambda n: (n, 0)), pl.BlockSpec((ng, GRP, GRP), lambda n: (0, 0, 0)),
                   pl.BlockSpec((GRP, GRP), const2), _vec_spec(gm), _vec_spec(gm)),
        compiler_params=_params(1),
    )(proj, don, v_norm_g, w_spatial, b_cols, gamma, after)


BLK = 256


def _log_sigmoid(z):
    return jnp.minimum(z, 0.0) - jnp.log(1.0 + jnp.exp(-jnp.abs(z)))


def _split_dot(x, tri, passes):
    n = x.shape[0]
    parts, rest = [], x
    for _ in range(passes):
        hi = rest.astype(BF16)
        parts.append(hi)
        rest = rest - hi.astype(F32)
    res = _dot(jnp.concatenate(parts, axis=0), tri, NN)
    out = res[0:n]
    for k in range(1, passes):
        out = out + res[k * n:(k + 1) * n]
    return out


HEADS_PER_STEP = 2


def _sb_fwd(proj, gamma, gm, sb, after):
    s = proj.shape[0]
    hp = HEADS_PER_STEP
    w = hp * GRP
    nhp, nq = sb // w, s // BLK
    qc, kc, vc, gc = 2 * gm // w, (2 * gm + sb) // w, (2 * gm + 2 * sb) // w, gm // w
    scale = GRP ** -0.5

    def body(q_ref, k_ref, v_ref, gam_ref, after_ref, on_ref, o_ref, mt_ref, kb, vb):
        i = pl.program_id(1)

        @pl.when(i == 0)
        def _():
            kb[...] = k_ref[...].astype(BF16)
            vb[...] = v_ref[...].astype(BF16)

        qb = q_ref[...].astype(BF16)
        row = lax.broadcasted_iota(jnp.int32, (BLK, BLK), 0)
        col = lax.broadcasted_iota(jnp.int32, (BLK, BLK), 1)
        later = (row > col).astype(BF16)

        def block(j, carry, diag):
            off = pl.multiple_of(j * BLK, BLK)
            kj, vj = kb[pl.ds(off, BLK), :], vb[pl.ds(off, BLK), :]
            out = []
            for h in range(hp):
                tail, acc = carry[h]
                sl = slice(h * GRP, (h + 1) * GRP)
                z = _dot(qb[:, sl], kj[:, sl], NT) * scale
                lb = _log_sigmoid(z)
                l1m = lb - z
                if diag:
                    l1m = jnp.where(col < row, l1m, 0.0)
                after_s = _split_dot(l1m, later, 2)
                a = jnp.exp(lb + after_s + tail)
                if diag:
                    a = jnp.where(col < row, a, 0.0)
                out.append((tail + after_s[:, 0:1] + l1m[:, 0:1], acc + _dot(a.astype(BF16), vj[:, sl], NN)))
            return tuple(out)

        init = tuple((jnp.zeros((BLK, 1), F32), jnp.zeros((BLK, GRP), F32)) for _ in range(hp))
        carry = block(i, init, True)
        carry = lax.fori_loop(0, i, lambda jj, c: block(i - 1 - jj, c, False), carry)
        for h in range(hp):
            tail, acc = carry[h]
            sl = slice(h * GRP, (h + 1) * GRP)
            rstd = lax.rsqrt(jnp.mean(acc * acc, axis=-1, keepdims=True) + EPS)
            on_ref[:, sl] = (acc * rstd * gam_ref[:, sl]).astype(BF16)
            o_ref[:, sl] = acc
            mt_ref[h] = tail

    return pl.pallas_call(
        body, name="sb_fwd", grid=(nhp, nq),
        out_shape=(jax.ShapeDtypeStruct((s, sb), BF16), jax.ShapeDtypeStruct((s, sb), F32),
                   jax.ShapeDtypeStruct((sb // GRP, s, 1), F32)),
        in_specs=[pl.BlockSpec((BLK, w), lambda h, i: (i, qc + h)), pl.BlockSpec((s, w), lambda h, i: (0, kc + h)),
                  pl.BlockSpec((s, w), lambda h, i: (0, vc + h)), pl.BlockSpec((1, w), lambda h, i: (0, gc + h)), _ANY],
        out_specs=(pl.BlockSpec((BLK, w), lambda h, i: (i, h)), pl.BlockSpec((BLK, w), lambda h, i: (i, h)),
                   pl.BlockSpec((hp, BLK, 1), lambda h, i: (h, i, 0))),
        scratch_shapes=[pltpu.VMEM((s, w), BF16), pltpu.VMEM((s, w), BF16)],
        compiler_params=_params(2),
    )(proj, proj, proj, gamma, after)


def _sb_bwd(proj, o_raw, mtot, don, gamma, gm, sb, after):
    s = proj.shape[0]
    hp = HEADS_PER_STEP
    w = hp * GRP
    nhp, nq = sb // w, s // BLK
    qc, kc, vc, gc = 2 * gm // w, (2 * gm + sb) // w, (2 * gm + 2 * sb) // w, gm // w
    scale = GRP ** -0.5

    def body(q_ref, k_ref, v_ref, o_ref, mt_ref, don_ref, gam_ref, after_ref, dq_ref, dk_ref, dv_ref, dgam_ref,
             kb, vb, dk_acc, dv_acc):
        i = pl.program_id(1)

        @pl.when(i == 0)
        def _():
            kb[...] = k_ref[...].astype(BF16)
            vb[...] = v_ref[...].astype(BF16)
            dk_acc[...] = jnp.zeros_like(dk_acc)
            dv_acc[...] = jnp.zeros_like(dv_acc)
            dgam_ref[...] = jnp.zeros_like(dgam_ref)

        dobs = []
        for h in range(hp):
            sl = slice(h * GRP, (h + 1) * GRP)
            o, dn = o_ref[:, sl], don_ref[:, sl]
            rstd = lax.rsqrt(jnp.mean(o * o, axis=-1, keepdims=True) + EPS)
            ohat = o * rstd
            dgam_ref[:, sl] += jnp.sum(dn * ohat, axis=0, keepdims=True)
            dohat = dn * gam_ref[:, sl]
            dobs.append((rstd * (dohat - ohat * jnp.mean(dohat * ohat, axis=-1, keepdims=True))).astype(BF16))

        qb = q_ref[...].astype(BF16)
        row = lax.broadcasted_iota(jnp.int32, (BLK, BLK), 0)
        col = lax.broadcasted_iota(jnp.int32, (BLK, BLK), 1)
        upto = (row <= col).astype(BF16)
        before = (row < col).astype(BF16)

        def block(j, carry, diag):
            off = pl.multiple_of(j * BLK, BLK)
            kj, vj = kb[pl.ds(off, BLK), :], vb[pl.ds(off, BLK), :]
            out = []
            for h in range(hp):
                m_pre, e_pre, dq = carry[h]
                sl = slice(h * GRP, (h + 1) * GRP)
                qh, kh, dob = qb[:, sl], kj[:, sl], dobs[h]
                z = _dot(qh, kh, NT) * scale
                lb = _log_sigmoid(z)
                l1m = lb - z
                if diag:
                    l1m = jnp.where(col < row, l1m, 0.0)
                upto_s = _split_dot(l1m, upto, 2)
                a = jnp.exp(lb + (mt_ref[h] - m_pre) - upto_s)
                if diag:
                    a = jnp.where(col < row, a, 0.0)
                de = a * _dot(dob, vj[:, sl], NT)
                dv_acc[pl.ds(off, BLK), sl] += _dot(a.astype(BF16), dob, TN)
                before_s = _split_dot(de, before, 1)
                dl1m = e_pre + before_s
                if diag:
                    dl1m = jnp.where(col < row, dl1m, 0.0)
                sig = jnp.exp(lb)
                dzb = ((de * (1.0 - sig) - dl1m * sig) * scale).astype(BF16)
                dk_acc[pl.ds(off, BLK), sl] += _dot(dzb, qh, TN)
                out.append((m_pre + upto_s[:, BLK - 1:BLK], e_pre + before_s[:, BLK - 1:BLK] + de[:, BLK - 1:BLK],
                            dq + _dot(dzb, kh, NN)))
            return tuple(out)

        zero = jnp.zeros((BLK, 1), F32)
        carry = tuple((zero, zero, jnp.zeros((BLK, GRP), F32)) for _ in range(hp))
        carry = lax.fori_loop(0, i, lambda j, c: block(j, c, False), carry)
        carry = block(i, carry, True)
        for h in range(hp):
            dq_ref[:, h * GRP:(h + 1) * GRP] = carry[h][2].astype(BF16)

        @pl.when(i == nq - 1)
        def _():
            dk_ref[...] = dk_acc[...].astype(BF16)
            dv_ref[...] = dv_acc[...].astype(BF16)

    blk_q = lambda h, i: (i, h)
    whole = lambda h, i: (0, h)
    return pl.pallas_call(
        body, name="sb_bwd", grid=(nhp, nq),
        out_shape=(jax.ShapeDtypeStruct((s, sb), BF16),) * 3 + (jax.ShapeDtypeStruct((1, sb), F32),),
        in_specs=[pl.BlockSpec((BLK, w), lambda h, i: (i, qc + h)), pl.BlockSpec((s, w), lambda h, i: (0, kc + h)),
                  pl.BlockSpec((s, w), lambda h, i: (0, vc + h)), pl.BlockSpec((BLK, w), blk_q),
                  pl.BlockSpec((hp, BLK, 1), lambda h, i: (h, i, 0)),
                  pl.BlockSpec((BLK, w), lambda h, i: (i, gc + h)),
                  pl.BlockSpec((1, w), lambda h, i: (0, gc + h)), _ANY],
        out_specs=(pl.BlockSpec((BLK, w), blk_q), pl.BlockSpec((s, w), whole), pl.BlockSpec((s, w), whole),
                   pl.BlockSpec((1, w), whole)),
        scratch_shapes=[pltpu.VMEM((s, w), BF16), pltpu.VMEM((s, w), BF16),
                        pltpu.VMEM((s, w), F32), pltpu.VMEM((s, w), F32)],
        compiler_params=_params(2),
    )(proj, proj, proj, o_raw, mtot, don, gamma, after)


def _sb_fwd_one_head(proj, gamma, gm, sb):
    s = proj.shape[0]
    nh, nq = sb // GRP, s // BLK
    qc, kc, vc = 2 * gm // GRP, (2 * gm + sb) // GRP, (2 * gm + 2 * sb) // GRP
    scale = GRP ** -0.5

    def body(q_ref, k_ref, v_ref, gam_ref, on_ref, o_ref, mt_ref, kb, vb):
        i = pl.program_id(1)

        @pl.when(i == 0)
        def _():
            kb[...] = k_ref[...].astype(BF16)
            vb[...] = v_ref[...].astype(BF16)

        qb = q_ref[...].astype(BF16)
        row = lax.broadcasted_iota(jnp.int32, (BLK, BLK), 0)
        col = lax.broadcasted_iota(jnp.int32, (BLK, BLK), 1)
        later = (row > col).astype(BF16)

        def step(jj, carry):
            tail, acc = carry
            j = i - jj
            off = pl.multiple_of(j * BLK, BLK)
            z = _dot(qb, kb[pl.ds(off, BLK), :], NT) * scale
            lb = _log_sigmoid(z)
            mask = col + j * BLK < row + i * BLK
            l1m = jnp.where(mask, lb - z, 0.0)
            a = jnp.where(mask, jnp.exp(lb + _split_dot(l1m, later, 3) + tail), 0.0)
            acc = acc + _dot(a.astype(BF16), vb[pl.ds(off, BLK), :], NN)
            return tail + jnp.sum(l1m, axis=-1, keepdims=True), acc

        tail, acc = lax.fori_loop(0, i + 1, step, (jnp.zeros((BLK, 1), F32), jnp.zeros((BLK, GRP), F32)))
        rstd = lax.rsqrt(jnp.mean(acc * acc, axis=-1, keepdims=True) + EPS)
        on_ref[...] = (acc * rstd * gam_ref[...]).astype(BF16)
        o_ref[...] = acc
        mt_ref[...] = tail

    return pl.pallas_call(
        body, name="sb_fwd", grid=(nh, nq),
        out_shape=(jax.ShapeDtypeStruct((s, sb), BF16), jax.ShapeDtypeStruct((s, sb), F32),
                   jax.ShapeDtypeStruct((nh, s, 1), F32)),
        in_specs=[pl.BlockSpec((BLK, GRP), lambda h, i: (i, qc + h)), pl.BlockSpec((s, GRP), lambda h, i: (0, kc + h)),
                  pl.BlockSpec((s, GRP), lambda h, i: (0, vc + h)),
                  pl.BlockSpec((1, GRP), lambda h, i: (0, gm // GRP + h))],
        out_specs=(pl.BlockSpec((BLK, GRP), lambda h, i: (i, h)), pl.BlockSpec((BLK, GRP), lambda h, i: (i, h)),
                   pl.BlockSpec((None, BLK, 1), lambda h, i: (h, i, 0))),
        scratch_shapes=[pltpu.VMEM((s, GRP), BF16), pltpu.VMEM((s, GRP), BF16)],
        compiler_params=_params(2),
    )(proj, proj, proj, gamma)


def _sb_bwd_one_head(proj, o_raw, mtot, don, gamma, gm, sb):
    s = proj.shape[0]
    nh, nq = sb // GRP, s // BLK
    qc, kc, vc = 2 * gm // GRP, (2 * gm + sb) // GRP, (2 * gm + 2 * sb) // GRP
    scale = GRP ** -0.5

    def body(q_ref, k_ref, v_ref, o_ref, mt_ref, don_ref, gam_ref, dq_ref, dk_ref, dv_ref, dgam_ref,
             kb, vb, dk_acc, dv_acc):
        i = pl.program_id(1)

        @pl.when(i == 0)
        def _():
            kb[...] = k_ref[...].astype(BF16)
            vb[...] = v_ref[...].astype(BF16)
            dk_acc[...] = jnp.zeros_like(dk_acc)
            dv_acc[...] = jnp.zeros_like(dv_acc)
            dgam_ref[...] = jnp.zeros_like(dgam_ref)

        o, dn, gam = o_ref[...], don_ref[...], gam_ref[...]
        rstd = lax.rsqrt(jnp.mean(o * o, axis=-1, keepdims=True) + EPS)
        ohat = o * rstd
        dgam_ref[...] += jnp.sum(dn * ohat, axis=0, keepdims=True)
        dohat = dn * gam
        dob = (rstd * (dohat - ohat * jnp.mean(dohat * ohat, axis=-1, keepdims=True))).astype(BF16)

        qb = q_ref[...].astype(BF16)
        mt = mt_ref[...]
        row = lax.broadcasted_iota(jnp.int32, (BLK, BLK), 0)
        col = lax.broadcasted_iota(jnp.int32, (BLK, BLK), 1)
        upto = (row <= col).astype(BF16)
        before = (row < col).astype(BF16)

        def step(j, carry):
            m_pre, e_pre, dq = carry
            off = pl.multiple_of(j * BLK, BLK)
            kj, vj = kb[pl.ds(off, BLK), :], vb[pl.ds(off, BLK), :]
            z = _dot(qb, kj, NT) * scale
            lb = _log_sigmoid(z)
            mask = col + j * BLK < row + i * BLK
            l1m = jnp.where(mask, lb - z, 0.0)
            tail = mt - m_pre - _split_dot(l1m, upto, 3)
            a = jnp.where(mask, jnp.exp(lb + tail), 0.0)
            de = a * _dot(dob, vj, NT)
            dv_acc[pl.ds(off, BLK), :] += _dot(a.astype(BF16), dob, TN)
            dl1m = e_pre + _split_dot(de, before, 2)
            sig = jnp.exp(lb)
            dz = (de * (1.0 - sig) - jnp.where(mask, dl1m * sig, 0.0)) * scale
            dzb = dz.astype(BF16)
            dk_acc[pl.ds(off, BLK), :] += _dot(dzb, qb, TN)
            return (m_pre + jnp.sum(l1m, axis=-1, keepdims=True), e_pre + jnp.sum(de, axis=-1, keepdims=True),
                    dq + _dot(dzb, kj, NN))

        zero = jnp.zeros((BLK, 1), F32)
        _, _, dq = lax.fori_loop(0, i + 1, step, (zero, zero, jnp.zeros((BLK, GRP), F32)))
        dq_ref[...] = dq.astype(BF16)

        @pl.when(i == nq - 1)
        def _():
            dk_ref[...] = dk_acc[...].astype(BF16)
            dv_ref[...] = dv_acc[...].astype(BF16)

    blk_q = lambda h, i: (i, h)
    whole = lambda h, i: (0, h)
    return pl.pallas_call(
        body, name="sb_bwd", grid=(nh, nq),
        out_shape=(jax.ShapeDtypeStruct((s, sb), BF16),) * 3 + (jax.ShapeDtypeStruct((1, sb), F32),),
        in_specs=[pl.BlockSpec((BLK, GRP), lambda h, i: (i, qc + h)), pl.BlockSpec((s, GRP), lambda h, i: (0, kc + h)),
                  pl.BlockSpec((s, GRP), lambda h, i: (0, vc + h)), pl.BlockSpec((BLK, GRP), blk_q),
                  pl.BlockSpec((None, BLK, 1), lambda h, i: (h, i, 0)),
                  pl.BlockSpec((BLK, GRP), lambda h, i: (i, gm // GRP + h)),
                  pl.BlockSpec((1, GRP), lambda h, i: (0, gm // GRP + h))],
        out_specs=(pl.BlockSpec((BLK, GRP), blk_q), pl.BlockSpec((s, GRP), whole), pl.BlockSpec((s, GRP), whole),
                   pl.BlockSpec((1, GRP), whole)),
        scratch_shapes=[pltpu.VMEM((s, GRP), BF16), pltpu.VMEM((s, GRP), BF16),
                        pltpu.VMEM((s, GRP), F32), pltpu.VMEM((s, GRP), F32)],
        compiler_params=_params(2),
    )(proj, proj, proj, o_raw, mtot, don, gamma)


def _ffn_fwd(h2, wg, wu, wd):
    s, d = h2.shape
    nb, fb, _ = wd.shape
    tm = _tile(s, 512, 16)

    def body(h_ref, wg_ref, wu_ref, wd_ref, f_ref, g_ref, u_ref):
        j = pl.program_id(1)
        hv = h_ref[...]
        g = _dot(hv, wg_ref[...], NT)
        u = _dot(hv, wu_ref[...], NT)
        g_ref[...] = g.astype(BF16)
        u_ref[...] = u.astype(BF16)
        part = _dot((g * jax.nn.sigmoid(g) * u).astype(BF16), wd_ref[...], NN)

        @pl.when(j == 0)
        def _():
            f_ref[...] = part

        @pl.when(j > 0)
        def _():
            f_ref[...] += part

    rows = pl.BlockSpec((tm, d), lambda i, j: (i, 0))
    wblk = pl.BlockSpec((None, fb, d), lambda i, j: (j, 0, 0))
    hid = pl.BlockSpec((None, tm, fb), lambda i, j: (j, i, 0))
    return pl.pallas_call(
        body, name="ffn_fwd", grid=(s // tm, nb),
        out_shape=(jax.ShapeDtypeStruct((s, d), F32), jax.ShapeDtypeStruct((nb, s, fb), BF16),
                   jax.ShapeDtypeStruct((nb, s, fb), BF16)),
        in_specs=[rows, wblk, wblk, wblk],
        out_specs=(rows, hid, hid), compiler_params=_params(2),
    )(h2, wg, wu, wd)


def _ffn_bwd(df, g_pre, u_pre, wg, wu, wd):
    s, d = df.shape
    nb, fb, _ = wd.shape
    tm = _tile(s, 512, 16)

    def body(df_ref, g_ref, u_ref, wg_ref, wu_ref, wd_ref, dh_ref, a_ref, dg_ref, du_ref):
        j = pl.program_id(1)
        g, u = g_ref[...].astype(F32), u_ref[...].astype(F32)
        da = _dot(df_ref[...], wd_ref[...], NT)
        sg = jax.nn.sigmoid(g)
        silu = g * sg
        a_ref[...] = (silu * u).astype(BF16)
        dg = (da * u * (sg * (1.0 + g * (1.0 - sg)))).astype(BF16)
        du = (da * silu).astype(BF16)
        dg_ref[...] = dg
        du_ref[...] = du
        part = _dot(dg, wg_ref[...], NN) + _dot(du, wu_ref[...], NN)

        @pl.when(j == 0)
        def _():
            dh_ref[...] = part

        @pl.when(j > 0)
        def _():
            dh_ref[...] += part

    rows = pl.BlockSpec((tm, d), lambda i, j: (i, 0))
    wblk = pl.BlockSpec((None, fb, d), lambda i, j: (j, 0, 0))
    hid = pl.BlockSpec((None, tm, fb), lambda i, j: (j, i, 0))
    hid_sds = jax.ShapeDtypeStruct((nb, s, fb), BF16)
    return pl.pallas_call(
        body, name="ffn_bwd", grid=(s // tm, nb),
        out_shape=(jax.ShapeDtypeStruct((s, d), F32), hid_sds, hid_sds, hid_sds),
        in_specs=[rows, hid, hid, wblk, wblk, wblk],
        out_specs=(rows, hid, hid, hid), compiler_params=_params(2),
    )(df, g_pre, u_pre, wg, wu, wd)


def _reduce_adamw(parts, w, m, v, name, after):
    npart, r, c = parts.shape
    tr = _tile(r, max(16, 262144 // c), 16)
    c1, c2 = 1.0 - ADAM_B1 ** ADAM_STEP, 1.0 - ADAM_B2 ** ADAM_STEP

    def body(p_ref, w_ref, m_ref, v_ref, after_ref, g_ref, d_ref, nm_ref, nv_ref):
        g = p_ref[0].astype(F32)
        for k in range(1, npart):
            g = g + p_ref[k].astype(F32)
        nm = ADAM_B1 * m_ref[...] + (1.0 - ADAM_B1) * g
        nv = ADAM_B2 * v_ref[...] + (1.0 - ADAM_B2) * (g * g)
        g_ref[...] = g
        nm_ref[...] = nm
        nv_ref[...] = nv
        d_ref[...] = -ADAM_LR * ((nm / c1) / (jnp.sqrt(nv / c2) + ADAM_EPS) + ADAM_WD * w_ref[...])

    blk = pl.BlockSpec((tr, c), lambda i: (i, 0))
    sds = jax.ShapeDtypeStruct((r, c), F32)
    return pl.pallas_call(
        body, name=name, grid=(r // tr,), out_shape=(sds,) * 4,
        in_specs=[pl.BlockSpec((npart, tr, c), lambda i: (0, i, 0)), blk, blk, blk, _ANY], out_specs=(blk,) * 4,
        compiler_params=_params(1),
    )(parts, w, m, v, after)


def _pack(vecs):
    rows = jnp.concatenate([a.reshape(-1, GRP) for a in vecs], axis=0)
    pad = -rows.shape[0] % 64
    return jnp.pad(rows, ((0, pad), (0, 0)))


def _unpack(rows, shapes):
    out, at = [], 0
    for shp in shapes:
        n = 1
        for k in shp:
            n *= k
        out.append(rows[at:at + n // GRP].reshape(shp))
        at += n // GRP
    return out


def kernel(x, c, w_ada, b_ada, norm1_g, w_in, v_norm_g, w_spatial, b_spatial, out_norm_g, w_out, norm2_g, w_gate, w_up, w_down, final_g, loss_target, m_w_ada, m_b_ada, m_norm1_g, m_w_in, m_v_norm_g, m_w_spatial, m_b_spatial, m_out_norm_g, m_w_out, m_norm2_g, m_w_gate, m_w_up, m_w_down, m_final_g, v_w_ada, v_b_ada, v_norm1_g, v_w_in, v_v_norm_g, v_w_spatial, v_b_spatial, v_out_norm_g, v_w_out, v_norm2_g, v_w_gate, v_w_up, v_w_down, v_final_g):
    s, d = x.shape[1], x.shape[2]
    gm = v_norm_g.shape[1]
    sb = d - gm
    n_in, ffb, ob = w_in.shape[2], w_gate.shape[2], w_out.shape[1]
    xs, tgt = x[0], loss_target[0]
    me = 4 * lax.axis_index("x") + 2 * lax.axis_index("y") + lax.axis_index("c")

    c_all = _exchange(c, False, "gather_c")[:, 0, :]
    mod_cols = _ada_fwd(c_all, w_ada[0])
    mod = _exchange(mod_cols[:, None, :], True, "scatter_mod").reshape(1, 6 * d) + b_ada
    shift1, scale1, gate1, shift2, scale2, gate2 = [mod[:, k * d:(k + 1) * d] for k in range(6)]

    tok, gathers = mod, {}
    tr_ = lambda a: jnp.swapaxes(a, 1, 2)
    w_gate, m_w_gate, v_w_gate, w_up, m_w_up, v_w_up = map(tr_, (w_gate, m_w_gate, v_w_gate, w_up, m_w_up, v_w_up))
    for nm, w in (("w_in", w_in), ("w_out", w_out), ("w_gate", w_gate), ("w_up", w_up), ("w_down", w_down)):
        gathers[nm], tok = _gather_start(w[0].astype(BF16), tok, "gather_start_" + nm)
    shift1 = shift1 + tok[0:1, 0:1]

    tm = _tile(s, 512, 16)
    nt = s // tm
    rows_d = pl.BlockSpec((tm, d), lambda i, j: (i, 0))
    h1 = _prenorm_fwd(xs, norm1_g, scale1, shift1, "prenorm1")
    w_in_all = _gather_wait(_gather_forward(gathers["w_in"], h1, "gather_fwd_w_in")[0], h1, "gather_wait_w_in")
    proj = _mm(h1, w_in_all, rows_d, pl.BlockSpec((None, d, n_in), lambda i, j: (j, 0, 0)),
               jax.ShapeDtypeStruct((s, NDEV * n_in), F32), pl.BlockSpec((tm, n_in), lambda i, j: (i, j)),
               (nt, NDEV), NN, False, "proj")
    fwd_w_out, tok = _gather_forward(gathers["w_out"], proj, "gather_fwd_w_out")
    b_cols = jnp.pad(b_spatial[0].T, ((0, 0), (0, GRP - b_spatial.shape[1])))
    on_sb, o_sb, mtot = _sb_fwd(proj, out_norm_g, gm, sb, tok)
    fwd_w_gate, tok = _gather_forward(gathers["w_gate"], on_sb, "gather_fwd_w_gate")
    on_gm = _gmlp_fwd(proj, v_norm_g, w_spatial[0], b_cols, out_norm_g, gm, tok)
    fwd_w_up, tok = _gather_forward(gathers["w_up"], on_gm, "gather_fwd_w_up")
    o_n = jnp.concatenate([on_gm, on_sb], axis=1)
    w_out_all = _gather_wait(fwd_w_out, tok, "gather_wait_w_out").reshape(d, d)
    rows_1 = pl.BlockSpec((tm, d), lambda i: (i, 0))
    whole_1 = pl.BlockSpec((d, d), lambda i: (0, 0))
    p_out = _mm(o_n, w_out_all, rows_1, whole_1, jax.ShapeDtypeStruct((s, d), F32), rows_1, (nt,), NN, False,
                "out_proj")
    x1 = _residual(xs, gate1, p_out, "residual1")
    h2 = _prenorm_fwd(x1, norm2_g, scale2, shift2, "prenorm2")
    fwd_w_down, tok = _gather_forward(gathers["w_down"], h2, "gather_fwd_w_down")
    w_gate_all = _gather_wait(fwd_w_gate, tok, "gather_wait_w_gate")
    w_up_all = _gather_wait(fwd_w_up, h2, "gather_wait_w_up")
    w_down_all = _gather_wait(fwd_w_down, h2, "gather_wait_w_down")
    f_out, g_pre, u_pre = _ffn_fwd(h2, w_gate_all, w_up_all, w_down_all)
    x2 = _residual(x1, gate2, f_out, "residual2")

    dx2, df, st_f = _final_loss(x2, final_g.reshape(1, d), tgt, gate2)
    dh2, act, dg, du = _ffn_bwd(df, g_pre, u_pre, w_gate_all, w_up_all, w_down_all)
    hid = pl.BlockSpec((None, s, ffb), lambda j: (j, 0, 0))
    all_d = pl.BlockSpec((s, d), lambda j: (0, 0))
    col_sds = jax.ShapeDtypeStruct((NDEV, d, ffb), BF16)
    col_out = pl.BlockSpec((None, d, ffb), lambda j: (j, 0, 0))

    def second_leg(first, after, tag):
        parts, lands = _scatter_wait1(first, after, "scatter_wait1_" + tag)
        sums = [_chip_sum(p, l, "chip_sum_%s_%d" % (tag, k)) for k, (p, l) in enumerate(zip(parts, lands))]
        return _scatter_start2(sums, after, "scatter_start2_" + tag)

    row_sds = jax.ShapeDtypeStruct((NDEV, ffb, d), BF16)
    row_out = pl.BlockSpec((None, ffb, d), lambda j: (j, 0, 0))
    gw_gate = _mm(dg, h2, hid, all_d, row_sds, row_out, (NDEV,), TN, False, "grad_w_gate")
    gw_up = _mm(du, h2, hid, all_d, row_sds, row_out, (NDEV,), TN, False, "grad_w_up")
    gw_down = _mm(act, df, hid, all_d, row_sds, row_out, (NDEV,), TN, False, "grad_w_down")
    first_ffn, tok = _scatter_start1([gw_gate, gw_up, gw_down], tok, "scatter_start1_ffn")
    dx1, dp, st2 = _prenorm_bwd(x1, dh2, dx2, f_out, norm2_g, scale2, gate1, "prenorm2_bwd", tok)
    don = _mm(dp, w_out_all, rows_1, whole_1, jax.ShapeDtypeStruct((s, d), F32), rows_1, (nt,), NT, False,
              "out_proj_bwd")
    gw_out = _mm(o_n, dp, pl.BlockSpec((s, ob), lambda j: (0, j)), all_d,
                 jax.ShapeDtypeStruct((NDEV, ob, d), BF16), pl.BlockSpec((None, ob, d), lambda j: (j, 0, 0)),
                 (NDEV,), TN, False, "grad_w_out", don)
    first_out, tok = _scatter_start1([gw_out], tok, "scatter_start1_out")
    second_ffn, tok = second_leg(first_ffn, tok, "ffn")
    dproj_gm, dw_sp, db_cols, dgv, dgam_gm = _gmlp_bwd(proj, don, v_norm_g, w_spatial[0], b_cols, out_norm_g, gm, tok)
    second_out, tok = second_leg(first_out, dproj_gm, "out")
    dq, dk, dv, dgam_sb = _sb_bwd(proj, o_sb, mtot, don, out_norm_g, gm, sb, tok)
    dproj = jnp.concatenate([dproj_gm, dq, dk, dv], axis=1)
    gw_in = _mm(h1, dproj, all_d, pl.BlockSpec((s, n_in), lambda j: (0, j)),
                jax.ShapeDtypeStruct((NDEV, d, n_in), BF16), pl.BlockSpec((None, d, n_in), lambda j: (j, 0, 0)),
                (NDEV,), TN, False, "grad_w_in")
    first_in, tok = _scatter_start1([gw_in], tok, "scatter_start1_in")
    dh1 = _mm(dproj, w_in_all, pl.BlockSpec((tm, n_in), lambda i, j: (i, j)),
              pl.BlockSpec((None, d, n_in), lambda i, j: (j, 0, 0)), jax.ShapeDtypeStruct((s, d), F32), rows_d,
              (nt, NDEV), NT, True, "in_proj_bwd", tok)
    grad_x, _, st1 = _prenorm_bwd(xs, dh1, dx1, p_out, norm1_g, scale1, gate1, "prenorm1_bwd", tok)

    dmod = jnp.concatenate([st1[0], st1[1], st1[3], st2[0], st2[1], st2[3]])
    loss_row = jnp.pad((0.5 * jnp.sum(st_f[1]) / d).reshape(1, 1), ((0, 0), (0, GRP - 1)))
    small = [st1[2], dgv, dw_sp, db_cols[:, :b_spatial.shape[1]].T, jnp.concatenate([dgam_gm, dgam_sb], axis=1),
             st2[2], st_f[0], dmod, loss_row]
    zero_row = jnp.zeros((1, GRP), F32)
    small_w = [norm1_g, v_norm_g, w_spatial, b_spatial, out_norm_g, norm2_g, final_g, b_ada, zero_row]
    small_m = [m_norm1_g, m_v_norm_g, m_w_spatial, m_b_spatial, m_out_norm_g, m_norm2_g, m_final_g, m_b_ada, zero_row]
    small_v = [v_norm1_g, v_v_norm_g, v_w_spatial, v_b_spatial, v_out_norm_g, v_norm2_g, v_final_g, v_b_ada, zero_row]
    small_first, tok = _gather_start(_pack(small), grad_x, "gather_small_start")
    second_in, tok = second_leg(first_in, tok, "in")

    big = {}
    prev = tok
    for second, group in ((second_ffn, (("w_gate", w_gate, m_w_gate, v_w_gate), ("w_up", w_up, m_w_up, v_w_up),
                                        ("w_down", w_down, m_w_down, v_w_down))),
                          (second_out, (("w_out", w_out, m_w_out, v_w_out),))):
        chip_sums = _scatter_wait2(second, prev, "scatter_wait2_" + group[0][0])
        for (nm, w, m, v), part in zip(group, chip_sums):
            big[nm] = _reduce_adamw(part, w[0], m[0], v[0], "adamw_" + nm, tok)
            prev = big[nm][1]

    small_second, tok2 = _gather_forward(small_first, prev, "gather_small_fwd")
    small_all = _gather_wait(small_second, tok2, "gather_small_wait")
    sm = _reduce_adamw(small_all, _pack(small_w), _pack(small_m), _pack(small_v), "adamw_small", tok)
    shapes = [a.shape for a in small_w]
    sm_g, sm_d, sm_m, sm_v = [_unpack(t, shapes) for t in sm]
    loss = sm_g[8][0, 0]

    at = sum(a.size for a in small_w[:7]) // GRP
    dmod_all = small_all[:, at:at + 6 * d // GRP, :].reshape(NDEV, 6 * d)
    n_ada = w_ada.shape[2]
    dmod_cols = lax.dynamic_slice(dmod_all, (0, me * n_ada), (NDEV, n_ada))
    g_ada = _ada_bwd(c_all, dmod_cols)
    big["w_ada"] = _reduce_adamw(g_ada[None], w_ada[0], m_w_ada[0], v_w_ada[0], "adamw_w_ada", sm[0])
    part_in, = _scatter_wait2(second_in, big["w_ada"][1], "scatter_wait2_w_in")
    big["w_in"] = _reduce_adamw(part_in, w_in[0], m_w_in[0], v_w_in[0], "adamw_w_in", tok)

    names = ["w_ada", "b_ada", "norm1_g", "w_in", "v_norm_g", "w_spatial", "b_spatial", "out_norm_g", "w_out",
             "norm2_g", "w_gate", "w_up", "w_down", "final_g"]
    small_at = {"norm1_g": 0, "v_norm_g": 1, "w_spatial": 2, "b_spatial": 3, "out_norm_g": 4, "norm2_g": 5,
                "final_g": 6, "b_ada": 7}
    outs = [[], [], [], []]
    for nm in names:
        for k in range(4):
            if nm in big:
                res = big[nm][k][None]
                outs[k].append(tr_(res) if nm in ("w_gate", "w_up") else res)
            else:
                outs[k].append((sm_g, sm_d, sm_m, sm_v)[k][small_at[nm]])
    return (loss, grad_x[None], *outs[0], *outs[1], *outs[2], *outs[3])

    hid = pl.BlockSpec((None, s, ffb), lambda j: (j, 0, 0))
    all_d = pl.BlockSpec((s, d), lambda j: (0, 0))
    col_sds = jax.ShapeDtypeStruct((NDEV, d, ffb), BF16)
    col_out = pl.BlockSpec((None, d, ffb), lambda j: (j, 0, 0))
    scatters = {}
    gw_gate = _mm(h2, dg, all_d, hid, col_sds, col_out, (NDEV,), TN, False, "grad_w_gate")
    scatters["w_gate"], tok = _exchange_start(gw_gate, True, tok, "scatter_start_w_gate")
    gw_up = _mm(h2, du, all_d, hid, col_sds, col_out, (NDEV,), TN, False, "grad_w_up", tok)
    scatters["w_up"], tok = _exchange_start(gw_up, True, tok, "scatter_start_w_up")
    gw_down = _mm(act, df, hid, all_d, jax.ShapeDtypeStruct((NDEV, ffb, d), BF16),
                  pl.BlockSpec((None, ffb, d), lambda j: (j, 0, 0)), (NDEV,), TN, False, "grad_w_down", tok)
    scatters["w_down"], tok = _exchange_start(gw_down, True, tok, "scatter_start_w_down")
    dx1, dp, st2 = _prenorm_bwd(x1, dh2, dx2, f_out, norm2_g, scale2, gate1, "prenorm2_bwd", tok)
    don = _mm(dp, w_out_all, rows_1, whole_1, jax.ShapeDtypeStruct((s, d), F32), rows_1, (nt,), NT, False,
              "out_proj_bwd")
    gw_out = _mm(o_n, dp, pl.BlockSpec((s, ob), lambda j: (0, j)), all_d,
                 jax.ShapeDtypeStruct((NDEV, ob, d), BF16), pl.BlockSpec((None, ob, d), lambda j: (j, 0, 0)),
                 (NDEV,), TN, False, "grad_w_out", don)
    scatters["w_out"], tok = _exchange_start(gw_out, True, tok, "scatter_start_w_out")
    dproj_gm, dw_sp, db_cols, dgv, dgam_gm = _gmlp_bwd(proj, don, v_norm_g, w_spatial[0], b_cols, out_norm_g, gm, tok)
    dq, dk, dv, dgam_sb = _sb_bwd(proj, o_sb, mtot, don, out_norm_g, gm, sb)
    dproj = jnp.concatenate([dproj_gm, dq, dk, dv], axis=1)
    dh1 = _mm(dproj, w_in_all, pl.BlockSpec((tm, n_in), lambda i, j: (i, j)),
              pl.BlockSpec((None, d, n_in), lambda i, j: (j, 0, 0)), jax.ShapeDtypeStruct((s, d), F32), rows_d,
              (nt, NDEV), NT, True, "in_proj_bwd")
    grad_x, _, st1 = _prenorm_bwd(xs, dh1, dx1, p_out, norm1_g, scale1, gate1, "prenorm1_bwd", tok)

    dmod = jnp.concatenate([st1[0], st1[1], st1[3], st2[0], st2[1], st2[3]])
    loss_row = jnp.pad((0.5 * jnp.sum(st_f[1]) / d).reshape(1, 1), ((0, 0), (0, GRP - 1)))
    small = [st1[2], dgv, dw_sp, db_cols[:, :b_spatial.shape[1]].T, jnp.concatenate([dgam_gm, dgam_sb], axis=1),
             st2[2], st_f[0], dmod, loss_row]
    zero_row = jnp.zeros((1, GRP), F32)
    small_w = [norm1_g, v_norm_g, w_spatial, b_spatial, out_norm_g, norm2_g, final_g, b_ada, zero_row]
    small_m = [m_norm1_g, m_v_norm_g, m_w_spatial, m_b_spatial, m_out_norm_g, m_norm2_g, m_final_g, m_b_ada, zero_row]
    small_v = [v_norm1_g, v_v_norm_g, v_w_spatial, v_b_spatial, v_out_norm_g, v_norm2_g, v_final_g, v_b_ada, zero_row]
    small_handles, tok = _exchange_start(_pack(small), False, tok, "gather_small_start")
    gw_in = _mm(h1, dproj, all_d, pl.BlockSpec((s, n_in), lambda j: (0, j)),
                jax.ShapeDtypeStruct((NDEV, d, n_in), BF16), pl.BlockSpec((None, d, n_in), lambda j: (j, 0, 0)),
                (NDEV,), TN, False, "grad_w_in", tok)
    scatters["w_in"], tok = _exchange_start(gw_in, True, tok, "scatter_start_w_in")
    small_all = _exchange_wait(small_handles, tok, "gather_small_wait")
    sm = _reduce_adamw(small_all, _pack(small_w), _pack(small_m), _pack(small_v), "adamw_small", tok)
    shapes = [a.shape for a in small_w]
    sm_g, sm_d, sm_m, sm_v = [_unpack(t, shapes) for t in sm]

    loss = sm_g[8][0, 0]
    at = sum(a.size for a in small_w[:7]) // GRP
    dmod_all = small_all[:, at:at + 6 * d // GRP, :].reshape(NDEV, 6 * d)
    n_ada = w_ada.shape[2]
    dmod_cols = lax.dynamic_slice(dmod_all, (0, me * n_ada), (NDEV, n_ada))
    g_ada = _ada_bwd(c_all, dmod_cols)
    big = {"w_ada": _reduce_adamw(g_ada[None], w_ada[0], m_w_ada[0], v_w_ada[0], "adamw_w_ada", sm[0])}

    prev = big["w_ada"][1]
    for nm, w, m, v in (("w_gate", w_gate, m_w_gate, v_w_gate), ("w_up", w_up, m_w_up, v_w_up),
                        ("w_down", w_down, m_w_down, v_w_down), ("w_out", w_out, m_w_out, v_w_out),
                        ("w_in", w_in, m_w_in, v_w_in)):
        parts = _exchange_wait(scatters[nm], prev, "scatter_wait_" + nm)
        big[nm] = _reduce_adamw(parts, w[0], m[0], v[0], "adamw_" + nm, tok)
        prev = big[nm][1]

    names = ["w_ada", "b_ada", "norm1_g", "w_in", "v_norm_g", "w_spatial", "b_spatial", "out_norm_g", "w_out",
             "norm2_g", "w_gate", "w_up", "w_down", "final_g"]
    small_at = {"norm1_g": 0, "v_norm_g": 1, "w_spatial": 2, "b_spatial": 3, "out_norm_g": 4, "norm2_g": 5,
                "final_g": 6, "b_ada": 7}
    outs = [[], [], [], []]
    for nm in names:
        for k in range(4):
            outs[k].append(big[nm][k][None] if nm in big else (sm_g, sm_d, sm_m, sm_v)[k][small_at[nm]])
    return (loss, grad_x[None], *outs[0], *outs[1], *outs[2], *outs[3])
```

```python
import functools

import jax
import jax.numpy as jnp
from jax import lax
from jax.experimental import pallas as pl
from jax.experimental.pallas import tpu as pltpu

F32, BF16 = jnp.float32, jnp.bfloat16
NDEV = 8
GRP = 128
EPS = 1e-6
VMEM_BYTES = 64 * 2 ** 20
VMEM_LIMIT = VMEM_BYTES - 8 * 2 ** 20
ADAM_LR, ADAM_B1, ADAM_B2, ADAM_EPS, ADAM_WD, ADAM_STEP = 0.001, 0.9, 0.999, 1e-08, 0.01, 10
MESH = pl.DeviceIdType.MESH
NN = (((1,), (0,)), ((), ()))
NT = (((1,), (1,)), ((), ()))
TN = (((0,), (0,)), ((), ()))


def _params(n_axes):
    return pltpu.CompilerParams(dimension_semantics=("arbitrary",) * n_axes, vmem_limit_bytes=VMEM_LIMIT)


def _tile(n, cap, mult):
    best = None
    for t in range(mult, min(n, cap) + 1, mult):
        if n % t == 0:
            best = t
    assert best is not None, (n, cap, mult)
    return best


def _dot(a, b, dims):
    return lax.dot_general(a, b, dims, preferred_element_type=F32)


def _exchange(src, per_peer, name):
    blk = src.shape[1:] if per_peer else src.shape

    def body(src_ref, out_ref, send_sems, recv_sems, local_sem):
        x, y, c = lax.axis_index("x"), lax.axis_index("y"), lax.axis_index("c")
        me = 4 * x + 2 * y + c
        local = pltpu.make_async_copy(src_ref.at[me] if per_peer else src_ref, out_ref.at[me], local_sem)
        local.start()
        sends, recvs = [], []
        for k in range(1, NDEV):
            px, py, pc = x ^ (k >> 2), y ^ ((k >> 1) & 1), c ^ (k & 1)
            p = 4 * px + 2 * py + pc
            mine = src_ref.at[p] if per_peer else src_ref
            sends.append(pltpu.make_async_remote_copy(
                src_ref=mine, dst_ref=out_ref.at[me], send_sem=send_sems.at[k - 1], recv_sem=recv_sems.at[k - 1],
                device_id=(px, py, pc), device_id_type=MESH))
            recvs.append(pltpu.make_async_remote_copy(
                src_ref=mine, dst_ref=out_ref.at[p], send_sem=send_sems.at[k - 1], recv_sem=recv_sems.at[k - 1],
                device_id=(px, py, pc), device_id_type=MESH))
        for cp in sends:
            cp.start()
        for cp in recvs:
            cp.wait_recv()
        for cp in sends:
            cp.wait_send()
        local.wait()

    return pl.pallas_call(
        body, name=name,
        out_shape=jax.ShapeDtypeStruct((NDEV,) + tuple(blk), src.dtype),
        in_specs=[pl.BlockSpec(memory_space=pl.ANY)],
        out_specs=pl.BlockSpec(memory_space=pl.ANY),
        scratch_shapes=[pltpu.SemaphoreType.DMA((NDEV - 1,)), pltpu.SemaphoreType.DMA((NDEV - 1,)),
                        pltpu.SemaphoreType.DMA],
    )(src)


_HBM = pl.BlockSpec(memory_space=pltpu.HBM)
_SEM = pl.BlockSpec(memory_space=pltpu.SEMAPHORE)
_ANY = pl.BlockSpec(memory_space=pl.ANY)
_EFFECT = pltpu.SideEffectType.DATAFLOW_SIDE_EFFECTING


def _peers():
    x, y, c = lax.axis_index("x"), lax.axis_index("y"), lax.axis_index("c")
    out = []
    for k in range(1, NDEV):
        px, py, pc = x ^ (k >> 2), y ^ ((k >> 1) & 1), c ^ (k & 1)
        out.append((k, (px, py, pc), 4 * px + 2 * py + pc))
    return 4 * x + 2 * y + c, out


def _exchange_start(src, per_peer, after, name):
    blk = src.shape[1:] if per_peer else src.shape
    me = 4 * lax.axis_index("x") + 2 * lax.axis_index("y") + lax.axis_index("c")
    own = lax.dynamic_index_in_dim(src, me, 0, keepdims=True) if per_peer else src[None]
    land = lax.dynamic_update_slice(lax.empty((NDEV,) + tuple(blk), src.dtype), own, (me,) + (0,) * len(blk))

    def body(src_ref, land_ref, after_ref, send_sems, recv_sems, src_thru, land_thru, token):
        my, peers = _peers()
        for k, coords, p in peers:
            pltpu.make_async_remote_copy(
                src_ref=src_ref.at[p] if per_peer else src_ref, dst_ref=land_ref.at[my],
                send_sem=send_sems.at[k - 1], recv_sem=recv_sems.at[k - 1], device_id=coords, device_id_type=MESH).start()
        token[...] = jnp.zeros_like(token)

    res = pl.pallas_call(
        body, name=name,
        out_shape=(pltpu.SemaphoreType.DMA((NDEV - 1,)), pltpu.SemaphoreType.DMA((NDEV - 1,)),
                   pltpu.HBM(src.shape, src.dtype), pltpu.HBM(land.shape, land.dtype), jax.ShapeDtypeStruct((8, GRP), F32)),
        in_specs=(_HBM, _HBM, _ANY), out_specs=(_SEM, _SEM, _HBM, _HBM, pl.BlockSpec(memory_space=pltpu.VMEM)),
        input_output_aliases={0: 2, 1: 3}, compiler_params=pltpu.CompilerParams(has_side_effects=_EFFECT),
    )(pltpu.with_memory_space_constraint(src, pltpu.HBM), pltpu.with_memory_space_constraint(land, pltpu.HBM), after)
    return (per_peer,) + tuple(res[:4]), res[4]


def _exchange_wait(handles, after, name):
    per_peer, send_sems, recv_sems, src_thru, land_thru = handles

    def body(src_ref, land_ref, send_sems, recv_sems, after_ref, src_dead, got_ref):
        _, peers = _peers()
        for k, coords, p in peers:
            cp = pltpu.make_async_remote_copy(
                src_ref=src_ref.at[p] if per_peer else src_ref, dst_ref=land_ref.at[p],
                send_sem=send_sems.at[k - 1], recv_sem=recv_sems.at[k - 1], device_id=coords, device_id_type=MESH)
            cp.wait_send()
            cp.wait_recv()

    return pl.pallas_call(
        body, name=name,
        out_shape=(pltpu.HBM(src_thru.shape, src_thru.dtype), pltpu.HBM(land_thru.shape, land_thru.dtype)),
        in_specs=(_HBM, _HBM, _SEM, _SEM, _ANY), out_specs=(_HBM, _HBM), input_output_aliases={0: 0, 1: 1},
        compiler_params=pltpu.CompilerParams(has_side_effects=_EFFECT),
    )(src_thru, land_thru, send_sems, recv_sems, after)[1]


def _chip_peers():
    x, y, c = lax.axis_index("x"), lax.axis_index("y"), lax.axis_index("c")
    chips = [(x, 1 - y), (1 - x, y), (1 - x, 1 - y)]
    return 4 * x + 2 * y + c, (x, y, 1 - c), [((px, py, c), 4 * px + 2 * py + c) for px, py in chips]


def _gather_start(src, after, name):
    me = 4 * lax.axis_index("x") + 2 * lax.axis_index("y") + lax.axis_index("c")
    land = lax.dynamic_update_slice(lax.empty((NDEV,) + src.shape, src.dtype), src[None], (me,) + (0,) * src.ndim)

    def body(src_ref, land_ref, after_ref, send_sems, recv_sems, src_thru, land_thru, token):
        my, sibling, chips = _chip_peers()
        for k, to in enumerate([sibling] + [coords for coords, _ in chips]):
            pltpu.make_async_remote_copy(src_ref=src_ref, dst_ref=land_ref.at[my], send_sem=send_sems.at[k],
                                         recv_sem=recv_sems.at[k], device_id=to, device_id_type=MESH).start()
        token[...] = jnp.zeros_like(token)

    res = pl.pallas_call(
        body, name=name,
        out_shape=(pltpu.SemaphoreType.DMA((4,)), pltpu.SemaphoreType.DMA((4,)), pltpu.HBM(src.shape, src.dtype),
                   pltpu.HBM(land.shape, land.dtype), jax.ShapeDtypeStruct((8, GRP), F32)),
        in_specs=(_HBM, _HBM, _ANY), out_specs=(_SEM, _SEM, _HBM, _HBM, pl.BlockSpec(memory_space=pltpu.VMEM)),
        input_output_aliases={0: 2, 1: 3}, compiler_params=pltpu.CompilerParams(has_side_effects=_EFFECT),
    )(pltpu.with_memory_space_constraint(src, pltpu.HBM), pltpu.with_memory_space_constraint(land, pltpu.HBM), after)
    return tuple(res[:4]), res[4]


def _gather_forward(handles, after, name):
    send_sems, recv_sems, src_thru, land_thru = handles

    def body(src_ref, land_ref, send_sems, recv_sems, after_ref, send2, recv2, land_out, token):
        my, sibling, chips = _chip_peers()
        for k, (to, p) in enumerate([(sibling, my ^ 1)] + chips):
            first = pltpu.make_async_remote_copy(src_ref=src_ref, dst_ref=land_ref.at[p], send_sem=send_sems.at[k],
                                                 recv_sem=recv_sems.at[k], device_id=to, device_id_type=MESH)
            first.wait_send()
            first.wait_recv()
        for k, (_, p) in enumerate(chips):
            pltpu.make_async_remote_copy(src_ref=land_ref.at[p], dst_ref=land_ref.at[p], send_sem=send2.at[k],
                                         recv_sem=recv2.at[k], device_id=sibling, device_id_type=MESH).start()
        token[...] = jnp.zeros_like(token)

    res = pl.pallas_call(
        body, name=name,
        out_shape=(pltpu.SemaphoreType.DMA((3,)), pltpu.SemaphoreType.DMA((3,)), pltpu.HBM(land_thru.shape, land_thru.dtype),
                   jax.ShapeDtypeStruct((8, GRP), F32)),
        in_specs=(_HBM, _HBM, _SEM, _SEM, _ANY), out_specs=(_SEM, _SEM, _HBM, pl.BlockSpec(memory_space=pltpu.VMEM)),
        input_output_aliases={1: 2}, compiler_params=pltpu.CompilerParams(has_side_effects=_EFFECT),
    )(src_thru, land_thru, send_sems, recv_sems, after)
    return tuple(res[:3]), res[3]


def _gather_wait(handles, after, name):
    send2, recv2, land_thru = handles

    def body(land_ref, send2, recv2, after_ref, got_ref):
        _, sibling, chips = _chip_peers()
        for k, (_, p) in enumerate(chips):
            cp = pltpu.make_async_remote_copy(src_ref=land_ref.at[p], dst_ref=land_ref.at[p ^ 1], send_sem=send2.at[k],
                                              recv_sem=recv2.at[k], device_id=sibling, device_id_type=MESH)
            cp.wait_send()
            cp.wait_recv()

    return pl.pallas_call(
        body, name=name, out_shape=pltpu.HBM(land_thru.shape, land_thru.dtype),
        in_specs=(_HBM, _SEM, _SEM, _ANY), out_specs=_HBM, input_output_aliases={0: 0},
        compiler_params=pltpu.CompilerParams(has_side_effects=_EFFECT),
    )(land_thru, send2, recv2, after)


NCHIP = NDEV // 2


def _scatter_start1(parts, after, name):
    n = len(parts)
    lands = [lax.empty((NCHIP,) + p.shape[1:], p.dtype) for p in parts]

    def body(*refs):
        part_refs, land_refs = refs[:n], refs[n:2 * n]
        send_sems, recv_sems, token = refs[2 * n + 1], refs[2 * n + 2], refs[-1]
        _, sibling, _ = _chip_peers()
        c = lax.axis_index("c")
        for a in range(n):
            for q in range(NCHIP):
                pltpu.make_async_remote_copy(
                    src_ref=part_refs[a].at[2 * q + 1 - c], dst_ref=land_refs[a].at[q], send_sem=send_sems.at[a * NCHIP + q],
                    recv_sem=recv_sems.at[a * NCHIP + q], device_id=sibling, device_id_type=MESH).start()
        token[...] = jnp.zeros_like(token)

    hbm = lambda arrs: tuple(pltpu.HBM(a.shape, a.dtype) for a in arrs)
    res = pl.pallas_call(
        body, name=name,
        out_shape=(pltpu.SemaphoreType.DMA((n * NCHIP,)), pltpu.SemaphoreType.DMA((n * NCHIP,))) + hbm(parts) + hbm(lands)
        + (jax.ShapeDtypeStruct((8, GRP), F32),),
        in_specs=(_HBM,) * (2 * n) + (_ANY,),
        out_specs=(_SEM, _SEM) + (_HBM,) * (2 * n) + (pl.BlockSpec(memory_space=pltpu.VMEM),),
        input_output_aliases={k: 2 + k for k in range(2 * n)}, compiler_params=pltpu.CompilerParams(has_side_effects=_EFFECT),
    )(*[pltpu.with_memory_space_constraint(a, pltpu.HBM) for a in list(parts) + lands], after)
    return (n,) + tuple(res[:-1]), res[-1]


def _scatter_wait1(handles, after, name):
    n, send_sems, recv_sems = handles[:3]
    thru = handles[3:]

    def body(*refs):
        part_refs, land_refs, send_sems, recv_sems = refs[:n], refs[n:2 * n], refs[2 * n], refs[2 * n + 1]
        _, sibling, _ = _chip_peers()
        for a in range(n):
            for q in range(NCHIP):
                cp = pltpu.make_async_remote_copy(
                    src_ref=part_refs[a].at[q], dst_ref=land_refs[a].at[q], send_sem=send_sems.at[a * NCHIP + q],
                    recv_sem=recv_sems.at[a * NCHIP + q], device_id=sibling, device_id_type=MESH)
                cp.wait_send()
                cp.wait_recv()

    res = pl.pallas_call(
        body, name=name, out_shape=tuple(pltpu.HBM(a.shape, a.dtype) for a in thru),
        in_specs=(_HBM,) * (2 * n) + (_SEM, _SEM, _ANY), out_specs=(_HBM,) * (2 * n),
        input_output_aliases={k: k for k in range(2 * n)}, compiler_params=pltpu.CompilerParams(has_side_effects=_EFFECT),
    )(*thru, send_sems, recv_sems, after)
    return res[:n], res[n:]


def _chip_sum(part, land, name):
    _, r, c = part.shape
    tr = _tile(r, max(16, 2 ** 21 // c), 16)

    def body(core_ref, p_ref, l_ref, o_ref):
        o_ref[...] = (p_ref[...].astype(F32) + l_ref[...].astype(F32)).astype(o_ref.dtype)

    blk = pl.BlockSpec((None, tr, c), lambda q, i, core: (q, i, 0))
    return pl.pallas_call(
        body, name=name, out_shape=jax.ShapeDtypeStruct(land.shape, land.dtype),
        grid_spec=pltpu.PrefetchScalarGridSpec(
            num_scalar_prefetch=1, grid=(NCHIP, r // tr),
            in_specs=[pl.BlockSpec((None, None, tr, c), lambda q, i, core: (q, core[0], i, 0)), blk], out_specs=blk),
        compiler_params=_params(2),
    )(lax.axis_index("c").astype(jnp.int32).reshape(1), part.reshape(NCHIP, 2, r, c), land)


def _scatter_start2(sums, after, name):
    n = len(sums)
    chip = 2 * lax.axis_index("x") + lax.axis_index("y")
    lands = [lax.dynamic_update_slice(lax.empty(s_.shape, s_.dtype), lax.dynamic_index_in_dim(s_, chip, 0, keepdims=True),
                                      (chip,) + (0,) * (s_.ndim - 1)) for s_ in sums]

    def body(*refs):
        sum_refs, land_refs = refs[:n], refs[n:2 * n]
        send_sems, recv_sems, token = refs[2 * n + 1], refs[2 * n + 2], refs[-1]
        my, _, chips = _chip_peers()
        for a in range(n):
            for k, (to, p) in enumerate(chips):
                pltpu.make_async_remote_copy(
                    src_ref=sum_refs[a].at[p // 2], dst_ref=land_refs[a].at[my // 2], send_sem=send_sems.at[a * 3 + k],
                    recv_sem=recv_sems.at[a * 3 + k], device_id=to, device_id_type=MESH).start()
        token[...] = jnp.zeros_like(token)

    hbm = lambda arrs: tuple(pltpu.HBM(a.shape, a.dtype) for a in arrs)
    res = pl.pallas_call(
        body, name=name,
        out_shape=(pltpu.SemaphoreType.DMA((n * 3,)), pltpu.SemaphoreType.DMA((n * 3,))) + hbm(sums) + hbm(lands)
        + (jax.ShapeDtypeStruct((8, GRP), F32),),
        in_specs=(_HBM,) * (2 * n) + (_ANY,),
        out_specs=(_SEM, _SEM) + (_HBM,) * (2 * n) + (pl.BlockSpec(memory_space=pltpu.VMEM),),
        input_output_aliases={k: 2 + k for k in range(2 * n)}, compiler_params=pltpu.CompilerParams(has_side_effects=_EFFECT),
    )(*[pltpu.with_memory_space_constraint(a, pltpu.HBM) for a in list(sums) + lands], after)
    return (n,) + tuple(res[:-1]), res[-1]


def _scatter_wait2(handles, after, name):
    n, send_sems, recv_sems = handles[:3]
    thru = handles[3:]

    def body(*refs):
        sum_refs, land_refs, send_sems, recv_sems = refs[:n], refs[n:2 * n], refs[2 * n], refs[2 * n + 1]
        _, _, chips = _chip_peers()
        for a in range(n):
            for k, (to, p) in enumerate(chips):
                cp = pltpu.make_async_remote_copy(
                    src_ref=sum_refs[a].at[p // 2], dst_ref=land_refs[a].at[p // 2], send_sem=send_sems.at[a * 3 + k],
                    recv_sem=recv_sems.at[a * 3 + k], device_id=to, device_id_type=MESH)
                cp.wait_send()
                cp.wait_recv()

    res = pl.pallas_call(
        body, name=name, out_shape=tuple(pltpu.HBM(a.shape, a.dtype) for a in thru),
        in_specs=(_HBM,) * (2 * n) + (_SEM, _SEM, _ANY), out_specs=(_HBM,) * (2 * n),
        input_output_aliases={k: k for k in range(2 * n)}, compiler_params=pltpu.CompilerParams(has_side_effects=_EFFECT),
    )(*thru, send_sems, recv_sems, after)
    return res[n:]


def _mm(a, b, a_spec, b_spec, out_sds, o_spec, grid, dims, acc, name, after=None):
    nk = grid[-1]
    extra = () if after is None else (after,)

    def body(a_ref, b_ref, *rest):
        o_ref, *scratch = rest[len(extra):]
        prod = _dot(a_ref[...], b_ref[...], dims)
        if not acc:
            o_ref[...] = prod.astype(o_ref.dtype)
            return
        acc_ref = scratch[0]
        k = pl.program_id(1)

        @pl.when(k == 0)
        def _():
            acc_ref[...] = prod

        @pl.when(k > 0)
        def _():
            acc_ref[...] += prod

        @pl.when(k == nk - 1)
        def _():
            o_ref[...] = acc_ref[...].astype(o_ref.dtype)

    blk = tuple(d for d in o_spec.block_shape if d is not None)
    return pl.pallas_call(
        body, name=name, grid=grid, out_shape=out_sds, in_specs=[a_spec, b_spec] + [_ANY] * len(extra), out_specs=o_spec,
        scratch_shapes=[pltpu.VMEM(blk, F32)] if acc else [],
        compiler_params=_params(len(grid)),
    )(a, b, *extra)


def _row_spec(tm, d):
    return pl.BlockSpec((tm, d), lambda i: (i, 0))


def _vec_spec(d):
    return pl.BlockSpec((1, d), lambda i: (0, 0))


def _prenorm_fwd(x, g, scale, shift, name):
    s, d = x.shape
    tm = _tile(s, 256, 16)

    def body(x_ref, g_ref, sc_ref, sh_ref, h_ref):
        xv = x_ref[...]
        rstd = lax.rsqrt(jnp.mean(xv * xv, axis=-1, keepdims=True) + EPS)
        h_ref[...] = ((xv * rstd * g_ref[...]) * (1.0 + sc_ref[...]) + sh_ref[...]).astype(BF16)

    return pl.pallas_call(
        body, name=name, grid=(s // tm,), out_shape=jax.ShapeDtypeStruct((s, d), BF16),
        in_specs=[_row_spec(tm, d), _vec_spec(d), _vec_spec(d), _vec_spec(d)], out_specs=_row_spec(tm, d),
        compiler_params=_params(1),
    )(x, g, scale, shift)


def _residual_prenorm(x, gate, p, g, scale, shift, name):
    s, d = x.shape
    tm = _tile(s, 256, 16)

    def body(x_ref, gate_ref, p_ref, g_ref, sc_ref, sh_ref, x1_ref, h_ref):
        xv = x_ref[...] + gate_ref[...] * p_ref[...]
        x1_ref[...] = xv
        rstd = lax.rsqrt(jnp.mean(xv * xv, axis=-1, keepdims=True) + EPS)
        h_ref[...] = ((xv * rstd * g_ref[...]) * (1.0 + sc_ref[...]) + sh_ref[...]).astype(BF16)

    return pl.pallas_call(
        body, name=name, grid=(s // tm,),
        out_shape=(jax.ShapeDtypeStruct((s, d), F32), jax.ShapeDtypeStruct((s, d), BF16)),
        in_specs=[_row_spec(tm, d), _vec_spec(d), _row_spec(tm, d), _vec_spec(d), _vec_spec(d), _vec_spec(d)],
        out_specs=(_row_spec(tm, d), _row_spec(tm, d)), compiler_params=_params(1),
    )(x, gate, p, g, scale, shift)


def _final_loss(x1, f, final_g, target, gate2):
    s, d = x1.shape
    tm = _tile(s, 256, 16)

    def body(x_ref, f_ref, g_ref, t_ref, gate_ref, dx_ref, df_ref, st_ref):
        @pl.when(pl.program_id(0) == 0)
        def _():
            st_ref[...] = jnp.zeros_like(st_ref)

        xv, gf = x_ref[...] + gate_ref[...] * f_ref[...], g_ref[...]
        rstd = lax.rsqrt(jnp.mean(xv * xv, axis=-1, keepdims=True) + EPS)
        xhat = xv * rstd
        err = xhat * gf - t_ref[...]
        dy = err * (1.0 / d)
        gdy = dy * gf
        dx = rstd * (gdy - xhat * jnp.mean(gdy * xhat, axis=-1, keepdims=True))
        dx_ref[...] = dx
        df_ref[...] = (gate_ref[...] * dx).astype(BF16)
        st_ref[0:1, :] += jnp.sum(dy * xhat, axis=0, keepdims=True)
        st_ref[1:2, :] += jnp.sum(err * err, axis=0, keepdims=True)

    return pl.pallas_call(
        body, name="final_loss", grid=(s // tm,),
        out_shape=(jax.ShapeDtypeStruct((s, d), F32), jax.ShapeDtypeStruct((s, d), BF16),
                   jax.ShapeDtypeStruct((8, d), F32)),
        in_specs=[_row_spec(tm, d), _row_spec(tm, d), _vec_spec(d), _row_spec(tm, d), _vec_spec(d)],
        out_specs=(_row_spec(tm, d), _row_spec(tm, d), pl.BlockSpec((8, d), lambda i: (0, 0))),
        compiler_params=_params(1),
    )(x1, f, final_g, target, gate2)


def _prenorm_bwd(xin, dh, dres, pf, g, scale, gate_next, name, after):
    s, d = xin.shape
    tm = _tile(s, 256, 16)

    def body(x_ref, dh_ref, dr_ref, pf_ref, g_ref, sc_ref, gn_ref, after_ref, dx_ref, dn_ref, st_ref):
        @pl.when(pl.program_id(0) == 0)
        def _():
            st_ref[...] = jnp.zeros_like(st_ref)

        xv, dhv, drv, gv = x_ref[...], dh_ref[...], dr_ref[...], g_ref[...]
        one_sc = 1.0 + sc_ref[...]
        rstd = lax.rsqrt(jnp.mean(xv * xv, axis=-1, keepdims=True) + EPS)
        xhat = xv * rstd
        dxhat = dhv * (gv * one_sc)
        dx = drv + rstd * (dxhat - xhat * jnp.mean(dxhat * xhat, axis=-1, keepdims=True))
        dx_ref[...] = dx
        dn_ref[...] = (gn_ref[...] * dx).astype(BF16)
        dhx = dhv * xhat
        st_ref[0:1, :] += jnp.sum(dhv, axis=0, keepdims=True)
        st_ref[1:2, :] += jnp.sum(dhx, axis=0, keepdims=True) * gv
        st_ref[2:3, :] += jnp.sum(dhx, axis=0, keepdims=True) * one_sc
        st_ref[3:4, :] += jnp.sum(drv * pf_ref[...], axis=0, keepdims=True)

    return pl.pallas_call(
        body, name=name, grid=(s // tm,),
        out_shape=(jax.ShapeDtypeStruct((s, d), F32), jax.ShapeDtypeStruct((s, d), BF16),
                   jax.ShapeDtypeStruct((8, d), F32)),
        in_specs=[_row_spec(tm, d)] * 4 + [_vec_spec(d)] * 3 + [_ANY],
        out_specs=(_row_spec(tm, d), _row_spec(tm, d), pl.BlockSpec((8, d), lambda i: (0, 0))),
        compiler_params=_params(1),
    )(xin, dh, dres, pf, g, scale, gate_next, after)


def _ada_fwd(c_all, w_loc):
    nb, d = c_all.shape
    n = w_loc.shape[1]
    tn = _tile(n, 512, 128) if n % 128 == 0 else n

    def body(c_ref, w_ref, o_ref):
        cv = c_ref[...]
        o_ref[...] = jnp.dot(cv * jax.nn.sigmoid(cv), w_ref[...], preferred_element_type=F32,
                             precision=lax.Precision.HIGHEST)

    return pl.pallas_call(
        body, name="ada_fwd", grid=(n // tn,), out_shape=jax.ShapeDtypeStruct((nb, n), F32),
        in_specs=[pl.BlockSpec((nb, d), lambda j: (0, 0)), pl.BlockSpec((d, tn), lambda j: (0, j))],
        out_specs=pl.BlockSpec((nb, tn), lambda j: (0, j)), compiler_params=_params(1),
    )(c_all, w_loc)


def _ada_bwd(c_all, dmod_cols):
    nb, d = c_all.shape
    n = dmod_cols.shape[1]
    tn = _tile(n, 512, 128) if n % 128 == 0 else n

    def body(c_ref, dm_ref, o_ref):
        cv = c_ref[...]
        o_ref[...] = lax.dot_general(cv * jax.nn.sigmoid(cv), dm_ref[...], TN, preferred_element_type=F32,
                                     precision=lax.Precision.HIGHEST)

    return pl.pallas_call(
        body, name="ada_bwd", grid=(n // tn,), out_shape=jax.ShapeDtypeStruct((d, n), F32),
        in_specs=[pl.BlockSpec((nb, d), lambda j: (0, 0)), pl.BlockSpec((nb, tn), lambda j: (0, j))],
        out_specs=pl.BlockSpec((d, tn), lambda j: (0, j)), compiler_params=_params(1),
    )(c_all, dmod_cols)


_INV_SQRT2 = 0.7071067811865476
_INV_SQRT2PI = 0.3989422804014327


def _gelu(x):
    return 0.5 * x * (1.0 + lax.erf(x * _INV_SQRT2))


def _gelu_grad(x):
    return 0.5 * (1.0 + lax.erf(x * _INV_SQRT2)) + x * jnp.exp(-0.5 * x * x) * _INV_SQRT2PI


def _gm_group_fwd(up, vp, gv, wt, bcol):
    u = _gelu(up)
    va = _gelu(vp)
    xc = va - jnp.mean(va, axis=-1, keepdims=True)
    rstd_v = lax.rsqrt(jnp.mean(xc * xc, axis=-1, keepdims=True) + EPS)
    yv = xc * rstd_v
    vn = (yv * gv).astype(BF16)
    mixed = _dot(wt, vn, NN) + bcol
    return u, rstd_v, yv, vn, mixed, u * mixed


def _tril_bf16(w):
    row = lax.broadcasted_iota(jnp.int32, w.shape, 0)
    col = lax.broadcasted_iota(jnp.int32, w.shape, 1)
    return jnp.where(col <= row, w, 0.0).astype(BF16)


def _gmlp_fwd(proj, v_norm_g, w_spatial, b_cols, gamma, gm, after):
    s = proj.shape[0]
    ng = gm // GRP

    def body(p_ref, gv_ref, w_ref, b_ref, gam_ref, after_ref, o_ref):
        for g in range(ng):
            lo = g * GRP
            wt = _tril_bf16(w_ref[g])
            *_, o = _gm_group_fwd(p_ref[:, lo:lo + GRP], p_ref[:, gm + lo:gm + lo + GRP], gv_ref[:, lo:lo + GRP],
                                  wt, b_ref[:, g:g + 1])
            rstd_o = lax.rsqrt(jnp.mean(o * o, axis=-1, keepdims=True) + EPS)
            o_ref[:, lo:lo + GRP] = (o * rstd_o * gam_ref[:, lo:lo + GRP]).astype(BF16)

    return pl.pallas_call(
        body, name="gmlp_fwd", grid=(s // GRP,), out_shape=jax.ShapeDtypeStruct((s, gm), BF16),
        in_specs=[pl.BlockSpec((GRP, 2 * gm), lambda n: (n, 0)), _vec_spec(gm),
                  pl.BlockSpec((ng, GRP, GRP), lambda n: (0, 0, 0)), pl.BlockSpec((GRP, GRP), lambda n: (0, 0)),
                  _vec_spec(gm), _ANY],
        out_specs=pl.BlockSpec((GRP, gm), lambda n: (n, 0)), compiler_params=_params(1),
    )(proj, v_norm_g, w_spatial, b_cols, gamma, after)


def _gmlp_bwd(proj, don, v_norm_g, w_spatial, b_cols, gamma, gm, after):
    s = proj.shape[0]
    ng = gm // GRP

    def body(p_ref, don_ref, gv_ref, w_ref, b_ref, gam_ref, after_ref, dp_ref, dw_ref, db_ref, dgv_ref, dgam_ref):
        @pl.when(pl.program_id(0) == 0)
        def _():
            dw_ref[...] = jnp.zeros_like(dw_ref)
            db_ref[...] = jnp.zeros_like(db_ref)
            dgv_ref[...] = jnp.zeros_like(dgv_ref)
            dgam_ref[...] = jnp.zeros_like(dgam_ref)

        lane = lax.broadcasted_iota(jnp.int32, (GRP, GRP), 1)
        row = lax.broadcasted_iota(jnp.int32, (GRP, GRP), 0)
        for g in range(ng):
            lo = g * GRP
            up, vp = p_ref[:, lo:lo + GRP], p_ref[:, gm + lo:gm + lo + GRP]
            gv, gam = gv_ref[:, lo:lo + GRP], gam_ref[:, lo:lo + GRP]
            wt = _tril_bf16(w_ref[g])
            u, rstd_v, yv, vn, mixed, o = _gm_group_fwd(up, vp, gv, wt, b_ref[:, g:g + 1])
            rstd_o = lax.rsqrt(jnp.mean(o * o, axis=-1, keepdims=True) + EPS)
            ohat = o * rstd_o
            dn = don_ref[:, lo:lo + GRP]
            dgam_ref[:, lo:lo + GRP] += jnp.sum(dn * ohat, axis=0, keepdims=True)
            dohat = dn * gam
            do = rstd_o * (dohat - ohat * jnp.mean(dohat * ohat, axis=-1, keepdims=True))
            du = do * mixed
            dmixed = do * u
            dmb = dmixed.astype(BF16)
            db_ref[...] += jnp.where(lane == g, jnp.sum(dmixed, axis=-1, keepdims=True), 0.0)
            dw_ref[g] += jnp.where(lane <= row, _dot(dmb, vn, NT), 0.0)
            dvn = _dot(wt, dmb, TN)
            dgv_ref[:, lo:lo + GRP] += jnp.sum(dvn * yv, axis=0, keepdims=True)
            dyv = dvn * gv
            dva = rstd_v * (dyv - jnp.mean(dyv, axis=-1, keepdims=True)
                            - yv * jnp.mean(dyv * yv, axis=-1, keepdims=True))
            dp_ref[:, lo:lo + GRP] = (du * _gelu_grad(up)).astype(BF16)
            dp_ref[:, gm + lo:gm + lo + GRP] = (dva * _gelu_grad(vp)).astype(BF16)

    const2 = lambda n: (0, 0)
    return pl.pallas_call(
        body, name="gmlp_bwd", grid=(s // GRP,),
        out_shape=(jax.ShapeDtypeStruct((s, 2 * gm), BF16), jax.ShapeDtypeStruct((ng, GRP, GRP), F32),
                   jax.ShapeDtypeStruct((GRP, GRP), F32), jax.ShapeDtypeStruct((1, gm), F32),
                   jax.ShapeDtypeStruct((1, gm), F32)),
        in_specs=[pl.BlockSpec((GRP, 2 * gm), lambda n: (n, 0)), pl.BlockSpec((GRP, gm), lambda n: (n, 0)),
                  _vec_spec(gm), pl.BlockSpec((ng, GRP, GRP), lambda n: (0, 0, 0)),
                  pl.BlockSpec((GRP, GRP), const2), _vec_spec(gm), _ANY],
        out_specs=(pl.BlockSpec((GRP, 2 * gm), lambda n: (n, 0)), pl.BlockSpec((ng, GRP, GRP), lambda n: (0, 0, 0)),
                   pl.BlockSpec((GRP, GRP), const2), _vec_spec(gm), _vec_spec(gm)),
        compiler_params=_params(1),
    )(proj, don, v_norm_g, w_spatial, b_cols, gamma, after)


BLK = 256


def _log_sigmoid(z):
    return jnp.minimum(z, 0.0) - jnp.log(1.0 + jnp.exp(-jnp.abs(z)))


def _split_dot(x, tri, passes):
    n = x.shape[0]
    parts, rest = [], x
    for _ in range(passes):
        hi = rest.astype(BF16)
        parts.append(hi)
        rest = rest - hi.astype(F32)
    res = _dot(jnp.concatenate(parts, axis=0), tri, NN)
    out = res[0:n]
    for k in range(1, passes):
        out = out + res[k * n:(k + 1) * n]
    return out


HEADS_PER_STEP = 4


def _sb_fwd(proj, gamma, gm, sb, after):
    s = proj.shape[0]
    hp = min(HEADS_PER_STEP, sb // GRP)
    w = hp * GRP
    nhp, nq = sb // w, s // BLK
    qc, kc, vc, gc = 2 * gm // w, (2 * gm + sb) // w, (2 * gm + 2 * sb) // w, gm // w
    scale = GRP ** -0.5

    def body(q_ref, k_ref, v_ref, gam_ref, after_ref, on_ref, o_ref, mt_ref, kb, vb):
        i = pl.program_id(1)

        @pl.when(i == 0)
        def _():
            kb[...] = k_ref[...].astype(BF16)
            vb[...] = v_ref[...].astype(BF16)

        qb = q_ref[...].astype(BF16)
        row = lax.broadcasted_iota(jnp.int32, (BLK, BLK), 0)
        col = lax.broadcasted_iota(jnp.int32, (BLK, BLK), 1)
        later = (row > col).astype(BF16)

        def block(j, carry, diag):
            off = pl.multiple_of(j * BLK, BLK)
            kj, vj = kb[pl.ds(off, BLK), :], vb[pl.ds(off, BLK), :]
            out = []
            for h in range(hp):
                tail, acc = carry[h]
                sl = slice(h * GRP, (h + 1) * GRP)
                z = _dot(qb[:, sl], kj[:, sl], NT) * scale
                lb = _log_sigmoid(z)
                l1m = lb - z
                if diag:
                    l1m = jnp.where(col < row, l1m, 0.0)
                after_s = _split_dot(l1m, later, 2)
                a = jnp.exp(lb + after_s + tail)
                if diag:
                    a = jnp.where(col < row, a, 0.0)
                out.append((tail + after_s[:, 0:1] + l1m[:, 0:1], acc + _dot(a.astype(BF16), vj[:, sl], NN)))
            return tuple(out)

        init = tuple((jnp.zeros((BLK, 1), F32), jnp.zeros((BLK, GRP), F32)) for _ in range(hp))
        carry = block(i, init, True)
        carry = lax.fori_loop(0, i, lambda jj, c: block(i - 1 - jj, c, False), carry)
        for h in range(hp):
            tail, acc = carry[h]
            sl = slice(h * GRP, (h + 1) * GRP)
            rstd = lax.rsqrt(jnp.mean(acc * acc, axis=-1, keepdims=True) + EPS)
            on_ref[:, sl] = (acc * rstd * gam_ref[:, sl]).astype(BF16)
            o_ref[:, sl] = acc
            mt_ref[h] = tail

    return pl.pallas_call(
        body, name="sb_fwd", grid=(nhp, nq),
        out_shape=(jax.ShapeDtypeStruct((s, sb), BF16), jax.ShapeDtypeStruct((s, sb), F32),
                   jax.ShapeDtypeStruct((sb // GRP, s, 1), F32)),
        in_specs=[pl.BlockSpec((BLK, w), lambda h, i: (i, qc + h)), pl.BlockSpec((s, w), lambda h, i: (0, kc + h)),
                  pl.BlockSpec((s, w), lambda h, i: (0, vc + h)), pl.BlockSpec((1, w), lambda h, i: (0, gc + h)), _ANY],
        out_specs=(pl.BlockSpec((BLK, w), lambda h, i: (i, h)), pl.BlockSpec((BLK, w), lambda h, i: (i, h)),
                   pl.BlockSpec((hp, BLK, 1), lambda h, i: (h, i, 0))),
        scratch_shapes=[pltpu.VMEM((s, w), BF16), pltpu.VMEM((s, w), BF16)],
        compiler_params=_params(2),
    )(proj, proj, proj, gamma, after)


def _sb_bwd(proj, o_raw, mtot, don, gamma, gm, sb, after):
    s = proj.shape[0]
    hp = min(HEADS_PER_STEP, sb // GRP)
    w = hp * GRP
    nhp, nq = sb // w, s // BLK
    qc, kc, vc, gc = 2 * gm // w, (2 * gm + sb) // w, (2 * gm + 2 * sb) // w, gm // w
    scale = GRP ** -0.5

    def body(q_ref, k_ref, v_ref, o_ref, mt_ref, don_ref, gam_ref, after_ref, dq_ref, dk_ref, dv_ref, dgam_ref,
             kb, vb, dk_acc, dv_acc):
        i = pl.program_id(1)

        @pl.when(i == 0)
        def _():
            kb[...] = k_ref[...].astype(BF16)
            vb[...] = v_ref[...].astype(BF16)
            dk_acc[...] = jnp.zeros_like(dk_acc)
            dv_acc[...] = jnp.zeros_like(dv_acc)
            dgam_ref[...] = jnp.zeros_like(dgam_ref)

        dobs = []
        for h in range(hp):
            sl = slice(h * GRP, (h + 1) * GRP)
            o, dn = o_ref[:, sl], don_ref[:, sl]
            rstd = lax.rsqrt(jnp.mean(o * o, axis=-1, keepdims=True) + EPS)
            ohat = o * rstd
            dgam_ref[:, sl] += jnp.sum(dn * ohat, axis=0, keepdims=True)
            dohat = dn * gam_ref[:, sl]
            dobs.append((rstd * (dohat - ohat * jnp.mean(dohat * ohat, axis=-1, keepdims=True))).astype(BF16))

        qb = q_ref[...].astype(BF16)
        row = lax.broadcasted_iota(jnp.int32, (BLK, BLK), 0)
        col = lax.broadcasted_iota(jnp.int32, (BLK, BLK), 1)
        upto = (row <= col).astype(BF16)
        before = (row < col).astype(BF16)

        def block(j, carry, diag):
            off = pl.multiple_of(j * BLK, BLK)
            kj, vj = kb[pl.ds(off, BLK), :], vb[pl.ds(off, BLK), :]
            out = []
            for h in range(hp):
                m_pre, e_pre, dq = carry[h]
                sl = slice(h * GRP, (h + 1) * GRP)
                qh, kh, dob = qb[:, sl], kj[:, sl], dobs[h]
                z = _dot(qh, kh, NT) * scale
                lb = _log_sigmoid(z)
                l1m = lb - z
                if diag:
                    l1m = jnp.where(col < row, l1m, 0.0)
                upto_s = _split_dot(l1m, upto, 2)
                a = jnp.exp(lb + (mt_ref[h] - m_pre) - upto_s)
                if diag:
                    a = jnp.where(col < row, a, 0.0)
                de = a * _dot(dob, vj[:, sl], NT)
                dv_acc[pl.ds(off, BLK), sl] += _dot(a.astype(BF16), dob, TN)
                before_s = _split_dot(de, before, 1)
                dl1m = e_pre + before_s
                if diag:
                    dl1m = jnp.where(col < row, dl1m, 0.0)
                sig = jnp.exp(lb)
                dzb = ((de * (1.0 - sig) - dl1m * sig) * scale).astype(BF16)
                dk_acc[pl.ds(off, BLK), sl] += _dot(dzb, qh, TN)
                out.append((m_pre + upto_s[:, BLK - 1:BLK], e_pre + before_s[:, BLK - 1:BLK] + de[:, BLK - 1:BLK],
                            dq + _dot(dzb, kh, NN)))
            return tuple(out)

        zero = jnp.zeros((BLK, 1), F32)
        carry = tuple((zero, zero, jnp.zeros((BLK, GRP), F32)) for _ in range(hp))
        carry = lax.fori_loop(0, i, lambda j, c: block(j, c, False), carry)
        carry = block(i, carry, True)
        for h in range(hp):
            dq_ref[:, h * GRP:(h + 1) * GRP] = carry[h][2].astype(BF16)

        @pl.when(i == nq - 1)
        def _():
            dk_ref[...] = dk_acc[...].astype(BF16)
            dv_ref[...] = dv_acc[...].astype(BF16)

    blk_q = lambda h, i: (i, h)
    whole = lambda h, i: (0, h)
    return pl.pallas_call(
        body, name="sb_bwd", grid=(nhp, nq),
        out_shape=(jax.ShapeDtypeStruct((s, sb), BF16),) * 3 + (jax.ShapeDtypeStruct((1, sb), F32),),
        in_specs=[pl.BlockSpec((BLK, w), lambda h, i: (i, qc + h)), pl.BlockSpec((s, w), lambda h, i: (0, kc + h)),
                  pl.BlockSpec((s, w), lambda h, i: (0, vc + h)), pl.BlockSpec((BLK, w), blk_q),
                  pl.BlockSpec((hp, BLK, 1), lambda h, i: (h, i, 0)),
                  pl.BlockSpec((BLK, w), lambda h, i: (i, gc + h)),
                  pl.BlockSpec((1, w), lambda h, i: (0, gc + h)), _ANY],
        out_specs=(pl.BlockSpec((BLK, w), blk_q), pl.BlockSpec((s, w), whole), pl.BlockSpec((s, w), whole),
                   pl.BlockSpec((1, w), whole)),
        scratch_shapes=[pltpu.VMEM((s, w), BF16), pltpu.VMEM((s, w), BF16),
                        pltpu.VMEM((s, w), F32), pltpu.VMEM((s, w), F32)],
        compiler_params=_params(2),
    )(proj, proj, proj, o_raw, mtot, don, gamma, after)


def _sb_fwd_one_head(proj, gamma, gm, sb):
    s = proj.shape[0]
    nh, nq = sb // GRP, s // BLK
    qc, kc, vc = 2 * gm // GRP, (2 * gm + sb) // GRP, (2 * gm + 2 * sb) // GRP
    scale = GRP ** -0.5

    def body(q_ref, k_ref, v_ref, gam_ref, on_ref, o_ref, mt_ref, kb, vb):
        i = pl.program_id(1)

        @pl.when(i == 0)
        def _():
            kb[...] = k_ref[...].astype(BF16)
            vb[...] = v_ref[...].astype(BF16)

        qb = q_ref[...].astype(BF16)
        row = lax.broadcasted_iota(jnp.int32, (BLK, BLK), 0)
        col = lax.broadcasted_iota(jnp.int32, (BLK, BLK), 1)
        later = (row > col).astype(BF16)

        def step(jj, carry):
            tail, acc = carry
            j = i - jj
            off = pl.multiple_of(j * BLK, BLK)
            z = _dot(qb, kb[pl.ds(off, BLK), :], NT) * scale
            lb = _log_sigmoid(z)
            mask = col + j * BLK < row + i * BLK
            l1m = jnp.where(mask, lb - z, 0.0)
            a = jnp.where(mask, jnp.exp(lb + _split_dot(l1m, later, 3) + tail), 0.0)
            acc = acc + _dot(a.astype(BF16), vb[pl.ds(off, BLK), :], NN)
            return tail + jnp.sum(l1m, axis=-1, keepdims=True), acc

        tail, acc = lax.fori_loop(0, i + 1, step, (jnp.zeros((BLK, 1), F32), jnp.zeros((BLK, GRP), F32)))
        rstd = lax.rsqrt(jnp.mean(acc * acc, axis=-1, keepdims=True) + EPS)
        on_ref[...] = (acc * rstd * gam_ref[...]).astype(BF16)
        o_ref[...] = acc
        mt_ref[...] = tail

    return pl.pallas_call(
        body, name="sb_fwd", grid=(nh, nq),
        out_shape=(jax.ShapeDtypeStruct((s, sb), BF16), jax.ShapeDtypeStruct((s, sb), F32),
                   jax.ShapeDtypeStruct((nh, s, 1), F32)),
        in_specs=[pl.BlockSpec((BLK, GRP), lambda h, i: (i, qc + h)), pl.BlockSpec((s, GRP), lambda h, i: (0, kc + h)),
                  pl.BlockSpec((s, GRP), lambda h, i: (0, vc + h)),
                  pl.BlockSpec((1, GRP), lambda h, i: (0, gm // GRP + h))],
        out_specs=(pl.BlockSpec((BLK, GRP), lambda h, i: (i, h)), pl.BlockSpec((BLK, GRP), lambda h, i: (i, h)),
                   pl.BlockSpec((None, BLK, 1), lambda h, i: (h, i, 0))),
        scratch_shapes=[pltpu.VMEM((s, GRP), BF16), pltpu.VMEM((s, GRP), BF16)],
        compiler_params=_params(2),
    )(proj, proj, proj, gamma)


def _sb_bwd_one_head(proj, o_raw, mtot, don, gamma, gm, sb):
    s = proj.shape[0]
    nh, nq = sb // GRP, s // BLK
    qc, kc, vc = 2 * gm // GRP, (2 * gm + sb) // GRP, (2 * gm + 2 * sb) // GRP
    scale = GRP ** -0.5

    def body(q_ref, k_ref, v_ref, o_ref, mt_ref, don_ref, gam_ref, dq_ref, dk_ref, dv_ref, dgam_ref,
             kb, vb, dk_acc, dv_acc):
        i = pl.program_id(1)

        @pl.when(i == 0)
        def _():
            kb[...] = k_ref[...].astype(BF16)
            vb[...] = v_ref[...].astype(BF16)
            dk_acc[...] = jnp.zeros_like(dk_acc)
            dv_acc[...] = jnp.zeros_like(dv_acc)
            dgam_ref[...] = jnp.zeros_like(dgam_ref)

        o, dn, gam = o_ref[...], don_ref[...], gam_ref[...]
        rstd = lax.rsqrt(jnp.mean(o * o, axis=-1, keepdims=True) + EPS)
        ohat = o * rstd
        dgam_ref[...] += jnp.sum(dn * ohat, axis=0, keepdims=True)
        dohat = dn * gam
        dob = (rstd * (dohat - ohat * jnp.mean(dohat * ohat, axis=-1, keepdims=True))).astype(BF16)

        qb = q_ref[...].astype(BF16)
        mt = mt_ref[...]
        row = lax.broadcasted_iota(jnp.int32, (BLK, BLK), 0)
        col = lax.broadcasted_iota(jnp.int32, (BLK, BLK), 1)
        upto = (row <= col).astype(BF16)
        before = (row < col).astype(BF16)

        def step(j, carry):
            m_pre, e_pre, dq = carry
            off = pl.multiple_of(j * BLK, BLK)
            kj, vj = kb[pl.ds(off, BLK), :], vb[pl.ds(off, BLK), :]
            z = _dot(qb, kj, NT) * scale
            lb = _log_sigmoid(z)
            mask = col + j * BLK < row + i * BLK
            l1m = jnp.where(mask, lb - z, 0.0)
            tail = mt - m_pre - _split_dot(l1m, upto, 3)
            a = jnp.where(mask, jnp.exp(lb + tail), 0.0)
            de = a * _dot(dob, vj, NT)
            dv_acc[pl.ds(off, BLK), :] += _dot(a.astype(BF16), dob, TN)
            dl1m = e_pre + _split_dot(de, before, 2)
            sig = jnp.exp(lb)
            dz = (de * (1.0 - sig) - jnp.where(mask, dl1m * sig, 0.0)) * scale
            dzb = dz.astype(BF16)
            dk_acc[pl.ds(off, BLK), :] += _dot(dzb, qb, TN)
            return (m_pre + jnp.sum(l1m, axis=-1, keepdims=True), e_pre + jnp.sum(de, axis=-1, keepdims=True),
                    dq + _dot(dzb, kj, NN))

        zero = jnp.zeros((BLK, 1), F32)
        _, _, dq = lax.fori_loop(0, i + 1, step, (zero, zero, jnp.zeros((BLK, GRP), F32)))
        dq_ref[...] = dq.astype(BF16)

        @pl.when(i == nq - 1)
        def _():
            dk_ref[...] = dk_acc[...].astype(BF16)
            dv_ref[...] = dv_acc[...].astype(BF16)

    blk_q = lambda h, i: (i, h)
    whole = lambda h, i: (0, h)
    return pl.pallas_call(
        body, name="sb_bwd", grid=(nh, nq),
        out_shape=(jax.ShapeDtypeStruct((s, sb), BF16),) * 3 + (jax.ShapeDtypeStruct((1, sb), F32),),
        in_specs=[pl.BlockSpec((BLK, GRP), lambda h, i: (i, qc + h)), pl.BlockSpec((s, GRP), lambda h, i: (0, kc + h)),
                  pl.BlockSpec((s, GRP), lambda h, i: (0, vc + h)), pl.BlockSpec((BLK, GRP), blk_q),
                  pl.BlockSpec((None, BLK, 1), lambda h, i: (h, i, 0)),
                  pl.BlockSpec((BLK, GRP), lambda h, i: (i, gm // GRP + h)),
                  pl.BlockSpec((1, GRP), lambda h, i: (0, gm // GRP + h))],
        out_specs=(pl.BlockSpec((BLK, GRP), blk_q), pl.BlockSpec((s, GRP), whole), pl.BlockSpec((s, GRP), whole),
                   pl.BlockSpec((1, GRP), whole)),
        scratch_shapes=[pltpu.VMEM((s, GRP), BF16), pltpu.VMEM((s, GRP), BF16),
                        pltpu.VMEM((s, GRP), F32), pltpu.VMEM((s, GRP), F32)],
        compiler_params=_params(2),
    )(proj, proj, proj, o_raw, mtot, don, gamma)


FFN_ROW_CHUNKS = 2


def _row_chunks(tm):
    n = FFN_ROW_CHUNKS if tm % (16 * FFN_ROW_CHUNKS) == 0 else 1
    return [slice(k * (tm // n), (k + 1) * (tm // n)) for k in range(n)]


def _ffn_fwd(h2, wg, wu, wd):
    s, d = h2.shape
    nb, fb, _ = wd.shape
    tm = _tile(s, 512, 16)

    def body(h_ref, wg_ref, wu_ref, wd_ref, f_ref, g_ref, u_ref):
        j = pl.program_id(1)
        parts = []
        for rows_ in _row_chunks(tm):
            hv = h_ref[rows_, :]
            g = _dot(hv, wg_ref[...], NT)
            u = _dot(hv, wu_ref[...], NT)
            g_ref[rows_, :] = g.astype(BF16)
            u_ref[rows_, :] = u.astype(BF16)
            parts.append(_dot((g * jax.nn.sigmoid(g) * u).astype(BF16), wd_ref[...], NN))

        @pl.when(j == 0)
        def _():
            for rows_, part in zip(_row_chunks(tm), parts):
                f_ref[rows_, :] = part

        @pl.when(j > 0)
        def _():
            for rows_, part in zip(_row_chunks(tm), parts):
                f_ref[rows_, :] += part

    rows = pl.BlockSpec((tm, d), lambda i, j: (i, 0))
    wblk = pl.BlockSpec((None, fb, d), lambda i, j: (j, 0, 0))
    hid = pl.BlockSpec((None, tm, fb), lambda i, j: (j, i, 0))
    return pl.pallas_call(
        body, name="ffn_fwd", grid=(s // tm, nb),
        out_shape=(jax.ShapeDtypeStruct((s, d), F32), jax.ShapeDtypeStruct((nb, s, fb), BF16),
                   jax.ShapeDtypeStruct((nb, s, fb), BF16)),
        in_specs=[rows, wblk, wblk, wblk],
        out_specs=(rows, hid, hid), compiler_params=_params(2),
    )(h2, wg, wu, wd)


def _ffn_bwd(df, g_pre, u_pre, wg, wu, wd):
    s, d = df.shape
    nb, fb, _ = wd.shape
    tm = _tile(s, 512, 16)

    def body(df_ref, g_ref, u_ref, wg_ref, wu_ref, wd_ref, dh_ref, a_ref, dg_ref, du_ref):
        j = pl.program_id(1)
        parts = []
        for rows_ in _row_chunks(tm):
            g, u = g_ref[rows_, :].astype(F32), u_ref[rows_, :].astype(F32)
            da = _dot(df_ref[rows_, :], wd_ref[...], NT)
            sg = jax.nn.sigmoid(g)
            silu = g * sg
            a_ref[rows_, :] = (silu * u).astype(BF16)
            dg = (da * u * (sg * (1.0 + g * (1.0 - sg)))).astype(BF16)
            du = (da * silu).astype(BF16)
            dg_ref[rows_, :] = dg
            du_ref[rows_, :] = du
            parts.append(_dot(dg, wg_ref[...], NN) + _dot(du, wu_ref[...], NN))

        @pl.when(j == 0)
        def _():
            for rows_, part in zip(_row_chunks(tm), parts):
                dh_ref[rows_, :] = part

        @pl.when(j > 0)
        def _():
            for rows_, part in zip(_row_chunks(tm), parts):
                dh_ref[rows_, :] += part

    rows = pl.BlockSpec((tm, d), lambda i, j: (i, 0))
    wblk = pl.BlockSpec((None, fb, d), lambda i, j: (j, 0, 0))
    hid = pl.BlockSpec((None, tm, fb), lambda i, j: (j, i, 0))
    hid_sds = jax.ShapeDtypeStruct((nb, s, fb), BF16)
    return pl.pallas_call(
        body, name="ffn_bwd", grid=(s // tm, nb),
        out_shape=(jax.ShapeDtypeStruct((s, d), F32), hid_sds, hid_sds, hid_sds),
        in_specs=[rows, hid, hid, wblk, wblk, wblk],
        out_specs=(rows, hid, hid, hid), compiler_params=_params(2),
    )(df, g_pre, u_pre, wg, wu, wd)


def _reduce_adamw(parts, w, m, v, name, after):
    npart, r, c = parts.shape
    tr = _tile(r, max(16, 262144 // c), 16)
    c1, c2 = 1.0 - ADAM_B1 ** ADAM_STEP, 1.0 - ADAM_B2 ** ADAM_STEP

    def body(p_ref, w_ref, m_ref, v_ref, after_ref, g_ref, d_ref, nm_ref, nv_ref):
        g = p_ref[0].astype(F32)
        for k in range(1, npart):
            g = g + p_ref[k].astype(F32)
        nm = ADAM_B1 * m_ref[...] + (1.0 - ADAM_B1) * g
        nv = ADAM_B2 * v_ref[...] + (1.0 - ADAM_B2) * (g * g)
        g_ref[...] = g
        nm_ref[...] = nm
        nv_ref[...] = nv
        d_ref[...] = -ADAM_LR * ((nm / c1) / (jnp.sqrt(nv / c2) + ADAM_EPS) + ADAM_WD * w_ref[...])

    blk = pl.BlockSpec((tr, c), lambda i: (i, 0))
    sds = jax.ShapeDtypeStruct((r, c), F32)
    return pl.pallas_call(
        body, name=name, grid=(r // tr,), out_shape=(sds,) * 4,
        in_specs=[pl.BlockSpec((npart, tr, c), lambda i: (0, i, 0)), blk, blk, blk, _ANY], out_specs=(blk,) * 4,
        compiler_params=_params(1),
    )(parts, w, m, v, after)


def _pack(vecs):
    rows = jnp.concatenate([a.reshape(-1, GRP) for a in vecs], axis=0)
    pad = -rows.shape[0] % 64
    return jnp.pad(rows, ((0, pad), (0, 0)))


def _unpack(rows, shapes):
    out, at = [], 0
    for shp in shapes:
        n = 1
        for k in shp:
            n *= k
        out.append(rows[at:at + n // GRP].reshape(shp))
        at += n // GRP
    return out


def kernel(x, c, w_ada, b_ada, norm1_g, w_in, v_norm_g, w_spatial, b_spatial, out_norm_g, w_out, norm2_g, w_gate, w_up, w_down, final_g, loss_target, m_w_ada, m_b_ada, m_norm1_g, m_w_in, m_v_norm_g, m_w_spatial, m_b_spatial, m_out_norm_g, m_w_out, m_norm2_g, m_w_gate, m_w_up, m_w_down, m_final_g, v_w_ada, v_b_ada, v_norm1_g, v_w_in, v_v_norm_g, v_w_spatial, v_b_spatial, v_out_norm_g, v_w_out, v_norm2_g, v_w_gate, v_w_up, v_w_down, v_final_g):
    s, d = x.shape[1], x.shape[2]
    gm = v_norm_g.shape[1]
    sb = d - gm
    n_in, ffb, ob = w_in.shape[2], w_gate.shape[2], w_out.shape[1]
    xs, tgt = x[0], loss_target[0]
    me = 4 * lax.axis_index("x") + 2 * lax.axis_index("y") + lax.axis_index("c")

    c_all = _exchange(c, False, "gather_c")[:, 0, :]
    mod_cols = _ada_fwd(c_all, w_ada[0])
    mod = _exchange(mod_cols[:, None, :], True, "scatter_mod").reshape(1, 6 * d) + b_ada
    shift1, scale1, gate1, shift2, scale2, gate2 = [mod[:, k * d:(k + 1) * d] for k in range(6)]

    tok, gathers = mod, {}
    tr_ = lambda a: jnp.swapaxes(a, 1, 2)
    w_gate, m_w_gate, v_w_gate, w_up, m_w_up, v_w_up = map(tr_, (w_gate, m_w_gate, v_w_gate, w_up, m_w_up, v_w_up))
    for nm, w in (("w_in", w_in), ("w_out", w_out), ("w_gate", w_gate), ("w_up", w_up), ("w_down", w_down)):
        gathers[nm], tok = _gather_start(w[0].astype(BF16), tok, "gather_start_" + nm)
    shift1 = shift1 + tok[0:1, 0:1]

    tm = _tile(s, 512, 16)
    nt = s // tm
    rows_d = pl.BlockSpec((tm, d), lambda i, j: (i, 0))
    h1 = _prenorm_fwd(xs, norm1_g, scale1, shift1, "prenorm1")
    w_in_all = _gather_wait(_gather_forward(gathers["w_in"], h1, "gather_fwd_w_in")[0], h1, "gather_wait_w_in")
    proj = _mm(h1, w_in_all, pl.BlockSpec((s, d), lambda j: (0, 0)), pl.BlockSpec((None, d, n_in), lambda j: (j, 0, 0)),
               jax.ShapeDtypeStruct((s, NDEV * n_in), F32), pl.BlockSpec((s, n_in), lambda j: (0, j)),
               (NDEV,), NN, False, "proj")
    fwd_w_out, tok = _gather_forward(gathers["w_out"], proj, "gather_fwd_w_out")
    b_cols = jnp.pad(b_spatial[0].T, ((0, 0), (0, GRP - b_spatial.shape[1])))
    on_sb, o_sb, mtot = _sb_fwd(proj, out_norm_g, gm, sb, tok)
    fwd_w_gate, tok = _gather_forward(gathers["w_gate"], on_sb, "gather_fwd_w_gate")
    on_gm = _gmlp_fwd(proj, v_norm_g, w_spatial[0], b_cols, out_norm_g, gm, tok)
    fwd_w_up, tok = _gather_forward(gathers["w_up"], on_gm, "gather_fwd_w_up")
    o_n = jnp.concatenate([on_gm, on_sb], axis=1)
    w_out_all = _gather_wait(fwd_w_out, tok, "gather_wait_w_out").reshape(d, d)
    rows_1 = pl.BlockSpec((tm, d), lambda i: (i, 0))
    whole_1 = pl.BlockSpec((d, d), lambda i: (0, 0))
    p_out = _mm(o_n, w_out_all, rows_1, whole_1, jax.ShapeDtypeStruct((s, d), F32), rows_1, (nt,), NN, False,
                "out_proj")
    x1, h2 = _residual_prenorm(xs, gate1, p_out, norm2_g, scale2, shift2, "residual1_prenorm2")
    fwd_w_down, tok = _gather_forward(gathers["w_down"], h2, "gather_fwd_w_down")
    w_gate_all = _gather_wait(fwd_w_gate, tok, "gather_wait_w_gate")
    w_up_all = _gather_wait(fwd_w_up, h2, "gather_wait_w_up")
    w_down_all = _gather_wait(fwd_w_down, h2, "gather_wait_w_down")
    f_out, g_pre, u_pre = _ffn_fwd(h2, w_gate_all, w_up_all, w_down_all)

    dx2, df, st_f = _final_loss(x1, f_out, final_g.reshape(1, d), tgt, gate2)
    dh2, act, dg, du = _ffn_bwd(df, g_pre, u_pre, w_gate_all, w_up_all, w_down_all)
    hid = pl.BlockSpec((None, s, ffb), lambda j: (j, 0, 0))
    all_d = pl.BlockSpec((s, d), lambda j: (0, 0))
    col_sds = jax.ShapeDtypeStruct((NDEV, d, ffb), BF16)
    col_out = pl.BlockSpec((None, d, ffb), lambda j: (j, 0, 0))

    def second_leg(first, after, tag):
        parts, lands = _scatter_wait1(first, after, "scatter_wait1_" + tag)
        sums = [_chip_sum(p, l, "chip_sum_%s_%d" % (tag, k)) for k, (p, l) in enumerate(zip(parts, lands))]
        return _scatter_start2(sums, after, "scatter_start2_" + tag)

    row_sds = jax.ShapeDtypeStruct((NDEV, ffb, d), BF16)
    row_out = pl.BlockSpec((None, ffb, d), lambda j: (j, 0, 0))
    gw_gate = _mm(dg, h2, hid, all_d, row_sds, row_out, (NDEV,), TN, False, "grad_w_gate")
    gw_up = _mm(du, h2, hid, all_d, row_sds, row_out, (NDEV,), TN, False, "grad_w_up")
    gw_down = _mm(act, df, hid, all_d, row_sds, row_out, (NDEV,), TN, False, "grad_w_down")
    first_ffn, tok = _scatter_start1([gw_gate, gw_up, gw_down], tok, "scatter_start1_ffn")
    dx1, dp, st2 = _prenorm_bwd(x1, dh2, dx2, f_out, norm2_g, scale2, gate1, "prenorm2_bwd", tok)
    don = _mm(dp, w_out_all, rows_1, whole_1, jax.ShapeDtypeStruct((s, d), F32), rows_1, (nt,), NT, False,
              "out_proj_bwd")
    gw_out = _mm(o_n, dp, pl.BlockSpec((s, ob), lambda j: (0, j)), all_d,
                 jax.ShapeDtypeStruct((NDEV, ob, d), BF16), pl.BlockSpec((None, ob, d), lambda j: (j, 0, 0)),
                 (NDEV,), TN, False, "grad_w_out", don)
    first_out, tok = _scatter_start1([gw_out], tok, "scatter_start1_out")
    second_ffn, tok = second_leg(first_ffn, tok, "ffn")
    dproj_gm, dw_sp, db_cols, dgv, dgam_gm = _gmlp_bwd(proj, don, v_norm_g, w_spatial[0], b_cols, out_norm_g, gm, tok)
    second_out, tok = second_leg(first_out, dproj_gm, "out")
    dq, dk, dv, dgam_sb = _sb_bwd(proj, o_sb, mtot, don, out_norm_g, gm, sb, tok)
    dproj = jnp.concatenate([dproj_gm, dq, dk, dv], axis=1)
    gw_in = _mm(h1, dproj, all_d, pl.BlockSpec((s, n_in), lambda j: (0, j)),
                jax.ShapeDtypeStruct((NDEV, d, n_in), BF16), pl.BlockSpec((None, d, n_in), lambda j: (j, 0, 0)),
                (NDEV,), TN, False, "grad_w_in")
    first_in, tok = _scatter_start1([gw_in], tok, "scatter_start1_in")
    tm2 = _tile(s, 1024, 16)
    dh1 = _mm(dproj, w_in_all, pl.BlockSpec((tm2, n_in), lambda i, j: (i, j)),
              pl.BlockSpec((None, d, n_in), lambda i, j: (j, 0, 0)), jax.ShapeDtypeStruct((s, d), F32),
              pl.BlockSpec((tm2, d), lambda i, j: (i, 0)), (s // tm2, NDEV), NT, True, "in_proj_bwd", tok)
    grad_x, _, st1 = _prenorm_bwd(xs, dh1, dx1, p_out, norm1_g, scale1, gate1, "prenorm1_bwd", tok)

    dmod = jnp.concatenate([st1[0], st1[1], st1[3], st2[0], st2[1], st2[3]])
    loss_row = jnp.pad((0.5 * jnp.sum(st_f[1]) / d).reshape(1, 1), ((0, 0), (0, GRP - 1)))
    small = [st1[2], dgv, dw_sp, db_cols[:, :b_spatial.shape[1]].T, jnp.concatenate([dgam_gm, dgam_sb], axis=1),
             st2[2], st_f[0], dmod, loss_row]
    zero_row = jnp.zeros((1, GRP), F32)
    small_w = [norm1_g, v_norm_g, w_spatial, b_spatial, out_norm_g, norm2_g, final_g, b_ada, zero_row]
    small_m = [m_norm1_g, m_v_norm_g, m_w_spatial, m_b_spatial, m_out_norm_g, m_norm2_g, m_final_g, m_b_ada, zero_row]
    small_v = [v_norm1_g, v_v_norm_g, v_w_spatial, v_b_spatial, v_out_norm_g, v_norm2_g, v_final_g, v_b_ada, zero_row]
    small_first, tok = _gather_start(_pack(small), grad_x, "gather_small_start")
    second_in, tok = second_leg(first_in, tok, "in")

    big = {}
    prev = tok
    for second, group in ((second_ffn, (("w_gate", w_gate, m_w_gate, v_w_gate), ("w_up", w_up, m_w_up, v_w_up),
                                        ("w_down", w_down, m_w_down, v_w_down))),
                          (second_out, (("w_out", w_out, m_w_out, v_w_out),))):
        chip_sums = _scatter_wait2(second, prev, "scatter_wait2_" + group[0][0])
        for (nm, w, m, v), part in zip(group, chip_sums):
            big[nm] = _reduce_adamw(part, w[0], m[0], v[0], "adamw_" + nm, tok)
            prev = big[nm][1]

    small_second, tok2 = _gather_forward(small_first, prev, "gather_small_fwd")
    small_all = _gather_wait(small_second, tok2, "gather_small_wait")
    sm = _reduce_adamw(small_all, _pack(small_w), _pack(small_m), _pack(small_v), "adamw_small", tok)
    shapes = [a.shape for a in small_w]
    sm_g, sm_d, sm_m, sm_v = [_unpack(t, shapes) for t in sm]
    loss = sm_g[8][0, 0]

    at = sum(a.size for a in small_w[:7]) // GRP
    dmod_all = small_all[:, at:at + 6 * d // GRP, :].reshape(NDEV, 6 * d)
    n_ada = w_ada.shape[2]
    dmod_cols = lax.dynamic_slice(dmod_all, (0, me * n_ada), (NDEV, n_ada))
    g_ada = _ada_bwd(c_all, dmod_cols)
    big["w_ada"] = _reduce_adamw(g_ada[None], w_ada[0], m_w_ada[0], v_w_ada[0], "adamw_w_ada", sm[0])
    part_in, = _scatter_wait2(second_in, big["w_ada"][1], "scatter_wait2_w_in")
    big["w_in"] = _reduce_adamw(part_in, w_in[0], m_w_in[0], v_w_in[0], "adamw_w_in", tok)

    names = ["w_ada", "b_ada", "norm1_g", "w_in", "v_norm_g", "w_spatial", "b_spatial", "out_norm_g", "w_out",
             "norm2_g", "w_gate", "w_up", "w_down", "final_g"]
    small_at = {"norm1_g": 0, "v_norm_g": 1, "w_spatial": 2, "b_spatial": 3, "out_norm_g": 4, "norm2_g": 5,
                "final_g": 6, "b_ada": 7}
    outs = [[], [], [], []]
    for nm in names:
        for k in range(4):
            if nm in big:
                res = big[nm][k][None]
                outs[k].append(tr_(res) if nm in ("w_gate", "w_up") else res)
            else:
                outs[k].append((sm_g, sm_d, sm_m, sm_v)[k][small_at[nm]])
    return (loss, grad_x[None], *outs[0], *outs[1], *outs[2], *outs[3])

    hid = pl.BlockSpec((None, s, ffb), lambda j: (j, 0, 0))
    all_d = pl.BlockSpec((s, d), lambda j: (0, 0))
    col_sds = jax.ShapeDtypeStruct((NDEV, d, ffb), BF16)
    col_out = pl.BlockSpec((None, d, ffb), lambda j: (j, 0, 0))
    scatters = {}
    gw_gate = _mm(h2, dg, all_d, hid, col_sds, col_out, (NDEV,), TN, False, "grad_w_gate")
    scatters["w_gate"], tok = _exchange_start(gw_gate, True, tok, "scatter_start_w_gate")
    gw_up = _mm(h2, du, all_d, hid, col_sds, col_out, (NDEV,), TN, False, "grad_w_up", tok)
    scatters["w_up"], tok = _exchange_start(gw_up, True, tok, "scatter_start_w_up")
    gw_down = _mm(act, df, hid, all_d, jax.ShapeDtypeStruct((NDEV, ffb, d), BF16),
                  pl.BlockSpec((None, ffb, d), lambda j: (j, 0, 0)), (NDEV,), TN, False, "grad_w_down", tok)
    scatters["w_down"], tok = _exchange_start(gw_down, True, tok, "scatter_start_w_down")
    dx1, dp, st2 = _prenorm_bwd(x1, dh2, dx2, f_out, norm2_g, scale2, gate1, "prenorm2_bwd", tok)
    don = _mm(dp, w_out_all, rows_1, whole_1, jax.ShapeDtypeStruct((s, d), F32), rows_1, (nt,), NT, False,
              "out_proj_bwd")
    gw_out = _mm(o_n, dp, pl.BlockSpec((s, ob), lambda j: (0, j)), all_d,
                 jax.ShapeDtypeStruct((NDEV, ob, d), BF16), pl.BlockSpec((None, ob, d), lambda j: (j, 0, 0)),
                 (NDEV,), TN, False, "grad_w_out", don)
    scatters["w_out"], tok = _exchange_start(gw_out, True, tok, "scatter_start_w_out")
    dproj_gm, dw_sp, db_cols, dgv, dgam_gm = _gmlp_bwd(proj, don, v_norm_g, w_spatial[0], b_cols, out_norm_g, gm, tok)
    dq, dk, dv, dgam_sb = _sb_bwd(proj, o_sb, mtot, don, out_norm_g, gm, sb)
    dproj = jnp.concatenate([dproj_gm, dq, dk, dv], axis=1)
    dh1 = _mm(dproj, w_in_all, pl.BlockSpec((tm, n_in), lambda i, j: (i, j)),
              pl.BlockSpec((None, d, n_in), lambda i, j: (j, 0, 0)), jax.ShapeDtypeStruct((s, d), F32), rows_d,
              (nt, NDEV), NT, True, "in_proj_bwd")
    grad_x, _, st1 = _prenorm_bwd(xs, dh1, dx1, p_out, norm1_g, scale1, gate1, "prenorm1_bwd", tok)

    dmod = jnp.concatenate([st1[0], st1[1], st1[3], st2[0], st2[1], st2[3]])
    loss_row = jnp.pad((0.5 * jnp.sum(st_f[1]) / d).reshape(1, 1), ((0, 0), (0, GRP - 1)))
    small = [st1[2], dgv, dw_sp, db_cols[:, :b_spatial.shape[1]].T, jnp.concatenate([dgam_gm, dgam_sb], axis=1),
             st2[2], st_f[0], dmod, loss_row]
    zero_row = jnp.zeros((1, GRP), F32)
    small_w = [norm1_g, v_norm_g, w_spatial, b_spatial, out_norm_g, norm2_g, final_g, b_ada, zero_row]
    small_m = [m_norm1_g, m_v_norm_g, m_w_spatial, m_b_spatial, m_out_norm_g, m_norm2_g, m_final_g, m_b_ada, zero_row]
    small_v = [v_norm1_g, v_v_norm_g, v_w_spatial, v_b_spatial, v_out_norm_g, v_norm2_g, v_final_g, v_b_ada, zero_row]
    small_handles, tok = _exchange_start(_pack(small), False, tok, "gather_small_start")
    gw_in = _mm(h1, dproj, all_d, pl.BlockSpec((s, n_in), lambda j: (0, j)),
                jax.ShapeDtypeStruct((NDEV, d, n_in), BF16), pl.BlockSpec((None, d, n_in), lambda j: (j, 0, 0)),
                (NDEV,), TN, False, "grad_w_in", tok)
    scatters["w_in"], tok = _exchange_start(gw_in, True, tok, "scatter_start_w_in")
    small_all = _exchange_wait(small_handles, tok, "gather_small_wait")
    sm = _reduce_adamw(small_all, _pack(small_w), _pack(small_m), _pack(small_v), "adamw_small", tok)
    shapes = [a.shape for a in small_w]
    sm_g, sm_d, sm_m, sm_v = [_unpack(t, shapes) for t in sm]

    loss = sm_g[8][0, 0]
    at = sum(a.size for a in small_w[:7]) // GRP
    dmod_all = small_all[:, at:at + 6 * d // GRP, :].reshape(NDEV, 6 * d)
    n_ada = w_ada.shape[2]
    dmod_cols = lax.dynamic_slice(dmod_all, (0, me * n_ada), (NDEV, n_ada))
    g_ada = _ada_bwd(c_all, dmod_cols)
    big = {"w_ada": _reduce_adamw(g_ada[None], w_ada[0], m_w_ada[0], v_w_ada[0], "adamw_w_ada", sm[0])}

    prev = big["w_ada"][1]
    for nm, w, m, v in (("w_gate", w_gate, m_w_gate, v_w_gate), ("w_up", w_up, m_w_up, v_w_up),
                        ("w_down", w_down, m_w_down, v_w_down), ("w_out", w_out, m_w_out, v_w_out),
                        ("w_in", w_in, m_w_in, v_w_in)):
        parts = _exchange_wait(scatters[nm], prev, "scatter_wait_" + nm)
        big[nm] = _reduce_adamw(parts, w[0], m[0], v[0], "adamw_" + nm, tok)
        prev = big[nm][1]

    names = ["w_ada", "b_ada", "norm1_g", "w_in", "v_norm_g", "w_spatial", "b_spatial", "out_norm_g", "w_out",
             "norm2_g", "w_gate", "w_up", "w_down", "final_g"]
    small_at = {"norm1_g": 0, "v_norm_g": 1, "w_spatial": 2, "b_spatial": 3, "out_norm_g": 4, "norm2_g": 5,
                "final_g": 6, "b_ada": 7}
    outs = [[], [], [], []]
    for nm in names:
        for k in range(4):
            outs[k].append(big[nm][k][None] if nm in big else (sm_g, sm_d, sm_m, sm_v)[k][small_at[nm]])
    return (loss, grad_x[None], *outs[0], *outs[1], *outs[2], *outs[3])
```

```python
import functools

import jax
import jax.numpy as jnp
from jax import lax
from jax.experimental import pallas as pl
from jax.experimental.pallas import tpu as pltpu

F32, BF16 = jnp.float32, jnp.bfloat16
NDEV = 8
GRP = 128
EPS = 1e-6
VMEM_BYTES = 64 * 2 ** 20
VMEM_LIMIT = VMEM_BYTES - 8 * 2 ** 20
ADAM_LR, ADAM_B1, ADAM_B2, ADAM_EPS, ADAM_WD, ADAM_STEP = 0.001, 0.9, 0.999, 1e-08, 0.01, 10
MESH = pl.DeviceIdType.MESH
NN = (((1,), (0,)), ((), ()))
NT = (((1,), (1,)), ((), ()))
TN = (((0,), (0,)), ((), ()))


def _params(n_axes):
    return pltpu.CompilerParams(dimension_semantics=("arbitrary",) * n_axes, vmem_limit_bytes=VMEM_LIMIT)


def _tile(n, cap, mult):
    best = None
    for t in range(mult, min(n, cap) + 1, mult):
        if n % t == 0:
            best = t
    assert best is not None, (n, cap, mult)
    return best


def _dot(a, b, dims):
    return lax.dot_general(a, b, dims, preferred_element_type=F32)


def _exchange(src, per_peer, name):
    blk = src.shape[1:] if per_peer else src.shape

    def body(src_ref, out_ref, send_sems, recv_sems, local_sem):
        x, y, c = lax.axis_index("x"), lax.axis_index("y"), lax.axis_index("c")
        me = 4 * x + 2 * y + c
        local = pltpu.make_async_copy(src_ref.at[me] if per_peer else src_ref, out_ref.at[me], local_sem)
        local.start()
        sends, recvs = [], []
        for k in range(1, NDEV):
            px, py, pc = x ^ (k >> 2), y ^ ((k >> 1) & 1), c ^ (k & 1)
            p = 4 * px + 2 * py + pc
            mine = src_ref.at[p] if per_peer else src_ref
            sends.append(pltpu.make_async_remote_copy(
                src_ref=mine, dst_ref=out_ref.at[me], send_sem=send_sems.at[k - 1], recv_sem=recv_sems.at[k - 1],
                device_id=(px, py, pc), device_id_type=MESH))
            recvs.append(pltpu.make_async_remote_copy(
                src_ref=mine, dst_ref=out_ref.at[p], send_sem=send_sems.at[k - 1], recv_sem=recv_sems.at[k - 1],
                device_id=(px, py, pc), device_id_type=MESH))
        for cp in sends:
            cp.start()
        for cp in recvs:
            cp.wait_recv()
        for cp in sends:
            cp.wait_send()
        local.wait()

    return pl.pallas_call(
        body, name=name,
        out_shape=jax.ShapeDtypeStruct((NDEV,) + tuple(blk), src.dtype),
        in_specs=[pl.BlockSpec(memory_space=pl.ANY)],
        out_specs=pl.BlockSpec(memory_space=pl.ANY),
        scratch_shapes=[pltpu.SemaphoreType.DMA((NDEV - 1,)), pltpu.SemaphoreType.DMA((NDEV - 1,)),
                        pltpu.SemaphoreType.DMA],
    )(src)


_HBM = pl.BlockSpec(memory_space=pltpu.HBM)
_SEM = pl.BlockSpec(memory_space=pltpu.SEMAPHORE)
_ANY = pl.BlockSpec(memory_space=pl.ANY)
_EFFECT = pltpu.SideEffectType.DATAFLOW_SIDE_EFFECTING


def _peers():
    x, y, c = lax.axis_index("x"), lax.axis_index("y"), lax.axis_index("c")
    out = []
    for k in range(1, NDEV):
        px, py, pc = x ^ (k >> 2), y ^ ((k >> 1) & 1), c ^ (k & 1)
        out.append((k, (px, py, pc), 4 * px + 2 * py + pc))
    return 4 * x + 2 * y + c, out


def _exchange_start(src, per_peer, after, name):
    blk = src.shape[1:] if per_peer else src.shape
    me = 4 * lax.axis_index("x") + 2 * lax.axis_index("y") + lax.axis_index("c")
    own = lax.dynamic_index_in_dim(src, me, 0, keepdims=True) if per_peer else src[None]
    land = lax.dynamic_update_slice(lax.empty((NDEV,) + tuple(blk), src.dtype), own, (me,) + (0,) * len(blk))

    def body(src_ref, land_ref, after_ref, send_sems, recv_sems, src_thru, land_thru, token):
        my, peers = _peers()
        for k, coords, p in peers:
            pltpu.make_async_remote_copy(
                src_ref=src_ref.at[p] if per_peer else src_ref, dst_ref=land_ref.at[my],
                send_sem=send_sems.at[k - 1], recv_sem=recv_sems.at[k - 1], device_id=coords, device_id_type=MESH).start()
        token[...] = jnp.zeros_like(token)

    res = pl.pallas_call(
        body, name=name,
        out_shape=(pltpu.SemaphoreType.DMA((NDEV - 1,)), pltpu.SemaphoreType.DMA((NDEV - 1,)),
                   pltpu.HBM(src.shape, src.dtype), pltpu.HBM(land.shape, land.dtype), jax.ShapeDtypeStruct((8, GRP), F32)),
        in_specs=(_HBM, _HBM, _ANY), out_specs=(_SEM, _SEM, _HBM, _HBM, pl.BlockSpec(memory_space=pltpu.VMEM)),
        input_output_aliases={0: 2, 1: 3}, compiler_params=pltpu.CompilerParams(has_side_effects=_EFFECT),
    )(pltpu.with_memory_space_constraint(src, pltpu.HBM), pltpu.with_memory_space_constraint(land, pltpu.HBM), after)
    return (per_peer,) + tuple(res[:4]), res[4]


def _exchange_wait(handles, after, name):
    per_peer, send_sems, recv_sems, src_thru, land_thru = handles

    def body(src_ref, land_ref, send_sems, recv_sems, after_ref, src_dead, got_ref):
        _, peers = _peers()
        for k, coords, p in peers:
            cp = pltpu.make_async_remote_copy(
                src_ref=src_ref.at[p] if per_peer else src_ref, dst_ref=land_ref.at[p],
                send_sem=send_sems.at[k - 1], recv_sem=recv_sems.at[k - 1], device_id=coords, device_id_type=MESH)
            cp.wait_send()
            cp.wait_recv()

    return pl.pallas_call(
        body, name=name,
        out_shape=(pltpu.HBM(src_thru.shape, src_thru.dtype), pltpu.HBM(land_thru.shape, land_thru.dtype)),
        in_specs=(_HBM, _HBM, _SEM, _SEM, _ANY), out_specs=(_HBM, _HBM), input_output_aliases={0: 0, 1: 1},
        compiler_params=pltpu.CompilerParams(has_side_effects=_EFFECT),
    )(src_thru, land_thru, send_sems, recv_sems, after)[1]


def _chip_peers():
    x, y, c = lax.axis_index("x"), lax.axis_index("y"), lax.axis_index("c")
    chips = [(x, 1 - y), (1 - x, y), (1 - x, 1 - y)]
    return 4 * x + 2 * y + c, (x, y, 1 - c), [((px, py, c), 4 * px + 2 * py + c) for px, py in chips]


def _gather_start(src, after, name):
    me = 4 * lax.axis_index("x") + 2 * lax.axis_index("y") + lax.axis_index("c")
    land = lax.dynamic_update_slice(lax.empty((NDEV,) + src.shape, src.dtype), src[None], (me,) + (0,) * src.ndim)

    def body(src_ref, land_ref, after_ref, send_sems, recv_sems, src_thru, land_thru, token):
        my, sibling, chips = _chip_peers()
        for k, to in enumerate([sibling] + [coords for coords, _ in chips]):
            pltpu.make_async_remote_copy(src_ref=src_ref, dst_ref=land_ref.at[my], send_sem=send_sems.at[k],
                                         recv_sem=recv_sems.at[k], device_id=to, device_id_type=MESH).start()
        token[...] = jnp.zeros_like(token)

    res = pl.pallas_call(
        body, name=name,
        out_shape=(pltpu.SemaphoreType.DMA((4,)), pltpu.SemaphoreType.DMA((4,)), pltpu.HBM(src.shape, src.dtype),
                   pltpu.HBM(land.shape, land.dtype), jax.ShapeDtypeStruct((8, GRP), F32)),
        in_specs=(_HBM, _HBM, _ANY), out_specs=(_SEM, _SEM, _HBM, _HBM, pl.BlockSpec(memory_space=pltpu.VMEM)),
        input_output_aliases={0: 2, 1: 3}, compiler_params=pltpu.CompilerParams(has_side_effects=_EFFECT),
    )(pltpu.with_memory_space_constraint(src, pltpu.HBM), pltpu.with_memory_space_constraint(land, pltpu.HBM), after)
    return tuple(res[:4]), res[4]


def _gather_forward(handles, after, name):
    send_sems, recv_sems, src_thru, land_thru = handles

    def body(src_ref, land_ref, send_sems, recv_sems, after_ref, send2, recv2, land_out, token):
        my, sibling, chips = _chip_peers()
        for k, (to, p) in enumerate([(sibling, my ^ 1)] + chips):
            first = pltpu.make_async_remote_copy(src_ref=src_ref, dst_ref=land_ref.at[p], send_sem=send_sems.at[k],
                                                 recv_sem=recv_sems.at[k], device_id=to, device_id_type=MESH)
            first.wait_send()
            first.wait_recv()
        for k, (_, p) in enumerate(chips):
            pltpu.make_async_remote_copy(src_ref=land_ref.at[p], dst_ref=land_ref.at[p], send_sem=send2.at[k],
                                         recv_sem=recv2.at[k], device_id=sibling, device_id_type=MESH).start()
        token[...] = jnp.zeros_like(token)

    res = pl.pallas_call(
        body, name=name,
        out_shape=(pltpu.SemaphoreType.DMA((3,)), pltpu.SemaphoreType.DMA((3,)), pltpu.HBM(land_thru.shape, land_thru.dtype),
                   jax.ShapeDtypeStruct((8, GRP), F32)),
        in_specs=(_HBM, _HBM, _SEM, _SEM, _ANY), out_specs=(_SEM, _SEM, _HBM, pl.BlockSpec(memory_space=pltpu.VMEM)),
        input_output_aliases={1: 2}, compiler_params=pltpu.CompilerParams(has_side_effects=_EFFECT),
    )(src_thru, land_thru, send_sems, recv_sems, after)
    return tuple(res[:3]), res[3]


def _gather_wait(handles, after, name):
    send2, recv2, land_thru = handles

    def body(land_ref, send2, recv2, after_ref, got_ref):
        _, sibling, chips = _chip_peers()
        for k, (_, p) in enumerate(chips):
            cp = pltpu.make_async_remote_copy(src_ref=land_ref.at[p], dst_ref=land_ref.at[p ^ 1], send_sem=send2.at[k],
                                              recv_sem=recv2.at[k], device_id=sibling, device_id_type=MESH)
            cp.wait_send()
            cp.wait_recv()

    return pl.pallas_call(
        body, name=name, out_shape=pltpu.HBM(land_thru.shape, land_thru.dtype),
        in_specs=(_HBM, _SEM, _SEM, _ANY), out_specs=_HBM, input_output_aliases={0: 0},
        compiler_params=pltpu.CompilerParams(has_side_effects=_EFFECT),
    )(land_thru, send2, recv2, after)


NCHIP = NDEV // 2


def _scatter_start1(parts, after, name):
    n = len(parts)
    lands = [lax.empty((NCHIP,) + p.shape[1:], p.dtype) for p in parts]

    def body(*refs):
        part_refs, land_refs = refs[:n], refs[n:2 * n]
        send_sems, recv_sems, token = refs[2 * n + 1], refs[2 * n + 2], refs[-1]
        _, sibling, _ = _chip_peers()
        c = lax.axis_index("c")
        for a in range(n):
            for q in range(NCHIP):
                pltpu.make_async_remote_copy(
                    src_ref=part_refs[a].at[2 * q + 1 - c], dst_ref=land_refs[a].at[q], send_sem=send_sems.at[a * NCHIP + q],
                    recv_sem=recv_sems.at[a * NCHIP + q], device_id=sibling, device_id_type=MESH).start()
        token[...] = jnp.zeros_like(token)

    hbm = lambda arrs: tuple(pltpu.HBM(a.shape, a.dtype) for a in arrs)
    res = pl.pallas_call(
        body, name=name,
        out_shape=(pltpu.SemaphoreType.DMA((n * NCHIP,)), pltpu.SemaphoreType.DMA((n * NCHIP,))) + hbm(parts) + hbm(lands)
        + (jax.ShapeDtypeStruct((8, GRP), F32),),
        in_specs=(_HBM,) * (2 * n) + (_ANY,),
        out_specs=(_SEM, _SEM) + (_HBM,) * (2 * n) + (pl.BlockSpec(memory_space=pltpu.VMEM),),
        input_output_aliases={k: 2 + k for k in range(2 * n)}, compiler_params=pltpu.CompilerParams(has_side_effects=_EFFECT),
    )(*[pltpu.with_memory_space_constraint(a, pltpu.HBM) for a in list(parts) + lands], after)
    return (n,) + tuple(res[:-1]), res[-1]


def _scatter_wait1(handles, after, name):
    n, send_sems, recv_sems = handles[:3]
    thru = handles[3:]

    def body(*refs):
        part_refs, land_refs, send_sems, recv_sems = refs[:n], refs[n:2 * n], refs[2 * n], refs[2 * n + 1]
        _, sibling, _ = _chip_peers()
        for a in range(n):
            for q in range(NCHIP):
                cp = pltpu.make_async_remote_copy(
                    src_ref=part_refs[a].at[q], dst_ref=land_refs[a].at[q], send_sem=send_sems.at[a * NCHIP + q],
                    recv_sem=recv_sems.at[a * NCHIP + q], device_id=sibling, device_id_type=MESH)
                cp.wait_send()
                cp.wait_recv()

    res = pl.pallas_call(
        body, name=name, out_shape=tuple(pltpu.HBM(a.shape, a.dtype) for a in thru),
        in_specs=(_HBM,) * (2 * n) + (_SEM, _SEM, _ANY), out_specs=(_HBM,) * (2 * n),
        input_output_aliases={k: k for k in range(2 * n)}, compiler_params=pltpu.CompilerParams(has_side_effects=_EFFECT),
    )(*thru, send_sems, recv_sems, after)
    return res[:n], res[n:]


def _chip_sum(part, land, name):
    _, r, c = part.shape
    tr = _tile(r, max(16, 2 ** 21 // c), 16)

    def body(core_ref, p_ref, l_ref, o_ref):
        o_ref[...] = (p_ref[...].astype(F32) + l_ref[...].astype(F32)).astype(o_ref.dtype)

    blk = pl.BlockSpec((None, tr, c), lambda q, i, core: (q, i, 0))
    return pl.pallas_call(
        body, name=name, out_shape=jax.ShapeDtypeStruct(land.shape, land.dtype),
        grid_spec=pltpu.PrefetchScalarGridSpec(
            num_scalar_prefetch=1, grid=(NCHIP, r // tr),
            in_specs=[pl.BlockSpec((None, None, tr, c), lambda q, i, core: (q, core[0], i, 0)), blk], out_specs=blk),
        compiler_params=_params(2),
    )(lax.axis_index("c").astype(jnp.int32).reshape(1), part.reshape(NCHIP, 2, r, c), land)


def _scatter_start2(sums, after, name):
    n = len(sums)
    chip = 2 * lax.axis_index("x") + lax.axis_index("y")
    lands = [lax.dynamic_update_slice(lax.empty(s_.shape, s_.dtype), lax.dynamic_index_in_dim(s_, chip, 0, keepdims=True),
                                      (chip,) + (0,) * (s_.ndim - 1)) for s_ in sums]

    def body(*refs):
        sum_refs, land_refs = refs[:n], refs[n:2 * n]
        send_sems, recv_sems, token = refs[2 * n + 1], refs[2 * n + 2], refs[-1]
        my, _, chips = _chip_peers()
        for a in range(n):
            for k, (to, p) in enumerate(chips):
                pltpu.make_async_remote_copy(
                    src_ref=sum_refs[a].at[p // 2], dst_ref=land_refs[a].at[my // 2], send_sem=send_sems.at[a * 3 + k],
                    recv_sem=recv_sems.at[a * 3 + k], device_id=to, device_id_type=MESH).start()
        token[...] = jnp.zeros_like(token)

    hbm = lambda arrs: tuple(pltpu.HBM(a.shape, a.dtype) for a in arrs)
    res = pl.pallas_call(
        body, name=name,
        out_shape=(pltpu.SemaphoreType.DMA((n * 3,)), pltpu.SemaphoreType.DMA((n * 3,))) + hbm(sums) + hbm(lands)
        + (jax.ShapeDtypeStruct((8, GRP), F32),),
        in_specs=(_HBM,) * (2 * n) + (_ANY,),
        out_specs=(_SEM, _SEM) + (_HBM,) * (2 * n) + (pl.BlockSpec(memory_space=pltpu.VMEM),),
        input_output_aliases={k: 2 + k for k in range(2 * n)}, compiler_params=pltpu.CompilerParams(has_side_effects=_EFFECT),
    )(*[pltpu.with_memory_space_constraint(a, pltpu.HBM) for a in list(sums) + lands], after)
    return (n,) + tuple(res[:-1]), res[-1]


def _scatter_wait2(handles, after, name):
    n, send_sems, recv_sems = handles[:3]
    thru = handles[3:]

    def body(*refs):
        sum_refs, land_refs, send_sems, recv_sems = refs[:n], refs[n:2 * n], refs[2 * n], refs[2 * n + 1]
        _, _, chips = _chip_peers()
        for a in range(n):
            for k, (to, p) in enumerate(chips):
                cp = pltpu.make_async_remote_copy(
                    src_ref=sum_refs[a].at[p // 2], dst_ref=land_refs[a].at[p // 2], send_sem=send_sems.at[a * 3 + k],
                    recv_sem=recv_sems.at[a * 3 + k], device_id=to, device_id_type=MESH)
                cp.wait_send()
                cp.wait_recv()

    res = pl.pallas_call(
        body, name=name, out_shape=tuple(pltpu.HBM(a.shape, a.dtype) for a in thru),
        in_specs=(_HBM,) * (2 * n) + (_SEM, _SEM, _ANY), out_specs=(_HBM,) * (2 * n),
        input_output_aliases={k: k for k in range(2 * n)}, compiler_params=pltpu.CompilerParams(has_side_effects=_EFFECT),
    )(*thru, send_sems, recv_sems, after)
    return res[n:]


def _mm(a, b, a_spec, b_spec, out_sds, o_spec, grid, dims, acc, name, after=None):
    extra = () if after is None else (after,)
    assert not acc or out_sds.dtype == F32

    def body(a_ref, b_ref, *rest):
        o_ref = rest[len(extra)]
        prod = _dot(a_ref[...], b_ref[...], dims)
        if not acc:
            o_ref[...] = prod.astype(o_ref.dtype)
            return
        k = pl.program_id(len(grid) - 1)

        @pl.when(k == 0)
        def _():
            o_ref[...] = prod

        @pl.when(k > 0)
        def _():
            o_ref[...] += prod

    return pl.pallas_call(
        body, name=name, grid=grid, out_shape=out_sds, in_specs=[a_spec, b_spec] + [_ANY] * len(extra), out_specs=o_spec,
        compiler_params=_params(len(grid)),
    )(a, b, *extra)


def _row_spec(tm, d):
    return pl.BlockSpec((tm, d), lambda i: (i, 0))


def _vec_spec(d):
    return pl.BlockSpec((1, d), lambda i: (0, 0))


def _prenorm_fwd(x, g, scale, shift, name):
    s, d = x.shape
    tm = _tile(s, 256, 16)

    def body(x_ref, g_ref, sc_ref, sh_ref, h_ref):
        xv = x_ref[...]
        rstd = lax.rsqrt(jnp.mean(xv * xv, axis=-1, keepdims=True) + EPS)
        h_ref[...] = ((xv * rstd * g_ref[...]) * (1.0 + sc_ref[...]) + sh_ref[...]).astype(BF16)

    return pl.pallas_call(
        body, name=name, grid=(s // tm,), out_shape=jax.ShapeDtypeStruct((s, d), BF16),
        in_specs=[_row_spec(tm, d), _vec_spec(d), _vec_spec(d), _vec_spec(d)], out_specs=_row_spec(tm, d),
        compiler_params=_params(1),
    )(x, g, scale, shift)


def _residual_prenorm(x, gate, p, g, scale, shift, name):
    s, d = x.shape
    tm = _tile(s, 256, 16)

    def body(x_ref, gate_ref, p_ref, g_ref, sc_ref, sh_ref, x1_ref, h_ref):
        xv = x_ref[...] + gate_ref[...] * p_ref[...]
        x1_ref[...] = xv
        rstd = lax.rsqrt(jnp.mean(xv * xv, axis=-1, keepdims=True) + EPS)
        h_ref[...] = ((xv * rstd * g_ref[...]) * (1.0 + sc_ref[...]) + sh_ref[...]).astype(BF16)

    return pl.pallas_call(
        body, name=name, grid=(s // tm,),
        out_shape=(jax.ShapeDtypeStruct((s, d), F32), jax.ShapeDtypeStruct((s, d), BF16)),
        in_specs=[_row_spec(tm, d), _vec_spec(d), _row_spec(tm, d), _vec_spec(d), _vec_spec(d), _vec_spec(d)],
        out_specs=(_row_spec(tm, d), _row_spec(tm, d)), compiler_params=_params(1),
    )(x, gate, p, g, scale, shift)


def _final_loss(x1, f, final_g, target, gate2):
    s, d = x1.shape
    tm = _tile(s, 256, 16)

    def body(x_ref, f_ref, g_ref, t_ref, gate_ref, dx_ref, df_ref, st_ref):
        @pl.when(pl.program_id(0) == 0)
        def _():
            st_ref[...] = jnp.zeros_like(st_ref)

        xv, gf = x_ref[...] + gate_ref[...] * f_ref[...], g_ref[...]
        rstd = lax.rsqrt(jnp.mean(xv * xv, axis=-1, keepdims=True) + EPS)
        xhat = xv * rstd
        err = xhat * gf - t_ref[...]
        dy = err * (1.0 / d)
        gdy = dy * gf
        dx = rstd * (gdy - xhat * jnp.mean(gdy * xhat, axis=-1, keepdims=True))
        dx_ref[...] = dx
        df_ref[...] = (gate_ref[...] * dx).astype(BF16)
        st_ref[0:1, :] += jnp.sum(dy * xhat, axis=0, keepdims=True)
        st_ref[1:2, :] += jnp.sum(err * err, axis=0, keepdims=True)

    return pl.pallas_call(
        body, name="final_loss", grid=(s // tm,),
        out_shape=(jax.ShapeDtypeStruct((s, d), F32), jax.ShapeDtypeStruct((s, d), BF16),
                   jax.ShapeDtypeStruct((8, d), F32)),
        in_specs=[_row_spec(tm, d), _row_spec(tm, d), _vec_spec(d), _row_spec(tm, d), _vec_spec(d)],
        out_specs=(_row_spec(tm, d), _row_spec(tm, d), pl.BlockSpec((8, d), lambda i: (0, 0))),
        compiler_params=_params(1),
    )(x1, f, final_g, target, gate2)


def _prenorm_bwd(xin, dh, dres, pf, g, scale, gate_next, name, after):
    s, d = xin.shape
    tm = _tile(s, 256, 16)

    def body(x_ref, dh_ref, dr_ref, pf_ref, g_ref, sc_ref, gn_ref, after_ref, dx_ref, dn_ref, st_ref):
        @pl.when(pl.program_id(0) == 0)
        def _():
            st_ref[...] = jnp.zeros_like(st_ref)

        xv, dhv, drv, gv = x_ref[...], dh_ref[...], dr_ref[...], g_ref[...]
        one_sc = 1.0 + sc_ref[...]
        rstd = lax.rsqrt(jnp.mean(xv * xv, axis=-1, keepdims=True) + EPS)
        xhat = xv * rstd
        dxhat = dhv * (gv * one_sc)
        dx = drv + rstd * (dxhat - xhat * jnp.mean(dxhat * xhat, axis=-1, keepdims=True))
        dx_ref[...] = dx
        dn_ref[...] = (gn_ref[...] * dx).astype(BF16)
        dhx = dhv * xhat
        st_ref[0:1, :] += jnp.sum(dhv, axis=0, keepdims=True)
        st_ref[1:2, :] += jnp.sum(dhx, axis=0, keepdims=True) * gv
        st_ref[2:3, :] += jnp.sum(dhx, axis=0, keepdims=True) * one_sc
        st_ref[3:4, :] += jnp.sum(drv * pf_ref[...], axis=0, keepdims=True)

    return pl.pallas_call(
        body, name=name, grid=(s // tm,),
        out_shape=(jax.ShapeDtypeStruct((s, d), F32), jax.ShapeDtypeStruct((s, d), BF16),
                   jax.ShapeDtypeStruct((8, d), F32)),
        in_specs=[_row_spec(tm, d)] * 4 + [_vec_spec(d)] * 3 + [_ANY],
        out_specs=(_row_spec(tm, d), _row_spec(tm, d), pl.BlockSpec((8, d), lambda i: (0, 0))),
        compiler_params=_params(1),
    )(xin, dh, dres, pf, g, scale, gate_next, after)


def _ada_fwd(c_all, w_loc):
    nb, d = c_all.shape
    n = w_loc.shape[1]
    tn = _tile(n, 512, 128) if n % 128 == 0 else n

    def body(c_ref, w_ref, o_ref):
        cv = c_ref[...]
        o_ref[...] = jnp.dot(cv * jax.nn.sigmoid(cv), w_ref[...], preferred_element_type=F32,
                             precision=lax.Precision.HIGHEST)

    return pl.pallas_call(
        body, name="ada_fwd", grid=(n // tn,), out_shape=jax.ShapeDtypeStruct((nb, n), F32),
        in_specs=[pl.BlockSpec((nb, d), lambda j: (0, 0)), pl.BlockSpec((d, tn), lambda j: (0, j))],
        out_specs=pl.BlockSpec((nb, tn), lambda j: (0, j)), compiler_params=_params(1),
    )(c_all, w_loc)


def _ada_bwd(c_all, dmod_cols):
    nb, d = c_all.shape
    n = dmod_cols.shape[1]
    tn = _tile(n, 512, 128) if n % 128 == 0 else n

    def body(c_ref, dm_ref, o_ref):
        cv = c_ref[...]
        o_ref[...] = lax.dot_general(cv * jax.nn.sigmoid(cv), dm_ref[...], TN, preferred_element_type=F32,
                                     precision=lax.Precision.HIGHEST)

    return pl.pallas_call(
        body, name="ada_bwd", grid=(n // tn,), out_shape=jax.ShapeDtypeStruct((d, n), F32),
        in_specs=[pl.BlockSpec((nb, d), lambda j: (0, 0)), pl.BlockSpec((nb, tn), lambda j: (0, j))],
        out_specs=pl.BlockSpec((d, tn), lambda j: (0, j)), compiler_params=_params(1),
    )(c_all, dmod_cols)


_INV_SQRT2 = 0.7071067811865476
_INV_SQRT2PI = 0.3989422804014327


def _gelu(x):
    return 0.5 * x * (1.0 + lax.erf(x * _INV_SQRT2))


def _gelu_grad(x):
    return 0.5 * (1.0 + lax.erf(x * _INV_SQRT2)) + x * jnp.exp(-0.5 * x * x) * _INV_SQRT2PI


def _gm_group_fwd(up, vp, gv, wt, bcol):
    u = _gelu(up)
    va = _gelu(vp)
    xc = va - jnp.mean(va, axis=-1, keepdims=True)
    rstd_v = lax.rsqrt(jnp.mean(xc * xc, axis=-1, keepdims=True) + EPS)
    yv = xc * rstd_v
    vn = (yv * gv).astype(BF16)
    mixed = _dot(wt, vn, NN) + bcol
    return u, rstd_v, yv, vn, mixed, u * mixed


def _tril_bf16(w):
    row = lax.broadcasted_iota(jnp.int32, w.shape, 0)
    col = lax.broadcasted_iota(jnp.int32, w.shape, 1)
    return jnp.where(col <= row, w, 0.0).astype(BF16)


def _gmlp_fwd(proj, v_norm_g, w_spatial, b_cols, gamma, gm, after):
    s = proj.shape[0]
    ng = gm // GRP

    def body(p_ref, gv_ref, w_ref, b_ref, gam_ref, after_ref, o_ref):
        for g in range(ng):
            lo = g * GRP
            wt = _tril_bf16(w_ref[g])
            *_, o = _gm_group_fwd(p_ref[:, lo:lo + GRP], p_ref[:, gm + lo:gm + lo + GRP], gv_ref[:, lo:lo + GRP],
                                  wt, b_ref[:, g:g + 1])
            rstd_o = lax.rsqrt(jnp.mean(o * o, axis=-1, keepdims=True) + EPS)
            o_ref[:, lo:lo + GRP] = (o * rstd_o * gam_ref[:, lo:lo + GRP]).astype(BF16)

    return pl.pallas_call(
        body, name="gmlp_fwd", grid=(s // GRP,), out_shape=jax.ShapeDtypeStruct((s, gm), BF16),
        in_specs=[pl.BlockSpec((GRP, 2 * gm), lambda n: (n, 0)), _vec_spec(gm),
                  pl.BlockSpec((ng, GRP, GRP), lambda n: (0, 0, 0)), pl.BlockSpec((GRP, GRP), lambda n: (0, 0)),
                  _vec_spec(gm), _ANY],
        out_specs=pl.BlockSpec((GRP, gm), lambda n: (n, 0)), compiler_params=_params(1),
    )(proj, v_norm_g, w_spatial, b_cols, gamma, after)


def _gmlp_bwd(proj, don, v_norm_g, w_spatial, b_cols, gamma, gm, after):
    s = proj.shape[0]
    ng = gm // GRP

    def body(p_ref, don_ref, gv_ref, w_ref, b_ref, gam_ref, after_ref, dp_ref, dw_ref, db_ref, dgv_ref, dgam_ref):
        @pl.when(pl.program_id(0) == 0)
        def _():
            dw_ref[...] = jnp.zeros_like(dw_ref)
            db_ref[...] = jnp.zeros_like(db_ref)
            dgv_ref[...] = jnp.zeros_like(dgv_ref)
            dgam_ref[...] = jnp.zeros_like(dgam_ref)

        lane = lax.broadcasted_iota(jnp.int32, (GRP, GRP), 1)
        row = lax.broadcasted_iota(jnp.int32, (GRP, GRP), 0)
        for g in range(ng):
            lo = g * GRP
            up, vp = p_ref[:, lo:lo + GRP], p_ref[:, gm + lo:gm + lo + GRP]
            gv, gam = gv_ref[:, lo:lo + GRP], gam_ref[:, lo:lo + GRP]
            wt = _tril_bf16(w_ref[g])
            u, rstd_v, yv, vn, mixed, o = _gm_group_fwd(up, vp, gv, wt, b_ref[:, g:g + 1])
            rstd_o = lax.rsqrt(jnp.mean(o * o, axis=-1, keepdims=True) + EPS)
            ohat = o * rstd_o
            dn = don_ref[:, lo:lo + GRP]
            dgam_ref[:, lo:lo + GRP] += jnp.sum(dn * ohat, axis=0, keepdims=True)
            dohat = dn * gam
            do = rstd_o * (dohat - ohat * jnp.mean(dohat * ohat, axis=-1, keepdims=True))
            du = do * mixed
            dmixed = do * u
            dmb = dmixed.astype(BF16)
            db_ref[...] += jnp.where(lane == g, jnp.sum(dmixed, axis=-1, keepdims=True), 0.0)
            dw_ref[g] += jnp.where(lane <= row, _dot(dmb, vn, NT), 0.0)
            dvn = _dot(wt, dmb, TN)
            dgv_ref[:, lo:lo + GRP] += jnp.sum(dvn * yv, axis=0, keepdims=True)
            dyv = dvn * gv
            dva = rstd_v * (dyv - jnp.mean(dyv, axis=-1, keepdims=True)
                            - yv * jnp.mean(dyv * yv, axis=-1, keepdims=True))
            dp_ref[:, lo:lo + GRP] = (du * _gelu_grad(up)).astype(BF16)
            dp_ref[:, gm + lo:gm + lo + GRP] = (dva * _gelu_grad(vp)).astype(BF16)

    const2 = lambda n: (0, 0)
    return pl.pallas_call(
        body, name="gmlp_bwd", grid=(s // GRP,),
        out_shape=(jax.ShapeDtypeStruct((s, 2 * gm), BF16), jax.ShapeDtypeStruct((ng, GRP, GRP), F32),
                   jax.ShapeDtypeStruct((GRP, GRP), F32), jax.ShapeDtypeStruct((1, gm), F32),
                   jax.ShapeDtypeStruct((1, gm), F32)),
        in_specs=[pl.BlockSpec((GRP, 2 * gm), lambda n: (n, 0)), pl.BlockSpec((GRP, gm), lambda n: (n, 0)),
                  _vec_spec(gm), pl.BlockSpec((ng, GRP, GRP), lambda n: (0, 0, 0)),
                  pl.BlockSpec((GRP, GRP), const2), _vec_spec(gm), _ANY],
        out_specs=(pl.BlockSpec((GRP, 2 * gm), lambda n: (n, 0)), pl.BlockSpec((ng, GRP, GRP), lambda n: (0, 0, 0)),
                   pl.BlockSpec((GRP, GRP), const2), _vec_spec(gm), _vec_spec(gm)),
        compiler_params=_params(1),
    )(proj, don, v_norm_g, w_spatial, b_cols, gamma, after)


BLK = 256


def _log_sigmoid(z):
    return jnp.minimum(z, 0.0) - jnp.log(1.0 + jnp.exp(-jnp.abs(z)))


def _split_dot(x, tri, passes):
    n = x.shape[0]
    parts, rest = [], x
    for _ in range(passes):
        hi = rest.astype(BF16)
        parts.append(hi)
        rest = rest - hi.astype(F32)
    res = _dot(jnp.concatenate(parts, axis=0), tri, NN)
    out = res[0:n]
    for k in range(1, passes):
        out = out + res[k * n:(k + 1) * n]
    return out


HEADS_PER_STEP = 4


def _sb_fwd(proj, gamma, gm, sb, after):
    s = proj.shape[0]
    hp = min(HEADS_PER_STEP, sb // GRP)
    w = hp * GRP
    nhp, nq = sb // w, s // BLK
    qc, kc, vc, gc = 2 * gm // w, (2 * gm + sb) // w, (2 * gm + 2 * sb) // w, gm // w
    scale = GRP ** -0.5

    def body(q_ref, k_ref, v_ref, gam_ref, after_ref, on_ref, o_ref, a_ref, s_ref, kb, vb):
        i = pl.program_id(1)

        @pl.when(i == 0)
        def _():
            kb[...] = k_ref[...].astype(BF16)
            vb[...] = v_ref[...].astype(BF16)

        qb = q_ref[...].astype(BF16)
        row = lax.broadcasted_iota(jnp.int32, (BLK, BLK), 0)
        col = lax.broadcasted_iota(jnp.int32, (BLK, BLK), 1)
        later = (row > col).astype(BF16)

        def block(j, carry, diag):
            off = pl.multiple_of(j * BLK, BLK)
            kj, vj = kb[pl.ds(off, BLK), :], vb[pl.ds(off, BLK), :]
            out = []
            for h in range(hp):
                tail, acc = carry[h]
                sl = slice(h * GRP, (h + 1) * GRP)
                z = _dot(qb[:, sl], kj[:, sl], NT) * scale
                lb = _log_sigmoid(z)
                l1m = lb - z
                sig = jnp.exp(lb)
                if diag:
                    l1m = jnp.where(col < row, l1m, 0.0)
                    sig = jnp.where(col < row, sig, 0.0)
                after_s = _split_dot(l1m, later, 2)
                a = jnp.exp(lb + after_s + tail)
                if diag:
                    a = jnp.where(col < row, a, 0.0)
                ab = a.astype(BF16)
                a_ref[h, j] = ab
                s_ref[h, j] = sig.astype(BF16)
                out.append((tail + after_s[:, 0:1] + l1m[:, 0:1], acc + _dot(ab, vj[:, sl], NN)))
            return tuple(out)

        init = tuple((jnp.zeros((BLK, 1), F32), jnp.zeros((BLK, GRP), F32)) for _ in range(hp))
        carry = block(i, init, True)
        carry = lax.fori_loop(0, i, lambda jj, c: block(i - 1 - jj, c, False), carry)
        for h in range(hp):
            _, acc = carry[h]
            sl = slice(h * GRP, (h + 1) * GRP)
            rstd = lax.rsqrt(jnp.mean(acc * acc, axis=-1, keepdims=True) + EPS)
            on_ref[:, sl] = (acc * rstd * gam_ref[:, sl]).astype(BF16)
            o_ref[:, sl] = acc

    saved = jax.ShapeDtypeStruct((sb // GRP, nq, nq, BLK, BLK), BF16)
    saved_spec = pl.BlockSpec((hp, None, nq, BLK, BLK), lambda h, i: (h, i, 0, 0, 0))
    return pl.pallas_call(
        body, name="sb_fwd", grid=(nhp, nq),
        out_shape=(jax.ShapeDtypeStruct((s, sb), BF16), jax.ShapeDtypeStruct((s, sb), F32), saved, saved),
        in_specs=[pl.BlockSpec((BLK, w), lambda h, i: (i, qc + h)), pl.BlockSpec((s, w), lambda h, i: (0, kc + h)),
                  pl.BlockSpec((s, w), lambda h, i: (0, vc + h)), pl.BlockSpec((1, w), lambda h, i: (0, gc + h)), _ANY],
        out_specs=(pl.BlockSpec((BLK, w), lambda h, i: (i, h)), pl.BlockSpec((BLK, w), lambda h, i: (i, h)),
                   saved_spec, saved_spec),
        scratch_shapes=[pltpu.VMEM((s, w), BF16), pltpu.VMEM((s, w), BF16)],
        compiler_params=_params(2),
    )(proj, proj, proj, gamma, after)


def _sb_bwd(proj, o_raw, a_sv, s_sv, don, gamma, gm, sb, after):
    s = proj.shape[0]
    hp = min(HEADS_PER_STEP // 2, sb // GRP)
    w = hp * GRP
    nhp, nq = sb // w, s // BLK
    qc, kc, vc, gc = 2 * gm // w, (2 * gm + sb) // w, (2 * gm + 2 * sb) // w, gm // w
    scale = GRP ** -0.5

    def body(q_ref, k_ref, v_ref, o_ref, a_ref, s_ref, don_ref, gam_ref, after_ref, dq_ref, dk_ref, dv_ref, dgam_ref,
             kb, vb, dk_acc, dv_acc):
        i = pl.program_id(1)

        @pl.when(i == 0)
        def _():
            kb[...] = k_ref[...].astype(BF16)
            vb[...] = v_ref[...].astype(BF16)
            dk_acc[...] = jnp.zeros_like(dk_acc)
            dv_acc[...] = jnp.zeros_like(dv_acc)
            dgam_ref[...] = jnp.zeros_like(dgam_ref)

        dobs = []
        for h in range(hp):
            sl = slice(h * GRP, (h + 1) * GRP)
            o, dn = o_ref[:, sl], don_ref[:, sl]
            rstd = lax.rsqrt(jnp.mean(o * o, axis=-1, keepdims=True) + EPS)
            ohat = o * rstd
            dgam_ref[:, sl] += jnp.sum(dn * ohat, axis=0, keepdims=True)
            dohat = dn * gam_ref[:, sl]
            dobs.append((rstd * (dohat - ohat * jnp.mean(dohat * ohat, axis=-1, keepdims=True))).astype(BF16))

        qb = q_ref[...].astype(BF16)
        row = lax.broadcasted_iota(jnp.int32, (BLK, BLK), 0)
        col = lax.broadcasted_iota(jnp.int32, (BLK, BLK), 1)
        before = (row < col).astype(BF16)

        def block(j, carry):
            off = pl.multiple_of(j * BLK, BLK)
            kj, vj = kb[pl.ds(off, BLK), :], vb[pl.ds(off, BLK), :]
            out = []
            for h in range(hp):
                e_pre, dq = carry[h]
                sl = slice(h * GRP, (h + 1) * GRP)
                qh, kh, dob = qb[:, sl], kj[:, sl], dobs[h]
                ab = a_ref[h, j]
                sig = s_ref[h, j].astype(F32)
                de = ab.astype(F32) * _dot(dob, vj[:, sl], NT)
                dv_acc[pl.ds(off, BLK), sl] += _dot(ab, dob, TN)
                before_s = _dot(de.astype(BF16), before, NN)
                dzb = ((de * (1.0 - sig) - (e_pre + before_s) * sig) * scale).astype(BF16)
                dk_acc[pl.ds(off, BLK), sl] += _dot(dzb, qh, TN)
                out.append((e_pre + before_s[:, BLK - 1:BLK] + de[:, BLK - 1:BLK], dq + _dot(dzb, kh, NN)))
            return tuple(out)

        carry = tuple((jnp.zeros((BLK, 1), F32), jnp.zeros((BLK, GRP), F32)) for _ in range(hp))
        carry = lax.fori_loop(0, i + 1, block, carry)
        for h in range(hp):
            dq_ref[:, h * GRP:(h + 1) * GRP] = carry[h][1].astype(BF16)

        @pl.when(i == nq - 1)
        def _():
            dk_ref[...] = dk_acc[...].astype(BF16)
            dv_ref[...] = dv_acc[...].astype(BF16)

    blk_q = lambda h, i: (i, h)
    whole = lambda h, i: (0, h)
    saved_spec = pl.BlockSpec((hp, None, nq, BLK, BLK), lambda h, i: (h, i, 0, 0, 0))
    return pl.pallas_call(
        body, name="sb_bwd", grid=(nhp, nq),
        out_shape=(jax.ShapeDtypeStruct((s, sb), BF16),) * 3 + (jax.ShapeDtypeStruct((1, sb), F32),),
        in_specs=[pl.BlockSpec((BLK, w), lambda h, i: (i, qc + h)), pl.BlockSpec((s, w), lambda h, i: (0, kc + h)),
                  pl.BlockSpec((s, w), lambda h, i: (0, vc + h)), pl.BlockSpec((BLK, w), blk_q), saved_spec, saved_spec,
                  pl.BlockSpec((BLK, w), lambda h, i: (i, gc + h)),
                  pl.BlockSpec((1, w), lambda h, i: (0, gc + h)), _ANY],
        out_specs=(pl.BlockSpec((BLK, w), blk_q), pl.BlockSpec((s, w), whole), pl.BlockSpec((s, w), whole),
                   pl.BlockSpec((1, w), whole)),
        scratch_shapes=[pltpu.VMEM((s, w), BF16), pltpu.VMEM((s, w), BF16),
                        pltpu.VMEM((s, w), F32), pltpu.VMEM((s, w), F32)],
        compiler_params=_params(2),
    )(proj, proj, proj, o_raw, a_sv, s_sv, don, gamma, after)


def _sb_fwd_recompute(proj, gamma, gm, sb, after):
    s = proj.shape[0]
    hp = min(HEADS_PER_STEP, sb // GRP)
    w = hp * GRP
    nhp, nq = sb // w, s // BLK
    qc, kc, vc, gc = 2 * gm // w, (2 * gm + sb) // w, (2 * gm + 2 * sb) // w, gm // w
    scale = GRP ** -0.5

    def body(q_ref, k_ref, v_ref, gam_ref, after_ref, on_ref, o_ref, mt_ref, kb, vb):
        i = pl.program_id(1)

        @pl.when(i == 0)
        def _():
            kb[...] = k_ref[...].astype(BF16)
            vb[...] = v_ref[...].astype(BF16)

        qb = q_ref[...].astype(BF16)
        row = lax.broadcasted_iota(jnp.int32, (BLK, BLK), 0)
        col = lax.broadcasted_iota(jnp.int32, (BLK, BLK), 1)
        later = (row > col).astype(BF16)

        def block(j, carry, diag):
            off = pl.multiple_of(j * BLK, BLK)
            kj, vj = kb[pl.ds(off, BLK), :], vb[pl.ds(off, BLK), :]
            out = []
            for h in range(hp):
                tail, acc = carry[h]
                sl = slice(h * GRP, (h + 1) * GRP)
                z = _dot(qb[:, sl], kj[:, sl], NT) * scale
                lb = _log_sigmoid(z)
                l1m = lb - z
                if diag:
                    l1m = jnp.where(col < row, l1m, 0.0)
                after_s = _split_dot(l1m, later, 2)
                a = jnp.exp(lb + after_s + tail)
                if diag:
                    a = jnp.where(col < row, a, 0.0)
                out.append((tail + after_s[:, 0:1] + l1m[:, 0:1], acc + _dot(a.astype(BF16), vj[:, sl], NN)))
            return tuple(out)

        init = tuple((jnp.zeros((BLK, 1), F32), jnp.zeros((BLK, GRP), F32)) for _ in range(hp))
        carry = block(i, init, True)
        carry = lax.fori_loop(0, i, lambda jj, c: block(i - 1 - jj, c, False), carry)
        for h in range(hp):
            tail, acc = carry[h]
            sl = slice(h * GRP, (h + 1) * GRP)
            rstd = lax.rsqrt(jnp.mean(acc * acc, axis=-1, keepdims=True) + EPS)
            on_ref[:, sl] = (acc * rstd * gam_ref[:, sl]).astype(BF16)
            o_ref[:, sl] = acc
            mt_ref[h] = tail

    return pl.pallas_call(
        body, name="sb_fwd", grid=(nhp, nq),
        out_shape=(jax.ShapeDtypeStruct((s, sb), BF16), jax.ShapeDtypeStruct((s, sb), F32),
                   jax.ShapeDtypeStruct((sb // GRP, s, 1), F32)),
        in_specs=[pl.BlockSpec((BLK, w), lambda h, i: (i, qc + h)), pl.BlockSpec((s, w), lambda h, i: (0, kc + h)),
                  pl.BlockSpec((s, w), lambda h, i: (0, vc + h)), pl.BlockSpec((1, w), lambda h, i: (0, gc + h)), _ANY],
        out_specs=(pl.BlockSpec((BLK, w), lambda h, i: (i, h)), pl.BlockSpec((BLK, w), lambda h, i: (i, h)),
                   pl.BlockSpec((hp, BLK, 1), lambda h, i: (h, i, 0))),
        scratch_shapes=[pltpu.VMEM((s, w), BF16), pltpu.VMEM((s, w), BF16)],
        compiler_params=_params(2),
    )(proj, proj, proj, gamma, after)


def _sb_bwd_recompute(proj, o_raw, mtot, don, gamma, gm, sb, after):
    s = proj.shape[0]
    hp = min(HEADS_PER_STEP, sb // GRP)
    w = hp * GRP
    nhp, nq = sb // w, s // BLK
    qc, kc, vc, gc = 2 * gm // w, (2 * gm + sb) // w, (2 * gm + 2 * sb) // w, gm // w
    scale = GRP ** -0.5

    def body(q_ref, k_ref, v_ref, o_ref, mt_ref, don_ref, gam_ref, after_ref, dq_ref, dk_ref, dv_ref, dgam_ref,
             kb, vb, dk_acc, dv_acc):
        i = pl.program_id(1)

        @pl.when(i == 0)
        def _():
            kb[...] = k_ref[...].astype(BF16)
            vb[...] = v_ref[...].astype(BF16)
            dk_acc[...] = jnp.zeros_like(dk_acc)
            dv_acc[...] = jnp.zeros_like(dv_acc)
            dgam_ref[...] = jnp.zeros_like(dgam_ref)

        dobs = []
        for h in range(hp):
            sl = slice(h * GRP, (h + 1) * GRP)
            o, dn = o_ref[:, sl], don_ref[:, sl]
            rstd = lax.rsqrt(jnp.mean(o * o, axis=-1, keepdims=True) + EPS)
            ohat = o * rstd
            dgam_ref[:, sl] += jnp.sum(dn * ohat, axis=0, keepdims=True)
            dohat = dn * gam_ref[:, sl]
            dobs.append((rstd * (dohat - ohat * jnp.mean(dohat * ohat, axis=-1, keepdims=True))).astype(BF16))

        qb = q_ref[...].astype(BF16)
        row = lax.broadcasted_iota(jnp.int32, (BLK, BLK), 0)
        col = lax.broadcasted_iota(jnp.int32, (BLK, BLK), 1)
        upto = (row <= col).astype(BF16)
        before = (row < col).astype(BF16)

        def block(j, carry, diag):
            off = pl.multiple_of(j * BLK, BLK)
            kj, vj = kb[pl.ds(off, BLK), :], vb[pl.ds(off, BLK), :]
            out = []
            for h in range(hp):
                m_pre, e_pre, dq = carry[h]
                sl = slice(h * GRP, (h + 1) * GRP)
                qh, kh, dob = qb[:, sl], kj[:, sl], dobs[h]
                z = _dot(qh, kh, NT) * scale
                lb = _log_sigmoid(z)
                l1m = lb - z
                if diag:
                    l1m = jnp.where(col < row, l1m, 0.0)
                upto_s = _split_dot(l1m, upto, 2)
                a = jnp.exp(lb + (mt_ref[h] - m_pre) - upto_s)
                if diag:
                    a = jnp.where(col < row, a, 0.0)
                de = a * _dot(dob, vj[:, sl], NT)
                dv_acc[pl.ds(off, BLK), sl] += _dot(a.astype(BF16), dob, TN)
                before_s = _split_dot(de, before, 1)
                dl1m = e_pre + before_s
                if diag:
                    dl1m = jnp.where(col < row, dl1m, 0.0)
                sig = jnp.exp(lb)
                dzb = ((de * (1.0 - sig) - dl1m * sig) * scale).astype(BF16)
                dk_acc[pl.ds(off, BLK), sl] += _dot(dzb, qh, TN)
                out.append((m_pre + upto_s[:, BLK - 1:BLK], e_pre + before_s[:, BLK - 1:BLK] + de[:, BLK - 1:BLK],
                            dq + _dot(dzb, kh, NN)))
            return tuple(out)

        zero = jnp.zeros((BLK, 1), F32)
        carry = tuple((zero, zero, jnp.zeros((BLK, GRP), F32)) for _ in range(hp))
        carry = lax.fori_loop(0, i, lambda j, c: block(j, c, False), carry)
        carry = block(i, carry, True)
        for h in range(hp):
            dq_ref[:, h * GRP:(h + 1) * GRP] = carry[h][2].astype(BF16)

        @pl.when(i == nq - 1)
        def _():
            dk_ref[...] = dk_acc[...].astype(BF16)
            dv_ref[...] = dv_acc[...].astype(BF16)

    blk_q = lambda h, i: (i, h)
    whole = lambda h, i: (0, h)
    return pl.pallas_call(
        body, name="sb_bwd", grid=(nhp, nq),
        out_shape=(jax.ShapeDtypeStruct((s, sb), BF16),) * 3 + (jax.ShapeDtypeStruct((1, sb), F32),),
        in_specs=[pl.BlockSpec((BLK, w), lambda h, i: (i, qc + h)), pl.BlockSpec((s, w), lambda h, i: (0, kc + h)),
                  pl.BlockSpec((s, w), lambda h, i: (0, vc + h)), pl.BlockSpec((BLK, w), blk_q),
                  pl.BlockSpec((hp, BLK, 1), lambda h, i: (h, i, 0)),
                  pl.BlockSpec((BLK, w), lambda h, i: (i, gc + h)),
                  pl.BlockSpec((1, w), lambda h, i: (0, gc + h)), _ANY],
        out_specs=(pl.BlockSpec((BLK, w), blk_q), pl.BlockSpec((s, w), whole), pl.BlockSpec((s, w), whole),
                   pl.BlockSpec((1, w), whole)),
        scratch_shapes=[pltpu.VMEM((s, w), BF16), pltpu.VMEM((s, w), BF16),
                        pltpu.VMEM((s, w), F32), pltpu.VMEM((s, w), F32)],
        compiler_params=_params(2),
    )(proj, proj, proj, o_raw, mtot, don, gamma, after)


def _sb_fwd_one_head(proj, gamma, gm, sb):
    s = proj.shape[0]
    nh, nq = sb // GRP, s // BLK
    qc, kc, vc = 2 * gm // GRP, (2 * gm + sb) // GRP, (2 * gm + 2 * sb) // GRP
    scale = GRP ** -0.5

    def body(q_ref, k_ref, v_ref, gam_ref, on_ref, o_ref, mt_ref, kb, vb):
        i = pl.program_id(1)

        @pl.when(i == 0)
        def _():
            kb[...] = k_ref[...].astype(BF16)
            vb[...] = v_ref[...].astype(BF16)

        qb = q_ref[...].astype(BF16)
        row = lax.broadcasted_iota(jnp.int32, (BLK, BLK), 0)
        col = lax.broadcasted_iota(jnp.int32, (BLK, BLK), 1)
        later = (row > col).astype(BF16)

        def step(jj, carry):
            tail, acc = carry
            j = i - jj
            off = pl.multiple_of(j * BLK, BLK)
            z = _dot(qb, kb[pl.ds(off, BLK), :], NT) * scale
            lb = _log_sigmoid(z)
            mask = col + j * BLK < row + i * BLK
            l1m = jnp.where(mask, lb - z, 0.0)
            a = jnp.where(mask, jnp.exp(lb + _split_dot(l1m, later, 3) + tail), 0.0)
            acc = acc + _dot(a.astype(BF16), vb[pl.ds(off, BLK), :], NN)
            return tail + jnp.sum(l1m, axis=-1, keepdims=True), acc

        tail, acc = lax.fori_loop(0, i + 1, step, (jnp.zeros((BLK, 1), F32), jnp.zeros((BLK, GRP), F32)))
        rstd = lax.rsqrt(jnp.mean(acc * acc, axis=-1, keepdims=True) + EPS)
        on_ref[...] = (acc * rstd * gam_ref[...]).astype(BF16)
        o_ref[...] = acc
        mt_ref[...] = tail

    return pl.pallas_call(
        body, name="sb_fwd", grid=(nh, nq),
        out_shape=(jax.ShapeDtypeStruct((s, sb), BF16), jax.ShapeDtypeStruct((s, sb), F32),
                   jax.ShapeDtypeStruct((nh, s, 1), F32)),
        in_specs=[pl.BlockSpec((BLK, GRP), lambda h, i: (i, qc + h)), pl.BlockSpec((s, GRP), lambda h, i: (0, kc + h)),
                  pl.BlockSpec((s, GRP), lambda h, i: (0, vc + h)),
                  pl.BlockSpec((1, GRP), lambda h, i: (0, gm // GRP + h))],
        out_specs=(pl.BlockSpec((BLK, GRP), lambda h, i: (i, h)), pl.BlockSpec((BLK, GRP), lambda h, i: (i, h)),
                   pl.BlockSpec((None, BLK, 1), lambda h, i: (h, i, 0))),
        scratch_shapes=[pltpu.VMEM((s, GRP), BF16), pltpu.VMEM((s, GRP), BF16)],
        compiler_params=_params(2),
    )(proj, proj, proj, gamma)


def _sb_bwd_one_head(proj, o_raw, mtot, don, gamma, gm, sb):
    s = proj.shape[0]
    nh, nq = sb // GRP, s // BLK
    qc, kc, vc = 2 * gm // GRP, (2 * gm + sb) // GRP, (2 * gm + 2 * sb) // GRP
    scale = GRP ** -0.5

    def body(q_ref, k_ref, v_ref, o_ref, mt_ref, don_ref, gam_ref, dq_ref, dk_ref, dv_ref, dgam_ref,
             kb, vb, dk_acc, dv_acc):
        i = pl.program_id(1)

        @pl.when(i == 0)
        def _():
            kb[...] = k_ref[...].astype(BF16)
            vb[...] = v_ref[...].astype(BF16)
            dk_acc[...] = jnp.zeros_like(dk_acc)
            dv_acc[...] = jnp.zeros_like(dv_acc)
            dgam_ref[...] = jnp.zeros_like(dgam_ref)

        o, dn, gam = o_ref[...], don_ref[...], gam_ref[...]
        rstd = lax.rsqrt(jnp.mean(o * o, axis=-1, keepdims=True) + EPS)
        ohat = o * rstd
        dgam_ref[...] += jnp.sum(dn * ohat, axis=0, keepdims=True)
        dohat = dn * gam
        dob = (rstd * (dohat - ohat * jnp.mean(dohat * ohat, axis=-1, keepdims=True))).astype(BF16)

        qb = q_ref[...].astype(BF16)
        mt = mt_ref[...]
        row = lax.broadcasted_iota(jnp.int32, (BLK, BLK), 0)
        col = lax.broadcasted_iota(jnp.int32, (BLK, BLK), 1)
        upto = (row <= col).astype(BF16)
        before = (row < col).astype(BF16)

        def step(j, carry):
            m_pre, e_pre, dq = carry
            off = pl.multiple_of(j * BLK, BLK)
            kj, vj = kb[pl.ds(off, BLK), :], vb[pl.ds(off, BLK), :]
            z = _dot(qb, kj, NT) * scale
            lb = _log_sigmoid(z)
            mask = col + j * BLK < row + i * BLK
            l1m = jnp.where(mask, lb - z, 0.0)
            tail = mt - m_pre - _split_dot(l1m, upto, 3)
            a = jnp.where(mask, jnp.exp(lb + tail), 0.0)
            de = a * _dot(dob, vj, NT)
            dv_acc[pl.ds(off, BLK), :] += _dot(a.astype(BF16), dob, TN)
            dl1m = e_pre + _split_dot(de, before, 2)
            sig = jnp.exp(lb)
            dz = (de * (1.0 - sig) - jnp.where(mask, dl1m * sig, 0.0)) * scale
            dzb = dz.astype(BF16)
            dk_acc[pl.ds(off, BLK), :] += _dot(dzb, qb, TN)
            return (m_pre + jnp.sum(l1m, axis=-1, keepdims=True), e_pre + jnp.sum(de, axis=-1, keepdims=True),
                    dq + _dot(dzb, kj, NN))

        zero = jnp.zeros((BLK, 1), F32)
        _, _, dq = lax.fori_loop(0, i + 1, step, (zero, zero, jnp.zeros((BLK, GRP), F32)))
        dq_ref[...] = dq.astype(BF16)

        @pl.when(i == nq - 1)
        def _():
            dk_ref[...] = dk_acc[...].astype(BF16)
            dv_ref[...] = dv_acc[...].astype(BF16)

    blk_q = lambda h, i: (i, h)
    whole = lambda h, i: (0, h)
    return pl.pallas_call(
        body, name="sb_bwd", grid=(nh, nq),
        out_shape=(jax.ShapeDtypeStruct((s, sb), BF16),) * 3 + (jax.ShapeDtypeStruct((1, sb), F32),),
        in_specs=[pl.BlockSpec((BLK, GRP), lambda h, i: (i, qc + h)), pl.BlockSpec((s, GRP), lambda h, i: (0, kc + h)),
                  pl.BlockSpec((s, GRP), lambda h, i: (0, vc + h)), pl.BlockSpec((BLK, GRP), blk_q),
                  pl.BlockSpec((None, BLK, 1), lambda h, i: (h, i, 0)),
                  pl.BlockSpec((BLK, GRP), lambda h, i: (i, gm // GRP + h)),
                  pl.BlockSpec((1, GRP), lambda h, i: (0, gm // GRP + h))],
        out_specs=(pl.BlockSpec((BLK, GRP), blk_q), pl.BlockSpec((s, GRP), whole), pl.BlockSpec((s, GRP), whole),
                   pl.BlockSpec((1, GRP), whole)),
        scratch_shapes=[pltpu.VMEM((s, GRP), BF16), pltpu.VMEM((s, GRP), BF16),
                        pltpu.VMEM((s, GRP), F32), pltpu.VMEM((s, GRP), F32)],
        compiler_params=_params(2),
    )(proj, proj, proj, o_raw, mtot, don, gamma)


FFN_ROW_CHUNKS = 2


def _row_chunks(tm):
    n = FFN_ROW_CHUNKS if tm % (16 * FFN_ROW_CHUNKS) == 0 else 1
    return [slice(k * (tm // n), (k + 1) * (tm // n)) for k in range(n)]


def _ffn_up(h2, wg, wu):
    s, d = h2.shape
    nb, fb, _ = wg.shape
    tm = _tile(s, 1024, 16)

    def body(h_ref, wg_ref, wu_ref, g_ref, u_ref, a_ref):
        for rows_ in _row_chunks(tm):
            hv = h_ref[rows_, :]
            g = _dot(hv, wg_ref[...], NT)
            u = _dot(hv, wu_ref[...], NT)
            g_ref[rows_, :] = g.astype(BF16)
            u_ref[rows_, :] = u.astype(BF16)
            a_ref[rows_, :] = (g * jax.nn.sigmoid(g) * u).astype(BF16)

    rows = pl.BlockSpec((tm, d), lambda i, j: (i, 0))
    wblk = pl.BlockSpec((None, fb, d), lambda i, j: (j, 0, 0))
    hid = pl.BlockSpec((None, tm, fb), lambda i, j: (j, i, 0))
    hid_sds = jax.ShapeDtypeStruct((nb, s, fb), BF16)
    return pl.pallas_call(
        body, name="ffn_up", grid=(s // tm, nb), out_shape=(hid_sds,) * 3, in_specs=[rows, wblk, wblk],
        out_specs=(hid,) * 3, compiler_params=_params(2),
    )(h2, wg, wu)


def _ffn_fwd_fused(h2, wg, wu, wd):
    s, d = h2.shape
    nb, fb, _ = wd.shape
    tm = _tile(s, 512, 16)

    def body(h_ref, wg_ref, wu_ref, wd_ref, f_ref, g_ref, u_ref):
        j = pl.program_id(1)
        parts = []
        for rows_ in _row_chunks(tm):
            hv = h_ref[rows_, :]
            g = _dot(hv, wg_ref[...], NT)
            u = _dot(hv, wu_ref[...], NT)
            g_ref[rows_, :] = g.astype(BF16)
            u_ref[rows_, :] = u.astype(BF16)
            parts.append(_dot((g * jax.nn.sigmoid(g) * u).astype(BF16), wd_ref[...], NN))

        @pl.when(j == 0)
        def _():
            for rows_, part in zip(_row_chunks(tm), parts):
                f_ref[rows_, :] = part

        @pl.when(j > 0)
        def _():
            for rows_, part in zip(_row_chunks(tm), parts):
                f_ref[rows_, :] += part

    rows = pl.BlockSpec((tm, d), lambda i, j: (i, 0))
    wblk = pl.BlockSpec((None, fb, d), lambda i, j: (j, 0, 0))
    hid = pl.BlockSpec((None, tm, fb), lambda i, j: (j, i, 0))
    return pl.pallas_call(
        body, name="ffn_fwd", grid=(s // tm, nb),
        out_shape=(jax.ShapeDtypeStruct((s, d), F32), jax.ShapeDtypeStruct((nb, s, fb), BF16),
                   jax.ShapeDtypeStruct((nb, s, fb), BF16)),
        in_specs=[rows, wblk, wblk, wblk],
        out_specs=(rows, hid, hid), compiler_params=_params(2),
    )(h2, wg, wu, wd)


def _ffn_bwd(df, g_pre, u_pre, wg, wu, wd):
    s, d = df.shape
    nb, fb, _ = wd.shape
    tm = _tile(s, 512, 16)

    def body(df_ref, g_ref, u_ref, wg_ref, wu_ref, wd_ref, dh_ref, dg_ref, du_ref):
        j = pl.program_id(1)
        parts = []
        for rows_ in _row_chunks(tm):
            g, u = g_ref[rows_, :].astype(F32), u_ref[rows_, :].astype(F32)
            da = _dot(df_ref[rows_, :], wd_ref[...], NT)
            sg = jax.nn.sigmoid(g)
            silu = g * sg
            dg =(da * u * (sg * (1.0 + g * (1.0 - sg)))).astype(BF16)
            du = (da * silu).astype(BF16)
            dg_ref[rows_, :] = dg
            du_ref[rows_, :] = du
            parts.append(_dot(dg, wg_ref[...], NN) + _dot(du, wu_ref[...], NN))

        @pl.when(j == 0)
        def _():
            for rows_, part in zip(_row_chunks(tm), parts):
                dh_ref[rows_, :] = part

        @pl.when(j > 0)
        def _():
            for rows_, part in zip(_row_chunks(tm), parts):
                dh_ref[rows_, :] += part

    rows = pl.BlockSpec((tm, d), lambda i, j: (i, 0))
    wblk = pl.BlockSpec((None, fb, d), lambda i, j: (j, 0, 0))
    hid = pl.BlockSpec((None, tm, fb), lambda i, j: (j, i, 0))
    hid_sds = jax.ShapeDtypeStruct((nb, s, fb), BF16)
    return pl.pallas_call(
        body, name="ffn_bwd", grid=(s // tm, nb),
        out_shape=(jax.ShapeDtypeStruct((s, d), F32), hid_sds, hid_sds),
        in_specs=[rows, hid, hid, wblk, wblk, wblk],
        out_specs=(rows, hid, hid), compiler_params=_params(2),
    )(df, g_pre, u_pre, wg, wu, wd)


def _reduce_adamw(parts, w, m, v, name, after):
    npart, r, c = parts.shape
    tr = _tile(r, max(16, 262144 // c), 16)
    c1, c2 = 1.0 - ADAM_B1 ** ADAM_STEP, 1.0 - ADAM_B2 ** ADAM_STEP

    def body(p_ref, w_ref, m_ref, v_ref, after_ref, g_ref, d_ref, nm_ref, nv_ref):
        g = p_ref[0].astype(F32)
        for k in range(1, npart):
            g = g + p_ref[k].astype(F32)
        nm = ADAM_B1 * m_ref[...] + (1.0 - ADAM_B1) * g
        nv = ADAM_B2 * v_ref[...] + (1.0 - ADAM_B2) * (g * g)
        g_ref[...] = g
        nm_ref[...] = nm
        nv_ref[...] = nv
        d_ref[...] = -ADAM_LR * ((nm / c1) / (jnp.sqrt(nv / c2) + ADAM_EPS) + ADAM_WD * w_ref[...])

    blk = pl.BlockSpec((tr, c), lambda i: (i, 0))
    sds = jax.ShapeDtypeStruct((r, c), F32)
    return pl.pallas_call(
        body, name=name, grid=(r // tr,), out_shape=(sds,) * 4,
        in_specs=[pl.BlockSpec((npart, tr, c), lambda i: (0, i, 0)), blk, blk, blk, _ANY], out_specs=(blk,) * 4,
        compiler_params=_params(1),
    )(parts, w, m, v, after)


def _pack(vecs):
    rows = jnp.concatenate([a.reshape(-1, GRP) for a in vecs], axis=0)
    pad = -rows.shape[0] % 64
    return jnp.pad(rows, ((0, pad), (0, 0)))


def _unpack(rows, shapes):
    out, at = [], 0
    for shp in shapes:
        n = 1
        for k in shp:
            n *= k
        out.append(rows[at:at + n // GRP].reshape(shp))
        at += n // GRP
    return out


def kernel(x, c, w_ada, b_ada, norm1_g, w_in, v_norm_g, w_spatial, b_spatial, out_norm_g, w_out, norm2_g, w_gate, w_up, w_down, final_g, loss_target, m_w_ada, m_b_ada, m_norm1_g, m_w_in, m_v_norm_g, m_w_spatial, m_b_spatial, m_out_norm_g, m_w_out, m_norm2_g, m_w_gate, m_w_up, m_w_down, m_final_g, v_w_ada, v_b_ada, v_norm1_g, v_w_in, v_v_norm_g, v_w_spatial, v_b_spatial, v_out_norm_g, v_w_out, v_norm2_g, v_w_gate, v_w_up, v_w_down, v_final_g):
    s, d = x.shape[1], x.shape[2]
    gm = v_norm_g.shape[1]
    sb = d - gm
    n_in, ffb, ob = w_in.shape[2], w_gate.shape[2], w_out.shape[1]
    xs, tgt = x[0], loss_target[0]
    me = 4 * lax.axis_index("x") + 2 * lax.axis_index("y") + lax.axis_index("c")

    c_all = _exchange(c, False, "gather_c")[:, 0, :]
    mod_cols = _ada_fwd(c_all, w_ada[0])
    mod = _exchange(mod_cols[:, None, :], True, "scatter_mod").reshape(1, 6 * d) + b_ada
    shift1, scale1, gate1, shift2, scale2, gate2 = [mod[:, k * d:(k + 1) * d] for k in range(6)]

    tok, gathers = mod, {}
    tr_ = lambda a: jnp.swapaxes(a, 1, 2)
    w_gate, m_w_gate, v_w_gate, w_up, m_w_up, v_w_up = map(tr_, (w_gate, m_w_gate, v_w_gate, w_up, m_w_up, v_w_up))
    for nm, w in (("w_in", w_in), ("w_out", w_out), ("w_gate", w_gate), ("w_up", w_up), ("w_down", w_down)):
        gathers[nm], tok = _gather_start(w[0].astype(BF16), tok, "gather_start_" + nm)
    shift1 = shift1 + tok[0:1, 0:1]

    tm = _tile(s, 512, 16)
    nt = s // tm
    rows_d = pl.BlockSpec((tm, d), lambda i, j: (i, 0))
    h1 = _prenorm_fwd(xs, norm1_g, scale1, shift1, "prenorm1")
    w_in_all = _gather_wait(_gather_forward(gathers["w_in"], h1, "gather_fwd_w_in")[0], h1, "gather_wait_w_in")
    proj = _mm(h1, w_in_all, pl.BlockSpec((s, d), lambda j: (0, 0)), pl.BlockSpec((None, d, n_in), lambda j: (j, 0, 0)),
               jax.ShapeDtypeStruct((s, NDEV * n_in), F32), pl.BlockSpec((s, n_in), lambda j: (0, j)),
               (NDEV,), NN, False, "proj")
    fwd_w_out, tok = _gather_forward(gathers["w_out"], proj, "gather_fwd_w_out")
    b_cols = jnp.pad(b_spatial[0].T, ((0, 0), (0, GRP - b_spatial.shape[1])))
    on_sb, o_sb, a_sv, s_sv = _sb_fwd(proj, out_norm_g, gm, sb, tok)
    fwd_w_gate, tok = _gather_forward(gathers["w_gate"], on_sb, "gather_fwd_w_gate")
    on_gm = _gmlp_fwd(proj, v_norm_g, w_spatial[0], b_cols, out_norm_g, gm, tok)
    fwd_w_up, tok = _gather_forward(gathers["w_up"], on_gm, "gather_fwd_w_up")
    o_n = jnp.concatenate([on_gm, on_sb], axis=1)
    w_out_all = _gather_wait(fwd_w_out, tok, "gather_wait_w_out").reshape(d, d)
    rows_1 = pl.BlockSpec((tm, d), lambda i: (i, 0))
    whole_1 = pl.BlockSpec((d, d), lambda i: (0, 0))
    p_out = _mm(o_n, w_out_all, rows_1, whole_1, jax.ShapeDtypeStruct((s, d), F32), rows_1, (nt,), NN, False,
                "out_proj")
    x1, h2 = _residual_prenorm(xs, gate1, p_out, norm2_g, scale2, shift2, "residual1_prenorm2")
    w_gate_all = _gather_wait(fwd_w_gate, h2, "gather_wait_w_gate")
    w_up_all = _gather_wait(fwd_w_up, h2, "gather_wait_w_up")
    g_pre, u_pre, act = _ffn_up(h2, w_gate_all, w_up_all)
    fwd_w_down, tok = _gather_forward(gathers["w_down"], act, "gather_fwd_w_down")
    w_down_all = _gather_wait(fwd_w_down, tok, "gather_wait_w_down")
    tm2 = _tile(s, 1024, 16)
    f_out = _mm(act, w_down_all, pl.BlockSpec((None, tm2, ffb), lambda i, j: (j, i, 0)),
                pl.BlockSpec((None, ffb, d), lambda i, j: (j, 0, 0)), jax.ShapeDtypeStruct((s, d), F32),
                pl.BlockSpec((tm2, d), lambda i, j: (i, 0)), (s // tm2, NDEV), NN, True, "ffn_down")

    dx2, df, st_f = _final_loss(x1, f_out, final_g.reshape(1, d), tgt, gate2)
    dh2, dg, du = _ffn_bwd(df, g_pre, u_pre, w_gate_all, w_up_all, w_down_all)
    hid = pl.BlockSpec((None, s, ffb), lambda j: (j, 0, 0))
    all_d = pl.BlockSpec((s, d), lambda j: (0, 0))
    col_sds = jax.ShapeDtypeStruct((NDEV, d, ffb), BF16)
    col_out = pl.BlockSpec((None, d, ffb), lambda j: (j, 0, 0))

    def second_leg(first, after, tag):
        parts, lands = _scatter_wait1(first, after, "scatter_wait1_" + tag)
        sums = [_chip_sum(p, l, "chip_sum_%s_%d" % (tag, k)) for k, (p, l) in enumerate(zip(parts, lands))]
        return _scatter_start2(sums, after, "scatter_start2_" + tag)

    row_sds = jax.ShapeDtypeStruct((NDEV, ffb, d), BF16)
    row_out = pl.BlockSpec((None, ffb, d), lambda j: (j, 0, 0))
    gw_gate = _mm(dg, h2, hid, all_d, row_sds, row_out, (NDEV,), TN, False, "grad_w_gate")
    gw_up = _mm(du, h2, hid, all_d, row_sds, row_out, (NDEV,), TN, False, "grad_w_up")
    gw_down = _mm(act, df, hid, all_d, row_sds, row_out, (NDEV,), TN, False, "grad_w_down")
    first_ffn, tok = _scatter_start1([gw_gate, gw_up, gw_down], tok, "scatter_start1_ffn")
    dx1, dp, st2 = _prenorm_bwd(x1, dh2, dx2, f_out, norm2_g, scale2, gate1, "prenorm2_bwd", tok)
    don = _mm(dp, w_out_all, rows_1, whole_1, jax.ShapeDtypeStruct((s, d), F32), rows_1, (nt,), NT, False,
              "out_proj_bwd")
    gw_out = _mm(o_n, dp, pl.BlockSpec((s, ob), lambda j: (0, j)), all_d,
                 jax.ShapeDtypeStruct((NDEV, ob, d), BF16), pl.BlockSpec((None, ob, d), lambda j: (j, 0, 0)),
                 (NDEV,), TN, False, "grad_w_out", don)
    first_out, tok = _scatter_start1([gw_out], tok, "scatter_start1_out")
    second_ffn, tok = second_leg(first_ffn, tok, "ffn")
    dproj_gm, dw_sp, db_cols, dgv, dgam_gm = _gmlp_bwd(proj, don, v_norm_g, w_spatial[0], b_cols, out_norm_g, gm, tok)
    second_out, tok = second_leg(first_out, dproj_gm, "out")
    dq, dk, dv, dgam_sb = _sb_bwd(proj, o_sb, a_sv, s_sv, don, out_norm_g, gm, sb, tok)
    dproj = jnp.concatenate([dproj_gm, dq, dk, dv], axis=1)
    gw_in = _mm(h1, dproj, all_d, pl.BlockSpec((s, n_in), lambda j: (0, j)),
                jax.ShapeDtypeStruct((NDEV, d, n_in), BF16), pl.BlockSpec((None, d, n_in), lambda j: (j, 0, 0)),
                (NDEV,), TN, False, "grad_w_in")
    first_in, tok = _scatter_start1([gw_in], tok, "scatter_start1_in")
    dh1 = _mm(dproj, w_in_all, pl.BlockSpec((tm2, n_in), lambda i, j: (i, j)),
              pl.BlockSpec((None, d, n_in), lambda i, j: (j, 0, 0)), jax.ShapeDtypeStruct((s, d), F32),
              pl.BlockSpec((tm2, d), lambda i, j: (i, 0)), (s // tm2, NDEV), NT, True, "in_proj_bwd", tok)
    grad_x, _, st1 = _prenorm_bwd(xs, dh1, dx1, p_out, norm1_g, scale1, gate1, "prenorm1_bwd", tok)

    dmod = jnp.concatenate([st1[0], st1[1], st1[3], st2[0], st2[1], st2[3]])
    loss_row = jnp.pad((0.5 * jnp.sum(st_f[1]) / d).reshape(1, 1), ((0, 0), (0, GRP - 1)))
    small = [st1[2], dgv, dw_sp, db_cols[:, :b_spatial.shape[1]].T, jnp.concatenate([dgam_gm, dgam_sb], axis=1),
             st2[2], st_f[0], dmod, loss_row]
    zero_row = jnp.zeros((1, GRP), F32)
    small_w = [norm1_g, v_norm_g, w_spatial, b_spatial, out_norm_g, norm2_g, final_g, b_ada, zero_row]
    small_m = [m_norm1_g, m_v_norm_g, m_w_spatial, m_b_spatial, m_out_norm_g, m_norm2_g, m_final_g, m_b_ada, zero_row]
    small_v = [v_norm1_g, v_v_norm_g, v_w_spatial, v_b_spatial, v_out_norm_g, v_norm2_g, v_final_g, v_b_ada, zero_row]
    small_first, tok = _gather_start(_pack(small), grad_x, "gather_small_start")
    second_in, tok = second_leg(first_in, tok, "in")

    big = {}
    prev = tok
    for second, group in ((second_ffn, (("w_gate", w_gate, m_w_gate, v_w_gate), ("w_up", w_up, m_w_up, v_w_up),
                                        ("w_down", w_down, m_w_down, v_w_down))),
                          (second_out, (("w_out", w_out, m_w_out, v_w_out),))):
        chip_sums = _scatter_wait2(second, prev, "scatter_wait2_" + group[0][0])
        for (nm, w, m, v), part in zip(group, chip_sums):
            big[nm] = _reduce_adamw(part, w[0], m[0], v[0], "adamw_" + nm, tok)
            prev = big[nm][1]

    small_second, tok2 = _gather_forward(small_first, prev, "gather_small_fwd")
    small_all = _gather_wait(small_second, tok2, "gather_small_wait")
    sm = _reduce_adamw(small_all, _pack(small_w), _pack(small_m), _pack(small_v), "adamw_small", tok)
    shapes = [a.shape for a in small_w]
    sm_g, sm_d, sm_m, sm_v = [_unpack(t, shapes) for t in sm]
    loss = sm_g[8][0, 0]

    at = sum(a.size for a in small_w[:7]) // GRP
    dmod_all = small_all[:, at:at + 6 * d // GRP, :].reshape(NDEV, 6 * d)
    n_ada = w_ada.shape[2]
    dmod_cols = lax.dynamic_slice(dmod_all, (0, me * n_ada), (NDEV, n_ada))
    g_ada = _ada_bwd(c_all, dmod_cols)
    big["w_ada"] = _reduce_adamw(g_ada[None], w_ada[0], m_w_ada[0], v_w_ada[0], "adamw_w_ada", sm[0])
    part_in, = _scatter_wait2(second_in, big["w_ada"][1], "scatter_wait2_w_in")
    big["w_in"] = _reduce_adamw(part_in, w_in[0], m_w_in[0], v_w_in[0], "adamw_w_in", tok)

    names = ["w_ada", "b_ada", "norm1_g", "w_in", "v_norm_g", "w_spatial", "b_spatial", "out_norm_g", "w_out",
             "norm2_g", "w_gate", "w_up", "w_down", "final_g"]
    small_at = {"norm1_g": 0, "v_norm_g": 1, "w_spatial": 2, "b_spatial": 3, "out_norm_g": 4, "norm2_g": 5,
                "final_g": 6, "b_ada": 7}
    outs = [[], [], [], []]
    for nm in names:
        for k in range(4):
            if nm in big:
                res = big[nm][k][None]
                outs[k].append(tr_(res) if nm in ("w_gate", "w_up") else res)
            else:
                outs[k].append((sm_g, sm_d, sm_m, sm_v)[k][small_at[nm]])
    return (loss, grad_x[None], *outs[0], *outs[1], *outs[2], *outs[3])

    hid = pl.BlockSpec((None, s, ffb), lambda j: (j, 0, 0))
    all_d = pl.BlockSpec((s, d), lambda j: (0, 0))
    col_sds = jax.ShapeDtypeStruct((NDEV, d, ffb), BF16)
    col_out = pl.BlockSpec((None, d, ffb), lambda j: (j, 0, 0))
    scatters = {}
    gw_gate = _mm(h2, dg, all_d, hid, col_sds, col_out, (NDEV,), TN, False, "grad_w_gate")
    scatters["w_gate"], tok = _exchange_start(gw_gate, True, tok, "scatter_start_w_gate")
    gw_up = _mm(h2, du, all_d, hid, col_sds, col_out, (NDEV,), TN, False, "grad_w_up", tok)
    scatters["w_up"], tok = _exchange_start(gw_up, True, tok, "scatter_start_w_up")
    gw_down = _mm(act, df, hid, all_d, jax.ShapeDtypeStruct((NDEV, ffb, d), BF16),
                  pl.BlockSpec((None, ffb, d), lambda j: (j, 0, 0)), (NDEV,), TN, False, "grad_w_down", tok)
    scatters["w_down"], tok = _exchange_start(gw_down, True, tok, "scatter_start_w_down")
    dx1, dp, st2 = _prenorm_bwd(x1, dh2, dx2, f_out, norm2_g, scale2, gate1, "prenorm2_bwd", tok)
    don = _mm(dp, w_out_all, rows_1, whole_1, jax.ShapeDtypeStruct((s, d), F32), rows_1, (nt,), NT, False,
              "out_proj_bwd")
    gw_out = _mm(o_n, dp, pl.BlockSpec((s, ob), lambda j: (0, j)), all_d,
                 jax.ShapeDtypeStruct((NDEV, ob, d), BF16), pl.BlockSpec((None, ob, d), lambda j: (j, 0, 0)),
                 (NDEV,), TN, False, "grad_w_out", don)
    scatters["w_out"], tok = _exchange_start(gw_out, True, tok, "scatter_start_w_out")
    dproj_gm, dw_sp, db_cols, dgv, dgam_gm = _gmlp_bwd(proj, don, v_norm_g, w_spatial[0], b_cols, out_norm_g, gm, tok)
    dq, dk, dv, dgam_sb = _sb_bwd(proj, o_sb, mtot, don, out_norm_g, gm, sb)
    dproj = jnp.concatenate([dproj_gm, dq, dk, dv], axis=1)
    dh1 = _mm(dproj, w_in_all, pl.BlockSpec((tm, n_in), lambda i, j: (i, j)),
              pl.BlockSpec((None, d, n_in), lambda i, j: (j, 0, 0)), jax.ShapeDtypeStruct((s, d), F32), rows_d,
              (nt, NDEV), NT, True, "in_proj_bwd")
    grad_x, _, st1 = _prenorm_bwd(xs, dh1, dx1, p_out, norm1_g, scale1, gate1, "prenorm1_bwd", tok)

    dmod = jnp.concatenate([st1[0], st1[1], st1[3], st2[0], st2[1], st2[3]])
    loss_row = jnp.pad((0.5 * jnp.sum(st_f[1]) / d).reshape(1, 1), ((0, 0), (0, GRP - 1)))
    small = [st1[2], dgv, dw_sp, db_cols[:, :b_spatial.shape[1]].T, jnp.concatenate([dgam_gm, dgam_sb], axis=1),
             st2[2], st_f[0], dmod, loss_row]
    zero_row = jnp.zeros((1, GRP), F32)
    small_w = [norm1_g, v_norm_g, w_spatial, b_spatial, out_norm_g, norm2_g, final_g, b_ada, zero_row]
    small_m = [m_norm1_g, m_v_norm_g, m_w_spatial, m_b_spatial, m_out_norm_g, m_norm2_g, m_final_g, m_b_ada, zero_row]
    small_v = [v_norm1_g, v_v_norm_g, v_w_spatial, v_b_spatial, v_out_norm_g, v_norm2_g, v_final_g, v_b_ada, zero_row]
    small_handles, tok = _exchange_start(_pack(small), False, tok, "gather_small_start")
    gw_in = _mm(h1, dproj, all_d, pl.BlockSpec((s, n_in), lambda j: (0, j)),
                jax.ShapeDtypeStruct((NDEV, d, n_in), BF16), pl.BlockSpec((None, d, n_in), lambda j: (j, 0, 0)),
                (NDEV,), TN, False, "grad_w_in", tok)
    scatters["w_in"], tok = _exchange_start(gw_in, True, tok, "scatter_start_w_in")
    small_all = _exchange_wait(small_handles, tok, "gather_small_wait")
    sm = _reduce_adamw(small_all, _pack(small_w), _pack(small_m), _pack(small_v), "adamw_small", tok)
    shapes = [a.shape for a in small_w]
    sm_g, sm_d, sm_m, sm_v = [_unpack(t, shapes) for t in sm]

    loss = sm_g[8][0, 0]
    at = sum(a.size for a in small_w[:7]) // GRP
    dmod_all = small_all[:, at:at + 6 * d // GRP, :].reshape(NDEV, 6 * d)
    n_ada = w_ada.shape[2]
    dmod_cols = lax.dynamic_slice(dmod_all, (0, me * n_ada), (NDEV, n_ada))
    g_ada = _ada_bwd(c_all, dmod_cols)
    big = {"w_ada": _reduce_adamw(g_ada[None], w_ada[0], m_w_ada[0], v_w_ada[0], "adamw_w_ada", sm[0])}

    prev = big["w_ada"][1]
    for nm, w, m, v in (("w_gate", w_gate, m_w_gate, v_w_gate), ("w_up", w_up, m_w_up, v_w_up),
                        ("w_down", w_down, m_w_down, v_w_down), ("w_out", w_out, m_w_out, v_w_out),
                        ("w_in", w_in, m_w_in, v_w_in)):
        parts = _exchange_wait(scatters[nm], prev, "scatter_wait_" + nm)
        big[nm] = _reduce_adamw(parts, w[0], m[0], v[0], "adamw_" + nm, tok)
        prev = big[nm][1]

    names = ["w_ada", "b_ada", "norm1_g", "w_in", "v_norm_g", "w_spatial", "b_spatial", "out_norm_g", "w_out",
             "norm2_g", "w_gate", "w_up", "w_down", "final_g"]
    small_at = {"norm1_g": 0, "v_norm_g": 1, "w_spatial": 2, "b_spatial": 3, "out_norm_g": 4, "norm2_g": 5,
                "final_g": 6, "b_ada": 7}
    outs = [[], [], [], []]
    for nm in names:
        for k in range(4):
            outs[k].append(big[nm][k][None] if nm in big else (sm_g, sm_d, sm_m, sm_v)[k][small_at[nm]])
    return (loss, grad_x[None], *outs[0], *outs[1], *outs[2], *outs[3])
```

```python
import functools

import jax
import jax.numpy as jnp
from jax import lax
from jax.experimental import pallas as pl
from jax.experimental.pallas import tpu as pltpu

F32, BF16 = jnp.float32, jnp.bfloat16
NDEV = 8
GRP = 128
EPS = 1e-6
VMEM_BYTES = 64 * 2 ** 20
VMEM_LIMIT = VMEM_BYTES - 8 * 2 ** 20
ADAM_LR, ADAM_B1, ADAM_B2, ADAM_EPS, ADAM_WD, ADAM_STEP = 0.001, 0.9, 0.999, 1e-08, 0.01, 10
MESH = pl.DeviceIdType.MESH
NN = (((1,), (0,)), ((), ()))
NT = (((1,), (1,)), ((), ()))
TN = (((0,), (0,)), ((), ()))


def _params(n_axes):
    return pltpu.CompilerParams(dimension_semantics=("arbitrary",) * n_axes, vmem_limit_bytes=VMEM_LIMIT)


def _tile(n, cap, mult):
    best = None
    for t in range(mult, min(n, cap) + 1, mult):
        if n % t == 0:
            best = t
    assert best is not None, (n, cap, mult)
    return best


def _dot(a, b, dims):
    return lax.dot_general(a, b, dims, preferred_element_type=F32)


def _exchange(src, per_peer, name):
    blk = src.shape[1:] if per_peer else src.shape

    def body(src_ref, out_ref, send_sems, recv_sems, local_sem):
        x, y, c = lax.axis_index("x"), lax.axis_index("y"), lax.axis_index("c")
        me = 4 * x + 2 * y + c
        local = pltpu.make_async_copy(src_ref.at[me] if per_peer else src_ref, out_ref.at[me], local_sem)
        local.start()
        sends, recvs = [], []
        for k in range(1, NDEV):
            px, py, pc = x ^ (k >> 2), y ^ ((k >> 1) & 1), c ^ (k & 1)
            p = 4 * px + 2 * py + pc
            mine = src_ref.at[p] if per_peer else src_ref
            sends.append(pltpu.make_async_remote_copy(
                src_ref=mine, dst_ref=out_ref.at[me], send_sem=send_sems.at[k - 1], recv_sem=recv_sems.at[k - 1],
                device_id=(px, py, pc), device_id_type=MESH))
            recvs.append(pltpu.make_async_remote_copy(
                src_ref=mine, dst_ref=out_ref.at[p], send_sem=send_sems.at[k - 1], recv_sem=recv_sems.at[k - 1],
                device_id=(px, py, pc), device_id_type=MESH))
        for cp in sends:
            cp.start()
        for cp in recvs:
            cp.wait_recv()
        for cp in sends:
            cp.wait_send()
        local.wait()

    return pl.pallas_call(
        body, name=name,
        out_shape=jax.ShapeDtypeStruct((NDEV,) + tuple(blk), src.dtype),
        in_specs=[pl.BlockSpec(memory_space=pl.ANY)],
        out_specs=pl.BlockSpec(memory_space=pl.ANY),
        scratch_shapes=[pltpu.SemaphoreType.DMA((NDEV - 1,)), pltpu.SemaphoreType.DMA((NDEV - 1,)),
                        pltpu.SemaphoreType.DMA],
    )(src)


_HBM = pl.BlockSpec(memory_space=pltpu.HBM)
_SEM = pl.BlockSpec(memory_space=pltpu.SEMAPHORE)
_ANY = pl.BlockSpec(memory_space=pl.ANY)
_EFFECT = pltpu.SideEffectType.DATAFLOW_SIDE_EFFECTING


def _peers():
    x, y, c = lax.axis_index("x"), lax.axis_index("y"), lax.axis_index("c")
    out = []
    for k in range(1, NDEV):
        px, py, pc = x ^ (k >> 2), y ^ ((k >> 1) & 1), c ^ (k & 1)
        out.append((k, (px, py, pc), 4 * px + 2 * py + pc))
    return 4 * x + 2 * y + c, out


def _exchange_start(src, per_peer, after, name):
    blk = src.shape[1:] if per_peer else src.shape
    me = 4 * lax.axis_index("x") + 2 * lax.axis_index("y") + lax.axis_index("c")
    own = lax.dynamic_index_in_dim(src, me, 0, keepdims=True) if per_peer else src[None]
    land = lax.dynamic_update_slice(lax.empty((NDEV,) + tuple(blk), src.dtype), own, (me,) + (0,) * len(blk))

    def body(src_ref, land_ref, after_ref, send_sems, recv_sems, src_thru, land_thru, token):
        my, peers = _peers()
        for k, coords, p in peers:
            pltpu.make_async_remote_copy(
                src_ref=src_ref.at[p] if per_peer else src_ref, dst_ref=land_ref.at[my],
                send_sem=send_sems.at[k - 1], recv_sem=recv_sems.at[k - 1], device_id=coords, device_id_type=MESH).start()
        token[...] = jnp.zeros_like(token)

    res = pl.pallas_call(
        body, name=name,
        out_shape=(pltpu.SemaphoreType.DMA((NDEV - 1,)), pltpu.SemaphoreType.DMA((NDEV - 1,)),
                   pltpu.HBM(src.shape, src.dtype), pltpu.HBM(land.shape, land.dtype), jax.ShapeDtypeStruct((8, GRP), F32)),
        in_specs=(_HBM, _HBM, _ANY), out_specs=(_SEM, _SEM, _HBM, _HBM, pl.BlockSpec(memory_space=pltpu.VMEM)),
        input_output_aliases={0: 2, 1: 3}, compiler_params=pltpu.CompilerParams(has_side_effects=_EFFECT),
    )(pltpu.with_memory_space_constraint(src, pltpu.HBM), pltpu.with_memory_space_constraint(land, pltpu.HBM), after)
    return (per_peer,) + tuple(res[:4]), res[4]


def _exchange_wait(handles, after, name):
    per_peer, send_sems, recv_sems, src_thru, land_thru = handles

    def body(src_ref, land_ref, send_sems, recv_sems, after_ref, src_dead, got_ref):
        _, peers = _peers()
        for k, coords, p in peers:
            cp = pltpu.make_async_remote_copy(
                src_ref=src_ref.at[p] if per_peer else src_ref, dst_ref=land_ref.at[p],
                send_sem=send_sems.at[k - 1], recv_sem=recv_sems.at[k - 1], device_id=coords, device_id_type=MESH)
            cp.wait_send()
            cp.wait_recv()

    return pl.pallas_call(
        body, name=name,
        out_shape=(pltpu.HBM(src_thru.shape, src_thru.dtype), pltpu.HBM(land_thru.shape, land_thru.dtype)),
        in_specs=(_HBM, _HBM, _SEM, _SEM, _ANY), out_specs=(_HBM, _HBM), input_output_aliases={0: 0, 1: 1},
        compiler_params=pltpu.CompilerParams(has_side_effects=_EFFECT),
    )(src_thru, land_thru, send_sems, recv_sems, after)[1]


def _chip_peers():
    x, y, c = lax.axis_index("x"), lax.axis_index("y"), lax.axis_index("c")
    chips = [(x, 1 - y), (1 - x, y), (1 - x, 1 - y)]
    return 4 * x + 2 * y + c, (x, y, 1 - c), [((px, py, c), 4 * px + 2 * py + c) for px, py in chips]


def _gather_start(srcs, after, name):
    n = len(srcs)
    me = 4 * lax.axis_index("x") + 2 * lax.axis_index("y") + lax.axis_index("c")
    lands = [lax.dynamic_update_slice(lax.empty((NDEV,) + a.shape, a.dtype), a[None], (me,) + (0,) * a.ndim) for a in srcs]

    def body(*refs):
        src_refs, land_refs = refs[:n], refs[n:2 * n]
        sems, token = refs[2 * n + 1:2 * n + 1 + 2 * n], refs[-1]
        my, sibling, chips = _chip_peers()
        for a in range(n):
            for k, to in enumerate([sibling] + [coords for coords, _ in chips]):
                pltpu.make_async_remote_copy(src_ref=src_refs[a], dst_ref=land_refs[a].at[my], send_sem=sems[2 * a].at[k],
                                             recv_sem=sems[2 * a + 1].at[k], device_id=to, device_id_type=MESH).start()
        token[...] = jnp.zeros_like(token)

    hbm = lambda arrs: tuple(pltpu.HBM(a.shape, a.dtype) for a in arrs)
    res = pl.pallas_call(
        body, name=name,
        out_shape=(pltpu.SemaphoreType.DMA((4,)),) * (2 * n) + hbm(srcs) + hbm(lands) + (jax.ShapeDtypeStruct((8, GRP), F32),),
        in_specs=(_HBM,) * (2 * n) + (_ANY,),
        out_specs=(_SEM,) * (2 * n) + (_HBM,) * (2 * n) + (pl.BlockSpec(memory_space=pltpu.VMEM),),
        input_output_aliases={k: 2 * n + k for k in range(2 * n)}, compiler_params=pltpu.CompilerParams(has_side_effects=_EFFECT),
    )(*[pltpu.with_memory_space_constraint(a, pltpu.HBM) for a in list(srcs) + lands], after)
    return [(res[2 * a], res[2 * a + 1], res[2 * n + a], res[3 * n + a]) for a in range(n)], res[-1]


def _gather_forward(handles, after, name):
    send_sems, recv_sems, src_thru, land_thru = handles

    def body(src_ref, land_ref, send_sems, recv_sems, after_ref, send2, recv2, land_out, token):
        my, sibling, chips = _chip_peers()
        for k, (to, p) in enumerate([(sibling, my ^ 1)] + chips):
            first = pltpu.make_async_remote_copy(src_ref=src_ref, dst_ref=land_ref.at[p], send_sem=send_sems.at[k],
                                                 recv_sem=recv_sems.at[k], device_id=to, device_id_type=MESH)
            first.wait_send()
            first.wait_recv()
        for k, (_, p) in enumerate(chips):
            pltpu.make_async_remote_copy(src_ref=land_ref.at[p], dst_ref=land_ref.at[p], send_sem=send2.at[k],
                                         recv_sem=recv2.at[k], device_id=sibling, device_id_type=MESH).start()
        token[...] = jnp.zeros_like(token)

    res = pl.pallas_call(
        body, name=name,
        out_shape=(pltpu.SemaphoreType.DMA((3,)), pltpu.SemaphoreType.DMA((3,)), pltpu.HBM(land_thru.shape, land_thru.dtype),
                   jax.ShapeDtypeStruct((8, GRP), F32)),
        in_specs=(_HBM, _HBM, _SEM, _SEM, _ANY), out_specs=(_SEM, _SEM, _HBM, pl.BlockSpec(memory_space=pltpu.VMEM)),
        input_output_aliases={1: 2}, compiler_params=pltpu.CompilerParams(has_side_effects=_EFFECT),
    )(src_thru, land_thru, send_sems, recv_sems, after)
    return tuple(res[:3]), res[3]


def _gather_wait(handles, after, name):
    send2, recv2, land_thru = handles

    def body(land_ref, send2, recv2, after_ref, got_ref):
        _, sibling, chips = _chip_peers()
        for k, (_, p) in enumerate(chips):
            cp = pltpu.make_async_remote_copy(src_ref=land_ref.at[p], dst_ref=land_ref.at[p ^ 1], send_sem=send2.at[k],
                                              recv_sem=recv2.at[k], device_id=sibling, device_id_type=MESH)
            cp.wait_send()
            cp.wait_recv()

    return pl.pallas_call(
        body, name=name, out_shape=pltpu.HBM(land_thru.shape, land_thru.dtype),
        in_specs=(_HBM, _SEM, _SEM, _ANY), out_specs=_HBM, input_output_aliases={0: 0},
        compiler_params=pltpu.CompilerParams(has_side_effects=_EFFECT),
    )(land_thru, send2, recv2, after)


NCHIP = NDEV // 2


def _scatter_start1(parts, after, name):
    n = len(parts)
    lands = [lax.empty((NCHIP,) + p.shape[1:], p.dtype) for p in parts]

    def body(*refs):
        part_refs, land_refs = refs[:n], refs[n:2 * n]
        send_sems, recv_sems, token = refs[2 * n + 1], refs[2 * n + 2], refs[-1]
        _, sibling, _ = _chip_peers()
        c = lax.axis_index("c")
        for a in range(n):
            for q in range(NCHIP):
                pltpu.make_async_remote_copy(
                    src_ref=part_refs[a].at[2 * q + 1 - c], dst_ref=land_refs[a].at[q], send_sem=send_sems.at[a * NCHIP + q],
                    recv_sem=recv_sems.at[a * NCHIP + q], device_id=sibling, device_id_type=MESH).start()
        token[...] = jnp.zeros_like(token)

    hbm = lambda arrs: tuple(pltpu.HBM(a.shape, a.dtype) for a in arrs)
    res = pl.pallas_call(
        body, name=name,
        out_shape=(pltpu.SemaphoreType.DMA((n * NCHIP,)), pltpu.SemaphoreType.DMA((n * NCHIP,))) + hbm(parts) + hbm(lands)
        + (jax.ShapeDtypeStruct((8, GRP), F32),),
        in_specs=(_HBM,) * (2 * n) + (_ANY,),
        out_specs=(_SEM, _SEM) + (_HBM,) * (2 * n) + (pl.BlockSpec(memory_space=pltpu.VMEM),),
        input_output_aliases={k: 2 + k for k in range(2 * n)}, compiler_params=pltpu.CompilerParams(has_side_effects=_EFFECT),
    )(*[pltpu.with_memory_space_constraint(a, pltpu.HBM) for a in list(parts) + lands], after)
    return (n,) + tuple(res[:-1]), res[-1]


def _scatter_wait1(handles, after, name):
    n, send_sems, recv_sems = handles[:3]
    thru = handles[3:]

    def body(*refs):
        part_refs, land_refs, send_sems, recv_sems = refs[:n], refs[n:2 * n], refs[2 * n], refs[2 * n + 1]
        _, sibling, _ = _chip_peers()
        for a in range(n):
            for q in range(NCHIP):
                cp = pltpu.make_async_remote_copy(
                    src_ref=part_refs[a].at[q], dst_ref=land_refs[a].at[q], send_sem=send_sems.at[a * NCHIP + q],
                    recv_sem=recv_sems.at[a * NCHIP + q], device_id=sibling, device_id_type=MESH)
                cp.wait_send()
                cp.wait_recv()

    res = pl.pallas_call(
        body, name=name, out_shape=tuple(pltpu.HBM(a.shape, a.dtype) for a in thru),
        in_specs=(_HBM,) * (2 * n) + (_SEM, _SEM, _ANY), out_specs=(_HBM,) * (2 * n),
        input_output_aliases={k: k for k in range(2 * n)}, compiler_params=pltpu.CompilerParams(has_side_effects=_EFFECT),
    )(*thru, send_sems, recv_sems, after)
    return res[:n], res[n:]


def _chip_sum(part, land, name):
    _, r, c = part.shape
    tr = _tile(r, max(16, 2 ** 21 // c), 16)

    def body(core_ref, p_ref, l_ref, o_ref):
        o_ref[...] = (p_ref[...].astype(F32) + l_ref[...].astype(F32)).astype(o_ref.dtype)

    blk = pl.BlockSpec((None, tr, c), lambda q, i, core: (q, i, 0))
    return pl.pallas_call(
        body, name=name, out_shape=jax.ShapeDtypeStruct(land.shape, land.dtype),
        grid_spec=pltpu.PrefetchScalarGridSpec(
            num_scalar_prefetch=1, grid=(NCHIP, r // tr),
            in_specs=[pl.BlockSpec((None, None, tr, c), lambda q, i, core: (q, core[0], i, 0)), blk], out_specs=blk),
        compiler_params=_params(2),
    )(lax.axis_index("c").astype(jnp.int32).reshape(1), part.reshape(NCHIP, 2, r, c), land)


def _scatter_start2(sums, after, name):
    n = len(sums)
    chip = 2 * lax.axis_index("x") + lax.axis_index("y")
    lands = [lax.dynamic_update_slice(lax.empty(s_.shape, s_.dtype), lax.dynamic_index_in_dim(s_, chip, 0, keepdims=True),
                                      (chip,) + (0,) * (s_.ndim - 1)) for s_ in sums]

    def body(*refs):
        sum_refs, land_refs = refs[:n], refs[n:2 * n]
        send_sems, recv_sems, token = refs[2 * n + 1], refs[2 * n + 2], refs[-1]
        my, _, chips = _chip_peers()
        for a in range(n):
            for k, (to, p) in enumerate(chips):
                pltpu.make_async_remote_copy(
                    src_ref=sum_refs[a].at[p // 2], dst_ref=land_refs[a].at[my // 2], send_sem=send_sems.at[a * 3 + k],
                    recv_sem=recv_sems.at[a * 3 + k], device_id=to, device_id_type=MESH).start()
        token[...] = jnp.zeros_like(token)

    hbm = lambda arrs: tuple(pltpu.HBM(a.shape, a.dtype) for a in arrs)
    res = pl.pallas_call(
        body, name=name,
        out_shape=(pltpu.SemaphoreType.DMA((n * 3,)), pltpu.SemaphoreType.DMA((n * 3,))) + hbm(sums) + hbm(lands)
        + (jax.ShapeDtypeStruct((8, GRP), F32),),
        in_specs=(_HBM,) * (2 * n) + (_ANY,),
        out_specs=(_SEM, _SEM) + (_HBM,) * (2 * n) + (pl.BlockSpec(memory_space=pltpu.VMEM),),
        input_output_aliases={k: 2 + k for k in range(2 * n)}, compiler_params=pltpu.CompilerParams(has_side_effects=_EFFECT),
    )(*[pltpu.with_memory_space_constraint(a, pltpu.HBM) for a in list(sums) + lands], after)
    return (n,) + tuple(res[:-1]), res[-1]


def _scatter_wait2(handles, after, name):
    n, send_sems, recv_sems = handles[:3]
    thru = handles[3:]

    def body(*refs):
        sum_refs, land_refs, send_sems, recv_sems = refs[:n], refs[n:2 * n], refs[2 * n], refs[2 * n + 1]
        _, _, chips = _chip_peers()
        for a in range(n):
            for k, (to, p) in enumerate(chips):
                cp = pltpu.make_async_remote_copy(
                    src_ref=sum_refs[a].at[p // 2], dst_ref=land_refs[a].at[p // 2], send_sem=send_sems.at[a * 3 + k],
                    recv_sem=recv_sems.at[a * 3 + k], device_id=to, device_id_type=MESH)
                cp.wait_send()
                cp.wait_recv()

    res = pl.pallas_call(
        body, name=name, out_shape=tuple(pltpu.HBM(a.shape, a.dtype) for a in thru),
        in_specs=(_HBM,) * (2 * n) + (_SEM, _SEM, _ANY), out_specs=(_HBM,) * (2 * n),
        input_output_aliases={k: k for k in range(2 * n)}, compiler_params=pltpu.CompilerParams(has_side_effects=_EFFECT),
    )(*thru, send_sems, recv_sems, after)
    return res[n:]


def _mm(a, b, a_spec, b_spec, out_sds, o_spec, grid, dims, acc, name, after=None):
    extra = () if after is None else (after,)
    assert not acc or out_sds.dtype == F32

    def body(a_ref, b_ref, *rest):
        o_ref = rest[len(extra)]
        prod = _dot(a_ref[...], b_ref[...], dims)
        if not acc:
            o_ref[...] = prod.astype(o_ref.dtype)
            return
        k = pl.program_id(len(grid) - 1)

        @pl.when(k == 0)
        def _():
            o_ref[...] = prod

        @pl.when(k > 0)
        def _():
            o_ref[...] += prod

    return pl.pallas_call(
        body, name=name, grid=grid, out_shape=out_sds, in_specs=[a_spec, b_spec] + [_ANY] * len(extra), out_specs=o_spec,
        compiler_params=_params(len(grid)),
    )(a, b, *extra)


def _row_spec(tm, d):
    return pl.BlockSpec((tm, d), lambda i: (i, 0))


def _vec_spec(d):
    return pl.BlockSpec((1, d), lambda i: (0, 0))


def _prenorm_fwd(x, g, scale, shift, name):
    s, d = x.shape
    tm = _tile(s, 256, 16)

    def body(x_ref, g_ref, sc_ref, sh_ref, h_ref):
        xv = x_ref[...]
        rstd = lax.rsqrt(jnp.mean(xv * xv, axis=-1, keepdims=True) + EPS)
        h_ref[...] = ((xv * rstd * g_ref[...]) * (1.0 + sc_ref[...]) + sh_ref[...]).astype(BF16)

    return pl.pallas_call(
        body, name=name, grid=(s // tm,), out_shape=jax.ShapeDtypeStruct((s, d), BF16),
        in_specs=[_row_spec(tm, d), _vec_spec(d), _vec_spec(d), _vec_spec(d)], out_specs=_row_spec(tm, d),
        compiler_params=_params(1),
    )(x, g, scale, shift)


def _residual_prenorm(x, gate, p, g, scale, shift, name):
    s, d = x.shape
    tm = _tile(s, 256, 16)

    def body(x_ref, gate_ref, p_ref, g_ref, sc_ref, sh_ref, x1_ref, h_ref):
        xv = x_ref[...] + gate_ref[...] * p_ref[...]
        x1_ref[...] = xv
        rstd = lax.rsqrt(jnp.mean(xv * xv, axis=-1, keepdims=True) + EPS)
        h_ref[...] = ((xv * rstd * g_ref[...]) * (1.0 + sc_ref[...]) + sh_ref[...]).astype(BF16)

    return pl.pallas_call(
        body, name=name, grid=(s // tm,),
        out_shape=(jax.ShapeDtypeStruct((s, d), F32), jax.ShapeDtypeStruct((s, d), BF16)),
        in_specs=[_row_spec(tm, d), _vec_spec(d), _row_spec(tm, d), _vec_spec(d), _vec_spec(d), _vec_spec(d)],
        out_specs=(_row_spec(tm, d), _row_spec(tm, d)), compiler_params=_params(1),
    )(x, gate, p, g, scale, shift)


def _final_loss(x1, f, final_g, target, gate2):
    s, d = x1.shape
    tm = _tile(s, 256, 16)

    def body(x_ref, f_ref, g_ref, t_ref, gate_ref, dx_ref, df_ref, st_ref):
        @pl.when(pl.program_id(0) == 0)
        def _():
            st_ref[...] = jnp.zeros_like(st_ref)

        xv, gf = x_ref[...] + gate_ref[...] * f_ref[...], g_ref[...]
        rstd = lax.rsqrt(jnp.mean(xv * xv, axis=-1, keepdims=True) + EPS)
        xhat = xv * rstd
        err = xhat * gf - t_ref[...]
        dy = err * (1.0 / d)
        gdy = dy * gf
        dx = rstd * (gdy - xhat * jnp.mean(gdy * xhat, axis=-1, keepdims=True))
        dx_ref[...] = dx
        df_ref[...] = (gate_ref[...] * dx).astype(BF16)
        st_ref[0:1, :] += jnp.sum(dy * xhat, axis=0, keepdims=True)
        st_ref[1:2, :] += jnp.sum(err * err, axis=0, keepdims=True)

    return pl.pallas_call(
        body, name="final_loss", grid=(s // tm,),
        out_shape=(jax.ShapeDtypeStruct((s, d), F32), jax.ShapeDtypeStruct((s, d), BF16),
                   jax.ShapeDtypeStruct((8, d), F32)),
        in_specs=[_row_spec(tm, d), _row_spec(tm, d), _vec_spec(d), _row_spec(tm, d), _vec_spec(d)],
        out_specs=(_row_spec(tm, d), _row_spec(tm, d), pl.BlockSpec((8, d), lambda i: (0, 0))),
        compiler_params=_params(1),
    )(x1, f, final_g, target, gate2)


def _prenorm_bwd(xin, dh, dres, pf, g, scale, gate_next, name, after):
    s, d = xin.shape
    tm = _tile(s, 256, 16)

    def body(x_ref, dh_ref, dr_ref, pf_ref, g_ref, sc_ref, gn_ref, after_ref, dx_ref, dn_ref, st_ref):
        @pl.when(pl.program_id(0) == 0)
        def _():
            st_ref[...] = jnp.zeros_like(st_ref)

        xv, dhv, drv, gv = x_ref[...], dh_ref[...], dr_ref[...], g_ref[...]
        one_sc = 1.0 + sc_ref[...]
        rstd = lax.rsqrt(jnp.mean(xv * xv, axis=-1, keepdims=True) + EPS)
        xhat = xv * rstd
        dxhat = dhv * (gv * one_sc)
        dx = drv + rstd * (dxhat - xhat * jnp.mean(dxhat * xhat, axis=-1, keepdims=True))
        dx_ref[...] = dx
        dn_ref[...] = (gn_ref[...] * dx).astype(BF16)
        dhx = dhv * xhat
        st_ref[0:1, :] += jnp.sum(dhv, axis=0, keepdims=True)
        st_ref[1:2, :] += jnp.sum(dhx, axis=0, keepdims=True) * gv
        st_ref[2:3, :] += jnp.sum(dhx, axis=0, keepdims=True) * one_sc
        st_ref[3:4, :] += jnp.sum(drv * pf_ref[...], axis=0, keepdims=True)

    return pl.pallas_call(
        body, name=name, grid=(s // tm,),
        out_shape=(jax.ShapeDtypeStruct((s, d), F32), jax.ShapeDtypeStruct((s, d), BF16),
                   jax.ShapeDtypeStruct((8, d), F32)),
        in_specs=[_row_spec(tm, d)] * 4 + [_vec_spec(d)] * 3 + [_ANY],
        out_specs=(_row_spec(tm, d), _row_spec(tm, d), pl.BlockSpec((8, d), lambda i: (0, 0))),
        compiler_params=_params(1),
    )(xin, dh, dres, pf, g, scale, gate_next, after)


def _ada_fwd(c_all, w_loc):
    nb, d = c_all.shape
    n = w_loc.shape[1]
    tn = _tile(n, 512, 128) if n % 128 == 0 else n

    def body(c_ref, w_ref, o_ref):
        cv = c_ref[...]
        o_ref[...] = jnp.dot(cv * jax.nn.sigmoid(cv), w_ref[...], preferred_element_type=F32,
                             precision=lax.Precision.HIGHEST)

    return pl.pallas_call(
        body, name="ada_fwd", grid=(n // tn,), out_shape=jax.ShapeDtypeStruct((nb, n), F32),
        in_specs=[pl.BlockSpec((nb, d), lambda j: (0, 0)), pl.BlockSpec((d, tn), lambda j: (0, j))],
        out_specs=pl.BlockSpec((nb, tn), lambda j: (0, j)), compiler_params=_params(1),
    )(c_all, w_loc)


def _ada_bwd(c_all, dmod_cols):
    nb, d = c_all.shape
    n = dmod_cols.shape[1]
    tn = _tile(n, 512, 128) if n % 128 == 0 else n

    def body(c_ref, dm_ref, o_ref):
        cv = c_ref[...]
        o_ref[...] = lax.dot_general(cv * jax.nn.sigmoid(cv), dm_ref[...], TN, preferred_element_type=F32,
                                     precision=lax.Precision.HIGHEST)

    return pl.pallas_call(
        body, name="ada_bwd", grid=(n // tn,), out_shape=jax.ShapeDtypeStruct((d, n), F32),
        in_specs=[pl.BlockSpec((nb, d), lambda j: (0, 0)), pl.BlockSpec((nb, tn), lambda j: (0, j))],
        out_specs=pl.BlockSpec((d, tn), lambda j: (0, j)), compiler_params=_params(1),
    )(c_all, dmod_cols)


_INV_SQRT2 = 0.7071067811865476
_INV_SQRT2PI = 0.3989422804014327


def _gelu(x):
    return 0.5 * x * (1.0 + lax.erf(x * _INV_SQRT2))


def _gelu_grad(x):
    return 0.5 * (1.0 + lax.erf(x * _INV_SQRT2)) + x * jnp.exp(-0.5 * x * x) * _INV_SQRT2PI


def _gm_group_fwd(up, vp, gv, wt, bcol):
    u = _gelu(up)
    va = _gelu(vp)
    xc = va - jnp.mean(va, axis=-1, keepdims=True)
    rstd_v = lax.rsqrt(jnp.mean(xc * xc, axis=-1, keepdims=True) + EPS)
    yv = xc * rstd_v
    vn = (yv * gv).astype(BF16)
    mixed = _dot(wt, vn, NN) + bcol
    return u, rstd_v, yv, vn, mixed, u * mixed


def _tril_bf16(w):
    row = lax.broadcasted_iota(jnp.int32, w.shape, 0)
    col = lax.broadcasted_iota(jnp.int32, w.shape, 1)
    return jnp.where(col <= row, w, 0.0).astype(BF16)


def _gmlp_fwd(proj, v_norm_g, w_spatial, b_cols, gamma, gm, after):
    s = proj.shape[0]
    ng = gm // GRP

    def body(p_ref, gv_ref, w_ref, b_ref, gam_ref, after_ref, o_ref):
        for g in range(ng):
            lo = g * GRP
            wt = _tril_bf16(w_ref[g])
            *_, o = _gm_group_fwd(p_ref[:, lo:lo + GRP], p_ref[:, gm + lo:gm + lo + GRP], gv_ref[:, lo:lo + GRP],
                                  wt, b_ref[:, g:g + 1])
            rstd_o = lax.rsqrt(jnp.mean(o * o, axis=-1, keepdims=True) + EPS)
            o_ref[:, lo:lo + GRP] = (o * rstd_o * gam_ref[:, lo:lo + GRP]).astype(BF16)

    return pl.pallas_call(
        body, name="gmlp_fwd", grid=(s // GRP,), out_shape=jax.ShapeDtypeStruct((s, gm), BF16),
        in_specs=[pl.BlockSpec((GRP, 2 * gm), lambda n: (n, 0)), _vec_spec(gm),
                  pl.BlockSpec((ng, GRP, GRP), lambda n: (0, 0, 0)), pl.BlockSpec((GRP, GRP), lambda n: (0, 0)),
                  _vec_spec(gm), _ANY],
        out_specs=pl.BlockSpec((GRP, gm), lambda n: (n, 0)), compiler_params=_params(1),
    )(proj, v_norm_g, w_spatial, b_cols, gamma, after)


def _gmlp_bwd(proj, don, v_norm_g, w_spatial, b_cols, gamma, gm, after):
    s = proj.shape[0]
    ng = gm // GRP

    def body(p_ref, don_ref, gv_ref, w_ref, b_ref, gam_ref, after_ref, dp_ref, dw_ref, db_ref, dgv_ref, dgam_ref):
        @pl.when(pl.program_id(0) == 0)
        def _():
            dw_ref[...] = jnp.zeros_like(dw_ref)
            db_ref[...] = jnp.zeros_like(db_ref)
            dgv_ref[...] = jnp.zeros_like(dgv_ref)
            dgam_ref[...] = jnp.zeros_like(dgam_ref)

        lane = lax.broadcasted_iota(jnp.int32, (GRP, GRP), 1)
        row = lax.broadcasted_iota(jnp.int32, (GRP, GRP), 0)
        for g in range(ng):
            lo = g * GRP
            up, vp = p_ref[:, lo:lo + GRP], p_ref[:, gm + lo:gm + lo + GRP]
            gv, gam = gv_ref[:, lo:lo + GRP], gam_ref[:, lo:lo + GRP]
            wt = _tril_bf16(w_ref[g])
            u, rstd_v, yv, vn, mixed, o = _gm_group_fwd(up, vp, gv, wt, b_ref[:, g:g + 1])
            rstd_o = lax.rsqrt(jnp.mean(o * o, axis=-1, keepdims=True) + EPS)
            ohat = o * rstd_o
            dn = don_ref[:, lo:lo + GRP]
            dgam_ref[:, lo:lo + GRP] += jnp.sum(dn * ohat, axis=0, keepdims=True)
            dohat = dn * gam
            do = rstd_o * (dohat - ohat * jnp.mean(dohat * ohat, axis=-1, keepdims=True))
            du = do * mixed
            dmixed = do * u
            dmb = dmixed.astype(BF16)
            db_ref[...] += jnp.where(lane == g, jnp.sum(dmixed, axis=-1, keepdims=True), 0.0)
            dw_ref[g] += jnp.where(lane <= row, _dot(dmb, vn, NT), 0.0)
            dvn = _dot(wt, dmb, TN)
            dgv_ref[:, lo:lo + GRP] += jnp.sum(dvn * yv, axis=0, keepdims=True)
            dyv = dvn * gv
            dva = rstd_v * (dyv - jnp.mean(dyv, axis=-1, keepdims=True)
                            - yv * jnp.mean(dyv * yv, axis=-1, keepdims=True))
            dp_ref[:, lo:lo + GRP] = (du * _gelu_grad(up)).astype(BF16)
            dp_ref[:, gm + lo:gm + lo + GRP] = (dva * _gelu_grad(vp)).astype(BF16)

    const2 = lambda n: (0, 0)
    return pl.pallas_call(
        body, name="gmlp_bwd", grid=(s // GRP,),
        out_shape=(jax.ShapeDtypeStruct((s, 2 * gm), BF16), jax.ShapeDtypeStruct((ng, GRP, GRP), F32),
                   jax.ShapeDtypeStruct((GRP, GRP), F32), jax.ShapeDtypeStruct((1, gm), F32),
                   jax.ShapeDtypeStruct((1, gm), F32)),
        in_specs=[pl.BlockSpec((GRP, 2 * gm), lambda n: (n, 0)), pl.BlockSpec((GRP, gm), lambda n: (n, 0)),
                  _vec_spec(gm), pl.BlockSpec((ng, GRP, GRP), lambda n: (0, 0, 0)),
                  pl.BlockSpec((GRP, GRP), const2), _vec_spec(gm), _ANY],
        out_specs=(pl.BlockSpec((GRP, 2 * gm), lambda n: (n, 0)), pl.BlockSpec((ng, GRP, GRP), lambda n: (0, 0, 0)),
                   pl.BlockSpec((GRP, GRP), const2), _vec_spec(gm), _vec_spec(gm)),
        compiler_params=_params(1),
    )(proj, don, v_norm_g, w_spatial, b_cols, gamma, after)


BLK = 256


def _log_sigmoid(z):
    return jnp.minimum(z, 0.0) - jnp.log(1.0 + jnp.exp(-jnp.abs(z)))


def _split_dot(x, tri, passes):
    n = x.shape[0]
    parts, rest = [], x
    for _ in range(passes):
        hi = rest.astype(BF16)
        parts.append(hi)
        rest = rest - hi.astype(F32)
    res = _dot(jnp.concatenate(parts, axis=0), tri, NN)
    out = res[0:n]
    for k in range(1, passes):
        out = out + res[k * n:(k + 1) * n]
    return out


HEADS_PER_STEP = 4


def _sb_fwd(proj, gamma, gm, sb, after):
    s = proj.shape[0]
    hp = min(HEADS_PER_STEP, sb // GRP)
    w = hp * GRP
    nhp, nq = sb // w, s // BLK
    qc, kc, vc, gc = 2 * gm // w, (2 * gm + sb) // w, (2 * gm + 2 * sb) // w, gm // w
    scale = GRP ** -0.5

    def body(q_ref, k_ref, v_ref, gam_ref, after_ref, on_ref, o_ref, a_ref, s_ref, kb, vb):
        i = pl.program_id(1)

        @pl.when(i == 0)
        def _():
            kb[...] = k_ref[...].astype(BF16)
            vb[...] = v_ref[...].astype(BF16)

        qb = q_ref[...].astype(BF16)
        row = lax.broadcasted_iota(jnp.int32, (BLK, BLK), 0)
        col = lax.broadcasted_iota(jnp.int32, (BLK, BLK), 1)
        later = (row > col).astype(BF16)

        def block(j, carry, diag):
            off = pl.multiple_of(j * BLK, BLK)
            kj, vj = kb[pl.ds(off, BLK), :], vb[pl.ds(off, BLK), :]
            out = []
            for h in range(hp):
                tail, acc = carry[h]
                sl = slice(h * GRP, (h + 1) * GRP)
                z = _dot(qb[:, sl], kj[:, sl], NT) * scale
                lb = _log_sigmoid(z)
                l1m = lb - z
                sig = jnp.exp(lb)
                if diag:
                    l1m = jnp.where(col < row, l1m, 0.0)
                    sig = jnp.where(col < row, sig, 0.0)
                after_s = _split_dot(l1m, later, 2)
                a = jnp.exp(lb + after_s + tail)
                if diag:
                    a = jnp.where(col < row, a, 0.0)
                ab = a.astype(BF16)
                a_ref[h, j] = ab
                s_ref[h, j] = sig.astype(BF16)
                out.append((tail + after_s[:, 0:1] + l1m[:, 0:1], acc + _dot(ab, vj[:, sl], NN)))
            return tuple(out)

        init = tuple((jnp.zeros((BLK, 1), F32), jnp.zeros((BLK, GRP), F32)) for _ in range(hp))
        carry = block(i, init, True)
        carry = lax.fori_loop(0, i, lambda jj, c: block(i - 1 - jj, c, False), carry)
        for h in range(hp):
            _, acc = carry[h]
            sl = slice(h * GRP, (h + 1) * GRP)
            rstd = lax.rsqrt(jnp.mean(acc * acc, axis=-1, keepdims=True) + EPS)
            on_ref[:, sl] = (acc * rstd * gam_ref[:, sl]).astype(BF16)
            o_ref[:, sl] = acc

    saved = jax.ShapeDtypeStruct((sb // GRP, nq, nq, BLK, BLK), BF16)
    saved_spec = pl.BlockSpec((hp, None, nq, BLK, BLK), lambda h, i: (h, i, 0, 0, 0))
    return pl.pallas_call(
        body, name="sb_fwd", grid=(nhp, nq),
        out_shape=(jax.ShapeDtypeStruct((s, sb), BF16), jax.ShapeDtypeStruct((s, sb), F32), saved, saved),
        in_specs=[pl.BlockSpec((BLK, w), lambda h, i: (i, qc + h)), pl.BlockSpec((s, w), lambda h, i: (0, kc + h)),
                  pl.BlockSpec((s, w), lambda h, i: (0, vc + h)), pl.BlockSpec((1, w), lambda h, i: (0, gc + h)), _ANY],
        out_specs=(pl.BlockSpec((BLK, w), lambda h, i: (i, h)), pl.BlockSpec((BLK, w), lambda h, i: (i, h)),
                   saved_spec, saved_spec),
        scratch_shapes=[pltpu.VMEM((s, w), BF16), pltpu.VMEM((s, w), BF16)],
        compiler_params=_params(2),
    )(proj, proj, proj, gamma, after)


def _sb_bwd(proj, o_raw, a_sv, s_sv, don, gamma, gm, sb, after):
    s = proj.shape[0]
    hp = min(HEADS_PER_STEP // 2, sb // GRP)
    w = hp * GRP
    nhp, nq = sb // w, s // BLK
    qc, kc, vc, gc = 2 * gm // w, (2 * gm + sb) // w, (2 * gm + 2 * sb) // w, gm // w
    scale = GRP ** -0.5

    def body(q_ref, k_ref, v_ref, o_ref, a_ref, s_ref, don_ref, gam_ref, after_ref, dq_ref, dk_ref, dv_ref, dgam_ref,
             kb, vb, dkt_acc, dvt_acc):
        i = pl.program_id(1)

        @pl.when(i == 0)
        def _():
            kb[...] = k_ref[...].astype(BF16)
            vb[...] = v_ref[...].astype(BF16)
            dkt_acc[...] = jnp.zeros_like(dkt_acc)
            dvt_acc[...] = jnp.zeros_like(dvt_acc)
            dgam_ref[...] = jnp.zeros_like(dgam_ref)

        dobs = []
        for h in range(hp):
            sl = slice(h * GRP, (h + 1) * GRP)
            o, dn = o_ref[:, sl], don_ref[:, sl]
            rstd = lax.rsqrt(jnp.mean(o * o, axis=-1, keepdims=True) + EPS)
            ohat = o * rstd
            dgam_ref[:, sl] += jnp.sum(dn * ohat, axis=0, keepdims=True)
            dohat = dn * gam_ref[:, sl]
            dobs.append((rstd * (dohat - ohat * jnp.mean(dohat * ohat, axis=-1, keepdims=True))).astype(BF16))

        qb = q_ref[...].astype(BF16)
        qts = [qb[:, h * GRP:(h + 1) * GRP].T for h in range(hp)]
        dots = [dob.T for dob in dobs]
        row = lax.broadcasted_iota(jnp.int32, (BLK, BLK), 0)
        col = lax.broadcasted_iota(jnp.int32, (BLK, BLK), 1)
        before = (row < col).astype(BF16)

        def block(j, carry):
            off = pl.multiple_of(j * BLK, BLK)
            kj, vj = kb[pl.ds(off, BLK), :], vb[pl.ds(off, BLK), :]
            out = []
            for h in range(hp):
                e_pre, dq = carry[h]
                sl = slice(h * GRP, (h + 1) * GRP)
                qh, kh, dob = qb[:, sl], kj[:, sl], dobs[h]
                ab = a_ref[h, j]
                sig = s_ref[h, j].astype(F32)
                de = ab.astype(F32) * _dot(dob, vj[:, sl], NT)
                dvt_acc[sl, pl.ds(off, BLK)] += _dot(dots[h], ab, NN)
                before_s = _dot(de.astype(BF16), before, NN)
                dzb = ((de * (1.0 - sig) - (e_pre + before_s) * sig) * scale).astype(BF16)
                dkt_acc[sl, pl.ds(off, BLK)] += _dot(qts[h], dzb, NN)
                out.append((e_pre + before_s[:, BLK - 1:BLK] + de[:, BLK - 1:BLK], dq + _dot(dzb, kh, NN)))
            return tuple(out)

        carry = tuple((jnp.zeros((BLK, 1), F32), jnp.zeros((BLK, GRP), F32)) for _ in range(hp))
        carry = lax.fori_loop(0, i + 1, block, carry)
        for h in range(hp):
            dq_ref[:, h * GRP:(h + 1) * GRP] = carry[h][1].astype(BF16)

        @pl.when(i == nq - 1)
        def _():
            dk_ref[...] = dkt_acc[...].T.astype(BF16)
            dv_ref[...] = dvt_acc[...].T.astype(BF16)

    blk_q = lambda h, i: (i, h)
    whole = lambda h, i: (0, h)
    saved_spec = pl.BlockSpec((hp, None, nq, BLK, BLK), lambda h, i: (h, i, 0, 0, 0))
    return pl.pallas_call(
        body, name="sb_bwd", grid=(nhp, nq),
        out_shape=(jax.ShapeDtypeStruct((s, sb), BF16),) * 3 + (jax.ShapeDtypeStruct((1, sb), F32),),
        in_specs=[pl.BlockSpec((BLK, w), lambda h, i: (i, qc + h)), pl.BlockSpec((s, w), lambda h, i: (0, kc + h)),
                  pl.BlockSpec((s, w), lambda h, i: (0, vc + h)), pl.BlockSpec((BLK, w), blk_q), saved_spec, saved_spec,
                  pl.BlockSpec((BLK, w), lambda h, i: (i, gc + h)),
                  pl.BlockSpec((1, w), lambda h, i: (0, gc + h)), _ANY],
        out_specs=(pl.BlockSpec((BLK, w), blk_q), pl.BlockSpec((s, w), whole), pl.BlockSpec((s, w), whole),
                   pl.BlockSpec((1, w), whole)),
        scratch_shapes=[pltpu.VMEM((s, w), BF16), pltpu.VMEM((s, w), BF16),
                        pltpu.VMEM((w, s), F32), pltpu.VMEM((w, s), F32)],
        compiler_params=_params(2),
    )(proj, proj, proj, o_raw, a_sv, s_sv, don, gamma, after)


def _sb_fwd_recompute(proj, gamma, gm, sb, after):
    s = proj.shape[0]
    hp = min(HEADS_PER_STEP, sb // GRP)
    w = hp * GRP
    nhp, nq = sb // w, s // BLK
    qc, kc, vc, gc = 2 * gm // w, (2 * gm + sb) // w, (2 * gm + 2 * sb) // w, gm // w
    scale = GRP ** -0.5

    def body(q_ref, k_ref, v_ref, gam_ref, after_ref, on_ref, o_ref, mt_ref, kb, vb):
        i = pl.program_id(1)

        @pl.when(i == 0)
        def _():
            kb[...] = k_ref[...].astype(BF16)
            vb[...] = v_ref[...].astype(BF16)

        qb = q_ref[...].astype(BF16)
        row = lax.broadcasted_iota(jnp.int32, (BLK, BLK), 0)
        col = lax.broadcasted_iota(jnp.int32, (BLK, BLK), 1)
        later = (row > col).astype(BF16)

        def block(j, carry, diag):
            off = pl.multiple_of(j * BLK, BLK)
            kj, vj = kb[pl.ds(off, BLK), :], vb[pl.ds(off, BLK), :]
            out = []
            for h in range(hp):
                tail, acc = carry[h]
                sl = slice(h * GRP, (h + 1) * GRP)
                z = _dot(qb[:, sl], kj[:, sl], NT) * scale
                lb = _log_sigmoid(z)
                l1m = lb - z
                if diag:
                    l1m = jnp.where(col < row, l1m, 0.0)
                after_s = _split_dot(l1m, later, 2)
                a = jnp.exp(lb + after_s + tail)
                if diag:
                    a = jnp.where(col < row, a, 0.0)
                out.append((tail + after_s[:, 0:1] + l1m[:, 0:1], acc + _dot(a.astype(BF16), vj[:, sl], NN)))
            return tuple(out)

        init = tuple((jnp.zeros((BLK, 1), F32), jnp.zeros((BLK, GRP), F32)) for _ in range(hp))
        carry = block(i, init, True)
        carry = lax.fori_loop(0, i, lambda jj, c: block(i - 1 - jj, c, False), carry)
        for h in range(hp):
            tail, acc = carry[h]
            sl = slice(h * GRP, (h + 1) * GRP)
            rstd = lax.rsqrt(jnp.mean(acc * acc, axis=-1, keepdims=True) + EPS)
            on_ref[:, sl] = (acc * rstd * gam_ref[:, sl]).astype(BF16)
            o_ref[:, sl] = acc
            mt_ref[h] = tail

    return pl.pallas_call(
        body, name="sb_fwd", grid=(nhp, nq),
        out_shape=(jax.ShapeDtypeStruct((s, sb), BF16), jax.ShapeDtypeStruct((s, sb), F32),
                   jax.ShapeDtypeStruct((sb // GRP, s, 1), F32)),
        in_specs=[pl.BlockSpec((BLK, w), lambda h, i: (i, qc + h)), pl.BlockSpec((s, w), lambda h, i: (0, kc + h)),
                  pl.BlockSpec((s, w), lambda h, i: (0, vc + h)), pl.BlockSpec((1, w), lambda h, i: (0, gc + h)), _ANY],
        out_specs=(pl.BlockSpec((BLK, w), lambda h, i: (i, h)), pl.BlockSpec((BLK, w), lambda h, i: (i, h)),
                   pl.BlockSpec((hp, BLK, 1), lambda h, i: (h, i, 0))),
        scratch_shapes=[pltpu.VMEM((s, w), BF16), pltpu.VMEM((s, w), BF16)],
        compiler_params=_params(2),
    )(proj, proj, proj, gamma, after)


def _sb_bwd_recompute(proj, o_raw, mtot, don, gamma, gm, sb, after):
    s = proj.shape[0]
    hp = min(HEADS_PER_STEP, sb // GRP)
    w = hp * GRP
    nhp, nq = sb // w, s // BLK
    qc, kc, vc, gc = 2 * gm // w, (2 * gm + sb) // w, (2 * gm + 2 * sb) // w, gm // w
    scale = GRP ** -0.5

    def body(q_ref, k_ref, v_ref, o_ref, mt_ref, don_ref, gam_ref, after_ref, dq_ref, dk_ref, dv_ref, dgam_ref,
             kb, vb, dk_acc, dv_acc):
        i = pl.program_id(1)

        @pl.when(i == 0)
        def _():
            kb[...] = k_ref[...].astype(BF16)
            vb[...] = v_ref[...].astype(BF16)
            dk_acc[...] = jnp.zeros_like(dk_acc)
            dv_acc[...] = jnp.zeros_like(dv_acc)
            dgam_ref[...] = jnp.zeros_like(dgam_ref)

        dobs = []
        for h in range(hp):
            sl = slice(h * GRP, (h + 1) * GRP)
            o, dn = o_ref[:, sl], don_ref[:, sl]
            rstd = lax.rsqrt(jnp.mean(o * o, axis=-1, keepdims=True) + EPS)
            ohat = o * rstd
            dgam_ref[:, sl] += jnp.sum(dn * ohat, axis=0, keepdims=True)
            dohat = dn * gam_ref[:, sl]
            dobs.append((rstd * (dohat - ohat * jnp.mean(dohat * ohat, axis=-1, keepdims=True))).astype(BF16))

        qb = q_ref[...].astype(BF16)
        row = lax.broadcasted_iota(jnp.int32, (BLK, BLK), 0)
        col = lax.broadcasted_iota(jnp.int32, (BLK, BLK), 1)
        upto = (row <= col).astype(BF16)
        before = (row < col).astype(BF16)

        def block(j, carry, diag):
            off = pl.multiple_of(j * BLK, BLK)
            kj, vj = kb[pl.ds(off, BLK), :], vb[pl.ds(off, BLK), :]
            out = []
            for h in range(hp):
                m_pre, e_pre, dq = carry[h]
                sl = slice(h * GRP, (h + 1) * GRP)
                qh, kh, dob = qb[:, sl], kj[:, sl], dobs[h]
                z = _dot(qh, kh, NT) * scale
                lb = _log_sigmoid(z)
                l1m = lb - z
                if diag:
                    l1m = jnp.where(col < row, l1m, 0.0)
                upto_s = _split_dot(l1m, upto, 2)
                a = jnp.exp(lb + (mt_ref[h] - m_pre) - upto_s)
                if diag:
                    a = jnp.where(col < row, a, 0.0)
                de = a * _dot(dob, vj[:, sl], NT)
                dv_acc[pl.ds(off, BLK), sl] += _dot(a.astype(BF16), dob, TN)
                before_s = _split_dot(de, before, 1)
                dl1m = e_pre + before_s
                if diag:
                    dl1m = jnp.where(col < row, dl1m, 0.0)
                sig = jnp.exp(lb)
                dzb = ((de * (1.0 - sig) - dl1m * sig) * scale).astype(BF16)
                dk_acc[pl.ds(off, BLK), sl] += _dot(dzb, qh, TN)
                out.append((m_pre + upto_s[:, BLK - 1:BLK], e_pre + before_s[:, BLK - 1:BLK] + de[:, BLK - 1:BLK],
                            dq + _dot(dzb, kh, NN)))
            return tuple(out)

        zero = jnp.zeros((BLK, 1), F32)
        carry = tuple((zero, zero, jnp.zeros((BLK, GRP), F32)) for _ in range(hp))
        carry = lax.fori_loop(0, i, lambda j, c: block(j, c, False), carry)
        carry = block(i, carry, True)
        for h in range(hp):
            dq_ref[:, h * GRP:(h + 1) * GRP] = carry[h][2].astype(BF16)

        @pl.when(i == nq - 1)
        def _():
            dk_ref[...] = dk_acc[...].astype(BF16)
            dv_ref[...] = dv_acc[...].astype(BF16)

    blk_q = lambda h, i: (i, h)
    whole = lambda h, i: (0, h)
    return pl.pallas_call(
        body, name="sb_bwd", grid=(nhp, nq),
        out_shape=(jax.ShapeDtypeStruct((s, sb), BF16),) * 3 + (jax.ShapeDtypeStruct((1, sb), F32),),
        in_specs=[pl.BlockSpec((BLK, w), lambda h, i: (i, qc + h)), pl.BlockSpec((s, w), lambda h, i: (0, kc + h)),
                  pl.BlockSpec((s, w), lambda h, i: (0, vc + h)), pl.BlockSpec((BLK, w), blk_q),
                  pl.BlockSpec((hp, BLK, 1), lambda h, i: (h, i, 0)),
                  pl.BlockSpec((BLK, w), lambda h, i: (i, gc + h)),
                  pl.BlockSpec((1, w), lambda h, i: (0, gc + h)), _ANY],
        out_specs=(pl.BlockSpec((BLK, w), blk_q), pl.BlockSpec((s, w), whole), pl.BlockSpec((s, w), whole),
                   pl.BlockSpec((1, w), whole)),
        scratch_shapes=[pltpu.VMEM((s, w), BF16), pltpu.VMEM((s, w), BF16),
                        pltpu.VMEM((s, w), F32), pltpu.VMEM((s, w), F32)],
        compiler_params=_params(2),
    )(proj, proj, proj, o_raw, mtot, don, gamma, after)


def _sb_fwd_one_head(proj, gamma, gm, sb):
    s = proj.shape[0]
    nh, nq = sb // GRP, s // BLK
    qc, kc, vc = 2 * gm // GRP, (2 * gm + sb) // GRP, (2 * gm + 2 * sb) // GRP
    scale = GRP ** -0.5

    def body(q_ref, k_ref, v_ref, gam_ref, on_ref, o_ref, mt_ref, kb, vb):
        i = pl.program_id(1)

        @pl.when(i == 0)
        def _():
            kb[...] = k_ref[...].astype(BF16)
            vb[...] = v_ref[...].astype(BF16)

        qb = q_ref[...].astype(BF16)
        row = lax.broadcasted_iota(jnp.int32, (BLK, BLK), 0)
        col = lax.broadcasted_iota(jnp.int32, (BLK, BLK), 1)
        later = (row > col).astype(BF16)

        def step(jj, carry):
            tail, acc = carry
            j = i - jj
            off = pl.multiple_of(j * BLK, BLK)
            z = _dot(qb, kb[pl.ds(off, BLK), :], NT) * scale
            lb = _log_sigmoid(z)
            mask = col + j * BLK < row + i * BLK
            l1m = jnp.where(mask, lb - z, 0.0)
            a = jnp.where(mask, jnp.exp(lb + _split_dot(l1m, later, 3) + tail), 0.0)
            acc = acc + _dot(a.astype(BF16), vb[pl.ds(off, BLK), :], NN)
            return tail + jnp.sum(l1m, axis=-1, keepdims=True), acc

        tail, acc = lax.fori_loop(0, i + 1, step, (jnp.zeros((BLK, 1), F32), jnp.zeros((BLK, GRP), F32)))
        rstd = lax.rsqrt(jnp.mean(acc * acc, axis=-1, keepdims=True) + EPS)
        on_ref[...] = (acc * rstd * gam_ref[...]).astype(BF16)
        o_ref[...] = acc
        mt_ref[...] = tail

    return pl.pallas_call(
        body, name="sb_fwd", grid=(nh, nq),
        out_shape=(jax.ShapeDtypeStruct((s, sb), BF16), jax.ShapeDtypeStruct((s, sb), F32),
                   jax.ShapeDtypeStruct((nh, s, 1), F32)),
        in_specs=[pl.BlockSpec((BLK, GRP), lambda h, i: (i, qc + h)), pl.BlockSpec((s, GRP), lambda h, i: (0, kc + h)),
                  pl.BlockSpec((s, GRP), lambda h, i: (0, vc + h)),
                  pl.BlockSpec((1, GRP), lambda h, i: (0, gm // GRP + h))],
        out_specs=(pl.BlockSpec((BLK, GRP), lambda h, i: (i, h)), pl.BlockSpec((BLK, GRP), lambda h, i: (i, h)),
                   pl.BlockSpec((None, BLK, 1), lambda h, i: (h, i, 0))),
        scratch_shapes=[pltpu.VMEM((s, GRP), BF16), pltpu.VMEM((s, GRP), BF16)],
        compiler_params=_params(2),
    )(proj, proj, proj, gamma)


def _sb_bwd_one_head(proj, o_raw, mtot, don, gamma, gm, sb):
    s = proj.shape[0]
    nh, nq = sb // GRP, s // BLK
    qc, kc, vc = 2 * gm // GRP, (2 * gm + sb) // GRP, (2 * gm + 2 * sb) // GRP
    scale = GRP ** -0.5

    def body(q_ref, k_ref, v_ref, o_ref, mt_ref, don_ref, gam_ref, dq_ref, dk_ref, dv_ref, dgam_ref,
             kb, vb, dk_acc, dv_acc):
        i = pl.program_id(1)

        @pl.when(i == 0)
        def _():
            kb[...] = k_ref[...].astype(BF16)
            vb[...] = v_ref[...].astype(BF16)
            dk_acc[...] = jnp.zeros_like(dk_acc)
            dv_acc[...] = jnp.zeros_like(dv_acc)
            dgam_ref[...] = jnp.zeros_like(dgam_ref)

        o, dn, gam = o_ref[...], don_ref[...], gam_ref[...]
        rstd = lax.rsqrt(jnp.mean(o * o, axis=-1, keepdims=True) + EPS)
        ohat = o * rstd
        dgam_ref[...] += jnp.sum(dn * ohat, axis=0, keepdims=True)
        dohat = dn * gam
        dob = (rstd * (dohat - ohat * jnp.mean(dohat * ohat, axis=-1, keepdims=True))).astype(BF16)

        qb = q_ref[...].astype(BF16)
        mt = mt_ref[...]
        row = lax.broadcasted_iota(jnp.int32, (BLK, BLK), 0)
        col = lax.broadcasted_iota(jnp.int32, (BLK, BLK), 1)
        upto = (row <= col).astype(BF16)
        before = (row < col).astype(BF16)

        def step(j, carry):
            m_pre, e_pre, dq = carry
            off = pl.multiple_of(j * BLK, BLK)
            kj, vj = kb[pl.ds(off, BLK), :], vb[pl.ds(off, BLK), :]
            z = _dot(qb, kj, NT) * scale
            lb = _log_sigmoid(z)
            mask = col + j * BLK < row + i * BLK
            l1m = jnp.where(mask, lb - z, 0.0)
            tail = mt - m_pre - _split_dot(l1m, upto, 3)
            a = jnp.where(mask, jnp.exp(lb + tail), 0.0)
            de = a * _dot(dob, vj, NT)
            dv_acc[pl.ds(off, BLK), :] += _dot(a.astype(BF16), dob, TN)
            dl1m = e_pre + _split_dot(de, before, 2)
            sig = jnp.exp(lb)
            dz = (de * (1.0 - sig) - jnp.where(mask, dl1m * sig, 0.0)) * scale
            dzb = dz.astype(BF16)
            dk_acc[pl.ds(off, BLK), :] += _dot(dzb, qb, TN)
            return (m_pre + jnp.sum(l1m, axis=-1, keepdims=True), e_pre + jnp.sum(de, axis=-1, keepdims=True),
                    dq + _dot(dzb, kj, NN))

        zero = jnp.zeros((BLK, 1), F32)
        _, _, dq = lax.fori_loop(0, i + 1, step, (zero, zero, jnp.zeros((BLK, GRP), F32)))
        dq_ref[...] = dq.astype(BF16)

        @pl.when(i == nq - 1)
        def _():
            dk_ref[...] = dk_acc[...].astype(BF16)
            dv_ref[...] = dv_acc[...].astype(BF16)

    blk_q = lambda h, i: (i, h)
    whole = lambda h, i: (0, h)
    return pl.pallas_call(
        body, name="sb_bwd", grid=(nh, nq),
        out_shape=(jax.ShapeDtypeStruct((s, sb), BF16),) * 3 + (jax.ShapeDtypeStruct((1, sb), F32),),
        in_specs=[pl.BlockSpec((BLK, GRP), lambda h, i: (i, qc + h)), pl.BlockSpec((s, GRP), lambda h, i: (0, kc + h)),
                  pl.BlockSpec((s, GRP), lambda h, i: (0, vc + h)), pl.BlockSpec((BLK, GRP), blk_q),
                  pl.BlockSpec((None, BLK, 1), lambda h, i: (h, i, 0)),
                  pl.BlockSpec((BLK, GRP), lambda h, i: (i, gm // GRP + h)),
                  pl.BlockSpec((1, GRP), lambda h, i: (0, gm // GRP + h))],
        out_specs=(pl.BlockSpec((BLK, GRP), blk_q), pl.BlockSpec((s, GRP), whole), pl.BlockSpec((s, GRP), whole),
                   pl.BlockSpec((1, GRP), whole)),
        scratch_shapes=[pltpu.VMEM((s, GRP), BF16), pltpu.VMEM((s, GRP), BF16),
                        pltpu.VMEM((s, GRP), F32), pltpu.VMEM((s, GRP), F32)],
        compiler_params=_params(2),
    )(proj, proj, proj, o_raw, mtot, don, gamma)


FFN_ROW_CHUNKS = 2


def _row_chunks(tm):
    n = FFN_ROW_CHUNKS if tm % (16 * FFN_ROW_CHUNKS) == 0 else 1
    return [slice(k * (tm // n), (k + 1) * (tm // n)) for k in range(n)]


def _ffn_up(h2, wg, wu):
    s, d = h2.shape
    nb, fb, _ = wg.shape
    tm = _tile(s, 1024, 16)

    def body(h_ref, wg_ref, wu_ref, g_ref, u_ref, a_ref):
        for rows_ in _row_chunks(tm):
            hv = h_ref[rows_, :]
            g = _dot(hv, wg_ref[...], NT)
            u = _dot(hv, wu_ref[...], NT)
            g_ref[rows_, :] = g.astype(BF16)
            u_ref[rows_, :] = u.astype(BF16)
            a_ref[rows_, :] = (g * jax.nn.sigmoid(g) * u).astype(BF16)

    rows = pl.BlockSpec((tm, d), lambda i, j: (i, 0))
    wblk = pl.BlockSpec((None, fb, d), lambda i, j: (j, 0, 0))
    hid = pl.BlockSpec((None, tm, fb), lambda i, j: (j, i, 0))
    hid_sds = jax.ShapeDtypeStruct((nb, s, fb), BF16)
    return pl.pallas_call(
        body, name="ffn_up", grid=(s // tm, nb), out_shape=(hid_sds,) * 3, in_specs=[rows, wblk, wblk],
        out_specs=(hid,) * 3, compiler_params=_params(2),
    )(h2, wg, wu)


def _ffn_fwd_fused(h2, wg, wu, wd):
    s, d = h2.shape
    nb, fb, _ = wd.shape
    tm = _tile(s, 512, 16)

    def body(h_ref, wg_ref, wu_ref, wd_ref, f_ref, g_ref, u_ref):
        j = pl.program_id(1)
        parts = []
        for rows_ in _row_chunks(tm):
            hv = h_ref[rows_, :]
            g = _dot(hv, wg_ref[...], NT)
            u = _dot(hv, wu_ref[...], NT)
            g_ref[rows_, :] = g.astype(BF16)
            u_ref[rows_, :] = u.astype(BF16)
            parts.append(_dot((g * jax.nn.sigmoid(g) * u).astype(BF16), wd_ref[...], NN))

        @pl.when(j == 0)
        def _():
            for rows_, part in zip(_row_chunks(tm), parts):
                f_ref[rows_, :] = part

        @pl.when(j > 0)
        def _():
            for rows_, part in zip(_row_chunks(tm), parts):
                f_ref[rows_, :] += part

    rows = pl.BlockSpec((tm, d), lambda i, j: (i, 0))
    wblk = pl.BlockSpec((None, fb, d), lambda i, j: (j, 0, 0))
    hid = pl.BlockSpec((None, tm, fb), lambda i, j: (j, i, 0))
    return pl.pallas_call(
        body, name="ffn_fwd", grid=(s // tm, nb),
        out_shape=(jax.ShapeDtypeStruct((s, d), F32), jax.ShapeDtypeStruct((nb, s, fb), BF16),
                   jax.ShapeDtypeStruct((nb, s, fb), BF16)),
        in_specs=[rows, wblk, wblk, wblk],
        out_specs=(rows, hid, hid), compiler_params=_params(2),
    )(h2, wg, wu, wd)


def _ffn_bwd_act(df, g_pre, u_pre, wd):
    s, d = df.shape
    nb, fb, _ = wd.shape
    tm = _tile(s, 1024, 16)

    def body(df_ref, g_ref, u_ref, wd_ref, dg_ref, du_ref):
        for rows_ in _row_chunks(tm):
            g, u = g_ref[rows_, :].astype(F32), u_ref[rows_, :].astype(F32)
            da = _dot(df_ref[rows_, :], wd_ref[...], NT)
            sg = jax.nn.sigmoid(g)
            dg_ref[rows_, :] = (da * u * (sg * (1.0 + g * (1.0 - sg)))).astype(BF16)
            du_ref[rows_, :] = (da * (g * sg)).astype(BF16)

    rows = pl.BlockSpec((tm, d), lambda i, j: (i, 0))
    wblk = pl.BlockSpec((None, fb, d), lambda i, j: (j, 0, 0))
    hid = pl.BlockSpec((None, tm, fb), lambda i, j: (j, i, 0))
    hid_sds = jax.ShapeDtypeStruct((nb, s, fb), BF16)
    return pl.pallas_call(
        body, name="ffn_bwd_act", grid=(s // tm, nb), out_shape=(hid_sds, hid_sds),
        in_specs=[rows, hid, hid, wblk], out_specs=(hid, hid), compiler_params=_params(2),
    )(df, g_pre, u_pre, wd)


def _ffn_bwd_in(dg, du, wg, wu):
    nb, s, fb = dg.shape
    d = wg.shape[2]
    tm = _tile(s, 1024, 16)

    def body(dg_ref, du_ref, wg_ref, wu_ref, dh_ref):
        j = pl.program_id(1)
        parts = [_dot(dg_ref[rows_, :], wg_ref[...], NN) + _dot(du_ref[rows_, :], wu_ref[...], NN)
                 for rows_ in _row_chunks(tm)]

        @pl.when(j == 0)
        def _():
            for rows_, part in zip(_row_chunks(tm), parts):
                dh_ref[rows_, :] = part

        @pl.when(j > 0)
        def _():
            for rows_, part in zip(_row_chunks(tm), parts):
                dh_ref[rows_, :] += part

    wblk = pl.BlockSpec((None, fb, d), lambda i, j: (j, 0, 0))
    hid = pl.BlockSpec((None, tm, fb), lambda i, j: (j, i, 0))
    return pl.pallas_call(
        body, name="ffn_bwd_in", grid=(s // tm, nb), out_shape=jax.ShapeDtypeStruct((s, d), F32),
        in_specs=[hid, hid, wblk, wblk], out_specs=pl.BlockSpec((tm, d), lambda i, j: (i, 0)),
        compiler_params=_params(2),
    )(dg, du, wg, wu)


def _ffn_bwd_fused(df, g_pre, u_pre, wg, wu, wd):
    s, d = df.shape
    nb, fb, _ = wd.shape
    tm = _tile(s, 512, 16)

    def body(df_ref, g_ref, u_ref, wg_ref, wu_ref, wd_ref, dh_ref, dg_ref, du_ref):
        j = pl.program_id(1)
        parts = []
        for rows_ in _row_chunks(tm):
            g, u = g_ref[rows_, :].astype(F32), u_ref[rows_, :].astype(F32)
            da = _dot(df_ref[rows_, :], wd_ref[...], NT)
            sg = jax.nn.sigmoid(g)
            silu = g * sg
            dg =(da * u * (sg * (1.0 + g * (1.0 - sg)))).astype(BF16)
            du = (da * silu).astype(BF16)
            dg_ref[rows_, :] = dg
            du_ref[rows_, :] = du
            parts.append(_dot(dg, wg_ref[...], NN) + _dot(du, wu_ref[...], NN))

        @pl.when(j == 0)
        def _():
            for rows_, part in zip(_row_chunks(tm), parts):
                dh_ref[rows_, :] = part

        @pl.when(j > 0)
        def _():
            for rows_, part in zip(_row_chunks(tm), parts):
                dh_ref[rows_, :] += part

    rows = pl.BlockSpec((tm, d), lambda i, j: (i, 0))
    wblk = pl.BlockSpec((None, fb, d), lambda i, j: (j, 0, 0))
    hid = pl.BlockSpec((None, tm, fb), lambda i, j: (j, i, 0))
    hid_sds = jax.ShapeDtypeStruct((nb, s, fb), BF16)
    return pl.pallas_call(
        body, name="ffn_bwd", grid=(s // tm, nb),
        out_shape=(jax.ShapeDtypeStruct((s, d), F32), hid_sds, hid_sds),
        in_specs=[rows, hid, hid, wblk, wblk, wblk],
        out_specs=(rows, hid, hid), compiler_params=_params(2),
    )(df, g_pre, u_pre, wg, wu, wd)


def _reduce_adamw(parts, w, m, v, name, after):
    npart, r, c = parts.shape
    tr = _tile(r, max(16, 524288 // c), 16)
    c1, c2 = 1.0 - ADAM_B1 ** ADAM_STEP, 1.0 - ADAM_B2 ** ADAM_STEP

    def body(p_ref, w_ref, m_ref, v_ref, after_ref, g_ref, d_ref, nm_ref, nv_ref):
        g = p_ref[0].astype(F32)
        for k in range(1, npart):
            g = g + p_ref[k].astype(F32)
        nm = ADAM_B1 * m_ref[...] + (1.0 - ADAM_B1) * g
        nv = ADAM_B2 * v_ref[...] + (1.0 - ADAM_B2) * (g * g)
        g_ref[...] = g
        nm_ref[...] = nm
        nv_ref[...] = nv
        d_ref[...] = -ADAM_LR * ((nm / c1) / (jnp.sqrt(nv / c2) + ADAM_EPS) + ADAM_WD * w_ref[...])

    blk = pl.BlockSpec((tr, c), lambda i: (i, 0))
    sds = jax.ShapeDtypeStruct((r, c), F32)
    return pl.pallas_call(
        body, name=name, grid=(r // tr,), out_shape=(sds,) * 4,
        in_specs=[pl.BlockSpec((npart, tr, c), lambda i: (0, i, 0)), blk, blk, blk, _ANY], out_specs=(blk,) * 4,
        compiler_params=_params(1),
    )(parts, w, m, v, after)


def _pack(vecs):
    rows = jnp.concatenate([a.reshape(-1, GRP) for a in vecs], axis=0)
    pad = -rows.shape[0] % 64
    return jnp.pad(rows, ((0, pad), (0, 0)))


def _unpack(rows, shapes):
    out, at = [], 0
    for shp in shapes:
        n = 1
        for k in shp:
            n *= k
        out.append(rows[at:at + n // GRP].reshape(shp))
        at += n // GRP
    return out


def kernel(x, c, w_ada, b_ada, norm1_g, w_in, v_norm_g, w_spatial, b_spatial, out_norm_g, w_out, norm2_g, w_gate, w_up, w_down, final_g, loss_target, m_w_ada, m_b_ada, m_norm1_g, m_w_in, m_v_norm_g, m_w_spatial, m_b_spatial, m_out_norm_g, m_w_out, m_norm2_g, m_w_gate, m_w_up, m_w_down, m_final_g, v_w_ada, v_b_ada, v_norm1_g, v_w_in, v_v_norm_g, v_w_spatial, v_b_spatial, v_out_norm_g, v_w_out, v_norm2_g, v_w_gate, v_w_up, v_w_down, v_final_g):
    s, d = x.shape[1], x.shape[2]
    gm = v_norm_g.shape[1]
    sb = d - gm
    n_in, ffb, ob = w_in.shape[2], w_gate.shape[2], w_out.shape[1]
    xs, tgt = x[0], loss_target[0]
    me = 4 * lax.axis_index("x") + 2 * lax.axis_index("y") + lax.axis_index("c")

    c_all = _exchange(c, False, "gather_c")[:, 0, :]
    mod_cols = _ada_fwd(c_all, w_ada[0])
    mod = _exchange(mod_cols[:, None, :], True, "scatter_mod").reshape(1, 6 * d) + b_ada
    shift1, scale1, gate1, shift2, scale2, gate2 = [mod[:, k * d:(k + 1) * d] for k in range(6)]

    tok, gathers = mod, {}
    tr_ = lambda a: jnp.swapaxes(a, 1, 2)
    w_gate, m_w_gate, v_w_gate, w_up, m_w_up, v_w_up = map(tr_, (w_gate, m_w_gate, v_w_gate, w_up, m_w_up, v_w_up))
    big_w = (("w_in", w_in), ("w_out", w_out), ("w_gate", w_gate), ("w_up", w_up), ("w_down", w_down))
    first, tok = _gather_start([w_in[0].astype(BF16)], tok, "gather_start_w_in")
    rest, tok = _gather_start([w[0].astype(BF16) for _, w in big_w[1:]], tok, "gather_start_weights")
    gathers = {nm: h for (nm, _), h in zip(big_w, first + rest)}
    shift1 = shift1 + tok[0:1, 0:1]

    tm = _tile(s, 512, 16)
    nt = s // tm
    h1 = _prenorm_fwd(xs, norm1_g, scale1, shift1, "prenorm1")
    w_in_all = _gather_wait(_gather_forward(gathers["w_in"], h1, "gather_fwd_w_in")[0], h1, "gather_wait_w_in")
    proj = _mm(h1, w_in_all, pl.BlockSpec((s, d), lambda j: (0, 0)), pl.BlockSpec((None, d, n_in), lambda j: (j, 0, 0)),
               jax.ShapeDtypeStruct((s, NDEV * n_in), F32), pl.BlockSpec((s, n_in), lambda j: (0, j)),
               (NDEV,), NN, False, "proj")
    fwd_w_out, tok = _gather_forward(gathers["w_out"], proj, "gather_fwd_w_out")
    b_cols = jnp.pad(b_spatial[0].T, ((0, 0), (0, GRP - b_spatial.shape[1])))
    on_sb, o_sb, a_sv, s_sv = _sb_fwd(proj, out_norm_g, gm, sb, tok)
    fwd_w_gate, tok = _gather_forward(gathers["w_gate"], on_sb, "gather_fwd_w_gate")
    on_gm = _gmlp_fwd(proj, v_norm_g, w_spatial[0], b_cols, out_norm_g, gm, tok)
    fwd_w_up, tok = _gather_forward(gathers["w_up"], on_gm, "gather_fwd_w_up")
    o_n = jnp.concatenate([on_gm, on_sb], axis=1)
    w_out_all = _gather_wait(fwd_w_out, tok, "gather_wait_w_out").reshape(d, d)
    rows_1 = pl.BlockSpec((tm, d), lambda i: (i, 0))
    whole_1 = pl.BlockSpec((d, d), lambda i: (0, 0))
    p_out = _mm(o_n, w_out_all, rows_1, whole_1, jax.ShapeDtypeStruct((s, d), F32), rows_1, (nt,), NN, False,
                "out_proj")
    x1, h2 = _residual_prenorm(xs, gate1, p_out, norm2_g, scale2, shift2, "residual1_prenorm2")
    w_gate_all = _gather_wait(fwd_w_gate, h2, "gather_wait_w_gate")
    w_up_all = _gather_wait(fwd_w_up, h2, "gather_wait_w_up")
    g_pre, u_pre, act = _ffn_up(h2, w_gate_all, w_up_all)
    fwd_w_down, tok = _gather_forward(gathers["w_down"], act, "gather_fwd_w_down")
    w_down_all = _gather_wait(fwd_w_down, tok, "gather_wait_w_down")
    tm2 = _tile(s, 1024, 16)
    f_out = _mm(act, w_down_all, pl.BlockSpec((None, tm2, ffb), lambda i, j: (j, i, 0)),
                pl.BlockSpec((None, ffb, d), lambda i, j: (j, 0, 0)), jax.ShapeDtypeStruct((s, d), F32),
                pl.BlockSpec((tm2, d), lambda i, j: (i, 0)), (s // tm2, NDEV), NN, True, "ffn_down")

    dx2, df, st_f = _final_loss(x1, f_out, final_g.reshape(1, d), tgt, gate2)
    dg, du = _ffn_bwd_act(df, g_pre, u_pre, w_down_all)
    dh2 = _ffn_bwd_in(dg, du, w_gate_all, w_up_all)
    hid = pl.BlockSpec((None, s, ffb), lambda j: (j, 0, 0))
    all_d = pl.BlockSpec((s, d), lambda j: (0, 0))
    col_sds = jax.ShapeDtypeStruct((NDEV, d, ffb), BF16)
    col_out = pl.BlockSpec((None, d, ffb), lambda j: (j, 0, 0))

    def second_leg(first, after, tag):
        parts, lands = _scatter_wait1(first, after, "scatter_wait1_" + tag)
        sums = [_chip_sum(p, l, "chip_sum_%s_%d" % (tag, k)) for k, (p, l) in enumerate(zip(parts, lands))]
        return _scatter_start2(sums, after, "scatter_start2_" + tag)

    row_sds = jax.ShapeDtypeStruct((NDEV, ffb, d), BF16)
    row_out = pl.BlockSpec((None, ffb, d), lambda j: (j, 0, 0))
    gw_gate = _mm(dg, h2, hid, all_d, row_sds, row_out, (NDEV,), TN, False, "grad_w_gate")
    gw_up = _mm(du, h2, hid, all_d, row_sds, row_out, (NDEV,), TN, False, "grad_w_up")
    gw_down = _mm(act, df, hid, all_d, row_sds, row_out, (NDEV,), TN, False, "grad_w_down")
    dx1, dp, st2 = _prenorm_bwd(x1, dh2, dx2, f_out, norm2_g, scale2, gate1, "prenorm2_bwd", gw_down)
    first_ffn, tok = _scatter_start1([gw_gate, gw_up, gw_down], dp, "scatter_start1_ffn")
    don = _mm(dp, w_out_all, rows_1, whole_1, jax.ShapeDtypeStruct((s, d), F32), rows_1, (nt,), NT, False,
              "out_proj_bwd", tok)
    gw_out = _mm(o_n, dp, pl.BlockSpec((s, ob), lambda j: (0, j)), all_d,
                 jax.ShapeDtypeStruct((NDEV, ob, d), BF16), pl.BlockSpec((None, ob, d), lambda j: (j, 0, 0)),
                 (NDEV,), TN, False, "grad_w_out", don)
    first_out, tok = _scatter_start1([gw_out], tok, "scatter_start1_out")
    second_ffn, tok = second_leg(first_ffn, tok, "ffn")
    dproj_gm, dw_sp, db_cols, dgv, dgam_gm = _gmlp_bwd(proj, don, v_norm_g, w_spatial[0], b_cols, out_norm_g, gm, tok)
    second_out, tok = second_leg(first_out, dproj_gm, "out")
    dq, dk, dv, dgam_sb = _sb_bwd(proj, o_sb, a_sv, s_sv, don, out_norm_g, gm, sb, tok)
    dproj = jnp.concatenate([dproj_gm, dq, dk, dv], axis=1)
    gw_in = _mm(h1, dproj, all_d, pl.BlockSpec((s, n_in), lambda j: (0, j)),
                jax.ShapeDtypeStruct((NDEV, d, n_in), BF16), pl.BlockSpec((None, d, n_in), lambda j: (j, 0, 0)),
                (NDEV,), TN, False, "grad_w_in")
    first_in, tok = _scatter_start1([gw_in], tok, "scatter_start1_in")
    dh1 = _mm(dproj, w_in_all, pl.BlockSpec((tm2, n_in), lambda i, j: (i, j)),
              pl.BlockSpec((None, d, n_in), lambda i, j: (j, 0, 0)), jax.ShapeDtypeStruct((s, d), F32),
              pl.BlockSpec((tm2, d), lambda i, j: (i, 0)), (s // tm2, NDEV), NT, True, "in_proj_bwd", tok)
    grad_x, _, st1 = _prenorm_bwd(xs, dh1, dx1, p_out, norm1_g, scale1, gate1, "prenorm1_bwd", tok)

    dmod = jnp.concatenate([st1[0], st1[1], st1[3], st2[0], st2[1], st2[3]])
    loss_row = jnp.pad((0.5 * jnp.sum(st_f[1]) / d).reshape(1, 1), ((0, 0), (0, GRP - 1)))
    small = [st1[2], dgv, dw_sp, db_cols[:, :b_spatial.shape[1]].T, jnp.concatenate([dgam_gm, dgam_sb], axis=1),
             st2[2], st_f[0], dmod, loss_row]
    zero_row = jnp.zeros((1, GRP), F32)
    small_w = [norm1_g, v_norm_g, w_spatial, b_spatial, out_norm_g, norm2_g, final_g, b_ada, zero_row]
    small_m = [m_norm1_g, m_v_norm_g, m_w_spatial, m_b_spatial, m_out_norm_g, m_norm2_g, m_final_g, m_b_ada, zero_row]
    small_v = [v_norm1_g, v_v_norm_g, v_w_spatial, v_b_spatial, v_out_norm_g, v_norm2_g, v_final_g, v_b_ada, zero_row]
    (small_first,), tok = _gather_start([_pack(small)], grad_x, "gather_small_start")
    second_in, tok = second_leg(first_in, tok, "in")

    big = {}
    prev = tok
    for second, group in ((second_ffn, (("w_gate", w_gate, m_w_gate, v_w_gate), ("w_up", w_up, m_w_up, v_w_up),
                                        ("w_down", w_down, m_w_down, v_w_down))),
                          (second_out, (("w_out", w_out, m_w_out, v_w_out),))):
        chip_sums = _scatter_wait2(second, prev, "scatter_wait2_" + group[0][0])
        for (nm, w, m, v), part in zip(group, chip_sums):
            big[nm] = _reduce_adamw(part, w[0], m[0], v[0], "adamw_" + nm, tok)
            prev = big[nm][1]

    small_second, tok2 = _gather_forward(small_first, prev, "gather_small_fwd")
    small_all = _gather_wait(small_second, tok2, "gather_small_wait")
    sm = _reduce_adamw(small_all, _pack(small_w), _pack(small_m), _pack(small_v), "adamw_small", tok)
    shapes = [a.shape for a in small_w]
    sm_g, sm_d, sm_m, sm_v = [_unpack(t, shapes) for t in sm]
    loss = sm_g[8][0, 0]

    at = sum(a.size for a in small_w[:7]) // GRP
    dmod_all = small_all[:, at:at + 6 * d // GRP, :].reshape(NDEV, 6 * d)
    n_ada = w_ada.shape[2]
    dmod_cols = lax.dynamic_slice(dmod_all, (0, me * n_ada), (NDEV, n_ada))
    g_ada = _ada_bwd(c_all, dmod_cols)
    big["w_ada"] = _reduce_adamw(g_ada[None], w_ada[0], m_w_ada[0], v_w_ada[0], "adamw_w_ada", sm[0])
    part_in, = _scatter_wait2(second_in, big["w_ada"][1], "scatter_wait2_w_in")
    big["w_in"] = _reduce_adamw(part_in, w_in[0], m_w_in[0], v_w_in[0], "adamw_w_in", tok)

    names = ["w_ada", "b_ada", "norm1_g", "w_in", "v_norm_g", "w_spatial", "b_spatial", "out_norm_g", "w_out",
             "norm2_g", "w_gate", "w_up", "w_down", "final_g"]
    small_at = {"norm1_g": 0, "v_norm_g": 1, "w_spatial": 2, "b_spatial": 3, "out_norm_g": 4, "norm2_g": 5,
                "final_g": 6, "b_ada": 7}
    outs = [[], [], [], []]
    for nm in names:
        for k in range(4):
            if nm in big:
                res = big[nm][k][None]
                outs[k].append(tr_(res) if nm in ("w_gate", "w_up") else res)
            else:
                outs[k].append((sm_g, sm_d, sm_m, sm_v)[k][small_at[nm]])
    return (loss, grad_x[None], *outs[0], *outs[1], *outs[2], *outs[3])

    hid = pl.BlockSpec((None, s, ffb), lambda j: (j, 0, 0))
    all_d = pl.BlockSpec((s, d), lambda j: (0, 0))
    col_sds = jax.ShapeDtypeStruct((NDEV, d, ffb), BF16)
    col_out = pl.BlockSpec((None, d, ffb), lambda j: (j, 0, 0))
    scatters = {}
    gw_gate = _mm(h2, dg, all_d, hid, col_sds, col_out, (NDEV,), TN, False, "grad_w_gate")
    scatters["w_gate"], tok = _exchange_start(gw_gate, True, tok, "scatter_start_w_gate")
    gw_up = _mm(h2, du, all_d, hid, col_sds, col_out, (NDEV,), TN, False, "grad_w_up", tok)
    scatters["w_up"], tok = _exchange_start(gw_up, True, tok, "scatter_start_w_up")
    gw_down = _mm(act, df, hid, all_d, jax.ShapeDtypeStruct((NDEV, ffb, d), BF16),
                  pl.BlockSpec((None, ffb, d), lambda j: (j, 0, 0)), (NDEV,), TN, False, "grad_w_down", tok)
    scatters["w_down"], tok = _exchange_start(gw_down, True, tok, "scatter_start_w_down")
    dx1, dp, st2 = _prenorm_bwd(x1, dh2, dx2, f_out, norm2_g, scale2, gate1, "prenorm2_bwd", tok)
    don = _mm(dp, w_out_all, rows_1, whole_1, jax.ShapeDtypeStruct((s, d), F32), rows_1, (nt,), NT, False,
              "out_proj_bwd")
    gw_out = _mm(o_n, dp, pl.BlockSpec((s, ob), lambda j: (0, j)), all_d,
                 jax.ShapeDtypeStruct((NDEV, ob, d), BF16), pl.BlockSpec((None, ob, d), lambda j: (j, 0, 0)),
                 (NDEV,), TN, False, "grad_w_out", don)
    scatters["w_out"], tok = _exchange_start(gw_out, True, tok, "scatter_start_w_out")
    dproj_gm, dw_sp, db_cols, dgv, dgam_gm = _gmlp_bwd(proj, don, v_norm_g, w_spatial[0], b_cols, out_norm_g, gm, tok)
    dq, dk, dv, dgam_sb = _sb_bwd(proj, o_sb, mtot, don, out_norm_g, gm, sb)
    dproj = jnp.concatenate([dproj_gm, dq, dk, dv], axis=1)
    dh1 = _mm(dproj, w_in_all, pl.BlockSpec((tm, n_in), lambda i, j: (i, j)),
              pl.BlockSpec((None, d, n_in), lambda i, j: (j, 0, 0)), jax.ShapeDtypeStruct((s, d), F32), rows_d,
              (nt, NDEV), NT, True, "in_proj_bwd")
    grad_x, _, st1 = _prenorm_bwd(xs, dh1, dx1, p_out, norm1_g, scale1, gate1, "prenorm1_bwd", tok)

    dmod = jnp.concatenate([st1[0], st1[1], st1[3], st2[0], st2[1], st2[3]])
    loss_row = jnp.pad((0.5 * jnp.sum(st_f[1]) / d).reshape(1, 1), ((0, 0), (0, GRP - 1)))
    small = [st1[2], dgv, dw_sp, db_cols[:, :b_spatial.shape[1]].T, jnp.concatenate([dgam_gm, dgam_sb], axis=1),
             st2[2], st_f[0], dmod, loss_row]
    zero_row = jnp.zeros((1, GRP), F32)
    small_w = [norm1_g, v_norm_g, w_spatial, b_spatial, out_norm_g, norm2_g, final_g, b_ada, zero_row]
    small_m = [m_norm1_g, m_v_norm_g, m_w_spatial, m_b_spatial, m_out_norm_g, m_norm2_g, m_final_g, m_b_ada, zero_row]
    small_v = [v_norm1_g, v_v_norm_g, v_w_spatial, v_b_spatial, v_out_norm_g, v_norm2_g, v_final_g, v_b_ada, zero_row]
    small_handles, tok = _exchange_start(_pack(small), False, tok, "gather_small_start")
    gw_in = _mm(h1, dproj, all_d, pl.BlockSpec((s, n_in), lambda j: (0, j)),
                jax.ShapeDtypeStruct((NDEV, d, n_in), BF16), pl.BlockSpec((None, d, n_in), lambda j: (j, 0, 0)),
                (NDEV,), TN, False, "grad_w_in", tok)
    scatters["w_in"], tok = _exchange_start(gw_in, True, tok, "scatter_start_w_in")
    small_all = _exchange_wait(small_handles, tok, "gather_small_wait")
    sm = _reduce_adamw(small_all, _pack(small_w), _pack(small_m), _pack(small_v), "adamw_small", tok)
    shapes = [a.shape for a in small_w]
    sm_g, sm_d, sm_m, sm_v = [_unpack(t, shapes) for t in sm]

    loss = sm_g[8][0, 0]
    at = sum(a.size for a in small_w[:7]) // GRP
    dmod_all = small_all[:, at:at + 6 * d // GRP, :].reshape(NDEV, 6 * d)
    n_ada = w_ada.shape[2]
    dmod_cols = lax.dynamic_slice(dmod_all, (0, me * n_ada), (NDEV, n_ada))
    g_ada = _ada_bwd(c_all, dmod_cols)
    big = {"w_ada": _reduce_adamw(g_ada[None], w_ada[0], m_w_ada[0], v_w_ada[0], "adamw_w_ada", sm[0])}

    prev = big["w_ada"][1]
    for nm, w, m, v in (("w_gate", w_gate, m_w_gate, v_w_gate), ("w_up", w_up, m_w_up, v_w_up),
                        ("w_down", w_down, m_w_down, v_w_down), ("w_out", w_out, m_w_out, v_w_out),
                        ("w_in", w_in, m_w_in, v_w_in)):
        parts = _exchange_wait(scatters[nm], prev, "scatter_wait_" + nm)
        big[nm] = _reduce_adamw(parts, w[0], m[0], v[0], "adamw_" + nm, tok)
        prev = big[nm][1]

    names = ["w_ada", "b_ada", "norm1_g", "w_in", "v_norm_g", "w_spatial", "b_spatial", "out_norm_g", "w_out",
             "norm2_g", "w_gate", "w_up", "w_down", "final_g"]
    small_at = {"norm1_g": 0, "v_norm_g": 1, "w_spatial": 2, "b_spatial": 3, "out_norm_g": 4, "norm2_g": 5,
                "final_g": 6, "b_ada": 7}
    outs = [[], [], [], []]
    for nm in names:
        for k in range(4):
            outs[k].append(big[nm][k][None] if nm in big else (sm_g, sm_d, sm_m, sm_v)[k][small_at[nm]])
    return (loss, grad_x[None], *outs[0], *outs[1], *outs[2], *outs[3])
```

```python
import jax
import jax.numpy as jnp
from jax import lax
from jax.experimental import pallas as pl
from jax.experimental.pallas import tpu as pltpu

F32, BF16 = jnp.float32, jnp.bfloat16
NDEV = 8
GRP = 128
EPS = 1e-6
VMEM_BYTES = 64 * 2 ** 20
VMEM_LIMIT = VMEM_BYTES - 8 * 2 ** 20
ADAM_LR, ADAM_B1, ADAM_B2, ADAM_EPS, ADAM_WD, ADAM_STEP = 0.001, 0.9, 0.999, 1e-08, 0.01, 10
MESH = pl.DeviceIdType.MESH
NN = (((1,), (0,)), ((), ()))
NT = (((1,), (1,)), ((), ()))
TN = (((0,), (0,)), ((), ()))


def _params(n_axes):
    return pltpu.CompilerParams(dimension_semantics=("arbitrary",) * n_axes, vmem_limit_bytes=VMEM_LIMIT)


def _tile(n, cap, mult):
    best = None
    for t in range(mult, min(n, cap) + 1, mult):
        if n % t == 0:
            best = t
    assert best is not None, (n, cap, mult)
    return best


def _dot(a, b, dims):
    return lax.dot_general(a, b, dims, preferred_element_type=F32)


def _exchange(src, per_peer, name):
    blk = src.shape[1:] if per_peer else src.shape

    def body(src_ref, out_ref, send_sems, recv_sems, local_sem):
        x, y, c = lax.axis_index("x"), lax.axis_index("y"), lax.axis_index("c")
        me = 4 * x + 2 * y + c
        local = pltpu.make_async_copy(src_ref.at[me] if per_peer else src_ref, out_ref.at[me], local_sem)
        local.start()
        sends, recvs = [], []
        for k in range(1, NDEV):
            px, py, pc = x ^ (k >> 2), y ^ ((k >> 1) & 1), c ^ (k & 1)
            p = 4 * px + 2 * py + pc
            mine = src_ref.at[p] if per_peer else src_ref
            sends.append(pltpu.make_async_remote_copy(
                src_ref=mine, dst_ref=out_ref.at[me], send_sem=send_sems.at[k - 1], recv_sem=recv_sems.at[k - 1],
                device_id=(px, py, pc), device_id_type=MESH))
            recvs.append(pltpu.make_async_remote_copy(
                src_ref=mine, dst_ref=out_ref.at[p], send_sem=send_sems.at[k - 1], recv_sem=recv_sems.at[k - 1],
                device_id=(px, py, pc), device_id_type=MESH))
        for cp in sends:
            cp.start()
        for cp in recvs:
            cp.wait_recv()
        for cp in sends:
            cp.wait_send()
        local.wait()

    return pl.pallas_call(
        body, name=name,
        out_shape=jax.ShapeDtypeStruct((NDEV,) + tuple(blk), src.dtype),
        in_specs=[pl.BlockSpec(memory_space=pl.ANY)],
        out_specs=pl.BlockSpec(memory_space=pl.ANY),
        scratch_shapes=[pltpu.SemaphoreType.DMA((NDEV - 1,)), pltpu.SemaphoreType.DMA((NDEV - 1,)),
                        pltpu.SemaphoreType.DMA],
    )(src)


_HBM = pl.BlockSpec(memory_space=pltpu.HBM)
_SEM = pl.BlockSpec(memory_space=pltpu.SEMAPHORE)
_ANY = pl.BlockSpec(memory_space=pl.ANY)
_EFFECT = pltpu.SideEffectType.DATAFLOW_SIDE_EFFECTING


def _chip_peers():
    x, y, c = lax.axis_index("x"), lax.axis_index("y"), lax.axis_index("c")
    chips = [(x, 1 - y), (1 - x, y), (1 - x, 1 - y)]
    return 4 * x + 2 * y + c, (x, y, 1 - c), [((px, py, c), 4 * px + 2 * py + c) for px, py in chips]


def _gather_start(srcs, after, name):
    n = len(srcs)
    me = 4 * lax.axis_index("x") + 2 * lax.axis_index("y") + lax.axis_index("c")
    lands = [lax.dynamic_update_slice(lax.empty((NDEV,) + a.shape, a.dtype), a[None], (me,) + (0,) * a.ndim) for a in srcs]

    def body(*refs):
        src_refs, land_refs = refs[:n], refs[n:2 * n]
        sems, token = refs[2 * n + 1:2 * n + 1 + 2 * n], refs[-1]
        my, sibling, chips = _chip_peers()
        for a in range(n):
            for k, to in enumerate([sibling] + [coords for coords, _ in chips]):
                pltpu.make_async_remote_copy(src_ref=src_refs[a], dst_ref=land_refs[a].at[my], send_sem=sems[2 * a].at[k],
                                             recv_sem=sems[2 * a + 1].at[k], device_id=to, device_id_type=MESH).start()
        token[...] = jnp.zeros_like(token)

    hbm = lambda arrs: tuple(pltpu.HBM(a.shape, a.dtype) for a in arrs)
    res = pl.pallas_call(
        body, name=name,
        out_shape=(pltpu.SemaphoreType.DMA((4,)),) * (2 * n) + hbm(srcs) + hbm(lands) + (jax.ShapeDtypeStruct((8, GRP), F32),),
        in_specs=(_HBM,) * (2 * n) + (_ANY,),
        out_specs=(_SEM,) * (2 * n) + (_HBM,) * (2 * n) + (pl.BlockSpec(memory_space=pltpu.VMEM),),
        input_output_aliases={k: 2 * n + k for k in range(2 * n)}, compiler_params=pltpu.CompilerParams(has_side_effects=_EFFECT),
    )(*[pltpu.with_memory_space_constraint(a, pltpu.HBM) for a in list(srcs) + lands], after)
    return [(res[2 * a], res[2 * a + 1], res[2 * n + a], res[3 * n + a]) for a in range(n)], res[-1]


def _gather_forward(handles, after, name):
    send_sems, recv_sems, src_thru, land_thru = handles

    def body(src_ref, land_ref, send_sems, recv_sems, after_ref, send2, recv2, land_out, token):
        my, sibling, chips = _chip_peers()
        for k, (to, p) in enumerate([(sibling, my ^ 1)] + chips):
            first = pltpu.make_async_remote_copy(src_ref=src_ref, dst_ref=land_ref.at[p], send_sem=send_sems.at[k],
                                                 recv_sem=recv_sems.at[k], device_id=to, device_id_type=MESH)
            first.wait_send()
            first.wait_recv()
        for k, (_, p) in enumerate(chips):
            pltpu.make_async_remote_copy(src_ref=land_ref.at[p], dst_ref=land_ref.at[p], send_sem=send2.at[k],
                                         recv_sem=recv2.at[k], device_id=sibling, device_id_type=MESH).start()
        token[...] = jnp.zeros_like(token)

    res = pl.pallas_call(
        body, name=name,
        out_shape=(pltpu.SemaphoreType.DMA((3,)), pltpu.SemaphoreType.DMA((3,)), pltpu.HBM(land_thru.shape, land_thru.dtype),
                   jax.ShapeDtypeStruct((8, GRP), F32)),
        in_specs=(_HBM, _HBM, _SEM, _SEM, _ANY), out_specs=(_SEM, _SEM, _HBM, pl.BlockSpec(memory_space=pltpu.VMEM)),
        input_output_aliases={1: 2}, compiler_params=pltpu.CompilerParams(has_side_effects=_EFFECT),
    )(src_thru, land_thru, send_sems, recv_sems, after)
    return tuple(res[:3]), res[3]


def _gather_wait(handles, after, name):
    send2, recv2, land_thru = handles

    def body(land_ref, send2, recv2, after_ref, got_ref):
        _, sibling, chips = _chip_peers()
        for k, (_, p) in enumerate(chips):
            cp = pltpu.make_async_remote_copy(src_ref=land_ref.at[p], dst_ref=land_ref.at[p ^ 1], send_sem=send2.at[k],
                                              recv_sem=recv2.at[k], device_id=sibling, device_id_type=MESH)
            cp.wait_send()
            cp.wait_recv()

    return pl.pallas_call(
        body, name=name, out_shape=pltpu.HBM(land_thru.shape, land_thru.dtype),
        in_specs=(_HBM, _SEM, _SEM, _ANY), out_specs=_HBM, input_output_aliases={0: 0},
        compiler_params=pltpu.CompilerParams(has_side_effects=_EFFECT),
    )(land_thru, send2, recv2, after)


NCHIP = NDEV // 2


def _scatter_start1(parts, after, name):
    n = len(parts)
    lands = [lax.empty((NCHIP,) + p.shape[1:], p.dtype) for p in parts]

    def body(*refs):
        part_refs, land_refs = refs[:n], refs[n:2 * n]
        send_sems, recv_sems, token = refs[2 * n + 1], refs[2 * n + 2], refs[-1]
        _, sibling, _ = _chip_peers()
        c = lax.axis_index("c")
        for a in range(n):
            for q in range(NCHIP):
                pltpu.make_async_remote_copy(
                    src_ref=part_refs[a].at[2 * q + 1 - c], dst_ref=land_refs[a].at[q], send_sem=send_sems.at[a * NCHIP + q],
                    recv_sem=recv_sems.at[a * NCHIP + q], device_id=sibling, device_id_type=MESH).start()
        token[...] = jnp.zeros_like(token)

    hbm = lambda arrs: tuple(pltpu.HBM(a.shape, a.dtype) for a in arrs)
    res = pl.pallas_call(
        body, name=name,
        out_shape=(pltpu.SemaphoreType.DMA((n * NCHIP,)), pltpu.SemaphoreType.DMA((n * NCHIP,))) + hbm(parts) + hbm(lands)
        + (jax.ShapeDtypeStruct((8, GRP), F32),),
        in_specs=(_HBM,) * (2 * n) + (_ANY,),
        out_specs=(_SEM, _SEM) + (_HBM,) * (2 * n) + (pl.BlockSpec(memory_space=pltpu.VMEM),),
        input_output_aliases={k: 2 + k for k in range(2 * n)}, compiler_params=pltpu.CompilerParams(has_side_effects=_EFFECT),
    )(*[pltpu.with_memory_space_constraint(a, pltpu.HBM) for a in list(parts) + lands], after)
    return (n,) + tuple(res[:-1]), res[-1]


def _scatter_wait1(handles, after, name):
    n, send_sems, recv_sems = handles[:3]
    thru = handles[3:]

    def body(*refs):
        part_refs, land_refs, send_sems, recv_sems = refs[:n], refs[n:2 * n], refs[2 * n], refs[2 * n + 1]
        _, sibling, _ = _chip_peers()
        for a in range(n):
            for q in range(NCHIP):
                cp = pltpu.make_async_remote_copy(
                    src_ref=part_refs[a].at[q], dst_ref=land_refs[a].at[q], send_sem=send_sems.at[a * NCHIP + q],
                    recv_sem=recv_sems.at[a * NCHIP + q], device_id=sibling, device_id_type=MESH)
                cp.wait_send()
                cp.wait_recv()

    res = pl.pallas_call(
        body, name=name, out_shape=tuple(pltpu.HBM(a.shape, a.dtype) for a in thru),
        in_specs=(_HBM,) * (2 * n) + (_SEM, _SEM, _ANY), out_specs=(_HBM,) * (2 * n),
        input_output_aliases={k: k for k in range(2 * n)}, compiler_params=pltpu.CompilerParams(has_side_effects=_EFFECT),
    )(*thru, send_sems, recv_sems, after)
    return res[:n], res[n:]


def _chip_sum(part, land, name):
    _, r, c = part.shape
    tr = _tile(r, max(16, 2 ** 21 // c), 16)

    def body(core_ref, p_ref, l_ref, o_ref):
        o_ref[...] = (p_ref[...].astype(F32) + l_ref[...].astype(F32)).astype(o_ref.dtype)

    blk = pl.BlockSpec((None, tr, c), lambda q, i, core: (q, i, 0))
    return pl.pallas_call(
        body, name=name, out_shape=jax.ShapeDtypeStruct(land.shape, land.dtype),
        grid_spec=pltpu.PrefetchScalarGridSpec(
            num_scalar_prefetch=1, grid=(NCHIP, r // tr),
            in_specs=[pl.BlockSpec((None, None, tr, c), lambda q, i, core: (q, core[0], i, 0)), blk], out_specs=blk),
        compiler_params=_params(2),
    )(lax.axis_index("c").astype(jnp.int32).reshape(1), part.reshape(NCHIP, 2, r, c), land)


def _scatter_start2(sums, after, name):
    n = len(sums)
    chip = 2 * lax.axis_index("x") + lax.axis_index("y")
    lands = [lax.dynamic_update_slice(lax.empty(s_.shape, s_.dtype), lax.dynamic_index_in_dim(s_, chip, 0, keepdims=True),
                                      (chip,) + (0,) * (s_.ndim - 1)) for s_ in sums]

    def body(*refs):
        sum_refs, land_refs = refs[:n], refs[n:2 * n]
        send_sems, recv_sems, token = refs[2 * n + 1], refs[2 * n + 2], refs[-1]
        my, _, chips = _chip_peers()
        for a in range(n):
            for k, (to, p) in enumerate(chips):
                pltpu.make_async_remote_copy(
                    src_ref=sum_refs[a].at[p // 2], dst_ref=land_refs[a].at[my // 2], send_sem=send_sems.at[a * 3 + k],
                    recv_sem=recv_sems.at[a * 3 + k], device_id=to, device_id_type=MESH).start()
        token[...] = jnp.zeros_like(token)

    hbm = lambda arrs: tuple(pltpu.HBM(a.shape, a.dtype) for a in arrs)
    res = pl.pallas_call(
        body, name=name,
        out_shape=(pltpu.SemaphoreType.DMA((n * 3,)), pltpu.SemaphoreType.DMA((n * 3,))) + hbm(sums) + hbm(lands)
        + (jax.ShapeDtypeStruct((8, GRP), F32),),
        in_specs=(_HBM,) * (2 * n) + (_ANY,),
        out_specs=(_SEM, _SEM) + (_HBM,) * (2 * n) + (pl.BlockSpec(memory_space=pltpu.VMEM),),
        input_output_aliases={k: 2 + k for k in range(2 * n)}, compiler_params=pltpu.CompilerParams(has_side_effects=_EFFECT),
    )(*[pltpu.with_memory_space_constraint(a, pltpu.HBM) for a in list(sums) + lands], after)
    return (n,) + tuple(res[:-1]), res[-1]


def _scatter_wait2(handles, after, name):
    n, send_sems, recv_sems = handles[:3]
    thru = handles[3:]

    def body(*refs):
        sum_refs, land_refs, send_sems, recv_sems = refs[:n], refs[n:2 * n], refs[2 * n], refs[2 * n + 1]
        _, _, chips = _chip_peers()
        for a in range(n):
            for k, (to, p) in enumerate(chips):
                cp = pltpu.make_async_remote_copy(
                    src_ref=sum_refs[a].at[p // 2], dst_ref=land_refs[a].at[p // 2], send_sem=send_sems.at[a * 3 + k],
                    recv_sem=recv_sems.at[a * 3 + k], device_id=to, device_id_type=MESH)
                cp.wait_send()
                cp.wait_recv()

    res = pl.pallas_call(
        body, name=name, out_shape=tuple(pltpu.HBM(a.shape, a.dtype) for a in thru),
        in_specs=(_HBM,) * (2 * n) + (_SEM, _SEM, _ANY), out_specs=(_HBM,) * (2 * n),
        input_output_aliases={k: k for k in range(2 * n)}, compiler_params=pltpu.CompilerParams(has_side_effects=_EFFECT),
    )(*thru, send_sems, recv_sems, after)
    return res[n:]


def _mm(a, b, a_spec, b_spec, out_sds, o_spec, grid, dims, acc, name, after=None):
    extra = () if after is None else (after,)
    assert not acc or out_sds.dtype == F32

    def body(a_ref, b_ref, *rest):
        o_ref = rest[len(extra)]
        if len(b_ref.shape) == 3:
            nkb = b_ref.shape[0]
            wk = a_ref.shape[-1] // nkb
            prod = sum(_dot(a_ref[k] if len(a_ref.shape) == 3 else a_ref[:, k * wk:(k + 1) * wk], b_ref[k], dims)
                       for k in range(nkb))
        else:
            prod = _dot(a_ref[...], b_ref[...], dims)
        if not acc:
            o_ref[...] = prod.astype(o_ref.dtype)
            return
        k = pl.program_id(len(grid) - 1)

        @pl.when(k == 0)
        def _():
            o_ref[...] = prod

        @pl.when(k > 0)
        def _():
            o_ref[...] += prod

    return pl.pallas_call(
        body, name=name, grid=grid, out_shape=out_sds, in_specs=[a_spec, b_spec] + [_ANY] * len(extra), out_specs=o_spec,
        compiler_params=_params(len(grid)),
    )(a, b, *extra)


def _row_spec(tm, d):
    return pl.BlockSpec((tm, d), lambda i: (i, 0))


def _vec_spec(d):
    return pl.BlockSpec((1, d), lambda i: (0, 0))


def _prenorm_fwd(x, g, scale, shift, name):
    s, d = x.shape
    tm = _tile(s, 256, 16)

    def body(x_ref, g_ref, sc_ref, sh_ref, h_ref):
        xv = x_ref[...]
        rstd = lax.rsqrt(jnp.mean(xv * xv, axis=-1, keepdims=True) + EPS)
        h_ref[...] = ((xv * rstd * g_ref[...]) * (1.0 + sc_ref[...]) + sh_ref[...]).astype(BF16)

    return pl.pallas_call(
        body, name=name, grid=(s // tm,), out_shape=jax.ShapeDtypeStruct((s, d), BF16),
        in_specs=[_row_spec(tm, d), _vec_spec(d), _vec_spec(d), _vec_spec(d)], out_specs=_row_spec(tm, d),
        compiler_params=_params(1),
    )(x, g, scale, shift)


def _residual_prenorm(x, gate, p, g, scale, shift, name):
    s, d = x.shape
    tm = _tile(s, 256, 16)

    def body(x_ref, gate_ref, p_ref, g_ref, sc_ref, sh_ref, x1_ref, h_ref):
        xv = x_ref[...] + gate_ref[...] * p_ref[...]
        x1_ref[...] = xv
        rstd = lax.rsqrt(jnp.mean(xv * xv, axis=-1, keepdims=True) + EPS)
        h_ref[...] = ((xv * rstd * g_ref[...]) * (1.0 + sc_ref[...]) + sh_ref[...]).astype(BF16)

    return pl.pallas_call(
        body, name=name, grid=(s // tm,),
        out_shape=(jax.ShapeDtypeStruct((s, d), F32), jax.ShapeDtypeStruct((s, d), BF16)),
        in_specs=[_row_spec(tm, d), _vec_spec(d), _row_spec(tm, d), _vec_spec(d), _vec_spec(d), _vec_spec(d)],
        out_specs=(_row_spec(tm, d), _row_spec(tm, d)), compiler_params=_params(1),
    )(x, gate, p, g, scale, shift)


def _final_loss(x1, f, final_g, target, gate2):
    s, d = x1.shape
    tm = _tile(s, 256, 16)

    def body(x_ref, f_ref, g_ref, t_ref, gate_ref, dx_ref, df_ref, st_ref):
        @pl.when(pl.program_id(0) == 0)
        def _():
            st_ref[...] = jnp.zeros_like(st_ref)

        xv, gf = x_ref[...] + gate_ref[...] * f_ref[...], g_ref[...]
        rstd = lax.rsqrt(jnp.mean(xv * xv, axis=-1, keepdims=True) + EPS)
        xhat = xv * rstd
        err = xhat * gf - t_ref[...]
        dy = err * (1.0 / d)
        gdy = dy * gf
        dx = rstd * (gdy - xhat * jnp.mean(gdy * xhat, axis=-1, keepdims=True))
        dx_ref[...] = dx
        df_ref[...] = (gate_ref[...] * dx).astype(BF16)
        st_ref[0:1, :] += jnp.sum(dy * xhat, axis=0, keepdims=True)
        st_ref[1:2, :] += jnp.sum(err * err, axis=0, keepdims=True)

    return pl.pallas_call(
        body, name="final_loss", grid=(s // tm,),
        out_shape=(jax.ShapeDtypeStruct((s, d), F32), jax.ShapeDtypeStruct((s, d), BF16),
                   jax.ShapeDtypeStruct((8, d), F32)),
        in_specs=[_row_spec(tm, d), _row_spec(tm, d), _vec_spec(d), _row_spec(tm, d), _vec_spec(d)],
        out_specs=(_row_spec(tm, d), _row_spec(tm, d), pl.BlockSpec((8, d), lambda i: (0, 0))),
        compiler_params=_params(1),
    )(x1, f, final_g, target, gate2)


def _prenorm_bwd(xin, dh, dres, pf, g, scale, gate_next, name, after):
    s, d = xin.shape
    tm = _tile(s, 256, 16)

    def body(x_ref, dh_ref, dr_ref, pf_ref, g_ref, sc_ref, gn_ref, after_ref, dx_ref, dn_ref, st_ref):
        @pl.when(pl.program_id(0) == 0)
        def _():
            st_ref[...] = jnp.zeros_like(st_ref)

        xv, dhv, drv, gv = x_ref[...], dh_ref[...], dr_ref[...], g_ref[...]
        one_sc = 1.0 + sc_ref[...]
        rstd = lax.rsqrt(jnp.mean(xv * xv, axis=-1, keepdims=True) + EPS)
        xhat = xv * rstd
        dxhat = dhv * (gv * one_sc)
        dx = drv + rstd * (dxhat - xhat * jnp.mean(dxhat * xhat, axis=-1, keepdims=True))
        dx_ref[...] = dx
        dn_ref[...] = (gn_ref[...] * dx).astype(BF16)
        dhx = dhv * xhat
        st_ref[0:1, :] += jnp.sum(dhv, axis=0, keepdims=True)
        st_ref[1:2, :] += jnp.sum(dhx, axis=0, keepdims=True) * gv
        st_ref[2:3, :] += jnp.sum(dhx, axis=0, keepdims=True) * one_sc
        st_ref[3:4, :] += jnp.sum(drv * pf_ref[...], axis=0, keepdims=True)

    return pl.pallas_call(
        body, name=name, grid=(s // tm,),
        out_shape=(jax.ShapeDtypeStruct((s, d), F32), jax.ShapeDtypeStruct((s, d), BF16),
                   jax.ShapeDtypeStruct((8, d), F32)),
        in_specs=[_row_spec(tm, d)] * 4 + [_vec_spec(d)] * 3 + [_ANY],
        out_specs=(_row_spec(tm, d), _row_spec(tm, d), pl.BlockSpec((8, d), lambda i: (0, 0))),
        compiler_params=_params(1),
    )(xin, dh, dres, pf, g, scale, gate_next, after)


def _ada_fwd(c_all, w_loc):
    nb, d = c_all.shape
    n = w_loc.shape[1]
    tn = _tile(n, 512, 128) if n % 128 == 0 else n

    def body(c_ref, w_ref, o_ref):
        cv = c_ref[...]
        o_ref[...] = jnp.dot(cv * jax.nn.sigmoid(cv), w_ref[...], preferred_element_type=F32,
                             precision=lax.Precision.HIGHEST)

    return pl.pallas_call(
        body, name="ada_fwd", grid=(n // tn,), out_shape=jax.ShapeDtypeStruct((nb, n), F32),
        in_specs=[pl.BlockSpec((nb, d), lambda j: (0, 0)), pl.BlockSpec((d, tn), lambda j: (0, j))],
        out_specs=pl.BlockSpec((nb, tn), lambda j: (0, j)), compiler_params=_params(1),
    )(c_all, w_loc)


def _ada_bwd(c_all, dmod_cols):
    nb, d = c_all.shape
    n = dmod_cols.shape[1]
    tn = _tile(n, 512, 128) if n % 128 == 0 else n

    def body(c_ref, dm_ref, o_ref):
        cv = c_ref[...]
        o_ref[...] = lax.dot_general(cv * jax.nn.sigmoid(cv), dm_ref[...], TN, preferred_element_type=F32,
                                     precision=lax.Precision.HIGHEST)

    return pl.pallas_call(
        body, name="ada_bwd", grid=(n // tn,), out_shape=jax.ShapeDtypeStruct((d, n), F32),
        in_specs=[pl.BlockSpec((nb, d), lambda j: (0, 0)), pl.BlockSpec((nb, tn), lambda j: (0, j))],
        out_specs=pl.BlockSpec((d, tn), lambda j: (0, j)), compiler_params=_params(1),
    )(c_all, dmod_cols)


_INV_SQRT2 = 0.7071067811865476
_INV_SQRT2PI = 0.3989422804014327


def _gelu_parts(x):
    cdf = 0.5 * (1.0 + lax.erf(x * _INV_SQRT2))
    return x * cdf, cdf + x * jnp.exp(-0.5 * x * x) * _INV_SQRT2PI


def _gm_group_fwd(up, vp, gv, wt, bcol):
    u = 0.5 * up * (1.0 + lax.erf(up * _INV_SQRT2))
    va = 0.5 * vp * (1.0 + lax.erf(vp * _INV_SQRT2))
    return _gm_group_mix(u, va, gv, wt, bcol)


def _gm_group_mix(u, va, gv, wt, bcol):
    xc = va - jnp.mean(va, axis=-1, keepdims=True)
    rstd_v = lax.rsqrt(jnp.mean(xc * xc, axis=-1, keepdims=True) + EPS)
    yv = xc * rstd_v
    vn = (yv * gv).astype(BF16)
    mixed = _dot(wt, vn, NN) + bcol
    return u, rstd_v, yv, vn, mixed, u * mixed


def _tril_bf16(w):
    row = lax.broadcasted_iota(jnp.int32, w.shape, 0)
    col = lax.broadcasted_iota(jnp.int32, w.shape, 1)
    return jnp.where(col <= row, w, 0.0).astype(BF16)


def _gmlp_fwd(proj, v_norm_g, w_spatial, b_cols, gamma, gm, after):
    s = proj.shape[0]
    ng = gm // GRP

    def body(p_ref, gv_ref, w_ref, b_ref, gam_ref, after_ref, o_ref):
        for g in range(ng):
            lo = g * GRP
            wt = _tril_bf16(w_ref[g])
            *_, o = _gm_group_fwd(p_ref[:, lo:lo + GRP], p_ref[:, gm + lo:gm + lo + GRP], gv_ref[:, lo:lo + GRP],
                                  wt, b_ref[:, g:g + 1])
            rstd_o = lax.rsqrt(jnp.mean(o * o, axis=-1, keepdims=True) + EPS)
            o_ref[:, lo:lo + GRP] = (o * rstd_o * gam_ref[:, lo:lo + GRP]).astype(BF16)

    return pl.pallas_call(
        body, name="gmlp_fwd", grid=(s // GRP,), out_shape=jax.ShapeDtypeStruct((s, gm), BF16),
        in_specs=[pl.BlockSpec((GRP, 2 * gm), lambda n: (n, 0)), _vec_spec(gm),
                  pl.BlockSpec((ng, GRP, GRP), lambda n: (0, 0, 0)), pl.BlockSpec((GRP, GRP), lambda n: (0, 0)),
                  _vec_spec(gm), _ANY],
        out_specs=pl.BlockSpec((GRP, gm), lambda n: (n, 0)), compiler_params=_params(1),
    )(proj, v_norm_g, w_spatial, b_cols, gamma, after)


def _gmlp_bwd(proj, don, v_norm_g, w_spatial, b_cols, gamma, gm, after):
    s = proj.shape[0]
    ng = gm // GRP

    def body(p_ref, don_ref, gv_ref, w_ref, b_ref, gam_ref, after_ref, dp_ref, dw_ref, db_ref, dgv_ref, dgam_ref):
        @pl.when(pl.program_id(0) == 0)
        def _():
            dw_ref[...] = jnp.zeros_like(dw_ref)
            db_ref[...] = jnp.zeros_like(db_ref)
            dgv_ref[...] = jnp.zeros_like(dgv_ref)
            dgam_ref[...] = jnp.zeros_like(dgam_ref)

        lane = lax.broadcasted_iota(jnp.int32, (GRP, GRP), 1)
        row = lax.broadcasted_iota(jnp.int32, (GRP, GRP), 0)
        for g in range(ng):
            lo = g * GRP
            up, vp = p_ref[:, lo:lo + GRP], p_ref[:, gm + lo:gm + lo + GRP]
            gv, gam = gv_ref[:, lo:lo + GRP], gam_ref[:, lo:lo + GRP]
            wt = _tril_bf16(w_ref[g])
            (u, du_dup), (va, dva_dvp) = _gelu_parts(up), _gelu_parts(vp)
            u, rstd_v, yv, vn, mixed, o = _gm_group_mix(u, va, gv, wt, b_ref[:, g:g + 1])
            rstd_o = lax.rsqrt(jnp.mean(o * o, axis=-1, keepdims=True) + EPS)
            ohat = o * rstd_o
            dn = don_ref[:, lo:lo + GRP]
            dgam_ref[:, lo:lo + GRP] += jnp.sum(dn * ohat, axis=0, keepdims=True)
            dohat = dn * gam
            do = rstd_o * (dohat - ohat * jnp.mean(dohat * ohat, axis=-1, keepdims=True))
            du = do * mixed
            dmixed = do * u
            dmb = dmixed.astype(BF16)
            db_ref[...] += jnp.where(lane == g, jnp.sum(dmixed, axis=-1, keepdims=True), 0.0)
            dw_ref[g] += jnp.where(lane <= row, _dot(dmb, vn, NT), 0.0)
            dvn = _dot(wt, dmb, TN)
            dgv_ref[:, lo:lo + GRP] += jnp.sum(dvn * yv, axis=0, keepdims=True)
            dyv = dvn * gv
            dva = rstd_v * (dyv - jnp.mean(dyv, axis=-1, keepdims=True)
                            - yv * jnp.mean(dyv * yv, axis=-1, keepdims=True))
            dp_ref[:, lo:lo + GRP] = (du * du_dup).astype(BF16)
            dp_ref[:, gm + lo:gm + lo + GRP] = (dva * dva_dvp).astype(BF16)

    const2 = lambda n: (0, 0)
    return pl.pallas_call(
        body, name="gmlp_bwd", grid=(s // GRP,),
        out_shape=(jax.ShapeDtypeStruct((s, 2 * gm), BF16), jax.ShapeDtypeStruct((ng, GRP, GRP), F32),
                   jax.ShapeDtypeStruct((GRP, GRP), F32), jax.ShapeDtypeStruct((1, gm), F32),
                   jax.ShapeDtypeStruct((1, gm), F32)),
        in_specs=[pl.BlockSpec((GRP, 2 * gm), lambda n: (n, 0)), pl.BlockSpec((GRP, gm), lambda n: (n, 0)),
                  _vec_spec(gm), pl.BlockSpec((ng, GRP, GRP), lambda n: (0, 0, 0)),
                  pl.BlockSpec((GRP, GRP), const2), _vec_spec(gm), _ANY],
        out_specs=(pl.BlockSpec((GRP, 2 * gm), lambda n: (n, 0)), pl.BlockSpec((ng, GRP, GRP), lambda n: (0, 0, 0)),
                   pl.BlockSpec((GRP, GRP), const2), _vec_spec(gm), _vec_spec(gm)),
        compiler_params=_params(1),
    )(proj, don, v_norm_g, w_spatial, b_cols, gamma, after)


BLK = 256


def _log_sigmoid(z):
    return jnp.minimum(z, 0.0) - jnp.log(1.0 + jnp.exp(-jnp.abs(z)))


def _split_dot(x, tri, passes):
    n = x.shape[0]
    parts, rest = [], x
    for _ in range(passes):
        hi = rest.astype(BF16)
        parts.append(hi)
        rest = rest - hi.astype(F32)
    res = _dot(jnp.concatenate(parts, axis=0), tri, NN)
    out = res[0:n]
    for k in range(1, passes):
        out = out + res[k * n:(k + 1) * n]
    return out


HEADS_PER_STEP = 4


def _sb_fwd(proj, gamma, gm, sb, after):
    s = proj.shape[0]
    hp = min(HEADS_PER_STEP, sb // GRP)
    w = hp * GRP
    nhp, nq = sb // w, s // BLK
    qc, kc, vc, gc = 2 * gm // w, (2 * gm + sb) // w, (2 * gm + 2 * sb) // w, gm // w
    scale = GRP ** -0.5

    def body(q_ref, k_ref, v_ref, gam_ref, after_ref, on_ref, o_ref, a_ref, s_ref, kb, vb):
        i = pl.program_id(1)

        @pl.when(i == 0)
        def _():
            kb[...] = k_ref[...].astype(BF16)
            vb[...] = v_ref[...].astype(BF16)

        qb = q_ref[...].astype(BF16)
        row = lax.broadcasted_iota(jnp.int32, (BLK, BLK), 0)
        col = lax.broadcasted_iota(jnp.int32, (BLK, BLK), 1)
        later = (row > col).astype(BF16)

        def block(j, carry, diag):
            off = pl.multiple_of(j * BLK, BLK)
            kj, vj = kb[pl.ds(off, BLK), :], vb[pl.ds(off, BLK), :]
            out = []
            for h in range(hp):
                tail, acc = carry[h]
                sl = slice(h * GRP, (h + 1) * GRP)
                z = _dot(qb[:, sl], kj[:, sl], NT) * scale
                lb = _log_sigmoid(z)
                l1m = lb - z
                sig = jnp.exp(lb)
                if diag:
                    l1m = jnp.where(col < row, l1m, 0.0)
                    sig = jnp.where(col < row, sig, 0.0)
                after_s = _split_dot(l1m, later, 2)
                a = jnp.exp(lb + after_s + tail)
                if diag:
                    a = jnp.where(col < row, a, 0.0)
                ab = a.astype(BF16)
                a_ref[h, j] = ab
                s_ref[h, j] = sig.astype(BF16)
                out.append((tail + after_s[:, 0:1] + l1m[:, 0:1], acc + _dot(ab, vj[:, sl], NN)))
            return tuple(out)

        init = tuple((jnp.zeros((BLK, 1), F32), jnp.zeros((BLK, GRP), F32)) for _ in range(hp))
        carry = block(i, init, True)
        carry = lax.fori_loop(0, i, lambda jj, c: block(i - 1 - jj, c, False), carry)
        for h in range(hp):
            _, acc = carry[h]
            sl = slice(h * GRP, (h + 1) * GRP)
            rstd = lax.rsqrt(jnp.mean(acc * acc, axis=-1, keepdims=True) + EPS)
            on_ref[:, sl] = (acc * rstd * gam_ref[:, sl]).astype(BF16)
            o_ref[:, sl] = acc

    saved = jax.ShapeDtypeStruct((sb // GRP, nq, nq, BLK, BLK), BF16)
    saved_spec = pl.BlockSpec((hp, None, nq, BLK, BLK), lambda h, i: (h, i, 0, 0, 0))
    return pl.pallas_call(
        body, name="sb_fwd", grid=(nhp, nq),
        out_shape=(jax.ShapeDtypeStruct((s, sb), BF16), jax.ShapeDtypeStruct((s, sb), F32), saved, saved),
        in_specs=[pl.BlockSpec((BLK, w), lambda h, i: (i, qc + h)), pl.BlockSpec((s, w), lambda h, i: (0, kc + h)),
                  pl.BlockSpec((s, w), lambda h, i: (0, vc + h)), pl.BlockSpec((1, w), lambda h, i: (0, gc + h)), _ANY],
        out_specs=(pl.BlockSpec((BLK, w), lambda h, i: (i, h)), pl.BlockSpec((BLK, w), lambda h, i: (i, h)),
                   saved_spec, saved_spec),
        scratch_shapes=[pltpu.VMEM((s, w), BF16), pltpu.VMEM((s, w), BF16)],
        compiler_params=_params(2),
    )(proj, proj, proj, gamma, after)


def _sb_bwd(proj, o_raw, a_sv, s_sv, don, gamma, gm, sb, after):
    s = proj.shape[0]
    hp = min(HEADS_PER_STEP // 2, sb // GRP)
    w = hp * GRP
    nhp, nq = sb // w, s // BLK
    qc, kc, vc, gc = 2 * gm // w, (2 * gm + sb) // w, (2 * gm + 2 * sb) // w, gm // w
    scale = GRP ** -0.5

    def body(q_ref, k_ref, v_ref, o_ref, a_ref, s_ref, don_ref, gam_ref, after_ref, dq_ref, dk_ref, dv_ref, dgam_ref,
             kb, vb, dkt_acc, dvt_acc):
        i = pl.program_id(1)

        @pl.when(i == 0)
        def _():
            kb[...] = k_ref[...].astype(BF16)
            vb[...] = v_ref[...].astype(BF16)
            dkt_acc[...] = jnp.zeros_like(dkt_acc)
            dvt_acc[...] = jnp.zeros_like(dvt_acc)
            dgam_ref[...] = jnp.zeros_like(dgam_ref)

        dobs = []
        for h in range(hp):
            sl = slice(h * GRP, (h + 1) * GRP)
            o, dn = o_ref[:, sl], don_ref[:, sl]
            rstd = lax.rsqrt(jnp.mean(o * o, axis=-1, keepdims=True) + EPS)
            ohat = o * rstd
            dgam_ref[:, sl] += jnp.sum(dn * ohat, axis=0, keepdims=True)
            dohat = dn * gam_ref[:, sl]
            dobs.append((rstd * (dohat - ohat * jnp.mean(dohat * ohat, axis=-1, keepdims=True))).astype(BF16))

        qb = q_ref[...].astype(BF16)
        qts = [qb[:, h * GRP:(h + 1) * GRP].T for h in range(hp)]
        dots = [dob.T for dob in dobs]
        row = lax.broadcasted_iota(jnp.int32, (BLK, BLK), 0)
        col = lax.broadcasted_iota(jnp.int32, (BLK, BLK), 1)
        before = (row < col).astype(BF16)

        def block(j, carry):
            off = pl.multiple_of(j * BLK, BLK)
            kj, vj = kb[pl.ds(off, BLK), :], vb[pl.ds(off, BLK), :]
            out = []
            for h in range(hp):
                e_pre, dq = carry[h]
                sl = slice(h * GRP, (h + 1) * GRP)
                qh, kh, dob = qb[:, sl], kj[:, sl], dobs[h]
                ab = a_ref[h, j]
                sig = s_ref[h, j].astype(F32)
                de = ab.astype(F32) * _dot(dob, vj[:, sl], NT)
                dvt_acc[sl, pl.ds(off, BLK)] += _dot(dots[h], ab, NN)
                before_s = _dot(de.astype(BF16), before, NN)
                dzb = ((de * (1.0 - sig) - (e_pre + before_s) * sig) * scale).astype(BF16)
                dkt_acc[sl, pl.ds(off, BLK)] += _dot(qts[h], dzb, NN)
                out.append((e_pre + before_s[:, BLK - 1:BLK] + de[:, BLK - 1:BLK], dq + _dot(dzb, kh, NN)))
            return tuple(out)

        carry = tuple((jnp.zeros((BLK, 1), F32), jnp.zeros((BLK, GRP), F32)) for _ in range(hp))
        carry = lax.fori_loop(0, i + 1, block, carry)
        for h in range(hp):
            dq_ref[:, h * GRP:(h + 1) * GRP] = carry[h][1].astype(BF16)

        @pl.when(i == nq - 1)
        def _():
            dk_ref[...] = dkt_acc[...].T.astype(BF16)
            dv_ref[...] = dvt_acc[...].T.astype(BF16)

    blk_q = lambda h, i: (i, h)
    whole = lambda h, i: (0, h)
    saved_spec = pl.BlockSpec((hp, None, nq, BLK, BLK), lambda h, i: (h, i, 0, 0, 0))
    return pl.pallas_call(
        body, name="sb_bwd", grid=(nhp, nq),
        out_shape=(jax.ShapeDtypeStruct((s, sb), BF16),) * 3 + (jax.ShapeDtypeStruct((1, sb), F32),),
        in_specs=[pl.BlockSpec((BLK, w), lambda h, i: (i, qc + h)), pl.BlockSpec((s, w), lambda h, i: (0, kc + h)),
                  pl.BlockSpec((s, w), lambda h, i: (0, vc + h)), pl.BlockSpec((BLK, w), blk_q), saved_spec, saved_spec,
                  pl.BlockSpec((BLK, w), lambda h, i: (i, gc + h)),
                  pl.BlockSpec((1, w), lambda h, i: (0, gc + h)), _ANY],
        out_specs=(pl.BlockSpec((BLK, w), blk_q), pl.BlockSpec((s, w), whole), pl.BlockSpec((s, w), whole),
                   pl.BlockSpec((1, w), whole)),
        scratch_shapes=[pltpu.VMEM((s, w), BF16), pltpu.VMEM((s, w), BF16),
                        pltpu.VMEM((w, s), F32), pltpu.VMEM((w, s), F32)],
        compiler_params=_params(2),
    )(proj, proj, proj, o_raw, a_sv, s_sv, don, gamma, after)


FFN_ROW_CHUNKS = 2


def _row_chunks(tm):
    n = FFN_ROW_CHUNKS if tm % (16 * FFN_ROW_CHUNKS) == 0 else 1
    return [slice(k * (tm // n), (k + 1) * (tm // n)) for k in range(n)]


def _ffn_up(h2, wg, wu):
    s, d = h2.shape
    nb, fb, _ = wg.shape
    tm = _tile(s, 1024, 16)

    def body(h_ref, wg_ref, wu_ref, g_ref, u_ref, a_ref):
        for rows_ in _row_chunks(tm):
            hv = h_ref[rows_, :]
            g = _dot(hv, wg_ref[...], NT)
            u = _dot(hv, wu_ref[...], NT)
            g_ref[rows_, :] = g.astype(BF16)
            u_ref[rows_, :] = u.astype(BF16)
            a_ref[rows_, :] = (g * jax.nn.sigmoid(g) * u).astype(BF16)

    rows = pl.BlockSpec((tm, d), lambda i, j: (i, 0))
    wblk = pl.BlockSpec((None, fb, d), lambda i, j: (j, 0, 0))
    hid = pl.BlockSpec((None, tm, fb), lambda i, j: (j, i, 0))
    hid_sds = jax.ShapeDtypeStruct((nb, s, fb), BF16)
    return pl.pallas_call(
        body, name="ffn_up", grid=(s // tm, nb), out_shape=(hid_sds,) * 3, in_specs=[rows, wblk, wblk],
        out_specs=(hid,) * 3, compiler_params=_params(2),
    )(h2, wg, wu)


def _ffn_bwd_act(df, g_pre, u_pre, wd):
    s, d = df.shape
    nb, fb, _ = wd.shape
    tm = _tile(s, 1024, 16)

    def body(df_ref, g_ref, u_ref, wd_ref, dg_ref, du_ref):
        for rows_ in _row_chunks(tm):
            g, u = g_ref[rows_, :].astype(F32), u_ref[rows_, :].astype(F32)
            da = _dot(df_ref[rows_, :], wd_ref[...], NT)
            sg = jax.nn.sigmoid(g)
            dg_ref[rows_, :] = (da * u * (sg * (1.0 + g * (1.0 - sg)))).astype(BF16)
            du_ref[rows_, :] = (da * (g * sg)).astype(BF16)

    rows = pl.BlockSpec((tm, d), lambda i, j: (i, 0))
    wblk = pl.BlockSpec((None, fb, d), lambda i, j: (j, 0, 0))
    hid = pl.BlockSpec((None, tm, fb), lambda i, j: (j, i, 0))
    hid_sds = jax.ShapeDtypeStruct((nb, s, fb), BF16)
    return pl.pallas_call(
        body, name="ffn_bwd_act", grid=(s // tm, nb), out_shape=(hid_sds, hid_sds),
        in_specs=[rows, hid, hid, wblk], out_specs=(hid, hid), compiler_params=_params(2),
    )(df, g_pre, u_pre, wd)


def _ffn_bwd_in(dg, du, wg, wu):
    nb, s, fb = dg.shape
    d = wg.shape[2]
    tm = _tile(s, 1024, 16)

    def body(dg_ref, du_ref, wg_ref, wu_ref, dh_ref):
        j = pl.program_id(1)
        parts = [_dot(dg_ref[rows_, :], wg_ref[...], NN) + _dot(du_ref[rows_, :], wu_ref[...], NN)
                 for rows_ in _row_chunks(tm)]

        @pl.when(j == 0)
        def _():
            for rows_, part in zip(_row_chunks(tm), parts):
                dh_ref[rows_, :] = part

        @pl.when(j > 0)
        def _():
            for rows_, part in zip(_row_chunks(tm), parts):
                dh_ref[rows_, :] += part

    wblk = pl.BlockSpec((None, fb, d), lambda i, j: (j, 0, 0))
    hid = pl.BlockSpec((None, tm, fb), lambda i, j: (j, i, 0))
    return pl.pallas_call(
        body, name="ffn_bwd_in", grid=(s // tm, nb), out_shape=jax.ShapeDtypeStruct((s, d), F32),
        in_specs=[hid, hid, wblk, wblk], out_specs=pl.BlockSpec((tm, d), lambda i, j: (i, 0)),
        compiler_params=_params(2),
    )(dg, du, wg, wu)


def _reduce_adamw(parts, w, m, v, name, after):
    npart, r, c = parts.shape
    tr = _tile(r, max(16, 524288 // c), 16)
    c1, c2 = 1.0 - ADAM_B1 ** ADAM_STEP, 1.0 - ADAM_B2 ** ADAM_STEP

    def body(p_ref, w_ref, m_ref, v_ref, after_ref, g_ref, d_ref, nm_ref, nv_ref):
        g = p_ref[0].astype(F32)
        for k in range(1, npart):
            g = g + p_ref[k].astype(F32)
        nm = ADAM_B1 * m_ref[...] + (1.0 - ADAM_B1) * g
        nv = ADAM_B2 * v_ref[...] + (1.0 - ADAM_B2) * (g * g)
        g_ref[...] = g
        nm_ref[...] = nm
        nv_ref[...] = nv
        d_ref[...] = -ADAM_LR * ((nm / c1) / (jnp.sqrt(nv / c2) + ADAM_EPS) + ADAM_WD * w_ref[...])

    blk = pl.BlockSpec((tr, c), lambda i: (i, 0))
    sds = jax.ShapeDtypeStruct((r, c), F32)
    return pl.pallas_call(
        body, name=name, grid=(r // tr,), out_shape=(sds,) * 4,
        in_specs=[pl.BlockSpec((npart, tr, c), lambda i: (0, i, 0)), blk, blk, blk, _ANY], out_specs=(blk,) * 4,
        compiler_params=_params(1),
    )(parts, w, m, v, after)


def _pack(vecs):
    rows = jnp.concatenate([a.reshape(-1, GRP) for a in vecs], axis=0)
    pad = -rows.shape[0] % 64
    return jnp.pad(rows, ((0, pad), (0, 0)))


def _unpack(rows, shapes):
    out, at = [], 0
    for shp in shapes:
        n = 1
        for k in shp:
            n *= k
        out.append(rows[at:at + n // GRP].reshape(shp))
        at += n // GRP
    return out


def kernel(x, c, w_ada, b_ada, norm1_g, w_in, v_norm_g, w_spatial, b_spatial, out_norm_g, w_out, norm2_g, w_gate, w_up, w_down, final_g, loss_target, m_w_ada, m_b_ada, m_norm1_g, m_w_in, m_v_norm_g, m_w_spatial, m_b_spatial, m_out_norm_g, m_w_out, m_norm2_g, m_w_gate, m_w_up, m_w_down, m_final_g, v_w_ada, v_b_ada, v_norm1_g, v_w_in, v_v_norm_g, v_w_spatial, v_b_spatial, v_out_norm_g, v_w_out, v_norm2_g, v_w_gate, v_w_up, v_w_down, v_final_g):
    s, d = x.shape[1], x.shape[2]
    gm = v_norm_g.shape[1]
    sb = d - gm
    n_in, ffb, ob = w_in.shape[2], w_gate.shape[2], w_out.shape[1]
    xs, tgt = x[0], loss_target[0]
    me = 4 * lax.axis_index("x") + 2 * lax.axis_index("y") + lax.axis_index("c")

    c_all = _exchange(c, False, "gather_c")[:, 0, :]
    mod_cols = _ada_fwd(c_all, w_ada[0])
    mod = _exchange(mod_cols[:, None, :], True, "scatter_mod").reshape(1, 6 * d) + b_ada
    shift1, scale1, gate1, shift2, scale2, gate2 = [mod[:, k * d:(k + 1) * d] for k in range(6)]

    tok, gathers = mod, {}
    tr_ = lambda a: jnp.swapaxes(a, 1, 2)
    w_gate, m_w_gate, v_w_gate, w_up, m_w_up, v_w_up = map(tr_, (w_gate, m_w_gate, v_w_gate, w_up, m_w_up, v_w_up))
    big_w = (("w_in", w_in), ("w_out", w_out), ("w_gate", w_gate), ("w_up", w_up), ("w_down", w_down))
    first, tok = _gather_start([w_in[0].astype(BF16)], tok, "gather_start_w_in")
    rest, tok = _gather_start([w[0].astype(BF16) for _, w in big_w[1:]], tok, "gather_start_weights")
    gathers = {nm: h for (nm, _), h in zip(big_w, first + rest)}
    shift1 = shift1 + tok[0:1, 0:1]

    tm = _tile(s, 512, 16)
    nt = s // tm
    h1 = _prenorm_fwd(xs, norm1_g, scale1, shift1, "prenorm1")
    w_in_all = _gather_wait(_gather_forward(gathers["w_in"], h1, "gather_fwd_w_in")[0], h1, "gather_wait_w_in")
    proj = _mm(h1, w_in_all, pl.BlockSpec((s, d), lambda j: (0, 0)), pl.BlockSpec((None, d, n_in), lambda j: (j, 0, 0)),
               jax.ShapeDtypeStruct((s, NDEV * n_in), F32), pl.BlockSpec((s, n_in), lambda j: (0, j)),
               (NDEV,), NN, False, "proj")
    fwd_w_out, tok = _gather_forward(gathers["w_out"], proj, "gather_fwd_w_out")
    b_cols = jnp.pad(b_spatial[0].T, ((0, 0), (0, GRP - b_spatial.shape[1])))
    on_sb, o_sb, a_sv, s_sv = _sb_fwd(proj, out_norm_g, gm, sb, tok)
    fwd_w_gate, tok = _gather_forward(gathers["w_gate"], on_sb, "gather_fwd_w_gate")
    on_gm = _gmlp_fwd(proj, v_norm_g, w_spatial[0], b_cols, out_norm_g, gm, tok)
    fwd_w_up, tok = _gather_forward(gathers["w_up"], on_gm, "gather_fwd_w_up")
    o_n = jnp.concatenate([on_gm, on_sb], axis=1)
    w_out_all = _gather_wait(fwd_w_out, tok, "gather_wait_w_out").reshape(d, d)
    rows_1 = pl.BlockSpec((tm, d), lambda i: (i, 0))
    whole_1 = pl.BlockSpec((d, d), lambda i: (0, 0))
    p_out = _mm(o_n, w_out_all, rows_1, whole_1, jax.ShapeDtypeStruct((s, d), F32), rows_1, (nt,), NN, False,
                "out_proj")
    x1, h2 = _residual_prenorm(xs, gate1, p_out, norm2_g, scale2, shift2, "residual1_prenorm2")
    w_gate_all = _gather_wait(fwd_w_gate, h2, "gather_wait_w_gate")
    w_up_all = _gather_wait(fwd_w_up, h2, "gather_wait_w_up")
    g_pre, u_pre, act = _ffn_up(h2, w_gate_all, w_up_all)
    fwd_w_down, tok = _gather_forward(gathers["w_down"], act, "gather_fwd_w_down")
    w_down_all = _gather_wait(fwd_w_down, tok, "gather_wait_w_down")
    tm2 = _tile(s, 1024, 16)
    f_out = _mm(act, w_down_all, pl.BlockSpec((2, tm2, ffb), lambda i, j: (j, i, 0)),
                pl.BlockSpec((2, ffb, d), lambda i, j: (j, 0, 0)), jax.ShapeDtypeStruct((s, d), F32),
                pl.BlockSpec((tm2, d), lambda i, j: (i, 0)), (s // tm2, NDEV // 2), NN, True, "ffn_down")

    dx2, df, st_f = _final_loss(x1, f_out, final_g.reshape(1, d), tgt, gate2)
    dg, du = _ffn_bwd_act(df, g_pre, u_pre, w_down_all)
    dh2 = _ffn_bwd_in(dg, du, w_gate_all, w_up_all)
    hid = pl.BlockSpec((None, s, ffb), lambda j: (j, 0, 0))
    all_d = pl.BlockSpec((s, d), lambda j: (0, 0))

    def second_leg(first, after, tag):
        parts, lands = _scatter_wait1(first, after, "scatter_wait1_" + tag)
        sums = [_chip_sum(p, l, "chip_sum_%s_%d" % (tag, k)) for k, (p, l) in enumerate(zip(parts, lands))]
        return _scatter_start2(sums, after, "scatter_start2_" + tag)

    row_sds = jax.ShapeDtypeStruct((NDEV, ffb, d), BF16)
    row_out = pl.BlockSpec((None, ffb, d), lambda j: (j, 0, 0))
    gw_gate = _mm(dg, h2, hid, all_d, row_sds, row_out, (NDEV,), TN, False, "grad_w_gate")
    gw_up = _mm(du, h2, hid, all_d, row_sds, row_out, (NDEV,), TN, False, "grad_w_up")
    gw_down = _mm(act, df, hid, all_d, row_sds, row_out, (NDEV,), TN, False, "grad_w_down")
    dx1, dp, st2 = _prenorm_bwd(x1, dh2, dx2, f_out, norm2_g, scale2, gate1, "prenorm2_bwd", gw_down)
    first_ffn, tok = _scatter_start1([gw_gate, gw_up, gw_down], dp, "scatter_start1_ffn")
    don = _mm(dp, w_out_all, rows_1, whole_1, jax.ShapeDtypeStruct((s, d), F32), rows_1, (nt,), NT, False,
              "out_proj_bwd", tok)
    gw_out = _mm(o_n, dp, pl.BlockSpec((s, ob), lambda j: (0, j)), all_d,
                 jax.ShapeDtypeStruct((NDEV, ob, d), BF16), pl.BlockSpec((None, ob, d), lambda j: (j, 0, 0)),
                 (NDEV,), TN, False, "grad_w_out", don)
    first_out, tok = _scatter_start1([gw_out], tok, "scatter_start1_out")
    second_ffn, tok = second_leg(first_ffn, tok, "ffn")
    dproj_gm, dw_sp, db_cols, dgv, dgam_gm = _gmlp_bwd(proj, don, v_norm_g, w_spatial[0], b_cols, out_norm_g, gm, tok)
    second_out, tok = second_leg(first_out, dproj_gm, "out")
    dq, dk, dv, dgam_sb = _sb_bwd(proj, o_sb, a_sv, s_sv, don, out_norm_g, gm, sb, tok)
    dproj = jnp.concatenate([dproj_gm, dq, dk, dv], axis=1)
    gw_in = _mm(h1, dproj, all_d, pl.BlockSpec((s, n_in), lambda j: (0, j)),
                jax.ShapeDtypeStruct((NDEV, d, n_in), BF16), pl.BlockSpec((None, d, n_in), lambda j: (j, 0, 0)),
                (NDEV,), TN, False, "grad_w_in")
    first_in, tok = _scatter_start1([gw_in], tok, "scatter_start1_in")
    dh1 = _mm(dproj, w_in_all, pl.BlockSpec((tm2, 2 * n_in), lambda i, j: (i, j)),
              pl.BlockSpec((2, d, n_in), lambda i, j: (j, 0, 0)), jax.ShapeDtypeStruct((s, d), F32),
              pl.BlockSpec((tm2, d), lambda i, j: (i, 0)), (s // tm2, NDEV // 2), NT, True, "in_proj_bwd", tok)
    grad_x, _, st1 = _prenorm_bwd(xs, dh1, dx1, p_out, norm1_g, scale1, gate1, "prenorm1_bwd", tok)

    dmod = jnp.concatenate([st1[0], st1[1], st1[3], st2[0], st2[1], st2[3]])
    loss_row = jnp.pad((0.5 * jnp.sum(st_f[1]) / d).reshape(1, 1), ((0, 0), (0, GRP - 1)))
    small = [st1[2], dgv, dw_sp, db_cols[:, :b_spatial.shape[1]].T, jnp.concatenate([dgam_gm, dgam_sb], axis=1),
             st2[2], st_f[0], dmod, loss_row]
    zero_row = jnp.zeros((1, GRP), F32)
    small_w = [norm1_g, v_norm_g, w_spatial, b_spatial, out_norm_g, norm2_g, final_g, b_ada, zero_row]
    small_m = [m_norm1_g, m_v_norm_g, m_w_spatial, m_b_spatial, m_out_norm_g, m_norm2_g, m_final_g, m_b_ada, zero_row]
    small_v = [v_norm1_g, v_v_norm_g, v_w_spatial, v_b_spatial, v_out_norm_g, v_norm2_g, v_final_g, v_b_ada, zero_row]
    (small_first,), tok = _gather_start([_pack(small)], grad_x, "gather_small_start")
    second_in, tok = second_leg(first_in, tok, "in")

    big = {}
    prev = tok
    for second, group in ((second_ffn, (("w_gate", w_gate, m_w_gate, v_w_gate), ("w_up", w_up, m_w_up, v_w_up),
                                        ("w_down", w_down, m_w_down, v_w_down))),
                          (second_out, (("w_out", w_out, m_w_out, v_w_out),))):
        chip_sums = _scatter_wait2(second, prev, "scatter_wait2_" + group[0][0])
        for (nm, w, m, v), part in zip(group, chip_sums):
            big[nm] = _reduce_adamw(part, w[0], m[0], v[0], "adamw_" + nm, tok)
            prev = big[nm][1]

    small_second, tok2 = _gather_forward(small_first, prev, "gather_small_fwd")
    small_all = _gather_wait(small_second, tok2, "gather_small_wait")
    sm = _reduce_adamw(small_all, _pack(small_w), _pack(small_m), _pack(small_v), "adamw_small", tok)
    shapes = [a.shape for a in small_w]
    sm_g, sm_d, sm_m, sm_v = [_unpack(t, shapes) for t in sm]
    loss = sm_g[8][0, 0]

    at = sum(a.size for a in small_w[:7]) // GRP
    dmod_all = small_all[:, at:at + 6 * d // GRP, :].reshape(NDEV, 6 * d)
    n_ada = w_ada.shape[2]
    dmod_cols = lax.dynamic_slice(dmod_all, (0, me * n_ada), (NDEV, n_ada))
    g_ada = _ada_bwd(c_all, dmod_cols)
    big["w_ada"] = _reduce_adamw(g_ada[None], w_ada[0], m_w_ada[0], v_w_ada[0], "adamw_w_ada", sm[0])
    part_in, = _scatter_wait2(second_in, big["w_ada"][1], "scatter_wait2_w_in")
    big["w_in"] = _reduce_adamw(part_in, w_in[0], m_w_in[0], v_w_in[0], "adamw_w_in", tok)

    names = ["w_ada", "b_ada", "norm1_g", "w_in", "v_norm_g", "w_spatial", "b_spatial", "out_norm_g", "w_out",
             "norm2_g", "w_gate", "w_up", "w_down", "final_g"]
    small_at = {"norm1_g": 0, "v_norm_g": 1, "w_spatial": 2, "b_spatial": 3, "out_norm_g": 4, "norm2_g": 5,
                "final_g": 6, "b_ada": 7}
    outs = [[], [], [], []]
    for nm in names:
        for k in range(4):
            if nm in big:
                res = big[nm][k][None]
                outs[k].append(tr_(res) if nm in ("w_gate", "w_up") else res)
            else:
                outs[k].append((sm_g, sm_d, sm_m, sm_v)[k][small_at[nm]])
    return (loss, grad_x[None], *outs[0], *outs[1], *outs[2], *outs[3])
```

```python
import jax
import jax.numpy as jnp
from jax import lax
from jax.experimental import pallas as pl
from jax.experimental.pallas import tpu as pltpu

F32, BF16 = jnp.float32, jnp.bfloat16
NDEV = 8
GRP = 128
EPS = 1e-6
VMEM_BYTES = 64 * 2 ** 20
VMEM_LIMIT = VMEM_BYTES - 8 * 2 ** 20
ADAM_LR, ADAM_B1, ADAM_B2, ADAM_EPS, ADAM_WD, ADAM_STEP = 0.001, 0.9, 0.999, 1e-08, 0.01, 10
MESH = pl.DeviceIdType.MESH
NN = (((1,), (0,)), ((), ()))
NT = (((1,), (1,)), ((), ()))
TN = (((0,), (0,)), ((), ()))


def _params(n_axes):
    return pltpu.CompilerParams(dimension_semantics=("arbitrary",) * n_axes, vmem_limit_bytes=VMEM_LIMIT)


def _tile(n, cap, mult):
    best = None
    for t in range(mult, min(n, cap) + 1, mult):
        if n % t == 0:
            best = t
    assert best is not None, (n, cap, mult)
    return best


def _dot(a, b, dims):
    return lax.dot_general(a, b, dims, preferred_element_type=F32)


def _exchange(src, per_peer, name):
    blk = src.shape[1:] if per_peer else src.shape

    def body(src_ref, out_ref, send_sems, recv_sems, local_sem):
        x, y, c = lax.axis_index("x"), lax.axis_index("y"), lax.axis_index("c")
        me = 4 * x + 2 * y + c
        local = pltpu.make_async_copy(src_ref.at[me] if per_peer else src_ref, out_ref.at[me], local_sem)
        local.start()
        sends, recvs = [], []
        for k in range(1, NDEV):
            px, py, pc = x ^ (k >> 2), y ^ ((k >> 1) & 1), c ^ (k & 1)
            p = 4 * px + 2 * py + pc
            mine = src_ref.at[p] if per_peer else src_ref
            sends.append(pltpu.make_async_remote_copy(
                src_ref=mine, dst_ref=out_ref.at[me], send_sem=send_sems.at[k - 1], recv_sem=recv_sems.at[k - 1],
                device_id=(px, py, pc), device_id_type=MESH))
            recvs.append(pltpu.make_async_remote_copy(
                src_ref=mine, dst_ref=out_ref.at[p], send_sem=send_sems.at[k - 1], recv_sem=recv_sems.at[k - 1],
                device_id=(px, py, pc), device_id_type=MESH))
        for cp in sends:
            cp.start()
        for cp in recvs:
            cp.wait_recv()
        for cp in sends:
            cp.wait_send()
        local.wait()

    return pl.pallas_call(
        body, name=name,
        out_shape=jax.ShapeDtypeStruct((NDEV,) + tuple(blk), src.dtype),
        in_specs=[pl.BlockSpec(memory_space=pl.ANY)],
        out_specs=pl.BlockSpec(memory_space=pl.ANY),
        scratch_shapes=[pltpu.SemaphoreType.DMA((NDEV - 1,)), pltpu.SemaphoreType.DMA((NDEV - 1,)),
                        pltpu.SemaphoreType.DMA],
    )(src)


_HBM = pl.BlockSpec(memory_space=pltpu.HBM)
_SEM = pl.BlockSpec(memory_space=pltpu.SEMAPHORE)
_ANY = pl.BlockSpec(memory_space=pl.ANY)
_EFFECT = pltpu.SideEffectType.DATAFLOW_SIDE_EFFECTING


def _chip_peers():
    x, y, c = lax.axis_index("x"), lax.axis_index("y"), lax.axis_index("c")
    chips = [(x, 1 - y), (1 - x, y), (1 - x, 1 - y)]
    return 4 * x + 2 * y + c, (x, y, 1 - c), [((px, py, c), 4 * px + 2 * py + c) for px, py in chips]


def _gather_start(srcs, after, name):
    n = len(srcs)
    me = 4 * lax.axis_index("x") + 2 * lax.axis_index("y") + lax.axis_index("c")
    lands = [lax.dynamic_update_slice(lax.empty((NDEV,) + a.shape, a.dtype), a[None], (me,) + (0,) * a.ndim) for a in srcs]

    def body(*refs):
        src_refs, land_refs = refs[:n], refs[n:2 * n]
        sems, token = refs[2 * n + 1:2 * n + 1 + 2 * n], refs[-1]
        my, sibling, chips = _chip_peers()
        for a in range(n):
            for k, to in enumerate([sibling] + [coords for coords, _ in chips]):
                pltpu.make_async_remote_copy(src_ref=src_refs[a], dst_ref=land_refs[a].at[my], send_sem=sems[2 * a].at[k],
                                             recv_sem=sems[2 * a + 1].at[k], device_id=to, device_id_type=MESH).start()
        token[...] = jnp.zeros_like(token)

    hbm = lambda arrs: tuple(pltpu.HBM(a.shape, a.dtype) for a in arrs)
    res = pl.pallas_call(
        body, name=name,
        out_shape=(pltpu.SemaphoreType.DMA((4,)),) * (2 * n) + hbm(srcs) + hbm(lands) + (jax.ShapeDtypeStruct((8, GRP), F32),),
        in_specs=(_HBM,) * (2 * n) + (_ANY,),
        out_specs=(_SEM,) * (2 * n) + (_HBM,) * (2 * n) + (pl.BlockSpec(memory_space=pltpu.VMEM),),
        input_output_aliases={k: 2 * n + k for k in range(2 * n)}, compiler_params=pltpu.CompilerParams(has_side_effects=_EFFECT),
    )(*[pltpu.with_memory_space_constraint(a, pltpu.HBM) for a in list(srcs) + lands], after)
    return [(res[2 * a], res[2 * a + 1], res[2 * n + a], res[3 * n + a]) for a in range(n)], res[-1]


def _gather_forward(handles, after, name):
    send_sems, recv_sems, src_thru, land_thru = handles

    def body(src_ref, land_ref, send_sems, recv_sems, after_ref, send2, recv2, land_out, token):
        my, sibling, chips = _chip_peers()
        for k, (to, p) in enumerate([(sibling, my ^ 1)] + chips):
            first = pltpu.make_async_remote_copy(src_ref=src_ref, dst_ref=land_ref.at[p], send_sem=send_sems.at[k],
                                                 recv_sem=recv_sems.at[k], device_id=to, device_id_type=MESH)
            first.wait_send()
            first.wait_recv()
        for k, (_, p) in enumerate(chips):
            pltpu.make_async_remote_copy(src_ref=land_ref.at[p], dst_ref=land_ref.at[p], send_sem=send2.at[k],
                                         recv_sem=recv2.at[k], device_id=sibling, device_id_type=MESH).start()
        token[...] = jnp.zeros_like(token)

    res = pl.pallas_call(
        body, name=name,
        out_shape=(pltpu.SemaphoreType.DMA((3,)), pltpu.SemaphoreType.DMA((3,)), pltpu.HBM(land_thru.shape, land_thru.dtype),
                   jax.ShapeDtypeStruct((8, GRP), F32)),
        in_specs=(_HBM, _HBM, _SEM, _SEM, _ANY), out_specs=(_SEM, _SEM, _HBM, pl.BlockSpec(memory_space=pltpu.VMEM)),
        input_output_aliases={1: 2}, compiler_params=pltpu.CompilerParams(has_side_effects=_EFFECT),
    )(src_thru, land_thru, send_sems, recv_sems, after)
    return tuple(res[:3]), res[3]


def _gather_wait(handles, after, name):
    send2, recv2, land_thru = handles

    def body(land_ref, send2, recv2, after_ref, got_ref):
        _, sibling, chips = _chip_peers()
        for k, (_, p) in enumerate(chips):
            cp = pltpu.make_async_remote_copy(src_ref=land_ref.at[p], dst_ref=land_ref.at[p ^ 1], send_sem=send2.at[k],
                                              recv_sem=recv2.at[k], device_id=sibling, device_id_type=MESH)
            cp.wait_send()
            cp.wait_recv()

    return pl.pallas_call(
        body, name=name, out_shape=pltpu.HBM(land_thru.shape, land_thru.dtype),
        in_specs=(_HBM, _SEM, _SEM, _ANY), out_specs=_HBM, input_output_aliases={0: 0},
        compiler_params=pltpu.CompilerParams(has_side_effects=_EFFECT),
    )(land_thru, send2, recv2, after)


NCHIP = NDEV // 2


def _scatter_start1(parts, after, name):
    n = len(parts)
    lands = [lax.empty((NCHIP,) + p.shape[1:], p.dtype) for p in parts]

    def body(*refs):
        part_refs, land_refs = refs[:n], refs[n:2 * n]
        send_sems, recv_sems, token = refs[2 * n + 1], refs[2 * n + 2], refs[-1]
        _, sibling, _ = _chip_peers()
        c = lax.axis_index("c")
        for a in range(n):
            for q in range(NCHIP):
                pltpu.make_async_remote_copy(
                    src_ref=part_refs[a].at[2 * q + 1 - c], dst_ref=land_refs[a].at[q], send_sem=send_sems.at[a * NCHIP + q],
                    recv_sem=recv_sems.at[a * NCHIP + q], device_id=sibling, device_id_type=MESH).start()
        token[...] = jnp.zeros_like(token)

    hbm = lambda arrs: tuple(pltpu.HBM(a.shape, a.dtype) for a in arrs)
    res = pl.pallas_call(
        body, name=name,
        out_shape=(pltpu.SemaphoreType.DMA((n * NCHIP,)), pltpu.SemaphoreType.DMA((n * NCHIP,))) + hbm(parts) + hbm(lands)
        + (jax.ShapeDtypeStruct((8, GRP), F32),),
        in_specs=(_HBM,) * (2 * n) + (_ANY,),
        out_specs=(_SEM, _SEM) + (_HBM,) * (2 * n) + (pl.BlockSpec(memory_space=pltpu.VMEM),),
        input_output_aliases={k: 2 + k for k in range(2 * n)}, compiler_params=pltpu.CompilerParams(has_side_effects=_EFFECT),
    )(*[pltpu.with_memory_space_constraint(a, pltpu.HBM) for a in list(parts) + lands], after)
    return (n,) + tuple(res[:-1]), res[-1]


def _scatter_wait1(handles, after, name):
    n, send_sems, recv_sems = handles[:3]
    thru = handles[3:]

    def body(*refs):
        part_refs, land_refs, send_sems, recv_sems = refs[:n], refs[n:2 * n], refs[2 * n], refs[2 * n + 1]
        _, sibling, _ = _chip_peers()
        for a in range(n):
            for q in range(NCHIP):
                cp = pltpu.make_async_remote_copy(
                    src_ref=part_refs[a].at[q], dst_ref=land_refs[a].at[q], send_sem=send_sems.at[a * NCHIP + q],
                    recv_sem=recv_sems.at[a * NCHIP + q], device_id=sibling, device_id_type=MESH)
                cp.wait_send()
                cp.wait_recv()

    res = pl.pallas_call(
        body, name=name, out_shape=tuple(pltpu.HBM(a.shape, a.dtype) for a in thru),
        in_specs=(_HBM,) * (2 * n) + (_SEM, _SEM, _ANY), out_specs=(_HBM,) * (2 * n),
        input_output_aliases={k: k for k in range(2 * n)}, compiler_params=pltpu.CompilerParams(has_side_effects=_EFFECT),
    )(*thru, send_sems, recv_sems, after)
    return res[:n], res[n:]


def _chip_sum(part, land, name):
    _, r, c = part.shape
    tr = _tile(r, max(16, 2 ** 21 // c), 16)

    def body(core_ref, p_ref, l_ref, o_ref):
        o_ref[...] = (p_ref[...].astype(F32) + l_ref[...].astype(F32)).astype(o_ref.dtype)

    blk = pl.BlockSpec((None, tr, c), lambda q, i, core: (q, i, 0))
    return pl.pallas_call(
        body, name=name, out_shape=jax.ShapeDtypeStruct(land.shape, land.dtype),
        grid_spec=pltpu.PrefetchScalarGridSpec(
            num_scalar_prefetch=1, grid=(NCHIP, r // tr),
            in_specs=[pl.BlockSpec((None, None, tr, c), lambda q, i, core: (q, core[0], i, 0)), blk], out_specs=blk),
        compiler_params=_params(2),
    )(lax.axis_index("c").astype(jnp.int32).reshape(1), part.reshape(NCHIP, 2, r, c), land)


def _scatter_start2(sums, after, name):
    n = len(sums)
    chip = 2 * lax.axis_index("x") + lax.axis_index("y")
    lands = [lax.dynamic_update_slice(lax.empty(s_.shape, s_.dtype), lax.dynamic_index_in_dim(s_, chip, 0, keepdims=True),
                                      (chip,) + (0,) * (s_.ndim - 1)) for s_ in sums]

    def body(*refs):
        sum_refs, land_refs = refs[:n], refs[n:2 * n]
        send_sems, recv_sems, token = refs[2 * n + 1], refs[2 * n + 2], refs[-1]
        my, _, chips = _chip_peers()
        for a in range(n):
            for k, (to, p) in enumerate(chips):
                pltpu.make_async_remote_copy(
                    src_ref=sum_refs[a].at[p // 2], dst_ref=land_refs[a].at[my // 2], send_sem=send_sems.at[a * 3 + k],
                    recv_sem=recv_sems.at[a * 3 + k], device_id=to, device_id_type=MESH).start()
        token[...] = jnp.zeros_like(token)

    hbm = lambda arrs: tuple(pltpu.HBM(a.shape, a.dtype) for a in arrs)
    res = pl.pallas_call(
        body, name=name,
        out_shape=(pltpu.SemaphoreType.DMA((n * 3,)), pltpu.SemaphoreType.DMA((n * 3,))) + hbm(sums) + hbm(lands)
        + (jax.ShapeDtypeStruct((8, GRP), F32),),
        in_specs=(_HBM,) * (2 * n) + (_ANY,),
        out_specs=(_SEM, _SEM) + (_HBM,) * (2 * n) + (pl.BlockSpec(memory_space=pltpu.VMEM),),
        input_output_aliases={k: 2 + k for k in range(2 * n)}, compiler_params=pltpu.CompilerParams(has_side_effects=_EFFECT),
    )(*[pltpu.with_memory_space_constraint(a, pltpu.HBM) for a in list(sums) + lands], after)
    return (n,) + tuple(res[:-1]), res[-1]


def _scatter_wait2(handles, after, name):
    n, send_sems, recv_sems = handles[:3]
    thru = handles[3:]

    def body(*refs):
        sum_refs, land_refs, send_sems, recv_sems = refs[:n], refs[n:2 * n], refs[2 * n], refs[2 * n + 1]
        _, _, chips = _chip_peers()
        for a in range(n):
            for k, (to, p) in enumerate(chips):
                cp = pltpu.make_async_remote_copy(
                    src_ref=sum_refs[a].at[p // 2], dst_ref=land_refs[a].at[p // 2], send_sem=send_sems.at[a * 3 + k],
                    recv_sem=recv_sems.at[a * 3 + k], device_id=to, device_id_type=MESH)
                cp.wait_send()
                cp.wait_recv()

    res = pl.pallas_call(
        body, name=name, out_shape=tuple(pltpu.HBM(a.shape, a.dtype) for a in thru),
        in_specs=(_HBM,) * (2 * n) + (_SEM, _SEM, _ANY), out_specs=(_HBM,) * (2 * n),
        input_output_aliases={k: k for k in range(2 * n)}, compiler_params=pltpu.CompilerParams(has_side_effects=_EFFECT),
    )(*thru, send_sems, recv_sems, after)
    return res[n:]


def _mm(a, b, a_spec, b_spec, out_sds, o_spec, grid, dims, acc, name, after=None):
    extra = () if after is None else (after,)
    assert not acc or out_sds.dtype == F32

    def body(a_ref, b_ref, *rest):
        o_ref = rest[len(extra)]
        if len(b_ref.shape) == 3:
            nkb = b_ref.shape[0]
            wk = a_ref.shape[-1] // nkb
            prod = sum(_dot(a_ref[k] if len(a_ref.shape) == 3 else a_ref[:, k * wk:(k + 1) * wk], b_ref[k], dims)
                       for k in range(nkb))
        else:
            prod = _dot(a_ref[...], b_ref[...], dims)
        if not acc:
            o_ref[...] = prod.astype(o_ref.dtype)
            return
        k = pl.program_id(len(grid) - 1)

        @pl.when(k == 0)
        def _():
            o_ref[...] = prod

        @pl.when(k > 0)
        def _():
            o_ref[...] += prod

    return pl.pallas_call(
        body, name=name, grid=grid, out_shape=out_sds, in_specs=[a_spec, b_spec] + [_ANY] * len(extra), out_specs=o_spec,
        compiler_params=_params(len(grid)),
    )(a, b, *extra)


def _row_spec(tm, d):
    return pl.BlockSpec((tm, d), lambda i: (i, 0))


def _vec_spec(d):
    return pl.BlockSpec((1, d), lambda i: (0, 0))


def _prenorm_fwd(x, g, scale, shift, name):
    s, d = x.shape
    tm = _tile(s, 256, 16)

    def body(x_ref, g_ref, sc_ref, sh_ref, h_ref):
        xv = x_ref[...]
        rstd = lax.rsqrt(jnp.mean(xv * xv, axis=-1, keepdims=True) + EPS)
        h_ref[...] = ((xv * rstd * g_ref[...]) * (1.0 + sc_ref[...]) + sh_ref[...]).astype(BF16)

    return pl.pallas_call(
        body, name=name, grid=(s // tm,), out_shape=jax.ShapeDtypeStruct((s, d), BF16),
        in_specs=[_row_spec(tm, d), _vec_spec(d), _vec_spec(d), _vec_spec(d)], out_specs=_row_spec(tm, d),
        compiler_params=_params(1),
    )(x, g, scale, shift)


def _residual_prenorm(x, gate, p, g, scale, shift, name):
    s, d = x.shape
    tm = _tile(s, 256, 16)

    def body(x_ref, gate_ref, p_ref, g_ref, sc_ref, sh_ref, x1_ref, h_ref):
        xv = x_ref[...] + gate_ref[...] * p_ref[...]
        x1_ref[...] = xv
        rstd = lax.rsqrt(jnp.mean(xv * xv, axis=-1, keepdims=True) + EPS)
        h_ref[...] = ((xv * rstd * g_ref[...]) * (1.0 + sc_ref[...]) + sh_ref[...]).astype(BF16)

    return pl.pallas_call(
        body, name=name, grid=(s // tm,),
        out_shape=(jax.ShapeDtypeStruct((s, d), F32), jax.ShapeDtypeStruct((s, d), BF16)),
        in_specs=[_row_spec(tm, d), _vec_spec(d), _row_spec(tm, d), _vec_spec(d), _vec_spec(d), _vec_spec(d)],
        out_specs=(_row_spec(tm, d), _row_spec(tm, d)), compiler_params=_params(1),
    )(x, gate, p, g, scale, shift)


def _final_loss(x1, f, final_g, target, gate2):
    s, d = x1.shape
    tm = _tile(s, 256, 16)

    def body(x_ref, f_ref, g_ref, t_ref, gate_ref, dx_ref, df_ref, st_ref):
        @pl.when(pl.program_id(0) == 0)
        def _():
            st_ref[...] = jnp.zeros_like(st_ref)

        xv, gf = x_ref[...] + gate_ref[...] * f_ref[...], g_ref[...]
        rstd = lax.rsqrt(jnp.mean(xv * xv, axis=-1, keepdims=True) + EPS)
        xhat = xv * rstd
        err = xhat * gf - t_ref[...]
        dy = err * (1.0 / d)
        gdy = dy * gf
        dx = rstd * (gdy - xhat * jnp.mean(gdy * xhat, axis=-1, keepdims=True))
        dx_ref[...] = dx
        df_ref[...] = (gate_ref[...] * dx).astype(BF16)
        st_ref[0:1, :] += jnp.sum(dy * xhat, axis=0, keepdims=True)
        st_ref[1:2, :] += jnp.sum(err * err, axis=0, keepdims=True)

    return pl.pallas_call(
        body, name="final_loss", grid=(s // tm,),
        out_shape=(jax.ShapeDtypeStruct((s, d), F32), jax.ShapeDtypeStruct((s, d), BF16),
                   jax.ShapeDtypeStruct((8, d), F32)),
        in_specs=[_row_spec(tm, d), _row_spec(tm, d), _vec_spec(d), _row_spec(tm, d), _vec_spec(d)],
        out_specs=(_row_spec(tm, d), _row_spec(tm, d), pl.BlockSpec((8, d), lambda i: (0, 0))),
        compiler_params=_params(1),
    )(x1, f, final_g, target, gate2)


def _prenorm_bwd(xin, dh, dres, pf, g, scale, gate_next, name, after):
    s, d = xin.shape
    tm = _tile(s, 256, 16)

    def body(x_ref, dh_ref, dr_ref, pf_ref, g_ref, sc_ref, gn_ref, after_ref, dx_ref, dn_ref, st_ref):
        @pl.when(pl.program_id(0) == 0)
        def _():
            st_ref[...] = jnp.zeros_like(st_ref)

        xv, dhv, drv, gv = x_ref[...], dh_ref[...], dr_ref[...], g_ref[...]
        one_sc = 1.0 + sc_ref[...]
        rstd = lax.rsqrt(jnp.mean(xv * xv, axis=-1, keepdims=True) + EPS)
        xhat = xv * rstd
        dxhat = dhv * (gv * one_sc)
        dx = drv + rstd * (dxhat - xhat * jnp.mean(dxhat * xhat, axis=-1, keepdims=True))
        dx_ref[...] = dx
        dn_ref[...] = (gn_ref[...] * dx).astype(BF16)
        dhx = dhv * xhat
        st_ref[0:1, :] += jnp.sum(dhv, axis=0, keepdims=True)
        st_ref[1:2, :] += jnp.sum(dhx, axis=0, keepdims=True) * gv
        st_ref[2:3, :] += jnp.sum(dhx, axis=0, keepdims=True) * one_sc
        st_ref[3:4, :] += jnp.sum(drv * pf_ref[...], axis=0, keepdims=True)

    return pl.pallas_call(
        body, name=name, grid=(s // tm,),
        out_shape=(jax.ShapeDtypeStruct((s, d), F32), jax.ShapeDtypeStruct((s, d), BF16),
                   jax.ShapeDtypeStruct((8, d), F32)),
        in_specs=[_row_spec(tm, d)] * 4 + [_vec_spec(d)] * 3 + [_ANY],
        out_specs=(_row_spec(tm, d), _row_spec(tm, d), pl.BlockSpec((8, d), lambda i: (0, 0))),
        compiler_params=_params(1),
    )(xin, dh, dres, pf, g, scale, gate_next, after)


def _ada_fwd(c_all, w_loc):
    nb, d = c_all.shape
    n = w_loc.shape[1]
    tn = _tile(n, 512, 128) if n % 128 == 0 else n

    def body(c_ref, w_ref, o_ref):
        cv = c_ref[...]
        o_ref[...] = jnp.dot(cv * jax.nn.sigmoid(cv), w_ref[...], preferred_element_type=F32,
                             precision=lax.Precision.HIGHEST)

    return pl.pallas_call(
        body, name="ada_fwd", grid=(n // tn,), out_shape=jax.ShapeDtypeStruct((nb, n), F32),
        in_specs=[pl.BlockSpec((nb, d), lambda j: (0, 0)), pl.BlockSpec((d, tn), lambda j: (0, j))],
        out_specs=pl.BlockSpec((nb, tn), lambda j: (0, j)), compiler_params=_params(1),
    )(c_all, w_loc)


def _ada_bwd(c_all, dmod_cols):
    nb, d = c_all.shape
    n = dmod_cols.shape[1]
    tn = _tile(n, 512, 128) if n % 128 == 0 else n

    def body(c_ref, dm_ref, o_ref):
        cv = c_ref[...]
        o_ref[...] = lax.dot_general(cv * jax.nn.sigmoid(cv), dm_ref[...], TN, preferred_element_type=F32,
                                     precision=lax.Precision.HIGHEST)

    return pl.pallas_call(
        body, name="ada_bwd", grid=(n // tn,), out_shape=jax.ShapeDtypeStruct((d, n), F32),
        in_specs=[pl.BlockSpec((nb, d), lambda j: (0, 0)), pl.BlockSpec((nb, tn), lambda j: (0, j))],
        out_specs=pl.BlockSpec((d, tn), lambda j: (0, j)), compiler_params=_params(1),
    )(c_all, dmod_cols)


_INV_SQRT2 = 0.7071067811865476
_INV_SQRT2PI = 0.3989422804014327


def _gelu_parts(x):
    cdf = 0.5 * (1.0 + lax.erf(x * _INV_SQRT2))
    return x * cdf, cdf + x * jnp.exp(-0.5 * x * x) * _INV_SQRT2PI


def _gm_group_fwd(up, vp, gv, wt, bcol):
    u = 0.5 * up * (1.0 + lax.erf(up * _INV_SQRT2))
    va = 0.5 * vp * (1.0 + lax.erf(vp * _INV_SQRT2))
    return _gm_group_mix(u, va, gv, wt, bcol)


def _gm_group_mix(u, va, gv, wt, bcol):
    xc = va - jnp.mean(va, axis=-1, keepdims=True)
    rstd_v = lax.rsqrt(jnp.mean(xc * xc, axis=-1, keepdims=True) + EPS)
    yv = xc * rstd_v
    vn = (yv * gv).astype(BF16)
    mixed = _dot(wt, vn, NN) + bcol
    return u, rstd_v, yv, vn, mixed, u * mixed


def _tril_bf16(w):
    row = lax.broadcasted_iota(jnp.int32, w.shape, 0)
    col = lax.broadcasted_iota(jnp.int32, w.shape, 1)
    return jnp.where(col <= row, w, 0.0).astype(BF16)


def _gmlp_fwd(proj, v_norm_g, w_spatial, b_cols, gamma, gm, after):
    s = proj.shape[0]
    ng = gm // GRP

    def body(p_ref, gv_ref, w_ref, b_ref, gam_ref, after_ref, o_ref):
        for g in range(ng):
            lo = g * GRP
            wt = _tril_bf16(w_ref[g])
            *_, o = _gm_group_fwd(p_ref[:, lo:lo + GRP], p_ref[:, gm + lo:gm + lo + GRP], gv_ref[:, lo:lo + GRP],
                                  wt, b_ref[:, g:g + 1])
            rstd_o = lax.rsqrt(jnp.mean(o * o, axis=-1, keepdims=True) + EPS)
            o_ref[:, lo:lo + GRP] = (o * rstd_o * gam_ref[:, lo:lo + GRP]).astype(BF16)

    return pl.pallas_call(
        body, name="gmlp_fwd", grid=(s // GRP,), out_shape=jax.ShapeDtypeStruct((s, gm), BF16),
        in_specs=[pl.BlockSpec((GRP, 2 * gm), lambda n: (n, 0)), _vec_spec(gm),
                  pl.BlockSpec((ng, GRP, GRP), lambda n: (0, 0, 0)), pl.BlockSpec((GRP, GRP), lambda n: (0, 0)),
                  _vec_spec(gm), _ANY],
        out_specs=pl.BlockSpec((GRP, gm), lambda n: (n, 0)), compiler_params=_params(1),
    )(proj, v_norm_g, w_spatial, b_cols, gamma, after)


def _gmlp_bwd(proj, don, v_norm_g, w_spatial, b_cols, gamma, gm, after):
    s = proj.shape[0]
    ng = gm // GRP

    def body(p_ref, don_ref, gv_ref, w_ref, b_ref, gam_ref, after_ref, dp_ref, dw_ref, db_ref, dgv_ref, dgam_ref):
        @pl.when(pl.program_id(0) == 0)
        def _():
            dw_ref[...] = jnp.zeros_like(dw_ref)
            db_ref[...] = jnp.zeros_like(db_ref)
            dgv_ref[...] = jnp.zeros_like(dgv_ref)
            dgam_ref[...] = jnp.zeros_like(dgam_ref)

        lane = lax.broadcasted_iota(jnp.int32, (GRP, GRP), 1)
        row = lax.broadcasted_iota(jnp.int32, (GRP, GRP), 0)
        for g in range(ng):
            lo = g * GRP
            up, vp = p_ref[:, lo:lo + GRP], p_ref[:, gm + lo:gm + lo + GRP]
            gv, gam = gv_ref[:, lo:lo + GRP], gam_ref[:, lo:lo + GRP]
            wt = _tril_bf16(w_ref[g])
            (u, du_dup), (va, dva_dvp) = _gelu_parts(up), _gelu_parts(vp)
            u, rstd_v, yv, vn, mixed, o = _gm_group_mix(u, va, gv, wt, b_ref[:, g:g + 1])
            rstd_o = lax.rsqrt(jnp.mean(o * o, axis=-1, keepdims=True) + EPS)
            ohat = o * rstd_o
            dn = don_ref[:, lo:lo + GRP]
            dgam_ref[:, lo:lo + GRP] += jnp.sum(dn * ohat, axis=0, keepdims=True)
            dohat = dn * gam
            do = rstd_o * (dohat - ohat * jnp.mean(dohat * ohat, axis=-1, keepdims=True))
            du = do * mixed
            dmixed = do * u
            dmb = dmixed.astype(BF16)
            db_ref[...] += jnp.where(lane == g, jnp.sum(dmixed, axis=-1, keepdims=True), 0.0)
            dw_ref[g] += jnp.where(lane <= row, _dot(dmb, vn, NT), 0.0)
            dvn = _dot(wt, dmb, TN)
            dgv_ref[:, lo:lo + GRP] += jnp.sum(dvn * yv, axis=0, keepdims=True)
            dyv = dvn * gv
            dva = rstd_v * (dyv - jnp.mean(dyv, axis=-1, keepdims=True)
                            - yv * jnp.mean(dyv * yv, axis=-1, keepdims=True))
            dp_ref[:, lo:lo + GRP] = (du * du_dup).astype(BF16)
            dp_ref[:, gm + lo:gm + lo + GRP] = (dva * dva_dvp).astype(BF16)

    const2 = lambda n: (0, 0)
    return pl.pallas_call(
        body, name="gmlp_bwd", grid=(s // GRP,),
        out_shape=(jax.ShapeDtypeStruct((s, 2 * gm), BF16), jax.ShapeDtypeStruct((ng, GRP, GRP), F32),
                   jax.ShapeDtypeStruct((GRP, GRP), F32), jax.ShapeDtypeStruct((1, gm), F32),
                   jax.ShapeDtypeStruct((1, gm), F32)),
        in_specs=[pl.BlockSpec((GRP, 2 * gm), lambda n: (n, 0)), pl.BlockSpec((GRP, gm), lambda n: (n, 0)),
                  _vec_spec(gm), pl.BlockSpec((ng, GRP, GRP), lambda n: (0, 0, 0)),
                  pl.BlockSpec((GRP, GRP), const2), _vec_spec(gm), _ANY],
        out_specs=(pl.BlockSpec((GRP, 2 * gm), lambda n: (n, 0)), pl.BlockSpec((ng, GRP, GRP), lambda n: (0, 0, 0)),
                   pl.BlockSpec((GRP, GRP), const2), _vec_spec(gm), _vec_spec(gm)),
        compiler_params=_params(1),
    )(proj, don, v_norm_g, w_spatial, b_cols, gamma, after)


BLK = 512


def _log_sigmoid(z):
    return jnp.minimum(z, 0.0) - jnp.log(1.0 + jnp.exp(-jnp.abs(z)))


def _split_dot(x, tri, passes):
    n = x.shape[0]
    parts, rest = [], x
    for _ in range(passes):
        hi = rest.astype(BF16)
        parts.append(hi)
        rest = rest - hi.astype(F32)
    res = _dot(jnp.concatenate(parts, axis=0), tri, NN)
    out = res[0:n]
    for k in range(1, passes):
        out = out + res[k * n:(k + 1) * n]
    return out


HEADS_PER_STEP = 2


def _sb_fwd(proj, gamma, gm, sb, after):
    s = proj.shape[0]
    hp = min(HEADS_PER_STEP, sb // GRP)
    w = hp * GRP
    nhp, nq = sb // w, s // BLK
    qc, kc, vc, gc = 2 * gm // w, (2 * gm + sb) // w, (2 * gm + 2 * sb) // w, gm // w
    scale = GRP ** -0.5

    def body(q_ref, k_ref, v_ref, gam_ref, after_ref, on_ref, o_ref, a_ref, s_ref, kb, vb):
        i = pl.program_id(1)

        @pl.when(i == 0)
        def _():
            kb[...] = k_ref[...].astype(BF16)
            vb[...] = v_ref[...].astype(BF16)

        qb = q_ref[...].astype(BF16)
        row = lax.broadcasted_iota(jnp.int32, (BLK, BLK), 0)
        col = lax.broadcasted_iota(jnp.int32, (BLK, BLK), 1)
        later = (row > col).astype(BF16)

        def block(j, carry, diag):
            off = pl.multiple_of(j * BLK, BLK)
            kj, vj = kb[pl.ds(off, BLK), :], vb[pl.ds(off, BLK), :]
            out = []
            for h in range(hp):
                tail, acc = carry[h]
                sl = slice(h * GRP, (h + 1) * GRP)
                z = _dot(qb[:, sl], kj[:, sl], NT) * scale
                lb = _log_sigmoid(z)
                l1m = lb - z
                sig = jnp.exp(lb)
                if diag:
                    l1m = jnp.where(col < row, l1m, 0.0)
                    sig = jnp.where(col < row, sig, 0.0)
                after_s = _split_dot(l1m, later, 2)
                a = jnp.exp(lb + after_s + tail)
                if diag:
                    a = jnp.where(col < row, a, 0.0)
                ab = a.astype(BF16)
                a_ref[h, j] = ab
                s_ref[h, j] = sig.astype(BF16)
                out.append((tail + after_s[:, 0:1] + l1m[:, 0:1], acc + _dot(ab, vj[:, sl], NN)))
            return tuple(out)

        init = tuple((jnp.zeros((BLK, 1), F32), jnp.zeros((BLK, GRP), F32)) for _ in range(hp))
        carry = block(i, init, True)
        carry = lax.fori_loop(0, i, lambda jj, c: block(i - 1 - jj, c, False), carry)
        for h in range(hp):
            _, acc = carry[h]
            sl = slice(h * GRP, (h + 1) * GRP)
            rstd = lax.rsqrt(jnp.mean(acc * acc, axis=-1, keepdims=True) + EPS)
            on_ref[:, sl] = (acc * rstd * gam_ref[:, sl]).astype(BF16)
            o_ref[:, sl] = acc

    saved = jax.ShapeDtypeStruct((sb // GRP, nq, nq, BLK, BLK), BF16)
    saved_spec = pl.BlockSpec((hp, None, nq, BLK, BLK), lambda h, i: (h, i, 0, 0, 0))
    return pl.pallas_call(
        body, name="sb_fwd", grid=(nhp, nq),
        out_shape=(jax.ShapeDtypeStruct((s, sb), BF16), jax.ShapeDtypeStruct((s, sb), F32), saved, saved),
        in_specs=[pl.BlockSpec((BLK, w), lambda h, i: (i, qc + h)), pl.BlockSpec((s, w), lambda h, i: (0, kc + h)),
                  pl.BlockSpec((s, w), lambda h, i: (0, vc + h)), pl.BlockSpec((1, w), lambda h, i: (0, gc + h)), _ANY],
        out_specs=(pl.BlockSpec((BLK, w), lambda h, i: (i, h)), pl.BlockSpec((BLK, w), lambda h, i: (i, h)),
                   saved_spec, saved_spec),
        scratch_shapes=[pltpu.VMEM((s, w), BF16), pltpu.VMEM((s, w), BF16)],
        compiler_params=_params(2),
    )(proj, proj, proj, gamma, after)


def _sb_bwd(proj, o_raw, a_sv, s_sv, don, gamma, gm, sb, after):
    s = proj.shape[0]
    hp = min(HEADS_PER_STEP, sb // GRP)
    w = hp * GRP
    nhp, nq = sb // w, s // BLK
    qc, kc, vc, gc = 2 * gm // w, (2 * gm + sb) // w, (2 * gm + 2 * sb) // w, gm // w
    scale = GRP ** -0.5

    def body(q_ref, k_ref, v_ref, o_ref, a_ref, s_ref, don_ref, gam_ref, after_ref, dq_ref, dk_ref, dv_ref, dgam_ref,
             kb, vb, dkt_acc, dvt_acc):
        i = pl.program_id(1)

        @pl.when(i == 0)
        def _():
            kb[...] = k_ref[...].astype(BF16)
            vb[...] = v_ref[...].astype(BF16)
            dkt_acc[...] = jnp.zeros_like(dkt_acc)
            dvt_acc[...] = jnp.zeros_like(dvt_acc)
            dgam_ref[...] = jnp.zeros_like(dgam_ref)

        dobs = []
        for h in range(hp):
            sl = slice(h * GRP, (h + 1) * GRP)
            o, dn = o_ref[:, sl], don_ref[:, sl]
            rstd = lax.rsqrt(jnp.mean(o * o, axis=-1, keepdims=True) + EPS)
            ohat = o * rstd
            dgam_ref[:, sl] += jnp.sum(dn * ohat, axis=0, keepdims=True)
            dohat = dn * gam_ref[:, sl]
            dobs.append((rstd * (dohat - ohat * jnp.mean(dohat * ohat, axis=-1, keepdims=True))).astype(BF16))

        qb = q_ref[...].astype(BF16)
        qts = [qb[:, h * GRP:(h + 1) * GRP].T for h in range(hp)]
        dots = [dob.T for dob in dobs]
        row = lax.broadcasted_iota(jnp.int32, (BLK, BLK), 0)
        col = lax.broadcasted_iota(jnp.int32, (BLK, BLK), 1)
        before = (row < col).astype(BF16)

        def block(j, carry):
            off = pl.multiple_of(j * BLK, BLK)
            kj, vj = kb[pl.ds(off, BLK), :], vb[pl.ds(off, BLK), :]
            out = []
            for h in range(hp):
                e_pre, dq = carry[h]
                sl = slice(h * GRP, (h + 1) * GRP)
                qh, kh, dob = qb[:, sl], kj[:, sl], dobs[h]
                ab = a_ref[h, j]
                sig = s_ref[h, j].astype(F32)
                de = ab.astype(F32) * _dot(dob, vj[:, sl], NT)
                dvt_acc[sl, pl.ds(off, BLK)] += _dot(dots[h], ab, NN)
                before_s = _dot(de.astype(BF16), before, NN)
                dzb = ((de * (1.0 - sig) - (e_pre + before_s) * sig) * scale).astype(BF16)
                dkt_acc[sl, pl.ds(off, BLK)] += _dot(qts[h], dzb, NN)
                out.append((e_pre + before_s[:, BLK - 1:BLK] + de[:, BLK - 1:BLK], dq + _dot(dzb, kh, NN)))
            return tuple(out)

        carry = tuple((jnp.zeros((BLK, 1), F32), jnp.zeros((BLK, GRP), F32)) for _ in range(hp))
        carry = lax.fori_loop(0, i + 1, block, carry)
        for h in range(hp):
            dq_ref[:, h * GRP:(h + 1) * GRP] = carry[h][1].astype(BF16)

        @pl.when(i == nq - 1)
        def _():
            dk_ref[...] = dkt_acc[...].T.astype(BF16)
            dv_ref[...] = dvt_acc[...].T.astype(BF16)

    blk_q = lambda h, i: (i, h)
    whole = lambda h, i: (0, h)
    saved_spec = pl.BlockSpec((hp, None, nq, BLK, BLK), lambda h, i: (h, i, 0, 0, 0))
    return pl.pallas_call(
        body, name="sb_bwd", grid=(nhp, nq),
        out_shape=(jax.ShapeDtypeStruct((s, sb), BF16),) * 3 + (jax.ShapeDtypeStruct((1, sb), F32),),
        in_specs=[pl.BlockSpec((BLK, w), lambda h, i: (i, qc + h)), pl.BlockSpec((s, w), lambda h, i: (0, kc + h)),
                  pl.BlockSpec((s, w), lambda h, i: (0, vc + h)), pl.BlockSpec((BLK, w), blk_q), saved_spec, saved_spec,
                  pl.BlockSpec((BLK, w), lambda h, i: (i, gc + h)),
                  pl.BlockSpec((1, w), lambda h, i: (0, gc + h)), _ANY],
        out_specs=(pl.BlockSpec((BLK, w), blk_q), pl.BlockSpec((s, w), whole), pl.BlockSpec((s, w), whole),
                   pl.BlockSpec((1, w), whole)),
        scratch_shapes=[pltpu.VMEM((s, w), BF16), pltpu.VMEM((s, w), BF16),
                        pltpu.VMEM((w, s), F32), pltpu.VMEM((w, s), F32)],
        compiler_params=_params(2),
    )(proj, proj, proj, o_raw, a_sv, s_sv, don, gamma, after)


FFN_ROW_CHUNKS = 2


def _row_chunks(tm):
    n = FFN_ROW_CHUNKS if tm % (16 * FFN_ROW_CHUNKS) == 0 else 1
    return [slice(k * (tm // n), (k + 1) * (tm // n)) for k in range(n)]


def _ffn_up(h2, wg, wu):
    s, d = h2.shape
    nb, fb, _ = wg.shape
    tm = _tile(s, 1024, 16)

    def body(h_ref, wg_ref, wu_ref, g_ref, u_ref, a_ref):
        for rows_ in _row_chunks(tm):
            hv = h_ref[rows_, :]
            g = _dot(hv, wg_ref[...], NT)
            u = _dot(hv, wu_ref[...], NT)
            g_ref[rows_, :] = g.astype(BF16)
            u_ref[rows_, :] = u.astype(BF16)
            a_ref[rows_, :] = (g * jax.nn.sigmoid(g) * u).astype(BF16)

    rows = pl.BlockSpec((tm, d), lambda i, j: (i, 0))
    wblk = pl.BlockSpec((None, fb, d), lambda i, j: (j, 0, 0))
    hid = pl.BlockSpec((None, tm, fb), lambda i, j: (j, i, 0))
    hid_sds = jax.ShapeDtypeStruct((nb, s, fb), BF16)
    return pl.pallas_call(
        body, name="ffn_up", grid=(s // tm, nb), out_shape=(hid_sds,) * 3, in_specs=[rows, wblk, wblk],
        out_specs=(hid,) * 3, compiler_params=_params(2),
    )(h2, wg, wu)


def _ffn_bwd_act(df, g_pre, u_pre, wd):
    s, d = df.shape
    nb, fb, _ = wd.shape
    tm = _tile(s, 1024, 16)

    def body(df_ref, g_ref, u_ref, wd_ref, dg_ref, du_ref):
        for rows_ in _row_chunks(tm):
            g, u = g_ref[rows_, :].astype(F32), u_ref[rows_, :].astype(F32)
            da = _dot(df_ref[rows_, :], wd_ref[...], NT)
            sg = jax.nn.sigmoid(g)
            dg_ref[rows_, :] = (da * u * (sg * (1.0 + g * (1.0 - sg)))).astype(BF16)
            du_ref[rows_, :] = (da * (g * sg)).astype(BF16)

    rows = pl.BlockSpec((tm, d), lambda i, j: (i, 0))
    wblk = pl.BlockSpec((None, fb, d), lambda i, j: (j, 0, 0))
    hid = pl.BlockSpec((None, tm, fb), lambda i, j: (j, i, 0))
    hid_sds = jax.ShapeDtypeStruct((nb, s, fb), BF16)
    return pl.pallas_call(
        body, name="ffn_bwd_act", grid=(s // tm, nb), out_shape=(hid_sds, hid_sds),
        in_specs=[rows, hid, hid, wblk], out_specs=(hid, hid), compiler_params=_params(2),
    )(df, g_pre, u_pre, wd)


def _ffn_bwd_in(dg, du, wg, wu):
    nb, s, fb = dg.shape
    d = wg.shape[2]
    tm = _tile(s, 1024, 16)

    def body(dg_ref, du_ref, wg_ref, wu_ref, dh_ref):
        j = pl.program_id(1)
        parts = [_dot(dg_ref[rows_, :], wg_ref[...], NN) + _dot(du_ref[rows_, :], wu_ref[...], NN)
                 for rows_ in _row_chunks(tm)]

        @pl.when(j == 0)
        def _():
            for rows_, part in zip(_row_chunks(tm), parts):
                dh_ref[rows_, :] = part

        @pl.when(j > 0)
        def _():
            for rows_, part in zip(_row_chunks(tm), parts):
                dh_ref[rows_, :] += part

    wblk = pl.BlockSpec((None, fb, d), lambda i, j: (j, 0, 0))
    hid = pl.BlockSpec((None, tm, fb), lambda i, j: (j, i, 0))
    return pl.pallas_call(
        body, name="ffn_bwd_in", grid=(s // tm, nb), out_shape=jax.ShapeDtypeStruct((s, d), F32),
        in_specs=[hid, hid, wblk, wblk], out_specs=pl.BlockSpec((tm, d), lambda i, j: (i, 0)),
        compiler_params=_params(2),
    )(dg, du, wg, wu)


def _reduce_adamw(parts, w, m, v, name, after):
    npart, r, c = parts.shape
    tr = _tile(r, max(16, 524288 // c), 16)
    c1, c2 = 1.0 - ADAM_B1 ** ADAM_STEP, 1.0 - ADAM_B2 ** ADAM_STEP

    def body(p_ref, w_ref, m_ref, v_ref, after_ref, g_ref, d_ref, nm_ref, nv_ref):
        g = p_ref[0].astype(F32)
        for k in range(1, npart):
            g = g + p_ref[k].astype(F32)
        nm = ADAM_B1 * m_ref[...] + (1.0 - ADAM_B1) * g
        nv = ADAM_B2 * v_ref[...] + (1.0 - ADAM_B2) * (g * g)
        g_ref[...] = g
        nm_ref[...] = nm
        nv_ref[...] = nv
        d_ref[...] = -ADAM_LR * ((nm / c1) / (jnp.sqrt(nv / c2) + ADAM_EPS) + ADAM_WD * w_ref[...])

    blk = pl.BlockSpec((tr, c), lambda i: (i, 0))
    sds = jax.ShapeDtypeStruct((r, c), F32)
    return pl.pallas_call(
        body, name=name, grid=(r // tr,), out_shape=(sds,) * 4,
        in_specs=[pl.BlockSpec((npart, tr, c), lambda i: (0, i, 0)), blk, blk, blk, _ANY], out_specs=(blk,) * 4,
        compiler_params=_params(1),
    )(parts, w, m, v, after)


def _pack(vecs):
    rows = jnp.concatenate([a.reshape(-1, GRP) for a in vecs], axis=0)
    pad = -rows.shape[0] % 64
    return jnp.pad(rows, ((0, pad), (0, 0)))


def _unpack(rows, shapes):
    out, at = [], 0
    for shp in shapes:
        n = 1
        for k in shp:
            n *= k
        out.append(rows[at:at + n // GRP].reshape(shp))
        at += n // GRP
    return out


def kernel(x, c, w_ada, b_ada, norm1_g, w_in, v_norm_g, w_spatial, b_spatial, out_norm_g, w_out, norm2_g, w_gate, w_up, w_down, final_g, loss_target, m_w_ada, m_b_ada, m_norm1_g, m_w_in, m_v_norm_g, m_w_spatial, m_b_spatial, m_out_norm_g, m_w_out, m_norm2_g, m_w_gate, m_w_up, m_w_down, m_final_g, v_w_ada, v_b_ada, v_norm1_g, v_w_in, v_v_norm_g, v_w_spatial, v_b_spatial, v_out_norm_g, v_w_out, v_norm2_g, v_w_gate, v_w_up, v_w_down, v_final_g):
    s, d = x.shape[1], x.shape[2]
    gm = v_norm_g.shape[1]
    sb = d - gm
    n_in, ffb, ob = w_in.shape[2], w_gate.shape[2], w_out.shape[1]
    xs, tgt = x[0], loss_target[0]
    me = 4 * lax.axis_index("x") + 2 * lax.axis_index("y") + lax.axis_index("c")

    c_all = _exchange(c, False, "gather_c")[:, 0, :]
    mod_cols = _ada_fwd(c_all, w_ada[0])
    mod = _exchange(mod_cols[:, None, :], True, "scatter_mod").reshape(1, 6 * d) + b_ada
    shift1, scale1, gate1, shift2, scale2, gate2 = [mod[:, k * d:(k + 1) * d] for k in range(6)]

    tok, gathers = mod, {}
    tr_ = lambda a: jnp.swapaxes(a, 1, 2)
    w_gate, m_w_gate, v_w_gate, w_up, m_w_up, v_w_up = map(tr_, (w_gate, m_w_gate, v_w_gate, w_up, m_w_up, v_w_up))
    big_w = (("w_in", w_in), ("w_out", w_out), ("w_gate", w_gate), ("w_up", w_up), ("w_down", w_down))
    first, tok = _gather_start([w_in[0].astype(BF16)], tok, "gather_start_w_in")
    rest, tok = _gather_start([w[0].astype(BF16) for _, w in big_w[1:]], tok, "gather_start_weights")
    gathers = {nm: h for (nm, _), h in zip(big_w, first + rest)}
    shift1 = shift1 + tok[0:1, 0:1]

    tm = _tile(s, 512, 16)
    nt = s // tm
    h1 = _prenorm_fwd(xs, norm1_g, scale1, shift1, "prenorm1")
    w_in_all = _gather_wait(_gather_forward(gathers["w_in"], h1, "gather_fwd_w_in")[0], h1, "gather_wait_w_in")
    proj = _mm(h1, w_in_all, pl.BlockSpec((s, d), lambda j: (0, 0)), pl.BlockSpec((None, d, n_in), lambda j: (j, 0, 0)),
               jax.ShapeDtypeStruct((s, NDEV * n_in), F32), pl.BlockSpec((s, n_in), lambda j: (0, j)),
               (NDEV,), NN, False, "proj")
    fwd_w_out, tok = _gather_forward(gathers["w_out"], proj, "gather_fwd_w_out")
    b_cols = jnp.pad(b_spatial[0].T, ((0, 0), (0, GRP - b_spatial.shape[1])))
    on_sb, o_sb, a_sv, s_sv = _sb_fwd(proj, out_norm_g, gm, sb, tok)
    fwd_w_gate, tok = _gather_forward(gathers["w_gate"], on_sb, "gather_fwd_w_gate")
    on_gm = _gmlp_fwd(proj, v_norm_g, w_spatial[0], b_cols, out_norm_g, gm, tok)
    fwd_w_up, tok = _gather_forward(gathers["w_up"], on_gm, "gather_fwd_w_up")
    o_n = jnp.concatenate([on_gm, on_sb], axis=1)
    w_out_all = _gather_wait(fwd_w_out, tok, "gather_wait_w_out").reshape(d, d)
    rows_1 = pl.BlockSpec((tm, d), lambda i: (i, 0))
    whole_1 = pl.BlockSpec((d, d), lambda i: (0, 0))
    p_out = _mm(o_n, w_out_all, rows_1, whole_1, jax.ShapeDtypeStruct((s, d), F32), rows_1, (nt,), NN, False,
                "out_proj")
    x1, h2 = _residual_prenorm(xs, gate1, p_out, norm2_g, scale2, shift2, "residual1_prenorm2")
    w_gate_all = _gather_wait(fwd_w_gate, h2, "gather_wait_w_gate")
    w_up_all = _gather_wait(fwd_w_up, h2, "gather_wait_w_up")
    g_pre, u_pre, act = _ffn_up(h2, w_gate_all, w_up_all)
    fwd_w_down, tok = _gather_forward(gathers["w_down"], act, "gather_fwd_w_down")
    w_down_all = _gather_wait(fwd_w_down, tok, "gather_wait_w_down")
    tm2 = _tile(s, 1024, 16)
    f_out = _mm(act, w_down_all, pl.BlockSpec((2, tm2, ffb), lambda i, j: (j, i, 0)),
                pl.BlockSpec((2, ffb, d), lambda i, j: (j, 0, 0)), jax.ShapeDtypeStruct((s, d), F32),
                pl.BlockSpec((tm2, d), lambda i, j: (i, 0)), (s // tm2, NDEV // 2), NN, True, "ffn_down")

    dx2, df, st_f = _final_loss(x1, f_out, final_g.reshape(1, d), tgt, gate2)
    dg, du = _ffn_bwd_act(df, g_pre, u_pre, w_down_all)
    dh2 = _ffn_bwd_in(dg, du, w_gate_all, w_up_all)
    hid = pl.BlockSpec((None, s, ffb), lambda j: (j, 0, 0))
    all_d = pl.BlockSpec((s, d), lambda j: (0, 0))

    def second_leg(first, after, tag):
        parts, lands = _scatter_wait1(first, after, "scatter_wait1_" + tag)
        sums = [_chip_sum(p, l, "chip_sum_%s_%d" % (tag, k)) for k, (p, l) in enumerate(zip(parts, lands))]
        return _scatter_start2(sums, after, "scatter_start2_" + tag)

    row_sds = jax.ShapeDtypeStruct((NDEV, ffb, d), BF16)
    row_out = pl.BlockSpec((None, ffb, d), lambda j: (j, 0, 0))
    gw_gate = _mm(dg, h2, hid, all_d, row_sds, row_out, (NDEV,), TN, False, "grad_w_gate")
    gw_up = _mm(du, h2, hid, all_d, row_sds, row_out, (NDEV,), TN, False, "grad_w_up")
    gw_down = _mm(act, df, hid, all_d, row_sds, row_out, (NDEV,), TN, False, "grad_w_down")
    dx1, dp, st2 = _prenorm_bwd(x1, dh2, dx2, f_out, norm2_g, scale2, gate1, "prenorm2_bwd", gw_down)
    first_ffn, tok = _scatter_start1([gw_gate, gw_up, gw_down], dp, "scatter_start1_ffn")
    don = _mm(dp, w_out_all, rows_1, whole_1, jax.ShapeDtypeStruct((s, d), F32), rows_1, (nt,), NT, False,
              "out_proj_bwd", tok)
    gw_out = _mm(o_n, dp, pl.BlockSpec((s, ob), lambda j: (0, j)), all_d,
                 jax.ShapeDtypeStruct((NDEV, ob, d), BF16), pl.BlockSpec((None, ob, d), lambda j: (j, 0, 0)),
                 (NDEV,), TN, False, "grad_w_out", don)
    first_out, tok = _scatter_start1([gw_out], tok, "scatter_start1_out")
    second_ffn, tok = second_leg(first_ffn, tok, "ffn")
    dproj_gm, dw_sp, db_cols, dgv, dgam_gm = _gmlp_bwd(proj, don, v_norm_g, w_spatial[0], b_cols, out_norm_g, gm, tok)
    second_out, tok = second_leg(first_out, dproj_gm, "out")
    dq, dk, dv, dgam_sb = _sb_bwd(proj, o_sb, a_sv, s_sv, don, out_norm_g, gm, sb, tok)
    dproj = jnp.concatenate([dproj_gm, dq, dk, dv], axis=1)
    gw_in = _mm(h1, dproj, all_d, pl.BlockSpec((s, n_in), lambda j: (0, j)),
                jax.ShapeDtypeStruct((NDEV, d, n_in), BF16), pl.BlockSpec((None, d, n_in), lambda j: (j, 0, 0)),
                (NDEV,), TN, False, "grad_w_in")
    first_in, tok = _scatter_start1([gw_in], tok, "scatter_start1_in")
    dh1 = _mm(dproj, w_in_all, pl.BlockSpec((tm2, 2 * n_in), lambda i, j: (i, j)),
              pl.BlockSpec((2, d, n_in), lambda i, j: (j, 0, 0)), jax.ShapeDtypeStruct((s, d), F32),
              pl.BlockSpec((tm2, d), lambda i, j: (i, 0)), (s // tm2, NDEV // 2), NT, True, "in_proj_bwd", tok)
    grad_x, _, st1 = _prenorm_bwd(xs, dh1, dx1, p_out, norm1_g, scale1, gate1, "prenorm1_bwd", tok)

    dmod = jnp.concatenate([st1[0], st1[1], st1[3], st2[0], st2[1], st2[3]])
    loss_row = jnp.pad((0.5 * jnp.sum(st_f[1]) / d).reshape(1, 1), ((0, 0), (0, GRP - 1)))
    small = [st1[2], dgv, dw_sp, db_cols[:, :b_spatial.shape[1]].T, jnp.concatenate([dgam_gm, dgam_sb], axis=1),
             st2[2], st_f[0], dmod, loss_row]
    zero_row = jnp.zeros((1, GRP), F32)
    small_w = [norm1_g, v_norm_g, w_spatial, b_spatial, out_norm_g, norm2_g, final_g, b_ada, zero_row]
    small_m = [m_norm1_g, m_v_norm_g, m_w_spatial, m_b_spatial, m_out_norm_g, m_norm2_g, m_final_g, m_b_ada, zero_row]
    small_v = [v_norm1_g, v_v_norm_g, v_w_spatial, v_b_spatial, v_out_norm_g, v_norm2_g, v_final_g, v_b_ada, zero_row]
    (small_first,), tok = _gather_start([_pack(small)], grad_x, "gather_small_start")
    second_in, tok = second_leg(first_in, tok, "in")

    big = {}
    prev = tok
    for second, group in ((second_ffn, (("w_gate", w_gate, m_w_gate, v_w_gate), ("w_up", w_up, m_w_up, v_w_up),
                                        ("w_down", w_down, m_w_down, v_w_down))),
                          (second_out, (("w_out", w_out, m_w_out, v_w_out),))):
        chip_sums = _scatter_wait2(second, prev, "scatter_wait2_" + group[0][0])
        for (nm, w, m, v), part in zip(group, chip_sums):
            big[nm] = _reduce_adamw(part, w[0], m[0], v[0], "adamw_" + nm, tok)
            prev = big[nm][1]

    small_second, tok2 = _gather_forward(small_first, prev, "gather_small_fwd")
    small_all = _gather_wait(small_second, tok2, "gather_small_wait")
    sm = _reduce_adamw(small_all, _pack(small_w), _pack(small_m), _pack(small_v), "adamw_small", tok)
    shapes = [a.shape for a in small_w]
    sm_g, sm_d, sm_m, sm_v = [_unpack(t, shapes) for t in sm]
    loss = sm_g[8][0, 0]

    at = sum(a.size for a in small_w[:7]) // GRP
    dmod_all = small_all[:, at:at + 6 * d // GRP, :].reshape(NDEV, 6 * d)
    n_ada = w_ada.shape[2]
    dmod_cols = lax.dynamic_slice(dmod_all, (0, me * n_ada), (NDEV, n_ada))
    g_ada = _ada_bwd(c_all, dmod_cols)
    big["w_ada"] = _reduce_adamw(g_ada[None], w_ada[0], m_w_ada[0], v_w_ada[0], "adamw_w_ada", sm[0])
    part_in, = _scatter_wait2(second_in, big["w_ada"][1], "scatter_wait2_w_in")
    big["w_in"] = _reduce_adamw(part_in, w_in[0], m_w_in[0], v_w_in[0], "adamw_w_in", tok)

    names = ["w_ada", "b_ada", "norm1_g", "w_in", "v_norm_g", "w_spatial", "b_spatial", "out_norm_g", "w_out",
             "norm2_g", "w_gate", "w_up", "w_down", "final_g"]
    small_at = {"norm1_g": 0, "v_norm_g": 1, "w_spatial": 2, "b_spatial": 3, "out_norm_g": 4, "norm2_g": 5,
                "final_g": 6, "b_ada": 7}
    outs = [[], [], [], []]
    for nm in names:
        for k in range(4):
            if nm in big:
                res = big[nm][k][None]
                outs[k].append(tr_(res) if nm in ("w_gate", "w_up") else res)
            else:
                outs[k].append((sm_g, sm_d, sm_m, sm_v)[k][small_at[nm]])
    return (loss, grad_x[None], *outs[0], *outs[1], *outs[2], *outs[3])
```

```python
import jax
import jax.numpy as jnp
from jax import lax
from jax.experimental import pallas as pl
from jax.experimental.pallas import tpu as pltpu

F32, BF16 = jnp.float32, jnp.bfloat16
NDEV = 8
GRP = 128
EPS = 1e-6
VMEM_BYTES = 64 * 2 ** 20
VMEM_LIMIT = VMEM_BYTES - 8 * 2 ** 20
ADAM_LR, ADAM_B1, ADAM_B2, ADAM_EPS, ADAM_WD, ADAM_STEP = 0.001, 0.9, 0.999, 1e-08, 0.01, 10
MESH = pl.DeviceIdType.MESH
NN = (((1,), (0,)), ((), ()))
NT = (((1,), (1,)), ((), ()))
TN = (((0,), (0,)), ((), ()))


def _params(n_axes):
    return pltpu.CompilerParams(dimension_semantics=("arbitrary",) * n_axes, vmem_limit_bytes=VMEM_LIMIT)


def _tile(n, cap, mult):
    best = None
    for t in range(mult, min(n, cap) + 1, mult):
        if n % t == 0:
            best = t
    assert best is not None, (n, cap, mult)
    return best


def _dot(a, b, dims):
    return lax.dot_general(a, b, dims, preferred_element_type=F32)


def _exchange(src, per_peer, name):
    blk = src.shape[1:] if per_peer else src.shape

    def body(src_ref, out_ref, send_sems, recv_sems, local_sem):
        x, y, c = lax.axis_index("x"), lax.axis_index("y"), lax.axis_index("c")
        me = 4 * x + 2 * y + c
        local = pltpu.make_async_copy(src_ref.at[me] if per_peer else src_ref, out_ref.at[me], local_sem)
        local.start()
        sends, recvs = [], []
        for k in range(1, NDEV):
            px, py, pc = x ^ (k >> 2), y ^ ((k >> 1) & 1), c ^ (k & 1)
            p = 4 * px + 2 * py + pc
            mine = src_ref.at[p] if per_peer else src_ref
            sends.append(pltpu.make_async_remote_copy(
                src_ref=mine, dst_ref=out_ref.at[me], send_sem=send_sems.at[k - 1], recv_sem=recv_sems.at[k - 1],
                device_id=(px, py, pc), device_id_type=MESH))
            recvs.append(pltpu.make_async_remote_copy(
                src_ref=mine, dst_ref=out_ref.at[p], send_sem=send_sems.at[k - 1], recv_sem=recv_sems.at[k - 1],
                device_id=(px, py, pc), device_id_type=MESH))
        for cp in sends:
            cp.start()
        for cp in recvs:
            cp.wait_recv()
        for cp in sends:
            cp.wait_send()
        local.wait()

    return pl.pallas_call(
        body, name=name,
        out_shape=jax.ShapeDtypeStruct((NDEV,) + tuple(blk), src.dtype),
        in_specs=[pl.BlockSpec(memory_space=pl.ANY)],
        out_specs=pl.BlockSpec(memory_space=pl.ANY),
        scratch_shapes=[pltpu.SemaphoreType.DMA((NDEV - 1,)), pltpu.SemaphoreType.DMA((NDEV - 1,)),
                        pltpu.SemaphoreType.DMA],
    )(src)


_HBM = pl.BlockSpec(memory_space=pltpu.HBM)
_SEM = pl.BlockSpec(memory_space=pltpu.SEMAPHORE)
_ANY = pl.BlockSpec(memory_space=pl.ANY)
_EFFECT = pltpu.SideEffectType.DATAFLOW_SIDE_EFFECTING


def _chip_peers():
    x, y, c = lax.axis_index("x"), lax.axis_index("y"), lax.axis_index("c")
    chips = [(x, 1 - y), (1 - x, y), (1 - x, 1 - y)]
    return 4 * x + 2 * y + c, (x, y, 1 - c), [((px, py, c), 4 * px + 2 * py + c) for px, py in chips]


def _gather_start(srcs, after, name):
    n = len(srcs)
    me = 4 * lax.axis_index("x") + 2 * lax.axis_index("y") + lax.axis_index("c")
    lands = [lax.dynamic_update_slice(lax.empty((NDEV,) + a.shape, a.dtype), a[None], (me,) + (0,) * a.ndim) for a in srcs]

    def body(*refs):
        src_refs, land_refs = refs[:n], refs[n:2 * n]
        sems, token = refs[2 * n + 1:2 * n + 1 + 2 * n], refs[-1]
        my, sibling, chips = _chip_peers()
        for a in range(n):
            for k, to in enumerate([sibling] + [coords for coords, _ in chips]):
                pltpu.make_async_remote_copy(src_ref=src_refs[a], dst_ref=land_refs[a].at[my], send_sem=sems[2 * a].at[k],
                                             recv_sem=sems[2 * a + 1].at[k], device_id=to, device_id_type=MESH).start()
        token[...] = jnp.zeros_like(token)

    hbm = lambda arrs: tuple(pltpu.HBM(a.shape, a.dtype) for a in arrs)
    res = pl.pallas_call(
        body, name=name,
        out_shape=(pltpu.SemaphoreType.DMA((4,)),) * (2 * n) + hbm(srcs) + hbm(lands) + (jax.ShapeDtypeStruct((8, GRP), F32),),
        in_specs=(_HBM,) * (2 * n) + (_ANY,),
        out_specs=(_SEM,) * (2 * n) + (_HBM,) * (2 * n) + (pl.BlockSpec(memory_space=pltpu.VMEM),),
        input_output_aliases={k: 2 * n + k for k in range(2 * n)}, compiler_params=pltpu.CompilerParams(has_side_effects=_EFFECT),
    )(*[pltpu.with_memory_space_constraint(a, pltpu.HBM) for a in list(srcs) + lands], after)
    return [(res[2 * a], res[2 * a + 1], res[2 * n + a], res[3 * n + a]) for a in range(n)], res[-1]


def _gather_forward(handles, after, name):
    send_sems, recv_sems, src_thru, land_thru = handles

    def body(src_ref, land_ref, send_sems, recv_sems, after_ref, send2, recv2, land_out, token):
        my, sibling, chips = _chip_peers()
        for k, (to, p) in enumerate([(sibling, my ^ 1)] + chips):
            first = pltpu.make_async_remote_copy(src_ref=src_ref, dst_ref=land_ref.at[p], send_sem=send_sems.at[k],
                                                 recv_sem=recv_sems.at[k], device_id=to, device_id_type=MESH)
            first.wait_send()
            first.wait_recv()
        for k, (_, p) in enumerate(chips):
            pltpu.make_async_remote_copy(src_ref=land_ref.at[p], dst_ref=land_ref.at[p], send_sem=send2.at[k],
                                         recv_sem=recv2.at[k], device_id=sibling, device_id_type=MESH).start()
        token[...] = jnp.zeros_like(token)

    res = pl.pallas_call(
        body, name=name,
        out_shape=(pltpu.SemaphoreType.DMA((3,)), pltpu.SemaphoreType.DMA((3,)), pltpu.HBM(land_thru.shape, land_thru.dtype),
                   jax.ShapeDtypeStruct((8, GRP), F32)),
        in_specs=(_HBM, _HBM, _SEM, _SEM, _ANY), out_specs=(_SEM, _SEM, _HBM, pl.BlockSpec(memory_space=pltpu.VMEM)),
        input_output_aliases={1: 2}, compiler_params=pltpu.CompilerParams(has_side_effects=_EFFECT),
    )(src_thru, land_thru, send_sems, recv_sems, after)
    return tuple(res[:3]), res[3]


def _gather_wait(handles, after, name):
    send2, recv2, land_thru = handles

    def body(land_ref, send2, recv2, after_ref, got_ref):
        _, sibling, chips = _chip_peers()
        for k, (_, p) in enumerate(chips):
            cp = pltpu.make_async_remote_copy(src_ref=land_ref.at[p], dst_ref=land_ref.at[p ^ 1], send_sem=send2.at[k],
                                              recv_sem=recv2.at[k], device_id=sibling, device_id_type=MESH)
            cp.wait_send()
            cp.wait_recv()

    return pl.pallas_call(
        body, name=name, out_shape=pltpu.HBM(land_thru.shape, land_thru.dtype),
        in_specs=(_HBM, _SEM, _SEM, _ANY), out_specs=_HBM, input_output_aliases={0: 0},
        compiler_params=pltpu.CompilerParams(has_side_effects=_EFFECT),
    )(land_thru, send2, recv2, after)


NCHIP = NDEV // 2


def _scatter_start1(parts, after, name):
    n = len(parts)
    lands = [lax.empty((NCHIP,) + p.shape[1:], p.dtype) for p in parts]

    def body(*refs):
        part_refs, land_refs = refs[:n], refs[n:2 * n]
        send_sems, recv_sems, token = refs[2 * n + 1], refs[2 * n + 2], refs[-1]
        _, sibling, _ = _chip_peers()
        c = lax.axis_index("c")
        for a in range(n):
            for q in range(NCHIP):
                pltpu.make_async_remote_copy(
                    src_ref=part_refs[a].at[2 * q + 1 - c], dst_ref=land_refs[a].at[q], send_sem=send_sems.at[a * NCHIP + q],
                    recv_sem=recv_sems.at[a * NCHIP + q], device_id=sibling, device_id_type=MESH).start()
        token[...] = jnp.zeros_like(token)

    hbm = lambda arrs: tuple(pltpu.HBM(a.shape, a.dtype) for a in arrs)
    res = pl.pallas_call(
        body, name=name,
        out_shape=(pltpu.SemaphoreType.DMA((n * NCHIP,)), pltpu.SemaphoreType.DMA((n * NCHIP,))) + hbm(parts) + hbm(lands)
        + (jax.ShapeDtypeStruct((8, GRP), F32),),
        in_specs=(_HBM,) * (2 * n) + (_ANY,),
        out_specs=(_SEM, _SEM) + (_HBM,) * (2 * n) + (pl.BlockSpec(memory_space=pltpu.VMEM),),
        input_output_aliases={k: 2 + k for k in range(2 * n)}, compiler_params=pltpu.CompilerParams(has_side_effects=_EFFECT),
    )(*[pltpu.with_memory_space_constraint(a, pltpu.HBM) for a in list(parts) + lands], after)
    return (n,) + tuple(res[:-1]), res[-1]


def _scatter_wait1(handles, after, name):
    n, send_sems, recv_sems = handles[:3]
    thru = handles[3:]

    def body(*refs):
        part_refs, land_refs, send_sems, recv_sems = refs[:n], refs[n:2 * n], refs[2 * n], refs[2 * n + 1]
        _, sibling, _ = _chip_peers()
        for a in range(n):
            for q in range(NCHIP):
                cp = pltpu.make_async_remote_copy(
                    src_ref=part_refs[a].at[q], dst_ref=land_refs[a].at[q], send_sem=send_sems.at[a * NCHIP + q],
                    recv_sem=recv_sems.at[a * NCHIP + q], device_id=sibling, device_id_type=MESH)
                cp.wait_send()
                cp.wait_recv()

    res = pl.pallas_call(
        body, name=name, out_shape=tuple(pltpu.HBM(a.shape, a.dtype) for a in thru),
        in_specs=(_HBM,) * (2 * n) + (_SEM, _SEM, _ANY), out_specs=(_HBM,) * (2 * n),
        input_output_aliases={k: k for k in range(2 * n)}, compiler_params=pltpu.CompilerParams(has_side_effects=_EFFECT),
    )(*thru, send_sems, recv_sems, after)
    return res[:n], res[n:]


def _chip_sum(part, land, name):
    _, r, c = part.shape
    tr = _tile(r, max(16, 2 ** 21 // c), 16)

    def body(core_ref, p_ref, l_ref, o_ref):
        o_ref[...] = (p_ref[...].astype(F32) + l_ref[...].astype(F32)).astype(o_ref.dtype)

    blk = pl.BlockSpec((None, tr, c), lambda q, i, core: (q, i, 0))
    return pl.pallas_call(
        body, name=name, out_shape=jax.ShapeDtypeStruct(land.shape, land.dtype),
        grid_spec=pltpu.PrefetchScalarGridSpec(
            num_scalar_prefetch=1, grid=(NCHIP, r // tr),
            in_specs=[pl.BlockSpec((None, None, tr, c), lambda q, i, core: (q, core[0], i, 0)), blk], out_specs=blk),
        compiler_params=_params(2),
    )(lax.axis_index("c").astype(jnp.int32).reshape(1), part.reshape(NCHIP, 2, r, c), land)


def _scatter_start2(sums, after, name):
    n = len(sums)
    chip = 2 * lax.axis_index("x") + lax.axis_index("y")
    lands = [lax.dynamic_update_slice(lax.empty(s_.shape, s_.dtype), lax.dynamic_index_in_dim(s_, chip, 0, keepdims=True),
                                      (chip,) + (0,) * (s_.ndim - 1)) for s_ in sums]

    def body(*refs):
        sum_refs, land_refs = refs[:n], refs[n:2 * n]
        send_sems, recv_sems, token = refs[2 * n + 1], refs[2 * n + 2], refs[-1]
        my, _, chips = _chip_peers()
        for a in range(n):
            for k, (to, p) in enumerate(chips):
                pltpu.make_async_remote_copy(
                    src_ref=sum_refs[a].at[p // 2], dst_ref=land_refs[a].at[my // 2], send_sem=send_sems.at[a * 3 + k],
                    recv_sem=recv_sems.at[a * 3 + k], device_id=to, device_id_type=MESH).start()
        token[...] = jnp.zeros_like(token)

    hbm = lambda arrs: tuple(pltpu.HBM(a.shape, a.dtype) for a in arrs)
    res = pl.pallas_call(
        body, name=name,
        out_shape=(pltpu.SemaphoreType.DMA((n * 3,)), pltpu.SemaphoreType.DMA((n * 3,))) + hbm(sums) + hbm(lands)
        + (jax.ShapeDtypeStruct((8, GRP), F32),),
        in_specs=(_HBM,) * (2 * n) + (_ANY,),
        out_specs=(_SEM, _SEM) + (_HBM,) * (2 * n) + (pl.BlockSpec(memory_space=pltpu.VMEM),),
        input_output_aliases={k: 2 + k for k in range(2 * n)}, compiler_params=pltpu.CompilerParams(has_side_effects=_EFFECT),
    )(*[pltpu.with_memory_space_constraint(a, pltpu.HBM) for a in list(sums) + lands], after)
    return (n,) + tuple(res[:-1]), res[-1]


def _scatter_wait2(handles, after, name):
    n, send_sems, recv_sems = handles[:3]
    thru = handles[3:]

    def body(*refs):
        sum_refs, land_refs, send_sems, recv_sems = refs[:n], refs[n:2 * n], refs[2 * n], refs[2 * n + 1]
        _, _, chips = _chip_peers()
        for a in range(n):
            for k, (to, p) in enumerate(chips):
                cp = pltpu.make_async_remote_copy(
                    src_ref=sum_refs[a].at[p // 2], dst_ref=land_refs[a].at[p // 2], send_sem=send_sems.at[a * 3 + k],
                    recv_sem=recv_sems.at[a * 3 + k], device_id=to, device_id_type=MESH)
                cp.wait_send()
                cp.wait_recv()

    res = pl.pallas_call(
        body, name=name, out_shape=tuple(pltpu.HBM(a.shape, a.dtype) for a in thru),
        in_specs=(_HBM,) * (2 * n) + (_SEM, _SEM, _ANY), out_specs=(_HBM,) * (2 * n),
        input_output_aliases={k: k for k in range(2 * n)}, compiler_params=pltpu.CompilerParams(has_side_effects=_EFFECT),
    )(*thru, send_sems, recv_sems, after)
    return res[n:]


def _mm(a, b, a_spec, b_spec, out_sds, o_spec, grid, dims, acc, name, after=None):
    extra = () if after is None else (after,)
    assert not acc or out_sds.dtype == F32

    def body(a_ref, b_ref, *rest):
        o_ref = rest[len(extra)]
        if len(b_ref.shape) == 3:
            nkb = b_ref.shape[0]
            wk = a_ref.shape[-1] // nkb
            prod = sum(_dot(a_ref[k] if len(a_ref.shape) == 3 else a_ref[:, k * wk:(k + 1) * wk], b_ref[k], dims)
                       for k in range(nkb))
        else:
            prod = _dot(a_ref[...], b_ref[...], dims)
        if not acc:
            o_ref[...] = prod.astype(o_ref.dtype)
            return
        k = pl.program_id(len(grid) - 1)

        @pl.when(k == 0)
        def _():
            o_ref[...] = prod

        @pl.when(k > 0)
        def _():
            o_ref[...] += prod

    return pl.pallas_call(
        body, name=name, grid=grid, out_shape=out_sds, in_specs=[a_spec, b_spec] + [_ANY] * len(extra), out_specs=o_spec,
        compiler_params=_params(len(grid)),
    )(a, b, *extra)


def _row_spec(tm, d):
    return pl.BlockSpec((tm, d), lambda i: (i, 0))


def _vec_spec(d):
    return pl.BlockSpec((1, d), lambda i: (0, 0))


def _prenorm_fwd(x, g, scale, shift, name):
    s, d = x.shape
    tm = _tile(s, 256, 16)

    def body(x_ref, g_ref, sc_ref, sh_ref, h_ref):
        xv = x_ref[...]
        rstd = lax.rsqrt(jnp.mean(xv * xv, axis=-1, keepdims=True) + EPS)
        h_ref[...] = ((xv * rstd * g_ref[...]) * (1.0 + sc_ref[...]) + sh_ref[...]).astype(BF16)

    return pl.pallas_call(
        body, name=name, grid=(s // tm,), out_shape=jax.ShapeDtypeStruct((s, d), BF16),
        in_specs=[_row_spec(tm, d), _vec_spec(d), _vec_spec(d), _vec_spec(d)], out_specs=_row_spec(tm, d),
        compiler_params=_params(1),
    )(x, g, scale, shift)


def _residual_prenorm(x, gate, p, g, scale, shift, name):
    s, d = x.shape
    tm = _tile(s, 256, 16)

    def body(x_ref, gate_ref, p_ref, g_ref, sc_ref, sh_ref, x1_ref, h_ref):
        xv = x_ref[...] + gate_ref[...] * p_ref[...]
        x1_ref[...] = xv
        rstd = lax.rsqrt(jnp.mean(xv * xv, axis=-1, keepdims=True) + EPS)
        h_ref[...] = ((xv * rstd * g_ref[...]) * (1.0 + sc_ref[...]) + sh_ref[...]).astype(BF16)

    return pl.pallas_call(
        body, name=name, grid=(s // tm,),
        out_shape=(jax.ShapeDtypeStruct((s, d), F32), jax.ShapeDtypeStruct((s, d), BF16)),
        in_specs=[_row_spec(tm, d), _vec_spec(d), _row_spec(tm, d), _vec_spec(d), _vec_spec(d), _vec_spec(d)],
        out_specs=(_row_spec(tm, d), _row_spec(tm, d)), compiler_params=_params(1),
    )(x, gate, p, g, scale, shift)


def _final_loss(x1, f, final_g, target, gate2):
    s, d = x1.shape
    tm = _tile(s, 256, 16)

    def body(x_ref, f_ref, g_ref, t_ref, gate_ref, dx_ref, df_ref, st_ref):
        @pl.when(pl.program_id(0) == 0)
        def _():
            st_ref[...] = jnp.zeros_like(st_ref)

        xv, gf = x_ref[...] + gate_ref[...] * f_ref[...], g_ref[...]
        rstd = lax.rsqrt(jnp.mean(xv * xv, axis=-1, keepdims=True) + EPS)
        xhat = xv * rstd
        err = xhat * gf - t_ref[...]
        dy = err * (1.0 / d)
        gdy = dy * gf
        dx = rstd * (gdy - xhat * jnp.mean(gdy * xhat, axis=-1, keepdims=True))
        dx_ref[...] = dx
        df_ref[...] = (gate_ref[...] * dx).astype(BF16)
        st_ref[0:1, :] += jnp.sum(dy * xhat, axis=0, keepdims=True)
        st_ref[1:2, :] += jnp.sum(err * err, axis=0, keepdims=True)

    return pl.pallas_call(
        body, name="final_loss", grid=(s // tm,),
        out_shape=(jax.ShapeDtypeStruct((s, d), F32), jax.ShapeDtypeStruct((s, d), BF16),
                   jax.ShapeDtypeStruct((8, d), F32)),
        in_specs=[_row_spec(tm, d), _row_spec(tm, d), _vec_spec(d), _row_spec(tm, d), _vec_spec(d)],
        out_specs=(_row_spec(tm, d), _row_spec(tm, d), pl.BlockSpec((8, d), lambda i: (0, 0))),
        compiler_params=_params(1),
    )(x1, f, final_g, target, gate2)


def _prenorm_bwd(xin, dh, dres, pf, g, scale, gate_next, name, after):
    s, d = xin.shape
    tm = _tile(s, 256, 16)

    def body(x_ref, dh_ref, dr_ref, pf_ref, g_ref, sc_ref, gn_ref, after_ref, dx_ref, dn_ref, st_ref):
        @pl.when(pl.program_id(0) == 0)
        def _():
            st_ref[...] = jnp.zeros_like(st_ref)

        xv, dhv, drv, gv = x_ref[...], dh_ref[...], dr_ref[...], g_ref[...]
        one_sc = 1.0 + sc_ref[...]
        rstd = lax.rsqrt(jnp.mean(xv * xv, axis=-1, keepdims=True) + EPS)
        xhat = xv * rstd
        dxhat = dhv * (gv * one_sc)
        dx = drv + rstd * (dxhat - xhat * jnp.mean(dxhat * xhat, axis=-1, keepdims=True))
        dx_ref[...] = dx
        dn_ref[...] = (gn_ref[...] * dx).astype(BF16)
        dhx = dhv * xhat
        st_ref[0:1, :] += jnp.sum(dhv, axis=0, keepdims=True)
        st_ref[1:2, :] += jnp.sum(dhx, axis=0, keepdims=True) * gv
        st_ref[2:3, :] += jnp.sum(dhx, axis=0, keepdims=True) * one_sc
        st_ref[3:4, :] += jnp.sum(drv * pf_ref[...], axis=0, keepdims=True)

    return pl.pallas_call(
        body, name=name, grid=(s // tm,),
        out_shape=(jax.ShapeDtypeStruct((s, d), F32), jax.ShapeDtypeStruct((s, d), BF16),
                   jax.ShapeDtypeStruct((8, d), F32)),
        in_specs=[_row_spec(tm, d)] * 4 + [_vec_spec(d)] * 3 + [_ANY],
        out_specs=(_row_spec(tm, d), _row_spec(tm, d), pl.BlockSpec((8, d), lambda i: (0, 0))),
        compiler_params=_params(1),
    )(xin, dh, dres, pf, g, scale, gate_next, after)


def _ada_fwd(c_all, w_loc):
    nb, d = c_all.shape
    n = w_loc.shape[1]
    tn = _tile(n, 512, 128) if n % 128 == 0 else n

    def body(c_ref, w_ref, o_ref):
        cv = c_ref[...]
        o_ref[...] = jnp.dot(cv * jax.nn.sigmoid(cv), w_ref[...], preferred_element_type=F32,
                             precision=lax.Precision.HIGHEST)

    return pl.pallas_call(
        body, name="ada_fwd", grid=(n // tn,), out_shape=jax.ShapeDtypeStruct((nb, n), F32),
        in_specs=[pl.BlockSpec((nb, d), lambda j: (0, 0)), pl.BlockSpec((d, tn), lambda j: (0, j))],
        out_specs=pl.BlockSpec((nb, tn), lambda j: (0, j)), compiler_params=_params(1),
    )(c_all, w_loc)


def _ada_bwd(c_all, dmod_cols):
    nb, d = c_all.shape
    n = dmod_cols.shape[1]
    tn = _tile(n, 512, 128) if n % 128 == 0 else n

    def body(c_ref, dm_ref, o_ref):
        cv = c_ref[...]
        o_ref[...] = lax.dot_general(cv * jax.nn.sigmoid(cv), dm_ref[...], TN, preferred_element_type=F32,
                                     precision=lax.Precision.HIGHEST)

    return pl.pallas_call(
        body, name="ada_bwd", grid=(n // tn,), out_shape=jax.ShapeDtypeStruct((d, n), F32),
        in_specs=[pl.BlockSpec((nb, d), lambda j: (0, 0)), pl.BlockSpec((nb, tn), lambda j: (0, j))],
        out_specs=pl.BlockSpec((d, tn), lambda j: (0, j)), compiler_params=_params(1),
    )(c_all, dmod_cols)


_INV_SQRT2 = 0.7071067811865476
_INV_SQRT2PI = 0.3989422804014327


def _gelu_parts(x):
    cdf = 0.5 * (1.0 + lax.erf(x * _INV_SQRT2))
    return x * cdf, cdf + x * jnp.exp(-0.5 * x * x) * _INV_SQRT2PI


def _gm_group_fwd(up, vp, gv, wt, bcol):
    u = 0.5 * up * (1.0 + lax.erf(up * _INV_SQRT2))
    va = 0.5 * vp * (1.0 + lax.erf(vp * _INV_SQRT2))
    return _gm_group_mix(u, va, gv, wt, bcol)


def _gm_group_mix(u, va, gv, wt, bcol):
    xc = va - jnp.mean(va, axis=-1, keepdims=True)
    rstd_v = lax.rsqrt(jnp.mean(xc * xc, axis=-1, keepdims=True) + EPS)
    yv = xc * rstd_v
    vn = (yv * gv).astype(BF16)
    mixed = _dot(wt, vn, NN) + bcol
    return u, rstd_v, yv, vn, mixed, u * mixed


def _tril_bf16(w):
    row = lax.broadcasted_iota(jnp.int32, w.shape, 0)
    col = lax.broadcasted_iota(jnp.int32, w.shape, 1)
    return jnp.where(col <= row, w, 0.0).astype(BF16)


def _gmlp_fwd(proj, v_norm_g, w_spatial, b_cols, gamma, gm, after):
    s = proj.shape[0]
    ng = gm // GRP

    def body(p_ref, gv_ref, w_ref, b_ref, gam_ref, after_ref, o_ref):
        for g in range(ng):
            lo = g * GRP
            wt = _tril_bf16(w_ref[g])
            *_, o = _gm_group_fwd(p_ref[:, lo:lo + GRP], p_ref[:, gm + lo:gm + lo + GRP], gv_ref[:, lo:lo + GRP],
                                  wt, b_ref[:, g:g + 1])
            rstd_o = lax.rsqrt(jnp.mean(o * o, axis=-1, keepdims=True) + EPS)
            o_ref[:, lo:lo + GRP] = (o * rstd_o * gam_ref[:, lo:lo + GRP]).astype(BF16)

    return pl.pallas_call(
        body, name="gmlp_fwd", grid=(s // GRP,), out_shape=jax.ShapeDtypeStruct((s, gm), BF16),
        in_specs=[pl.BlockSpec((GRP, 2 * gm), lambda n: (n, 0)), _vec_spec(gm),
                  pl.BlockSpec((ng, GRP, GRP), lambda n: (0, 0, 0)), pl.BlockSpec((GRP, GRP), lambda n: (0, 0)),
                  _vec_spec(gm), _ANY],
        out_specs=pl.BlockSpec((GRP, gm), lambda n: (n, 0)), compiler_params=_params(1),
    )(proj, v_norm_g, w_spatial, b_cols, gamma, after)


def _gmlp_bwd(proj, don, v_norm_g, w_spatial, b_cols, gamma, gm, after):
    s = proj.shape[0]
    ng = gm // GRP

    def body(p_ref, don_ref, gv_ref, w_ref, b_ref, gam_ref, after_ref, dp_ref, dw_ref, db_ref, dgv_ref, dgam_ref):
        @pl.when(pl.program_id(0) == 0)
        def _():
            dw_ref[...] = jnp.zeros_like(dw_ref)
            db_ref[...] = jnp.zeros_like(db_ref)
            dgv_ref[...] = jnp.zeros_like(dgv_ref)
            dgam_ref[...] = jnp.zeros_like(dgam_ref)

        lane = lax.broadcasted_iota(jnp.int32, (GRP, GRP), 1)
        row = lax.broadcasted_iota(jnp.int32, (GRP, GRP), 0)
        for g in range(ng):
            lo = g * GRP
            up, vp = p_ref[:, lo:lo + GRP], p_ref[:, gm + lo:gm + lo + GRP]
            gv, gam = gv_ref[:, lo:lo + GRP], gam_ref[:, lo:lo + GRP]
            wt = _tril_bf16(w_ref[g])
            (u, du_dup), (va, dva_dvp) = _gelu_parts(up), _gelu_parts(vp)
            u, rstd_v, yv, vn, mixed, o = _gm_group_mix(u, va, gv, wt, b_ref[:, g:g + 1])
            rstd_o = lax.rsqrt(jnp.mean(o * o, axis=-1, keepdims=True) + EPS)
            ohat = o * rstd_o
            dn = don_ref[:, lo:lo + GRP]
            dgam_ref[:, lo:lo + GRP] += jnp.sum(dn * ohat, axis=0, keepdims=True)
            dohat = dn * gam
            do = rstd_o * (dohat - ohat * jnp.mean(dohat * ohat, axis=-1, keepdims=True))
            du = do * mixed
            dmixed = do * u
            dmb = dmixed.astype(BF16)
            db_ref[...] += jnp.where(lane == g, jnp.sum(dmixed, axis=-1, keepdims=True), 0.0)
            dw_ref[g] += jnp.where(lane <= row, _dot(dmb, vn, NT), 0.0)
            dvn = _dot(wt, dmb, TN)
            dgv_ref[:, lo:lo + GRP] += jnp.sum(dvn * yv, axis=0, keepdims=True)
            dyv = dvn * gv
            dva = rstd_v * (dyv - jnp.mean(dyv, axis=-1, keepdims=True)
                            - yv * jnp.mean(dyv * yv, axis=-1, keepdims=True))
            dp_ref[:, lo:lo + GRP] = (du * du_dup).astype(BF16)
            dp_ref[:, gm + lo:gm + lo + GRP] = (dva * dva_dvp).astype(BF16)

    const2 = lambda n: (0, 0)
    return pl.pallas_call(
        body, name="gmlp_bwd", grid=(s // GRP,),
        out_shape=(jax.ShapeDtypeStruct((s, 2 * gm), BF16), jax.ShapeDtypeStruct((ng, GRP, GRP), F32),
                   jax.ShapeDtypeStruct((GRP, GRP), F32), jax.ShapeDtypeStruct((1, gm), F32),
                   jax.ShapeDtypeStruct((1, gm), F32)),
        in_specs=[pl.BlockSpec((GRP, 2 * gm), lambda n: (n, 0)), pl.BlockSpec((GRP, gm), lambda n: (n, 0)),
                  _vec_spec(gm), pl.BlockSpec((ng, GRP, GRP), lambda n: (0, 0, 0)),
                  pl.BlockSpec((GRP, GRP), const2), _vec_spec(gm), _ANY],
        out_specs=(pl.BlockSpec((GRP, 2 * gm), lambda n: (n, 0)), pl.BlockSpec((ng, GRP, GRP), lambda n: (0, 0, 0)),
                   pl.BlockSpec((GRP, GRP), const2), _vec_spec(gm), _vec_spec(gm)),
        compiler_params=_params(1),
    )(proj, don, v_norm_g, w_spatial, b_cols, gamma, after)


BLK = 512


def _log_sigmoid(z):
    return jnp.minimum(z, 0.0) - jnp.log(1.0 + jnp.exp(-jnp.abs(z)))


def _split_dot(x, tri, passes):
    n = x.shape[0]
    parts, rest = [], x
    for _ in range(passes):
        hi = rest.astype(BF16)
        parts.append(hi)
        rest = rest - hi.astype(F32)
    res = _dot(jnp.concatenate(parts, axis=0), tri, NN)
    out = res[0:n]
    for k in range(1, passes):
        out = out + res[k * n:(k + 1) * n]
    return out


HEADS_PER_STEP = 2


def _sb_fwd(proj, gamma, gm, sb, after):
    s = proj.shape[0]
    hp = min(HEADS_PER_STEP, sb // GRP)
    w = hp * GRP
    nhp, nq = sb // w, s // BLK
    qc, kc, vc, gc = 2 * gm // w, (2 * gm + sb) // w, (2 * gm + 2 * sb) // w, gm // w
    scale = GRP ** -0.5

    def body(q_ref, k_ref, v_ref, gam_ref, after_ref, on_ref, o_ref, a_ref, s_ref, kb, vb):
        i = pl.program_id(1)

        @pl.when(i == 0)
        def _():
            kb[...] = k_ref[...].astype(BF16)
            vb[...] = v_ref[...].astype(BF16)

        qb = q_ref[...].astype(BF16)
        row = lax.broadcasted_iota(jnp.int32, (BLK, BLK), 0)
        col = lax.broadcasted_iota(jnp.int32, (BLK, BLK), 1)
        later = (row > col).astype(BF16)

        def block(j, carry, diag):
            off = pl.multiple_of(j * BLK, BLK)
            kj, vj = kb[pl.ds(off, BLK), :], vb[pl.ds(off, BLK), :]
            out = []
            for h in range(hp):
                tail, acc = carry[h]
                sl = slice(h * GRP, (h + 1) * GRP)
                z = _dot(qb[:, sl], kj[:, sl], NT) * scale
                lb = _log_sigmoid(z)
                l1m = lb - z
                sig = jnp.exp(lb)
                if diag:
                    l1m = jnp.where(col < row, l1m, 0.0)
                    sig = jnp.where(col < row, sig, 0.0)
                after_s = _split_dot(l1m, later, 2)
                a = jnp.exp(lb + after_s + tail)
                if diag:
                    a = jnp.where(col < row, a, 0.0)
                ab = a.astype(BF16)
                a_ref[h, j] = ab
                s_ref[h, j] = sig.astype(BF16)
                out.append((tail + after_s[:, 0:1] + l1m[:, 0:1], acc + _dot(ab, vj[:, sl], NN)))
            return tuple(out)

        init = tuple((jnp.zeros((BLK, 1), F32), jnp.zeros((BLK, GRP), F32)) for _ in range(hp))
        carry = block(i, init, True)
        carry = lax.fori_loop(0, i, lambda jj, c: block(i - 1 - jj, c, False), carry)
        for h in range(hp):
            _, acc = carry[h]
            sl = slice(h * GRP, (h + 1) * GRP)
            rstd = lax.rsqrt(jnp.mean(acc * acc, axis=-1, keepdims=True) + EPS)
            on_ref[:, sl] = (acc * rstd * gam_ref[:, sl]).astype(BF16)
            o_ref[:, sl] = acc

    saved = jax.ShapeDtypeStruct((sb // GRP, nq, nq, BLK, BLK), BF16)
    saved_spec = pl.BlockSpec((hp, None, nq, BLK, BLK), lambda h, i: (h, i, 0, 0, 0))
    return pl.pallas_call(
        body, name="sb_fwd", grid=(nhp, nq),
        out_shape=(jax.ShapeDtypeStruct((s, sb), BF16), jax.ShapeDtypeStruct((s, sb), F32), saved, saved),
        in_specs=[pl.BlockSpec((BLK, w), lambda h, i: (i, qc + h)), pl.BlockSpec((s, w), lambda h, i: (0, kc + h)),
                  pl.BlockSpec((s, w), lambda h, i: (0, vc + h)), pl.BlockSpec((1, w), lambda h, i: (0, gc + h)), _ANY],
        out_specs=(pl.BlockSpec((BLK, w), lambda h, i: (i, h)), pl.BlockSpec((BLK, w), lambda h, i: (i, h)),
                   saved_spec, saved_spec),
        scratch_shapes=[pltpu.VMEM((s, w), BF16), pltpu.VMEM((s, w), BF16)],
        compiler_params=_params(2),
    )(proj, proj, proj, gamma, after)


def _sb_bwd(proj, o_raw, a_sv, s_sv, don, gamma, gm, sb, after):
    s = proj.shape[0]
    hp = min(HEADS_PER_STEP, sb // GRP)
    w = hp * GRP
    nhp, nq = sb // w, s // BLK
    qc, kc, vc, gc = 2 * gm // w, (2 * gm + sb) // w, (2 * gm + 2 * sb) // w, gm // w
    scale = GRP ** -0.5

    def body(q_ref, k_ref, v_ref, o_ref, a_ref, s_ref, don_ref, gam_ref, after_ref, dq_ref, dk_ref, dv_ref, dgam_ref,
             kb, vb, dkt_acc, dvt_acc):
        i = pl.program_id(1)

        @pl.when(i == 0)
        def _():
            kb[...] = k_ref[...].astype(BF16)
            vb[...] = v_ref[...].astype(BF16)
            dkt_acc[...] = jnp.zeros_like(dkt_acc)
            dvt_acc[...] = jnp.zeros_like(dvt_acc)
            dgam_ref[...] = jnp.zeros_like(dgam_ref)

        dobs = []
        for h in range(hp):
            sl = slice(h * GRP, (h + 1) * GRP)
            o, dn = o_ref[:, sl], don_ref[:, sl]
            rstd = lax.rsqrt(jnp.mean(o * o, axis=-1, keepdims=True) + EPS)
            ohat = o * rstd
            dgam_ref[:, sl] += jnp.sum(dn * ohat, axis=0, keepdims=True)
            dohat = dn * gam_ref[:, sl]
            dobs.append((rstd * (dohat - ohat * jnp.mean(dohat * ohat, axis=-1, keepdims=True))).astype(BF16))

        qb = q_ref[...].astype(BF16)
        qts = [qb[:, h * GRP:(h + 1) * GRP].T for h in range(hp)]
        dots = [dob.T for dob in dobs]
        row = lax.broadcasted_iota(jnp.int32, (BLK, BLK), 0)
        col = lax.broadcasted_iota(jnp.int32, (BLK, BLK), 1)
        before = (row < col).astype(BF16)

        def block(j, carry):
            off = pl.multiple_of(j * BLK, BLK)
            kj, vj = kb[pl.ds(off, BLK), :], vb[pl.ds(off, BLK), :]
            out = []
            for h in range(hp):
                e_pre, dq = carry[h]
                sl = slice(h * GRP, (h + 1) * GRP)
                qh, kh, dob = qb[:, sl], kj[:, sl], dobs[h]
                ab = a_ref[h, j]
                sig = s_ref[h, j].astype(F32)
                de = ab.astype(F32) * _dot(dob, vj[:, sl], NT)
                dvt_acc[sl, pl.ds(off, BLK)] += _dot(dots[h], ab, NN)
                before_s = _dot(de.astype(BF16), before, NN)
                dzb = ((de * (1.0 - sig) - (e_pre + before_s) * sig) * scale).astype(BF16)
                dkt_acc[sl, pl.ds(off, BLK)] += _dot(qts[h], dzb, NN)
                out.append((e_pre + before_s[:, BLK - 1:BLK] + de[:, BLK - 1:BLK], dq + _dot(dzb, kh, NN)))
            return tuple(out)

        carry = tuple((jnp.zeros((BLK, 1), F32), jnp.zeros((BLK, GRP), F32)) for _ in range(hp))
        carry = lax.fori_loop(0, i + 1, block, carry)
        for h in range(hp):
            dq_ref[:, h * GRP:(h + 1) * GRP] = carry[h][1].astype(BF16)

        @pl.when(i == nq - 1)
        def _():
            dk_ref[...] = dkt_acc[...].T.astype(BF16)
            dv_ref[...] = dvt_acc[...].T.astype(BF16)

    blk_q = lambda h, i: (i, h)
    whole = lambda h, i: (0, h)
    saved_spec = pl.BlockSpec((hp, None, nq, BLK, BLK), lambda h, i: (h, i, 0, 0, 0))
    return pl.pallas_call(
        body, name="sb_bwd", grid=(nhp, nq),
        out_shape=(jax.ShapeDtypeStruct((s, sb), BF16),) * 3 + (jax.ShapeDtypeStruct((1, sb), F32),),
        in_specs=[pl.BlockSpec((BLK, w), lambda h, i: (i, qc + h)), pl.BlockSpec((s, w), lambda h, i: (0, kc + h)),
                  pl.BlockSpec((s, w), lambda h, i: (0, vc + h)), pl.BlockSpec((BLK, w), blk_q), saved_spec, saved_spec,
                  pl.BlockSpec((BLK, w), lambda h, i: (i, gc + h)),
                  pl.BlockSpec((1, w), lambda h, i: (0, gc + h)), _ANY],
        out_specs=(pl.BlockSpec((BLK, w), blk_q), pl.BlockSpec((s, w), whole), pl.BlockSpec((s, w), whole),
                   pl.BlockSpec((1, w), whole)),
        scratch_shapes=[pltpu.VMEM((s, w), BF16), pltpu.VMEM((s, w), BF16),
                        pltpu.VMEM((w, s), F32), pltpu.VMEM((w, s), F32)],
        compiler_params=_params(2),
    )(proj, proj, proj, o_raw, a_sv, s_sv, don, gamma, after)


FFN_ROW_CHUNKS = 2


def _row_chunks(tm):
    n = FFN_ROW_CHUNKS if tm % (16 * FFN_ROW_CHUNKS) == 0 else 1
    return [slice(k * (tm // n), (k + 1) * (tm // n)) for k in range(n)]


def _ffn_up(h2, wg, wu):
    s, d = h2.shape
    nb, fb, _ = wg.shape
    tm = _tile(s, 1024, 16)

    def body(h_ref, wg_ref, wu_ref, g_ref, u_ref, a_ref):
        for rows_ in _row_chunks(tm):
            hv = h_ref[rows_, :]
            g = _dot(hv, wg_ref[...], NT)
            u = _dot(hv, wu_ref[...], NT)
            g_ref[rows_, :] = g.astype(BF16)
            u_ref[rows_, :] = u.astype(BF16)
            a_ref[rows_, :] = (g * jax.nn.sigmoid(g) * u).astype(BF16)

    rows = pl.BlockSpec((tm, d), lambda i, j: (i, 0))
    wblk = pl.BlockSpec((None, fb, d), lambda i, j: (j, 0, 0))
    hid = pl.BlockSpec((None, tm, fb), lambda i, j: (j, i, 0))
    hid_sds = jax.ShapeDtypeStruct((nb, s, fb), BF16)
    return pl.pallas_call(
        body, name="ffn_up", grid=(s // tm, nb), out_shape=(hid_sds,) * 3, in_specs=[rows, wblk, wblk],
        out_specs=(hid,) * 3, compiler_params=_params(2),
    )(h2, wg, wu)


def _ffn_bwd_act(df, g_pre, u_pre, wd):
    s, d = df.shape
    nb, fb, _ = wd.shape
    tm = _tile(s, 1024, 16)

    def body(df_ref, g_ref, u_ref, wd_ref, dg_ref, du_ref):
        for rows_ in _row_chunks(tm):
            g, u = g_ref[rows_, :].astype(F32), u_ref[rows_, :].astype(F32)
            da = _dot(df_ref[rows_, :], wd_ref[...], NT)
            sg = jax.nn.sigmoid(g)
            dg_ref[rows_, :] = (da * u * (sg * (1.0 + g * (1.0 - sg)))).astype(BF16)
            du_ref[rows_, :] = (da * (g * sg)).astype(BF16)

    rows = pl.BlockSpec((tm, d), lambda i, j: (i, 0))
    wblk = pl.BlockSpec((None, fb, d), lambda i, j: (j, 0, 0))
    hid = pl.BlockSpec((None, tm, fb), lambda i, j: (j, i, 0))
    hid_sds = jax.ShapeDtypeStruct((nb, s, fb), BF16)
    return pl.pallas_call(
        body, name="ffn_bwd_act", grid=(s // tm, nb), out_shape=(hid_sds, hid_sds),
        in_specs=[rows, hid, hid, wblk], out_specs=(hid, hid), compiler_params=_params(2),
    )(df, g_pre, u_pre, wd)


def _ffn_bwd_in(dg, du, wg, wu):
    nb, s, fb = dg.shape
    d = wg.shape[2]
    tm = _tile(s, 1024, 16)

    def body(dg_ref, du_ref, wg_ref, wu_ref, dh_ref):
        j = pl.program_id(1)
        parts = [_dot(dg_ref[rows_, :], wg_ref[...], NN) + _dot(du_ref[rows_, :], wu_ref[...], NN)
                 for rows_ in _row_chunks(tm)]

        @pl.when(j == 0)
        def _():
            for rows_, part in zip(_row_chunks(tm), parts):
                dh_ref[rows_, :] = part

        @pl.when(j > 0)
        def _():
            for rows_, part in zip(_row_chunks(tm), parts):
                dh_ref[rows_, :] += part

    wblk = pl.BlockSpec((None, fb, d), lambda i, j: (j, 0, 0))
    hid = pl.BlockSpec((None, tm, fb), lambda i, j: (j, i, 0))
    return pl.pallas_call(
        body, name="ffn_bwd_in", grid=(s // tm, nb), out_shape=jax.ShapeDtypeStruct((s, d), F32),
        in_specs=[hid, hid, wblk, wblk], out_specs=pl.BlockSpec((tm, d), lambda i, j: (i, 0)),
        compiler_params=_params(2),
    )(dg, du, wg, wu)


def _reduce_adamw(parts, w, m, v, name, after):
    npart, r, c = parts.shape
    tr = _tile(r, max(16, 524288 // c), 16)
    c1, c2 = 1.0 - ADAM_B1 ** ADAM_STEP, 1.0 - ADAM_B2 ** ADAM_STEP

    def body(p_ref, w_ref, m_ref, v_ref, after_ref, g_ref, d_ref, nm_ref, nv_ref):
        g = p_ref[0].astype(F32)
        for k in range(1, npart):
            g = g + p_ref[k].astype(F32)
        nm = ADAM_B1 * m_ref[...] + (1.0 - ADAM_B1) * g
        nv = ADAM_B2 * v_ref[...] + (1.0 - ADAM_B2) * (g * g)
        g_ref[...] = g
        nm_ref[...] = nm
        nv_ref[...] = nv
        d_ref[...] = -ADAM_LR * ((nm / c1) / (jnp.sqrt(nv / c2) + ADAM_EPS) + ADAM_WD * w_ref[...])

    blk = pl.BlockSpec((tr, c), lambda i: (i, 0))
    sds = jax.ShapeDtypeStruct((r, c), F32)
    return pl.pallas_call(
        body, name=name, grid=(r // tr,), out_shape=(sds,) * 4,
        in_specs=[pl.BlockSpec((npart, tr, c), lambda i: (0, i, 0)), blk, blk, blk, _ANY], out_specs=(blk,) * 4,
        compiler_params=_params(1),
    )(parts, w, m, v, after)


def _pack(vecs):
    rows = jnp.concatenate([a.reshape(-1, GRP) for a in vecs], axis=0)
    pad = -rows.shape[0] % 64
    return jnp.pad(rows, ((0, pad), (0, 0)))


def _unpack(rows, shapes):
    out, at = [], 0
    for shp in shapes:
        n = 1
        for k in shp:
            n *= k
        out.append(rows[at:at + n // GRP].reshape(shp))
        at += n // GRP
    return out


def kernel(x, c, w_ada, b_ada, norm1_g, w_in, v_norm_g, w_spatial, b_spatial, out_norm_g, w_out, norm2_g, w_gate, w_up, w_down, final_g, loss_target, m_w_ada, m_b_ada, m_norm1_g, m_w_in, m_v_norm_g, m_w_spatial, m_b_spatial, m_out_norm_g, m_w_out, m_norm2_g, m_w_gate, m_w_up, m_w_down, m_final_g, v_w_ada, v_b_ada, v_norm1_g, v_w_in, v_v_norm_g, v_w_spatial, v_b_spatial, v_out_norm_g, v_w_out, v_norm2_g, v_w_gate, v_w_up, v_w_down, v_final_g):
    s, d = x.shape[1], x.shape[2]
    gm = v_norm_g.shape[1]
    sb = d - gm
    n_in, ffb, ob = w_in.shape[2], w_gate.shape[2], w_out.shape[1]
    xs, tgt = x[0], loss_target[0]
    me = 4 * lax.axis_index("x") + 2 * lax.axis_index("y") + lax.axis_index("c")

    c_all = _exchange(c, False, "gather_c")[:, 0, :]
    mod_cols = _ada_fwd(c_all, w_ada[0])
    mod = _exchange(mod_cols[:, None, :], True, "scatter_mod").reshape(1, 6 * d) + b_ada
    shift1, scale1, gate1, shift2, scale2, gate2 = [mod[:, k * d:(k + 1) * d] for k in range(6)]

    tok, gathers = mod, {}
    tr_ = lambda a: jnp.swapaxes(a, 1, 2)
    w_gate, m_w_gate, v_w_gate, w_up, m_w_up, v_w_up = map(tr_, (w_gate, m_w_gate, v_w_gate, w_up, m_w_up, v_w_up))
    big_w = (("w_in", w_in), ("w_out", w_out), ("w_gate", w_gate), ("w_up", w_up), ("w_down", w_down))
    first, tok = _gather_start([w_in[0].astype(BF16)], tok, "gather_start_w_in")
    rest, tok = _gather_start([w[0].astype(BF16) for _, w in big_w[1:]], tok, "gather_start_weights")
    gathers = {nm: h for (nm, _), h in zip(big_w, first + rest)}
    shift1 = shift1 + tok[0:1, 0:1]

    tm = _tile(s, 512, 16)
    nt = s // tm
    h1 = _prenorm_fwd(xs, norm1_g, scale1, shift1, "prenorm1")
    w_in_all = _gather_wait(_gather_forward(gathers["w_in"], h1, "gather_fwd_w_in")[0], h1, "gather_wait_w_in")
    proj = _mm(h1, w_in_all, pl.BlockSpec((s, d), lambda j: (0, 0)), pl.BlockSpec((None, d, n_in), lambda j: (j, 0, 0)),
               jax.ShapeDtypeStruct((s, NDEV * n_in), F32), pl.BlockSpec((s, n_in), lambda j: (0, j)),
               (NDEV,), NN, False, "proj")
    fwd_w_out, tok = _gather_forward(gathers["w_out"], proj, "gather_fwd_w_out")
    b_cols = jnp.pad(b_spatial[0].T, ((0, 0), (0, GRP - b_spatial.shape[1])))
    on_sb, o_sb, a_sv, s_sv = _sb_fwd(proj, out_norm_g, gm, sb, tok)
    on_gm = _gmlp_fwd(proj, v_norm_g, w_spatial[0], b_cols, out_norm_g, gm, on_sb)
    fwd_w_gate, tok = _gather_forward(gathers["w_gate"], on_gm, "gather_fwd_w_gate")
    o_n = jnp.concatenate([on_gm, on_sb], axis=1)
    w_out_all = _gather_wait(fwd_w_out, tok, "gather_wait_w_out").reshape(d, d)
    rows_1 = pl.BlockSpec((tm, d), lambda i: (i, 0))
    whole_1 = pl.BlockSpec((d, d), lambda i: (0, 0))
    p_out = _mm(o_n, w_out_all, rows_1, whole_1, jax.ShapeDtypeStruct((s, d), F32), rows_1, (nt,), NN, False,
                "out_proj")
    fwd_w_up, tok = _gather_forward(gathers["w_up"], p_out, "gather_fwd_w_up")
    scale2 = scale2 + tok[0:1, 0:1]
    x1, h2 = _residual_prenorm(xs, gate1, p_out, norm2_g, scale2, shift2, "residual1_prenorm2")
    w_gate_all = _gather_wait(fwd_w_gate, h2, "gather_wait_w_gate")
    w_up_all = _gather_wait(fwd_w_up, h2, "gather_wait_w_up")
    g_pre, u_pre, act = _ffn_up(h2, w_gate_all, w_up_all)
    fwd_w_down, tok = _gather_forward(gathers["w_down"], act, "gather_fwd_w_down")
    w_down_all = _gather_wait(fwd_w_down, tok, "gather_wait_w_down")
    tm2 = _tile(s, 1024, 16)
    f_out = _mm(act, w_down_all, pl.BlockSpec((2, tm2, ffb), lambda i, j: (j, i, 0)),
                pl.BlockSpec((2, ffb, d), lambda i, j: (j, 0, 0)), jax.ShapeDtypeStruct((s, d), F32),
                pl.BlockSpec((tm2, d), lambda i, j: (i, 0)), (s // tm2, NDEV // 2), NN, True, "ffn_down")

    dx2, df, st_f = _final_loss(x1, f_out, final_g.reshape(1, d), tgt, gate2)
    dg, du = _ffn_bwd_act(df, g_pre, u_pre, w_down_all)
    dh2 = _ffn_bwd_in(dg, du, w_gate_all, w_up_all)
    hid = pl.BlockSpec((None, s, ffb), lambda j: (j, 0, 0))
    all_d = pl.BlockSpec((s, d), lambda j: (0, 0))

    def second_leg(first, after, tag):
        parts, lands = _scatter_wait1(first, after, "scatter_wait1_" + tag)
        sums = [_chip_sum(p, l, "chip_sum_%s_%d" % (tag, k)) for k, (p, l) in enumerate(zip(parts, lands))]
        return _scatter_start2(sums, after, "scatter_start2_" + tag)

    row_sds = jax.ShapeDtypeStruct((NDEV, ffb, d), BF16)
    row_out = pl.BlockSpec((None, ffb, d), lambda j: (j, 0, 0))
    gw_gate = _mm(dg, h2, hid, all_d, row_sds, row_out, (NDEV,), TN, False, "grad_w_gate")
    gw_up = _mm(du, h2, hid, all_d, row_sds, row_out, (NDEV,), TN, False, "grad_w_up")
    gw_down = _mm(act, df, hid, all_d, row_sds, row_out, (NDEV,), TN, False, "grad_w_down")
    dx1, dp, st2 = _prenorm_bwd(x1, dh2, dx2, f_out, norm2_g, scale2, gate1, "prenorm2_bwd", gw_down)
    first_ffn, tok = _scatter_start1([gw_gate, gw_up, gw_down], dp, "scatter_start1_ffn")
    don = _mm(dp, w_out_all, rows_1, whole_1, jax.ShapeDtypeStruct((s, d), F32), rows_1, (nt,), NT, False,
              "out_proj_bwd", tok)
    gw_out = _mm(o_n, dp, pl.BlockSpec((s, ob), lambda j: (0, j)), all_d,
                 jax.ShapeDtypeStruct((NDEV, ob, d), BF16), pl.BlockSpec((None, ob, d), lambda j: (j, 0, 0)),
                 (NDEV,), TN, False, "grad_w_out", don)
    first_out, tok = _scatter_start1([gw_out], tok, "scatter_start1_out")
    second_ffn, tok = second_leg(first_ffn, tok, "ffn")
    dproj_gm, dw_sp, db_cols, dgv, dgam_gm = _gmlp_bwd(proj, don, v_norm_g, w_spatial[0], b_cols, out_norm_g, gm, tok)
    second_out, tok = second_leg(first_out, dproj_gm, "out")
    dq, dk, dv, dgam_sb = _sb_bwd(proj, o_sb, a_sv, s_sv, don, out_norm_g, gm, sb, tok)
    dproj = jnp.concatenate([dproj_gm, dq, dk, dv], axis=1)
    gw_in = _mm(h1, dproj, all_d, pl.BlockSpec((s, n_in), lambda j: (0, j)),
                jax.ShapeDtypeStruct((NDEV, d, n_in), BF16), pl.BlockSpec((None, d, n_in), lambda j: (j, 0, 0)),
                (NDEV,), TN, False, "grad_w_in")
    first_in, tok = _scatter_start1([gw_in], tok, "scatter_start1_in")
    dh1 = _mm(dproj, w_in_all, pl.BlockSpec((tm2, 2 * n_in), lambda i, j: (i, j)),
              pl.BlockSpec((2, d, n_in), lambda i, j: (j, 0, 0)), jax.ShapeDtypeStruct((s, d), F32),
              pl.BlockSpec((tm2, d), lambda i, j: (i, 0)), (s // tm2, NDEV // 2), NT, True, "in_proj_bwd", tok)
    grad_x, _, st1 = _prenorm_bwd(xs, dh1, dx1, p_out, norm1_g, scale1, gate1, "prenorm1_bwd", tok)

    dmod = jnp.concatenate([st1[0], st1[1], st1[3], st2[0], st2[1], st2[3]])
    loss_row = jnp.pad((0.5 * jnp.sum(st_f[1]) / d).reshape(1, 1), ((0, 0), (0, GRP - 1)))
    small = [st1[2], dgv, dw_sp, db_cols[:, :b_spatial.shape[1]].T, jnp.concatenate([dgam_gm, dgam_sb], axis=1),
             st2[2], st_f[0], dmod, loss_row]
    zero_row = jnp.zeros((1, GRP), F32)
    small_w = [norm1_g, v_norm_g, w_spatial, b_spatial, out_norm_g, norm2_g, final_g, b_ada, zero_row]
    small_m = [m_norm1_g, m_v_norm_g, m_w_spatial, m_b_spatial, m_out_norm_g, m_norm2_g, m_final_g, m_b_ada, zero_row]
    small_v = [v_norm1_g, v_v_norm_g, v_w_spatial, v_b_spatial, v_out_norm_g, v_norm2_g, v_final_g, v_b_ada, zero_row]
    (small_first,), tok = _gather_start([_pack(small)], grad_x, "gather_small_start")
    second_in, tok = second_leg(first_in, tok, "in")

    big = {}
    prev = tok
    for second, group in ((second_ffn, (("w_gate", w_gate, m_w_gate, v_w_gate), ("w_up", w_up, m_w_up, v_w_up),
                                        ("w_down", w_down, m_w_down, v_w_down))),
                          (second_out, (("w_out", w_out, m_w_out, v_w_out),))):
        chip_sums = _scatter_wait2(second, prev, "scatter_wait2_" + group[0][0])
        for (nm, w, m, v), part in zip(group, chip_sums):
            big[nm] = _reduce_adamw(part, w[0], m[0], v[0], "adamw_" + nm, tok)
            prev = big[nm][1]

    small_second, tok2 = _gather_forward(small_first, prev, "gather_small_fwd")
    small_all = _gather_wait(small_second, tok2, "gather_small_wait")
    sm = _reduce_adamw(small_all, _pack(small_w), _pack(small_m), _pack(small_v), "adamw_small", tok)
    shapes = [a.shape for a in small_w]
    sm_g, sm_d, sm_m, sm_v = [_unpack(t, shapes) for t in sm]
    loss = sm_g[8][0, 0]

    at = sum(a.size for a in small_w[:7]) // GRP
    dmod_all = small_all[:, at:at + 6 * d // GRP, :].reshape(NDEV, 6 * d)
    n_ada = w_ada.shape[2]
    dmod_cols = lax.dynamic_slice(dmod_all, (0, me * n_ada), (NDEV, n_ada))
    g_ada = _ada_bwd(c_all, dmod_cols)
    big["w_ada"] = _reduce_adamw(g_ada[None], w_ada[0], m_w_ada[0], v_w_ada[0], "adamw_w_ada", sm[0])
    part_in, = _scatter_wait2(second_in, big["w_ada"][1], "scatter_wait2_w_in")
    big["w_in"] = _reduce_adamw(part_in, w_in[0], m_w_in[0], v_w_in[0], "adamw_w_in", tok)

    names = ["w_ada", "b_ada", "norm1_g", "w_in", "v_norm_g", "w_spatial", "b_spatial", "out_norm_g", "w_out",
             "norm2_g", "w_gate", "w_up", "w_down", "final_g"]
    small_at = {"norm1_g": 0, "v_norm_g": 1, "w_spatial": 2, "b_spatial": 3, "out_norm_g": 4, "norm2_g": 5,
                "final_g": 6, "b_ada": 7}
    outs = [[], [], [], []]
    for nm in names:
        for k in range(4):
            if nm in big:
                res = big[nm][k][None]
                outs[k].append(tr_(res) if nm in ("w_gate", "w_up") else res)
            else:
                outs[k].append((sm_g, sm_d, sm_m, sm_v)[k][small_at[nm]])
    return (loss, grad_x[None], *outs[0], *outs[1], *outs[2], *outs[3])
```

```python
import jax
import jax.numpy as jnp
from jax import lax
from jax.experimental import pallas as pl
from jax.experimental.pallas import tpu as pltpu

F32, BF16 = jnp.float32, jnp.bfloat16
NDEV = 8
GRP = 128
EPS = 1e-6
VMEM_BYTES = 64 * 2 ** 20
VMEM_LIMIT = VMEM_BYTES - 8 * 2 ** 20
ADAM_LR, ADAM_B1, ADAM_B2, ADAM_EPS, ADAM_WD, ADAM_STEP = 0.001, 0.9, 0.999, 1e-08, 0.01, 10
MESH = pl.DeviceIdType.MESH
NN = (((1,), (0,)), ((), ()))
NT = (((1,), (1,)), ((), ()))
TN = (((0,), (0,)), ((), ()))


def _params(n_axes):
    return pltpu.CompilerParams(dimension_semantics=("arbitrary",) * n_axes, vmem_limit_bytes=VMEM_LIMIT)


def _tile(n, cap, mult):
    best = None
    for t in range(mult, min(n, cap) + 1, mult):
        if n % t == 0:
            best = t
    assert best is not None, (n, cap, mult)
    return best


def _dot(a, b, dims):
    return lax.dot_general(a, b, dims, preferred_element_type=F32)


_HBM = pl.BlockSpec(memory_space=pltpu.HBM)
_SEM = pl.BlockSpec(memory_space=pltpu.SEMAPHORE)
_ANY = pl.BlockSpec(memory_space=pl.ANY)
_EFFECT = pltpu.SideEffectType.DATAFLOW_SIDE_EFFECTING


def _chip_peers():
    x, y, c = lax.axis_index("x"), lax.axis_index("y"), lax.axis_index("c")
    chips = [(x, 1 - y), (1 - x, y), (1 - x, 1 - y)]
    return 4 * x + 2 * y + c, (x, y, 1 - c), [((px, py, c), 4 * px + 2 * py + c) for px, py in chips]


def _gather_start(srcs, after, name):
    n = len(srcs)
    me = 4 * lax.axis_index("x") + 2 * lax.axis_index("y") + lax.axis_index("c")
    lands = [lax.dynamic_update_slice(lax.empty((NDEV,) + a.shape, a.dtype), a[None], (me,) + (0,) * a.ndim) for a in srcs]

    def body(*refs):
        src_refs, land_refs = refs[:n], refs[n:2 * n]
        sems, token = refs[2 * n + 1:2 * n + 1 + 2 * n], refs[-1]
        my, sibling, chips = _chip_peers()
        for a in range(n):
            for k, to in enumerate([sibling] + [coords for coords, _ in chips]):
                pltpu.make_async_remote_copy(src_ref=src_refs[a], dst_ref=land_refs[a].at[my], send_sem=sems[2 * a].at[k],
                                             recv_sem=sems[2 * a + 1].at[k], device_id=to, device_id_type=MESH).start()
        token[...] = jnp.zeros_like(token)

    hbm = lambda arrs: tuple(pltpu.HBM(a.shape, a.dtype) for a in arrs)
    res = pl.pallas_call(
        body, name=name,
        out_shape=(pltpu.SemaphoreType.DMA((4,)),) * (2 * n) + hbm(srcs) + hbm(lands) + (jax.ShapeDtypeStruct((8, GRP), F32),),
        in_specs=(_HBM,) * (2 * n) + (_ANY,),
        out_specs=(_SEM,) * (2 * n) + (_HBM,) * (2 * n) + (pl.BlockSpec(memory_space=pltpu.VMEM),),
        input_output_aliases={k: 2 * n + k for k in range(2 * n)}, compiler_params=pltpu.CompilerParams(has_side_effects=_EFFECT),
    )(*[pltpu.with_memory_space_constraint(a, pltpu.HBM) for a in list(srcs) + lands], after)
    return [(res[2 * a], res[2 * a + 1], res[2 * n + a], res[3 * n + a]) for a in range(n)], res[-1]


def _gather_forward(handles, after, name):
    send_sems, recv_sems, src_thru, land_thru = handles

    def body(src_ref, land_ref, send_sems, recv_sems, after_ref, send2, recv2, land_out, token):
        my, sibling, chips = _chip_peers()
        for k, (to, p) in enumerate([(sibling, my ^ 1)] + chips):
            first = pltpu.make_async_remote_copy(src_ref=src_ref, dst_ref=land_ref.at[p], send_sem=send_sems.at[k],
                                                 recv_sem=recv_sems.at[k], device_id=to, device_id_type=MESH)
            first.wait_send()
            first.wait_recv()
        for k, (_, p) in enumerate(chips):
            pltpu.make_async_remote_copy(src_ref=land_ref.at[p], dst_ref=land_ref.at[p], send_sem=send2.at[k],
                                         recv_sem=recv2.at[k], device_id=sibling, device_id_type=MESH).start()
        token[...] = jnp.zeros_like(token)

    res = pl.pallas_call(
        body, name=name,
        out_shape=(pltpu.SemaphoreType.DMA((3,)), pltpu.SemaphoreType.DMA((3,)), pltpu.HBM(land_thru.shape, land_thru.dtype),
                   jax.ShapeDtypeStruct((8, GRP), F32)),
        in_specs=(_HBM, _HBM, _SEM, _SEM, _ANY), out_specs=(_SEM, _SEM, _HBM, pl.BlockSpec(memory_space=pltpu.VMEM)),
        input_output_aliases={1: 2}, compiler_params=pltpu.CompilerParams(has_side_effects=_EFFECT),
    )(src_thru, land_thru, send_sems, recv_sems, after)
    return tuple(res[:3]), res[3]


def _gather_wait(handles, after, name):
    send2, recv2, land_thru = handles

    def body(land_ref, send2, recv2, after_ref, got_ref):
        _, sibling, chips = _chip_peers()
        for k, (_, p) in enumerate(chips):
            cp = pltpu.make_async_remote_copy(src_ref=land_ref.at[p], dst_ref=land_ref.at[p ^ 1], send_sem=send2.at[k],
                                              recv_sem=recv2.at[k], device_id=sibling, device_id_type=MESH)
            cp.wait_send()
            cp.wait_recv()

    return pl.pallas_call(
        body, name=name, out_shape=pltpu.HBM(land_thru.shape, land_thru.dtype),
        in_specs=(_HBM, _SEM, _SEM, _ANY), out_specs=_HBM, input_output_aliases={0: 0},
        compiler_params=pltpu.CompilerParams(has_side_effects=_EFFECT),
    )(land_thru, send2, recv2, after)


NCHIP = NDEV // 2


def _scatter_start1(parts, after, name):
    n = len(parts)
    lands = [lax.empty((NCHIP,) + p.shape[1:], p.dtype) for p in parts]

    def body(*refs):
        part_refs, land_refs = refs[:n], refs[n:2 * n]
        send_sems, recv_sems, token = refs[2 * n + 1], refs[2 * n + 2], refs[-1]
        _, sibling, _ = _chip_peers()
        c = lax.axis_index("c")
        for a in range(n):
            for q in range(NCHIP):
                pltpu.make_async_remote_copy(
                    src_ref=part_refs[a].at[2 * q + 1 - c], dst_ref=land_refs[a].at[q], send_sem=send_sems.at[a * NCHIP + q],
                    recv_sem=recv_sems.at[a * NCHIP + q], device_id=sibling, device_id_type=MESH).start()
        token[...] = jnp.zeros_like(token)

    hbm = lambda arrs: tuple(pltpu.HBM(a.shape, a.dtype) for a in arrs)
    res = pl.pallas_call(
        body, name=name,
        out_shape=(pltpu.SemaphoreType.DMA((n * NCHIP,)), pltpu.SemaphoreType.DMA((n * NCHIP,))) + hbm(parts) + hbm(lands)
        + (jax.ShapeDtypeStruct((8, GRP), F32),),
        in_specs=(_HBM,) * (2 * n) + (_ANY,),
        out_specs=(_SEM, _SEM) + (_HBM,) * (2 * n) + (pl.BlockSpec(memory_space=pltpu.VMEM),),
        input_output_aliases={k: 2 + k for k in range(2 * n)}, compiler_params=pltpu.CompilerParams(has_side_effects=_EFFECT),
    )(*[pltpu.with_memory_space_constraint(a, pltpu.HBM) for a in list(parts) + lands], after)
    return (n,) + tuple(res[:-1]), res[-1]


def _scatter_wait1(handles, after, name):
    n, send_sems, recv_sems = handles[:3]
    thru = handles[3:]

    def body(*refs):
        part_refs, land_refs, send_sems, recv_sems = refs[:n], refs[n:2 * n], refs[2 * n], refs[2 * n + 1]
        _, sibling, _ = _chip_peers()
        for a in range(n):
            for q in range(NCHIP):
                cp = pltpu.make_async_remote_copy(
                    src_ref=part_refs[a].at[q], dst_ref=land_refs[a].at[q], send_sem=send_sems.at[a * NCHIP + q],
                    recv_sem=recv_sems.at[a * NCHIP + q], device_id=sibling, device_id_type=MESH)
                cp.wait_send()
                cp.wait_recv()

    res = pl.pallas_call(
        body, name=name, out_shape=tuple(pltpu.HBM(a.shape, a.dtype) for a in thru),
        in_specs=(_HBM,) * (2 * n) + (_SEM, _SEM, _ANY), out_specs=(_HBM,) * (2 * n),
        input_output_aliases={k: k for k in range(2 * n)}, compiler_params=pltpu.CompilerParams(has_side_effects=_EFFECT),
    )(*thru, send_sems, recv_sems, after)
    return res[:n], res[n:]


def _chip_sum(part, land, name):
    _, r, c = part.shape
    tr = _tile(r, max(16, 2 ** 21 // c), 16)

    def body(core_ref, p_ref, l_ref, o_ref):
        o_ref[...] = (p_ref[...].astype(F32) + l_ref[...].astype(F32)).astype(o_ref.dtype)

    blk = pl.BlockSpec((None, tr, c), lambda q, i, core: (q, i, 0))
    return pl.pallas_call(
        body, name=name, out_shape=jax.ShapeDtypeStruct(land.shape, land.dtype),
        grid_spec=pltpu.PrefetchScalarGridSpec(
            num_scalar_prefetch=1, grid=(NCHIP, r // tr),
            in_specs=[pl.BlockSpec((None, None, tr, c), lambda q, i, core: (q, core[0], i, 0)), blk], out_specs=blk),
        compiler_params=_params(2),
    )(lax.axis_index("c").astype(jnp.int32).reshape(1), part.reshape(NCHIP, 2, r, c), land)


def _scatter_start2(sums, after, name):
    n = len(sums)
    chip = 2 * lax.axis_index("x") + lax.axis_index("y")
    lands = [lax.dynamic_update_slice(lax.empty(s_.shape, s_.dtype), lax.dynamic_index_in_dim(s_, chip, 0, keepdims=True),
                                      (chip,) + (0,) * (s_.ndim - 1)) for s_ in sums]

    def body(*refs):
        sum_refs, land_refs = refs[:n], refs[n:2 * n]
        send_sems, recv_sems, token = refs[2 * n + 1], refs[2 * n + 2], refs[-1]
        my, _, chips = _chip_peers()
        for a in range(n):
            for k, (to, p) in enumerate(chips):
                pltpu.make_async_remote_copy(
                    src_ref=sum_refs[a].at[p // 2], dst_ref=land_refs[a].at[my // 2], send_sem=send_sems.at[a * 3 + k],
                    recv_sem=recv_sems.at[a * 3 + k], device_id=to, device_id_type=MESH).start()
        token[...] = jnp.zeros_like(token)

    hbm = lambda arrs: tuple(pltpu.HBM(a.shape, a.dtype) for a in arrs)
    res = pl.pallas_call(
        body, name=name,
        out_shape=(pltpu.SemaphoreType.DMA((n * 3,)), pltpu.SemaphoreType.DMA((n * 3,))) + hbm(sums) + hbm(lands)
        + (jax.ShapeDtypeStruct((8, GRP), F32),),
        in_specs=(_HBM,) * (2 * n) + (_ANY,),
        out_specs=(_SEM, _SEM) + (_HBM,) * (2 * n) + (pl.BlockSpec(memory_space=pltpu.VMEM),),
        input_output_aliases={k: 2 + k for k in range(2 * n)}, compiler_params=pltpu.CompilerParams(has_side_effects=_EFFECT),
    )(*[pltpu.with_memory_space_constraint(a, pltpu.HBM) for a in list(sums) + lands], after)
    return (n,) + tuple(res[:-1]), res[-1]


def _scatter_wait2(handles, after, name):
    n, send_sems, recv_sems = handles[:3]
    thru = handles[3:]

    def body(*refs):
        sum_refs, land_refs, send_sems, recv_sems = refs[:n], refs[n:2 * n], refs[2 * n], refs[2 * n + 1]
        _, _, chips = _chip_peers()
        for a in range(n):
            for k, (to, p) in enumerate(chips):
                cp = pltpu.make_async_remote_copy(
                    src_ref=sum_refs[a].at[p // 2], dst_ref=land_refs[a].at[p // 2], send_sem=send_sems.at[a * 3 + k],
                    recv_sem=recv_sems.at[a * 3 + k], device_id=to, device_id_type=MESH)
                cp.wait_send()
                cp.wait_recv()

    res = pl.pallas_call(
        body, name=name, out_shape=tuple(pltpu.HBM(a.shape, a.dtype) for a in thru),
        in_specs=(_HBM,) * (2 * n) + (_SEM, _SEM, _ANY), out_specs=(_HBM,) * (2 * n),
        input_output_aliases={k: k for k in range(2 * n)}, compiler_params=pltpu.CompilerParams(has_side_effects=_EFFECT),
    )(*thru, send_sems, recv_sems, after)
    return res[n:]


def _mm(a, b, a_spec, b_spec, out_sds, o_spec, grid, dims, acc, name, after=None):
    extra = () if after is None else (after,)
    assert not acc or out_sds.dtype == F32

    def body(a_ref, b_ref, *rest):
        o_ref = rest[len(extra)]
        if len(b_ref.shape) == 3:
            nkb = b_ref.shape[0]
            wk = a_ref.shape[-1] // nkb
            prod = sum(_dot(a_ref[k] if len(a_ref.shape) == 3 else a_ref[:, k * wk:(k + 1) * wk], b_ref[k], dims)
                       for k in range(nkb))
        else:
            prod = _dot(a_ref[...], b_ref[...], dims)
        if not acc:
            o_ref[...] = prod.astype(o_ref.dtype)
            return
        k = pl.program_id(len(grid) - 1)

        @pl.when(k == 0)
        def _():
            o_ref[...] = prod

        @pl.when(k > 0)
        def _():
            o_ref[...] += prod

    return pl.pallas_call(
        body, name=name, grid=grid, out_shape=out_sds, in_specs=[a_spec, b_spec] + [_ANY] * len(extra), out_specs=o_spec,
        compiler_params=_params(len(grid)),
    )(a, b, *extra)


def _row_spec(tm, d):
    return pl.BlockSpec((tm, d), lambda i: (i, 0))


def _vec_spec(d):
    return pl.BlockSpec((1, d), lambda i: (0, 0))


def _prenorm_fwd(x, g, scale, shift, name):
    s, d = x.shape
    tm = _tile(s, 256, 16)

    def body(x_ref, g_ref, sc_ref, sh_ref, h_ref):
        xv = x_ref[...]
        rstd = lax.rsqrt(jnp.mean(xv * xv, axis=-1, keepdims=True) + EPS)
        h_ref[...] = ((xv * rstd * g_ref[...]) * (1.0 + sc_ref[...]) + sh_ref[...]).astype(BF16)

    return pl.pallas_call(
        body, name=name, grid=(s // tm,), out_shape=jax.ShapeDtypeStruct((s, d), BF16),
        in_specs=[_row_spec(tm, d), _vec_spec(d), _vec_spec(d), _vec_spec(d)], out_specs=_row_spec(tm, d),
        compiler_params=_params(1),
    )(x, g, scale, shift)


def _residual_prenorm(x, gate, p, g, scale, shift, name):
    s, d = x.shape
    tm = _tile(s, 256, 16)

    def body(x_ref, gate_ref, p_ref, g_ref, sc_ref, sh_ref, x1_ref, h_ref):
        xv = x_ref[...] + gate_ref[...] * p_ref[...]
        x1_ref[...] = xv
        rstd = lax.rsqrt(jnp.mean(xv * xv, axis=-1, keepdims=True) + EPS)
        h_ref[...] = ((xv * rstd * g_ref[...]) * (1.0 + sc_ref[...]) + sh_ref[...]).astype(BF16)

    return pl.pallas_call(
        body, name=name, grid=(s // tm,),
        out_shape=(jax.ShapeDtypeStruct((s, d), F32), jax.ShapeDtypeStruct((s, d), BF16)),
        in_specs=[_row_spec(tm, d), _vec_spec(d), _row_spec(tm, d), _vec_spec(d), _vec_spec(d), _vec_spec(d)],
        out_specs=(_row_spec(tm, d), _row_spec(tm, d)), compiler_params=_params(1),
    )(x, gate, p, g, scale, shift)


def _final_loss(x1, f, final_g, target, gate2):
    s, d = x1.shape
    tm = _tile(s, 256, 16)

    def body(x_ref, f_ref, g_ref, t_ref, gate_ref, dx_ref, df_ref, st_ref):
        @pl.when(pl.program_id(0) == 0)
        def _():
            st_ref[...] = jnp.zeros_like(st_ref)

        xv, gf = x_ref[...] + gate_ref[...] * f_ref[...], g_ref[...]
        rstd = lax.rsqrt(jnp.mean(xv * xv, axis=-1, keepdims=True) + EPS)
        xhat = xv * rstd
        err = xhat * gf - t_ref[...]
        dy = err * (1.0 / d)
        gdy = dy * gf
        dx = rstd * (gdy - xhat * jnp.mean(gdy * xhat, axis=-1, keepdims=True))
        dx_ref[...] = dx
        df_ref[...] = (gate_ref[...] * dx).astype(BF16)
        st_ref[0:1, :] += jnp.sum(dy * xhat, axis=0, keepdims=True)
        st_ref[1:2, :] += jnp.sum(err * err, axis=0, keepdims=True)

    return pl.pallas_call(
        body, name="final_loss", grid=(s // tm,),
        out_shape=(jax.ShapeDtypeStruct((s, d), F32), jax.ShapeDtypeStruct((s, d), BF16),
                   jax.ShapeDtypeStruct((8, d), F32)),
        in_specs=[_row_spec(tm, d), _row_spec(tm, d), _vec_spec(d), _row_spec(tm, d), _vec_spec(d)],
        out_specs=(_row_spec(tm, d), _row_spec(tm, d), pl.BlockSpec((8, d), lambda i: (0, 0))),
        compiler_params=_params(1),
    )(x1, f, final_g, target, gate2)


def _prenorm_bwd(xin, dh, dres, pf, g, scale, gate_next, name, after):
    s, d = xin.shape
    tm = _tile(s, 256, 16)

    def body(x_ref, dh_ref, dr_ref, pf_ref, g_ref, sc_ref, gn_ref, after_ref, dx_ref, dn_ref, st_ref):
        @pl.when(pl.program_id(0) == 0)
        def _():
            st_ref[...] = jnp.zeros_like(st_ref)

        xv, dhv, drv, gv = x_ref[...], dh_ref[...], dr_ref[...], g_ref[...]
        one_sc = 1.0 + sc_ref[...]
        rstd = lax.rsqrt(jnp.mean(xv * xv, axis=-1, keepdims=True) + EPS)
        xhat = xv * rstd
        dxhat = dhv * (gv * one_sc)
        dx = drv + rstd * (dxhat - xhat * jnp.mean(dxhat * xhat, axis=-1, keepdims=True))
        dx_ref[...] = dx
        dn_ref[...] = (gn_ref[...] * dx).astype(BF16)
        dhx = dhv * xhat
        st_ref[0:1, :] += jnp.sum(dhv, axis=0, keepdims=True)
        st_ref[1:2, :] += jnp.sum(dhx, axis=0, keepdims=True) * gv
        st_ref[2:3, :] += jnp.sum(dhx, axis=0, keepdims=True) * one_sc
        st_ref[3:4, :] += jnp.sum(drv * pf_ref[...], axis=0, keepdims=True)

    return pl.pallas_call(
        body, name=name, grid=(s // tm,),
        out_shape=(jax.ShapeDtypeStruct((s, d), F32), jax.ShapeDtypeStruct((s, d), BF16),
                   jax.ShapeDtypeStruct((8, d), F32)),
        in_specs=[_row_spec(tm, d)] * 4 + [_vec_spec(d)] * 3 + [_ANY],
        out_specs=(_row_spec(tm, d), _row_spec(tm, d), pl.BlockSpec((8, d), lambda i: (0, 0))),
        compiler_params=_params(1),
    )(xin, dh, dres, pf, g, scale, gate_next, after)


def _ada_exchange(c, w_loc):
    _, d = c.shape
    n = w_loc.shape[1]
    tn = _tile(n, 512, 128) if n % 128 == 0 else n

    def body(c_ref, w_hbm, call_ref, mod_ref, w_ref, rows, send_a, recv_a, send_b, recv_b, local_sems):
        fetch_w = pltpu.make_async_copy(w_hbm, w_ref, local_sems.at[2])
        fetch_w.start()
        x, y, cc = lax.axis_index("x"), lax.axis_index("y"), lax.axis_index("c")
        me = 4 * x + 2 * y + cc
        peers = []
        for k in range(1, NDEV):
            px, py, pc = x ^ (k >> 2), y ^ ((k >> 1) & 1), cc ^ (k & 1)
            peers.append((k - 1, (px, py, pc), 4 * px + 2 * py + pc))

        def exchange(src_of, dst_ref, send_sems, recv_sems, local_sem):
            local = pltpu.make_async_copy(src_of(me), dst_ref.at[me], local_sem)
            local.start()
            sends = [pltpu.make_async_remote_copy(src_ref=src_of(p), dst_ref=dst_ref.at[me], send_sem=send_sems.at[k],
                                                  recv_sem=recv_sems.at[k], device_id=to, device_id_type=MESH)
                     for k, to, p in peers]
            for cp in sends:
                cp.start()
            for k, to, p in peers:
                pltpu.make_async_remote_copy(src_ref=src_of(p), dst_ref=dst_ref.at[p], send_sem=send_sems.at[k],
                                             recv_sem=recv_sems.at[k], device_id=to, device_id_type=MESH).wait_recv()
            for cp in sends:
                cp.wait_send()
            local.wait()

        exchange(lambda p: c_ref, call_ref, send_a, recv_a, local_sems.at[0])
        cv = call_ref[:, 0, :]
        act = cv * jax.nn.sigmoid(cv)
        fetch_w.wait()
        for j in range(n // tn):
            rows[:, 0, j * tn:(j + 1) * tn] = jnp.dot(act, w_ref[:, j * tn:(j + 1) * tn], preferred_element_type=F32,
                                                      precision=lax.Precision.HIGHEST)
        exchange(lambda p: rows.at[p], mod_ref, send_b, recv_b, local_sems.at[1])

    vmem = pl.BlockSpec(memory_space=pltpu.VMEM)
    return pl.pallas_call(
        body, name="ada_exchange",
        out_shape=(jax.ShapeDtypeStruct((NDEV, 1, d), F32), jax.ShapeDtypeStruct((NDEV, 1, n), F32)),
        in_specs=[vmem, _ANY], out_specs=(vmem, vmem),
        scratch_shapes=[pltpu.VMEM((d, n), F32), pltpu.VMEM((NDEV, 1, n), F32)] + [pltpu.SemaphoreType.DMA((NDEV - 1,))] * 4
        + [pltpu.SemaphoreType.DMA((3,))],
        compiler_params=pltpu.CompilerParams(vmem_limit_bytes=VMEM_LIMIT),
    )(c, w_loc)


def _ada_bwd(c_all, dmod_cols):
    nb, d = c_all.shape
    n = dmod_cols.shape[1]
    tn = _tile(n, 512, 128) if n % 128 == 0 else n

    def body(c_ref, dm_ref, o_ref):
        cv = c_ref[...]
        o_ref[...] = lax.dot_general(cv * jax.nn.sigmoid(cv), dm_ref[...], TN, preferred_element_type=F32,
                                     precision=lax.Precision.HIGHEST)

    return pl.pallas_call(
        body, name="ada_bwd", grid=(n // tn,), out_shape=jax.ShapeDtypeStruct((d, n), F32),
        in_specs=[pl.BlockSpec((nb, d), lambda j: (0, 0)), pl.BlockSpec((nb, tn), lambda j: (0, j))],
        out_specs=pl.BlockSpec((d, tn), lambda j: (0, j)), compiler_params=_params(1),
    )(c_all, dmod_cols)


_INV_SQRT2 = 0.7071067811865476
_INV_SQRT2PI = 0.3989422804014327


def _gelu_parts(x):
    cdf = 0.5 * (1.0 + lax.erf(x * _INV_SQRT2))
    return x * cdf, cdf + x * jnp.exp(-0.5 * x * x) * _INV_SQRT2PI


def _gm_group_fwd(up, vp, gv, wt, bcol):
    u = 0.5 * up * (1.0 + lax.erf(up * _INV_SQRT2))
    va = 0.5 * vp * (1.0 + lax.erf(vp * _INV_SQRT2))
    return _gm_group_mix(u, va, gv, wt, bcol)


def _gm_group_mix(u, va, gv, wt, bcol):
    xc = va - jnp.mean(va, axis=-1, keepdims=True)
    rstd_v = lax.rsqrt(jnp.mean(xc * xc, axis=-1, keepdims=True) + EPS)
    yv = xc * rstd_v
    vn = (yv * gv).astype(BF16)
    mixed = _dot(wt, vn, NN) + bcol
    return u, rstd_v, yv, vn, mixed, u * mixed


def _tril_bf16(w):
    row = lax.broadcasted_iota(jnp.int32, w.shape, 0)
    col = lax.broadcasted_iota(jnp.int32, w.shape, 1)
    return jnp.where(col <= row, w, 0.0).astype(BF16)


def _gmlp_fwd(proj, v_norm_g, w_spatial, b_cols, gamma, gm, after):
    s = proj.shape[0]
    ng = gm // GRP

    def body(p_ref, gv_ref, w_ref, b_ref, gam_ref, after_ref, o_ref):
        for g in range(ng):
            lo = g * GRP
            wt = _tril_bf16(w_ref[g])
            *_, o = _gm_group_fwd(p_ref[:, lo:lo + GRP], p_ref[:, gm + lo:gm + lo + GRP], gv_ref[:, lo:lo + GRP],
                                  wt, b_ref[:, g:g + 1])
            rstd_o = lax.rsqrt(jnp.mean(o * o, axis=-1, keepdims=True) + EPS)
            o_ref[:, lo:lo + GRP] = (o * rstd_o * gam_ref[:, lo:lo + GRP]).astype(BF16)

    return pl.pallas_call(
        body, name="gmlp_fwd", grid=(s // GRP,), out_shape=jax.ShapeDtypeStruct((s, gm), BF16),
        in_specs=[pl.BlockSpec((GRP, 2 * gm), lambda n: (n, 0)), _vec_spec(gm),
                  pl.BlockSpec((ng, GRP, GRP), lambda n: (0, 0, 0)), pl.BlockSpec((GRP, GRP), lambda n: (0, 0)),
                  _vec_spec(gm), _ANY],
        out_specs=pl.BlockSpec((GRP, gm), lambda n: (n, 0)), compiler_params=_params(1),
    )(proj, v_norm_g, w_spatial, b_cols, gamma, after)


def _gmlp_bwd(proj, don, v_norm_g, w_spatial, b_cols, gamma, gm, after):
    s = proj.shape[0]
    ng = gm // GRP

    def body(p_ref, don_ref, gv_ref, w_ref, b_ref, gam_ref, after_ref, dp_ref, dw_ref, db_ref, dgv_ref, dgam_ref):
        @pl.when(pl.program_id(0) == 0)
        def _():
            dw_ref[...] = jnp.zeros_like(dw_ref)
            db_ref[...] = jnp.zeros_like(db_ref)
            dgv_ref[...] = jnp.zeros_like(dgv_ref)
            dgam_ref[...] = jnp.zeros_like(dgam_ref)

        lane = lax.broadcasted_iota(jnp.int32, (GRP, GRP), 1)
        row = lax.broadcasted_iota(jnp.int32, (GRP, GRP), 0)
        for g in range(ng):
            lo = g * GRP
            up, vp = p_ref[:, lo:lo + GRP], p_ref[:, gm + lo:gm + lo + GRP]
            gv, gam = gv_ref[:, lo:lo + GRP], gam_ref[:, lo:lo + GRP]
            wt = _tril_bf16(w_ref[g])
            (u, du_dup), (va, dva_dvp) = _gelu_parts(up), _gelu_parts(vp)
            u, rstd_v, yv, vn, mixed, o = _gm_group_mix(u, va, gv, wt, b_ref[:, g:g + 1])
            rstd_o = lax.rsqrt(jnp.mean(o * o, axis=-1, keepdims=True) + EPS)
            ohat = o * rstd_o
            dn = don_ref[:, lo:lo + GRP]
            dgam_ref[:, lo:lo + GRP] += jnp.sum(dn * ohat, axis=0, keepdims=True)
            dohat = dn * gam
            do = rstd_o * (dohat - ohat * jnp.mean(dohat * ohat, axis=-1, keepdims=True))
            du = do * mixed
            dmixed = do * u
            dmb = dmixed.astype(BF16)
            db_ref[...] += jnp.where(lane == g, jnp.sum(dmixed, axis=-1, keepdims=True), 0.0)
            dw_ref[g] += jnp.where(lane <= row, _dot(dmb, vn, NT), 0.0)
            dvn = _dot(wt, dmb, TN)
            dgv_ref[:, lo:lo + GRP] += jnp.sum(dvn * yv, axis=0, keepdims=True)
            dyv = dvn * gv
            dva = rstd_v * (dyv - jnp.mean(dyv, axis=-1, keepdims=True)
                            - yv * jnp.mean(dyv * yv, axis=-1, keepdims=True))
            dp_ref[:, lo:lo + GRP] = (du * du_dup).astype(BF16)
            dp_ref[:, gm + lo:gm + lo + GRP] = (dva * dva_dvp).astype(BF16)

    const2 = lambda n: (0, 0)
    return pl.pallas_call(
        body, name="gmlp_bwd", grid=(s // GRP,),
        out_shape=(jax.ShapeDtypeStruct((s, 2 * gm), BF16), jax.ShapeDtypeStruct((ng, GRP, GRP), F32),
                   jax.ShapeDtypeStruct((GRP, GRP), F32), jax.ShapeDtypeStruct((1, gm), F32),
                   jax.ShapeDtypeStruct((1, gm), F32)),
        in_specs=[pl.BlockSpec((GRP, 2 * gm), lambda n: (n, 0)), pl.BlockSpec((GRP, gm), lambda n: (n, 0)),
                  _vec_spec(gm), pl.BlockSpec((ng, GRP, GRP), lambda n: (0, 0, 0)),
                  pl.BlockSpec((GRP, GRP), const2), _vec_spec(gm), _ANY],
        out_specs=(pl.BlockSpec((GRP, 2 * gm), lambda n: (n, 0)), pl.BlockSpec((ng, GRP, GRP), lambda n: (0, 0, 0)),
                   pl.BlockSpec((GRP, GRP), const2), _vec_spec(gm), _vec_spec(gm)),
        compiler_params=_params(1),
    )(proj, don, v_norm_g, w_spatial, b_cols, gamma, after)


BLK = 512


def _log_sigmoid(z):
    return jnp.minimum(z, 0.0) - jnp.log(1.0 + jnp.exp(-jnp.abs(z)))


def _split_dot(x, tri, passes):
    n = x.shape[0]
    parts, rest = [], x
    for _ in range(passes):
        hi = rest.astype(BF16)
        parts.append(hi)
        rest = rest - hi.astype(F32)
    res = _dot(jnp.concatenate(parts, axis=0), tri, NN)
    out = res[0:n]
    for k in range(1, passes):
        out = out + res[k * n:(k + 1) * n]
    return out


HEADS_PER_STEP = 2


def _sb_fwd(proj, gamma, gm, sb, after):
    s = proj.shape[0]
    hp = min(HEADS_PER_STEP, sb // GRP)
    w = hp * GRP
    nhp, nq = sb // w, s // BLK
    qc, kc, vc, gc = 2 * gm // w, (2 * gm + sb) // w, (2 * gm + 2 * sb) // w, gm // w
    scale = GRP ** -0.5

    def body(q_ref, k_ref, v_ref, gam_ref, after_ref, on_ref, o_ref, a_ref, s_ref, kb, vb):
        i = pl.program_id(1)

        @pl.when(i == 0)
        def _():
            kb[...] = k_ref[...].astype(BF16)
            vb[...] = v_ref[...].astype(BF16)

        qb = q_ref[...].astype(BF16)
        row = lax.broadcasted_iota(jnp.int32, (BLK, BLK), 0)
        col = lax.broadcasted_iota(jnp.int32, (BLK, BLK), 1)
        later = (row > col).astype(BF16)

        def block(j, carry, diag):
            off = pl.multiple_of(j * BLK, BLK)
            kj, vj = kb[pl.ds(off, BLK), :], vb[pl.ds(off, BLK), :]
            out = []
            for h in range(hp):
                tail, acc = carry[h]
                sl = slice(h * GRP, (h + 1) * GRP)
                z = _dot(qb[:, sl], kj[:, sl], NT) * scale
                lb = _log_sigmoid(z)
                l1m = lb - z
                sig = jnp.exp(lb)
                if diag:
                    l1m = jnp.where(col < row, l1m, 0.0)
                    sig = jnp.where(col < row, sig, 0.0)
                after_s = _split_dot(l1m, later, 2)
                a = jnp.exp(lb + after_s + tail)
                if diag:
                    a = jnp.where(col < row, a, 0.0)
                ab = a.astype(BF16)
                a_ref[h, j] = ab
                s_ref[h, j] = sig.astype(BF16)
                out.append((tail + after_s[:, 0:1] + l1m[:, 0:1], acc + _dot(ab, vj[:, sl], NN)))
            return tuple(out)

        init = tuple((jnp.zeros((BLK, 1), F32), jnp.zeros((BLK, GRP), F32)) for _ in range(hp))
        carry = block(i, init, True)
        carry = lax.fori_loop(0, i, lambda jj, c: block(i - 1 - jj, c, False), carry)
        for h in range(hp):
            _, acc = carry[h]
            sl = slice(h * GRP, (h + 1) * GRP)
            rstd = lax.rsqrt(jnp.mean(acc * acc, axis=-1, keepdims=True) + EPS)
            on_ref[:, sl] = (acc * rstd * gam_ref[:, sl]).astype(BF16)
            o_ref[:, sl] = acc

    saved = jax.ShapeDtypeStruct((sb // GRP, nq, nq, BLK, BLK), BF16)
    saved_spec = pl.BlockSpec((hp, None, nq, BLK, BLK), lambda h, i: (h, i, 0, 0, 0))
    return pl.pallas_call(
        body, name="sb_fwd", grid=(nhp, nq),
        out_shape=(jax.ShapeDtypeStruct((s, sb), BF16), jax.ShapeDtypeStruct((s, sb), F32), saved, saved),
        in_specs=[pl.BlockSpec((BLK, w), lambda h, i: (i, qc + h)), pl.BlockSpec((s, w), lambda h, i: (0, kc + h)),
                  pl.BlockSpec((s, w), lambda h, i: (0, vc + h)), pl.BlockSpec((1, w), lambda h, i: (0, gc + h)), _ANY],
        out_specs=(pl.BlockSpec((BLK, w), lambda h, i: (i, h)), pl.BlockSpec((BLK, w), lambda h, i: (i, h)),
                   saved_spec, saved_spec),
        scratch_shapes=[pltpu.VMEM((s, w), BF16), pltpu.VMEM((s, w), BF16)],
        compiler_params=_params(2),
    )(proj, proj, proj, gamma, after)


def _sb_bwd(proj, o_raw, a_sv, s_sv, don, gamma, gm, sb, after):
    s = proj.shape[0]
    hp = min(HEADS_PER_STEP, sb // GRP)
    w = hp * GRP
    nhp, nq = sb // w, s // BLK
    qc, kc, vc, gc = 2 * gm // w, (2 * gm + sb) // w, (2 * gm + 2 * sb) // w, gm // w
    scale = GRP ** -0.5

    def body(q_ref, k_ref, v_ref, o_ref, a_ref, s_ref, don_ref, gam_ref, after_ref, dq_ref, dk_ref, dv_ref, dgam_ref,
             kb, vb, dkt_acc, dvt_acc):
        i = pl.program_id(1)

        @pl.when(i == 0)
        def _():
            kb[...] = k_ref[...].astype(BF16)
            vb[...] = v_ref[...].astype(BF16)
            dkt_acc[...] = jnp.zeros_like(dkt_acc)
            dvt_acc[...] = jnp.zeros_like(dvt_acc)
            dgam_ref[...] = jnp.zeros_like(dgam_ref)

        dobs = []
        for h in range(hp):
            sl = slice(h * GRP, (h + 1) * GRP)
            o, dn = o_ref[:, sl], don_ref[:, sl]
            rstd = lax.rsqrt(jnp.mean(o * o, axis=-1, keepdims=True) + EPS)
            ohat = o * rstd
            dgam_ref[:, sl] += jnp.sum(dn * ohat, axis=0, keepdims=True)
            dohat = dn * gam_ref[:, sl]
            dobs.append((rstd * (dohat - ohat * jnp.mean(dohat * ohat, axis=-1, keepdims=True))).astype(BF16))

        qb = q_ref[...].astype(BF16)
        qts = [qb[:, h * GRP:(h + 1) * GRP].T for h in range(hp)]
        dots = [dob.T for dob in dobs]
        row = lax.broadcasted_iota(jnp.int32, (BLK, BLK), 0)
        col = lax.broadcasted_iota(jnp.int32, (BLK, BLK), 1)
        before = (row < col).astype(BF16)

        def block(j, carry):
            off = pl.multiple_of(j * BLK, BLK)
            kj, vj = kb[pl.ds(off, BLK), :], vb[pl.ds(off, BLK), :]
            out = []
            for h in range(hp):
                e_pre, dq = carry[h]
                sl = slice(h * GRP, (h + 1) * GRP)
                qh, kh, dob = qb[:, sl], kj[:, sl], dobs[h]
                ab = a_ref[h, j]
                sig = s_ref[h, j].astype(F32)
                de = ab.astype(F32) * _dot(dob, vj[:, sl], NT)
                dvt_acc[sl, pl.ds(off, BLK)] += _dot(dots[h], ab, NN)
                before_s = _dot(de.astype(BF16), before, NN)
                dzb = ((de * (1.0 - sig) - (e_pre + before_s) * sig) * scale).astype(BF16)
                dkt_acc[sl, pl.ds(off, BLK)] += _dot(qts[h], dzb, NN)
                out.append((e_pre + before_s[:, BLK - 1:BLK] + de[:, BLK - 1:BLK], dq + _dot(dzb, kh, NN)))
            return tuple(out)

        carry = tuple((jnp.zeros((BLK, 1), F32), jnp.zeros((BLK, GRP), F32)) for _ in range(hp))
        carry = lax.fori_loop(0, i + 1, block, carry)
        for h in range(hp):
            dq_ref[:, h * GRP:(h + 1) * GRP] = carry[h][1].astype(BF16)

        @pl.when(i == nq - 1)
        def _():
            dk_ref[...] = dkt_acc[...].T.astype(BF16)
            dv_ref[...] = dvt_acc[...].T.astype(BF16)

    blk_q = lambda h, i: (i, h)
    whole = lambda h, i: (0, h)
    saved_spec = pl.BlockSpec((hp, None, nq, BLK, BLK), lambda h, i: (h, i, 0, 0, 0))
    return pl.pallas_call(
        body, name="sb_bwd", grid=(nhp, nq),
        out_shape=(jax.ShapeDtypeStruct((s, sb), BF16),) * 3 + (jax.ShapeDtypeStruct((1, sb), F32),),
        in_specs=[pl.BlockSpec((BLK, w), lambda h, i: (i, qc + h)), pl.BlockSpec((s, w), lambda h, i: (0, kc + h)),
                  pl.BlockSpec((s, w), lambda h, i: (0, vc + h)), pl.BlockSpec((BLK, w), blk_q), saved_spec, saved_spec,
                  pl.BlockSpec((BLK, w), lambda h, i: (i, gc + h)),
                  pl.BlockSpec((1, w), lambda h, i: (0, gc + h)), _ANY],
        out_specs=(pl.BlockSpec((BLK, w), blk_q), pl.BlockSpec((s, w), whole), pl.BlockSpec((s, w), whole),
                   pl.BlockSpec((1, w), whole)),
        scratch_shapes=[pltpu.VMEM((s, w), BF16), pltpu.VMEM((s, w), BF16),
                        pltpu.VMEM((w, s), F32), pltpu.VMEM((w, s), F32)],
        compiler_params=_params(2),
    )(proj, proj, proj, o_raw, a_sv, s_sv, don, gamma, after)


FFN_ROW_CHUNKS = 2


def _row_chunks(tm):
    n = FFN_ROW_CHUNKS if tm % (16 * FFN_ROW_CHUNKS) == 0 else 1
    return [slice(k * (tm // n), (k + 1) * (tm // n)) for k in range(n)]


def _ffn_up(h2, wg, wu):
    s, d = h2.shape
    nb, fb, _ = wg.shape
    tm = _tile(s, 1024, 16)

    def body(h_ref, wg_ref, wu_ref, g_ref, u_ref, a_ref):
        for rows_ in _row_chunks(tm):
            hv = h_ref[rows_, :]
            g = _dot(hv, wg_ref[...], NT)
            u = _dot(hv, wu_ref[...], NT)
            g_ref[rows_, :] = g.astype(BF16)
            u_ref[rows_, :] = u.astype(BF16)
            a_ref[rows_, :] = (g * jax.nn.sigmoid(g) * u).astype(BF16)

    rows = pl.BlockSpec((tm, d), lambda i, j: (i, 0))
    wblk = pl.BlockSpec((None, fb, d), lambda i, j: (j, 0, 0))
    hid = pl.BlockSpec((None, tm, fb), lambda i, j: (j, i, 0))
    hid_sds = jax.ShapeDtypeStruct((nb, s, fb), BF16)
    return pl.pallas_call(
        body, name="ffn_up", grid=(s // tm, nb), out_shape=(hid_sds,) * 3, in_specs=[rows, wblk, wblk],
        out_specs=(hid,) * 3, compiler_params=_params(2),
    )(h2, wg, wu)


def _ffn_bwd_act(df, g_pre, u_pre, wd):
    s, d = df.shape
    nb, fb, _ = wd.shape
    tm = _tile(s, 1024, 16)

    def body(df_ref, g_ref, u_ref, wd_ref, dg_ref, du_ref):
        for rows_ in _row_chunks(tm):
            g, u = g_ref[rows_, :].astype(F32), u_ref[rows_, :].astype(F32)
            da = _dot(df_ref[rows_, :], wd_ref[...], NT)
            sg = jax.nn.sigmoid(g)
            dg_ref[rows_, :] = (da * u * (sg * (1.0 + g * (1.0 - sg)))).astype(BF16)
            du_ref[rows_, :] = (da * (g * sg)).astype(BF16)

    rows = pl.BlockSpec((tm, d), lambda i, j: (i, 0))
    wblk = pl.BlockSpec((None, fb, d), lambda i, j: (j, 0, 0))
    hid = pl.BlockSpec((None, tm, fb), lambda i, j: (j, i, 0))
    hid_sds = jax.ShapeDtypeStruct((nb, s, fb), BF16)
    return pl.pallas_call(
        body, name="ffn_bwd_act", grid=(s // tm, nb), out_shape=(hid_sds, hid_sds),
        in_specs=[rows, hid, hid, wblk], out_specs=(hid, hid), compiler_params=_params(2),
    )(df, g_pre, u_pre, wd)


def _ffn_bwd_in(dg, du, wg, wu):
    nb, s, fb = dg.shape
    d = wg.shape[2]
    tm = _tile(s, 1024, 16)

    def body(dg_ref, du_ref, wg_ref, wu_ref, dh_ref):
        j = pl.program_id(1)
        parts = [_dot(dg_ref[rows_, :], wg_ref[...], NN) + _dot(du_ref[rows_, :], wu_ref[...], NN)
                 for rows_ in _row_chunks(tm)]

        @pl.when(j == 0)
        def _():
            for rows_, part in zip(_row_chunks(tm), parts):
                dh_ref[rows_, :] = part

        @pl.when(j > 0)
        def _():
            for rows_, part in zip(_row_chunks(tm), parts):
                dh_ref[rows_, :] += part

    wblk = pl.BlockSpec((None, fb, d), lambda i, j: (j, 0, 0))
    hid = pl.BlockSpec((None, tm, fb), lambda i, j: (j, i, 0))
    return pl.pallas_call(
        body, name="ffn_bwd_in", grid=(s // tm, nb), out_shape=jax.ShapeDtypeStruct((s, d), F32),
        in_specs=[hid, hid, wblk, wblk], out_specs=pl.BlockSpec((tm, d), lambda i, j: (i, 0)),
        compiler_params=_params(2),
    )(dg, du, wg, wu)


def _reduce_adamw(parts, w, m, v, name, after):
    npart, r, c = parts.shape
    tr = _tile(r, max(16, 524288 // c), 16)
    c1, c2 = 1.0 - ADAM_B1 ** ADAM_STEP, 1.0 - ADAM_B2 ** ADAM_STEP

    def body(p_ref, w_ref, m_ref, v_ref, after_ref, g_ref, d_ref, nm_ref, nv_ref):
        g = p_ref[0].astype(F32)
        for k in range(1, npart):
            g = g + p_ref[k].astype(F32)
        nm = ADAM_B1 * m_ref[...] + (1.0 - ADAM_B1) * g
        nv = ADAM_B2 * v_ref[...] + (1.0 - ADAM_B2) * (g * g)
        g_ref[...] = g
        nm_ref[...] = nm
        nv_ref[...] = nv
        d_ref[...] = -ADAM_LR * ((nm / c1) / (jnp.sqrt(nv / c2) + ADAM_EPS) + ADAM_WD * w_ref[...])

    blk = pl.BlockSpec((tr, c), lambda i: (i, 0))
    sds = jax.ShapeDtypeStruct((r, c), F32)
    return pl.pallas_call(
        body, name=name, grid=(r // tr,), out_shape=(sds,) * 4,
        in_specs=[pl.BlockSpec((npart, tr, c), lambda i: (0, i, 0)), blk, blk, blk, _ANY], out_specs=(blk,) * 4,
        compiler_params=_params(1),
    )(parts, w, m, v, after)


def _pack(vecs):
    rows = jnp.concatenate([a.reshape(-1, GRP) for a in vecs], axis=0)
    pad = -rows.shape[0] % 64
    return jnp.pad(rows, ((0, pad), (0, 0)))


def _unpack(rows, shapes):
    out, at = [], 0
    for shp in shapes:
        n = 1
        for k in shp:
            n *= k
        out.append(rows[at:at + n // GRP].reshape(shp))
        at += n // GRP
    return out


def kernel(x, c, w_ada, b_ada, norm1_g, w_in, v_norm_g, w_spatial, b_spatial, out_norm_g, w_out, norm2_g, w_gate, w_up, w_down, final_g, loss_target, m_w_ada, m_b_ada, m_norm1_g, m_w_in, m_v_norm_g, m_w_spatial, m_b_spatial, m_out_norm_g, m_w_out, m_norm2_g, m_w_gate, m_w_up, m_w_down, m_final_g, v_w_ada, v_b_ada, v_norm1_g, v_w_in, v_v_norm_g, v_w_spatial, v_b_spatial, v_out_norm_g, v_w_out, v_norm2_g, v_w_gate, v_w_up, v_w_down, v_final_g):
    s, d = x.shape[1], x.shape[2]
    gm = v_norm_g.shape[1]
    sb = d - gm
    n_in, ffb, ob = w_in.shape[2], w_gate.shape[2], w_out.shape[1]
    xs, tgt = x[0], loss_target[0]
    me = 4 * lax.axis_index("x") + 2 * lax.axis_index("y") + lax.axis_index("c")

    c_all, mod = _ada_exchange(c, w_ada[0])
    c_all, mod = c_all[:, 0, :], mod.reshape(1, 6 * d) + b_ada
    shift1, scale1, gate1, shift2, scale2, gate2 = [mod[:, k * d:(k + 1) * d] for k in range(6)]

    tok, gathers = mod, {}
    tr_ = lambda a: jnp.swapaxes(a, 1, 2)
    w_gate, m_w_gate, v_w_gate, w_up, m_w_up, v_w_up = map(tr_, (w_gate, m_w_gate, v_w_gate, w_up, m_w_up, v_w_up))
    big_w = (("w_in", w_in), ("w_out", w_out), ("w_gate", w_gate), ("w_up", w_up), ("w_down", w_down))
    first, tok = _gather_start([w_in[0].astype(BF16)], tok, "gather_start_w_in")
    rest, tok = _gather_start([w[0].astype(BF16) for _, w in big_w[1:]], tok, "gather_start_weights")
    gathers = {nm: h for (nm, _), h in zip(big_w, first + rest)}
    shift1 = shift1 + tok[0:1, 0:1]

    tm = _tile(s, 512, 16)
    nt = s // tm
    h1 = _prenorm_fwd(xs, norm1_g, scale1, shift1, "prenorm1")
    w_in_all = _gather_wait(_gather_forward(gathers["w_in"], h1, "gather_fwd_w_in")[0], h1, "gather_wait_w_in")
    proj = _mm(h1, w_in_all, pl.BlockSpec((s, d), lambda j: (0, 0)), pl.BlockSpec((None, d, n_in), lambda j: (j, 0, 0)),
               jax.ShapeDtypeStruct((s, NDEV * n_in), F32), pl.BlockSpec((s, n_in), lambda j: (0, j)),
               (NDEV,), NN, False, "proj")
    fwd_w_out, tok = _gather_forward(gathers["w_out"], proj, "gather_fwd_w_out")
    b_cols = jnp.pad(b_spatial[0].T, ((0, 0), (0, GRP - b_spatial.shape[1])))
    on_sb, o_sb, a_sv, s_sv = _sb_fwd(proj, out_norm_g, gm, sb, tok)
    on_gm = _gmlp_fwd(proj, v_norm_g, w_spatial[0], b_cols, out_norm_g, gm, on_sb)
    fwd_w_gate, tok = _gather_forward(gathers["w_gate"], on_gm, "gather_fwd_w_gate")
    o_n = jnp.concatenate([on_gm, on_sb], axis=1)
    w_out_all = _gather_wait(fwd_w_out, tok, "gather_wait_w_out").reshape(d, d)
    rows_1 = pl.BlockSpec((tm, d), lambda i: (i, 0))
    whole_1 = pl.BlockSpec((d, d), lambda i: (0, 0))
    p_out = _mm(o_n, w_out_all, rows_1, whole_1, jax.ShapeDtypeStruct((s, d), F32), rows_1, (nt,), NN, False,
                "out_proj")
    fwd_w_up, tok = _gather_forward(gathers["w_up"], p_out, "gather_fwd_w_up")
    scale2 = scale2 + tok[0:1, 0:1]
    x1, h2 = _residual_prenorm(xs, gate1, p_out, norm2_g, scale2, shift2, "residual1_prenorm2")
    w_gate_all = _gather_wait(fwd_w_gate, h2, "gather_wait_w_gate")
    w_up_all = _gather_wait(fwd_w_up, h2, "gather_wait_w_up")
    g_pre, u_pre, act = _ffn_up(h2, w_gate_all, w_up_all)
    fwd_w_down, tok = _gather_forward(gathers["w_down"], act, "gather_fwd_w_down")
    w_down_all = _gather_wait(fwd_w_down, tok, "gather_wait_w_down")
    tm2 = _tile(s, 1024, 16)
    f_out = _mm(act, w_down_all, pl.BlockSpec((2, tm2, ffb), lambda i, j: (j, i, 0)),
                pl.BlockSpec((2, ffb, d), lambda i, j: (j, 0, 0)), jax.ShapeDtypeStruct((s, d), F32),
                pl.BlockSpec((tm2, d), lambda i, j: (i, 0)), (s // tm2, NDEV // 2), NN, True, "ffn_down")

    dx2, df, st_f = _final_loss(x1, f_out, final_g.reshape(1, d), tgt, gate2)
    dg, du = _ffn_bwd_act(df, g_pre, u_pre, w_down_all)
    dh2 = _ffn_bwd_in(dg, du, w_gate_all, w_up_all)
    hid = pl.BlockSpec((None, s, ffb), lambda j: (j, 0, 0))
    all_d = pl.BlockSpec((s, d), lambda j: (0, 0))

    def second_leg(first, after, tag):
        parts, lands = _scatter_wait1(first, after, "scatter_wait1_" + tag)
        sums = [_chip_sum(p, l, "chip_sum_%s_%d" % (tag, k)) for k, (p, l) in enumerate(zip(parts, lands))]
        return _scatter_start2(sums, after, "scatter_start2_" + tag)

    row_sds = jax.ShapeDtypeStruct((NDEV, ffb, d), BF16)
    row_out = pl.BlockSpec((None, ffb, d), lambda j: (j, 0, 0))
    gw_gate = _mm(dg, h2, hid, all_d, row_sds, row_out, (NDEV,), TN, False, "grad_w_gate")
    gw_up = _mm(du, h2, hid, all_d, row_sds, row_out, (NDEV,), TN, False, "grad_w_up")
    gw_down = _mm(act, df, hid, all_d, row_sds, row_out, (NDEV,), TN, False, "grad_w_down")
    dx1, dp, st2 = _prenorm_bwd(x1, dh2, dx2, f_out, norm2_g, scale2, gate1, "prenorm2_bwd", gw_down)
    first_ffn, tok = _scatter_start1([gw_gate, gw_up, gw_down], dp, "scatter_start1_ffn")
    don = _mm(dp, w_out_all, rows_1, whole_1, jax.ShapeDtypeStruct((s, d), F32), rows_1, (nt,), NT, False,
              "out_proj_bwd", tok)
    gw_out = _mm(o_n, dp, pl.BlockSpec((s, ob), lambda j: (0, j)), all_d,
                 jax.ShapeDtypeStruct((NDEV, ob, d), BF16), pl.BlockSpec((None, ob, d), lambda j: (j, 0, 0)),
                 (NDEV,), TN, False, "grad_w_out", don)
    first_out, tok = _scatter_start1([gw_out], tok, "scatter_start1_out")
    second_ffn, tok = second_leg(first_ffn, tok, "ffn")
    dproj_gm, dw_sp, db_cols, dgv, dgam_gm = _gmlp_bwd(proj, don, v_norm_g, w_spatial[0], b_cols, out_norm_g, gm, tok)
    second_out, tok = second_leg(first_out, dproj_gm, "out")
    dq, dk, dv, dgam_sb = _sb_bwd(proj, o_sb, a_sv, s_sv, don, out_norm_g, gm, sb, tok)
    dproj = jnp.concatenate([dproj_gm, dq, dk, dv], axis=1)
    gw_in = _mm(h1, dproj, all_d, pl.BlockSpec((s, n_in), lambda j: (0, j)),
                jax.ShapeDtypeStruct((NDEV, d, n_in), BF16), pl.BlockSpec((None, d, n_in), lambda j: (j, 0, 0)),
                (NDEV,), TN, False, "grad_w_in")
    first_in, tok = _scatter_start1([gw_in], tok, "scatter_start1_in")
    dh1 = _mm(dproj, w_in_all, pl.BlockSpec((tm2, 2 * n_in), lambda i, j: (i, j)),
              pl.BlockSpec((2, d, n_in), lambda i, j: (j, 0, 0)), jax.ShapeDtypeStruct((s, d), F32),
              pl.BlockSpec((tm2, d), lambda i, j: (i, 0)), (s // tm2, NDEV // 2), NT, True, "in_proj_bwd", tok)
    grad_x, _, st1 = _prenorm_bwd(xs, dh1, dx1, p_out, norm1_g, scale1, gate1, "prenorm1_bwd", tok)

    dmod = jnp.concatenate([st1[0], st1[1], st1[3], st2[0], st2[1], st2[3]])
    loss_row = jnp.pad((0.5 * jnp.sum(st_f[1]) / d).reshape(1, 1), ((0, 0), (0, GRP - 1)))
    small = [st1[2], dgv, dw_sp, db_cols[:, :b_spatial.shape[1]].T, jnp.concatenate([dgam_gm, dgam_sb], axis=1),
             st2[2], st_f[0], dmod, loss_row]
    zero_row = jnp.zeros((1, GRP), F32)
    small_w = [norm1_g, v_norm_g, w_spatial, b_spatial, out_norm_g, norm2_g, final_g, b_ada, zero_row]
    small_m = [m_norm1_g, m_v_norm_g, m_w_spatial, m_b_spatial, m_out_norm_g, m_norm2_g, m_final_g, m_b_ada, zero_row]
    small_v = [v_norm1_g, v_v_norm_g, v_w_spatial, v_b_spatial, v_out_norm_g, v_norm2_g, v_final_g, v_b_ada, zero_row]
    (small_first,), tok = _gather_start([_pack(small)], grad_x, "gather_small_start")
    second_in, tok = second_leg(first_in, tok, "in")

    big = {}
    prev = tok
    for second, group in ((second_ffn, (("w_gate", w_gate, m_w_gate, v_w_gate), ("w_up", w_up, m_w_up, v_w_up),
                                        ("w_down", w_down, m_w_down, v_w_down))),
                          (second_out, (("w_out", w_out, m_w_out, v_w_out),))):
        chip_sums = _scatter_wait2(second, prev, "scatter_wait2_" + group[0][0])
        for (nm, w, m, v), part in zip(group, chip_sums):
            big[nm] = _reduce_adamw(part, w[0], m[0], v[0], "adamw_" + nm, tok)
            prev = big[nm][1]

    small_second, tok2 = _gather_forward(small_first, prev, "gather_small_fwd")
    small_all = _gather_wait(small_second, tok2, "gather_small_wait")
    sm = _reduce_adamw(small_all, _pack(small_w), _pack(small_m), _pack(small_v), "adamw_small", tok)
    shapes = [a.shape for a in small_w]
    sm_g, sm_d, sm_m, sm_v = [_unpack(t, shapes) for t in sm]
    loss = sm_g[8][0, 0]

    at = sum(a.size for a in small_w[:7]) // GRP
    dmod_all = small_all[:, at:at + 6 * d // GRP, :].reshape(NDEV, 6 * d)
    n_ada = w_ada.shape[2]
    dmod_cols = lax.dynamic_slice(dmod_all, (0, me * n_ada), (NDEV, n_ada))
    g_ada = _ada_bwd(c_all, dmod_cols)
    big["w_ada"] = _reduce_adamw(g_ada[None], w_ada[0], m_w_ada[0], v_w_ada[0], "adamw_w_ada", sm[0])
    part_in, = _scatter_wait2(second_in, big["w_ada"][1], "scatter_wait2_w_in")
    big["w_in"] = _reduce_adamw(part_in, w_in[0], m_w_in[0], v_w_in[0], "adamw_w_in", tok)

    names = ["w_ada", "b_ada", "norm1_g", "w_in", "v_norm_g", "w_spatial", "b_spatial", "out_norm_g", "w_out",
             "norm2_g", "w_gate", "w_up", "w_down", "final_g"]
    small_at = {"norm1_g": 0, "v_norm_g": 1, "w_spatial": 2, "b_spatial": 3, "out_norm_g": 4, "norm2_g": 5,
                "final_g": 6, "b_ada": 7}
    outs = [[], [], [], []]
    for nm in names:
        for k in range(4):
            if nm in big:
                res = big[nm][k][None]
                outs[k].append(tr_(res) if nm in ("w_gate", "w_up") else res)
            else:
                outs[k].append((sm_g, sm_d, sm_m, sm_v)[k][small_at[nm]])
    return (loss, grad_x[None], *outs[0], *outs[1], *outs[2], *outs[3])
```

```python
import jax
import jax.numpy as jnp
from jax import lax
from jax.experimental import pallas as pl
from jax.experimental.pallas import tpu as pltpu

F32, BF16 = jnp.float32, jnp.bfloat16
NDEV = 8
GRP = 128
EPS = 1e-6
VMEM_BYTES = 64 * 2 ** 20
VMEM_LIMIT = VMEM_BYTES - 8 * 2 ** 20
ADAM_LR, ADAM_B1, ADAM_B2, ADAM_EPS, ADAM_WD, ADAM_STEP = 0.001, 0.9, 0.999, 1e-08, 0.01, 10
MESH = pl.DeviceIdType.MESH
NN = (((1,), (0,)), ((), ()))
NT = (((1,), (1,)), ((), ()))
TN = (((0,), (0,)), ((), ()))


def _params(n_axes):
    return pltpu.CompilerParams(dimension_semantics=("arbitrary",) * n_axes, vmem_limit_bytes=VMEM_LIMIT)


def _tile(n, cap, mult):
    best = None
    for t in range(mult, min(n, cap) + 1, mult):
        if n % t == 0:
            best = t
    assert best is not None, (n, cap, mult)
    return best


def _dot(a, b, dims):
    return lax.dot_general(a, b, dims, preferred_element_type=F32)


_HBM = pl.BlockSpec(memory_space=pltpu.HBM)
_SEM = pl.BlockSpec(memory_space=pltpu.SEMAPHORE)
_ANY = pl.BlockSpec(memory_space=pl.ANY)
_EFFECT = pltpu.SideEffectType.DATAFLOW_SIDE_EFFECTING


def _chip_peers():
    x, y, c = lax.axis_index("x"), lax.axis_index("y"), lax.axis_index("c")
    chips = [(x, 1 - y), (1 - x, y), (1 - x, 1 - y)]
    return 4 * x + 2 * y + c, (x, y, 1 - c), [((px, py, c), 4 * px + 2 * py + c) for px, py in chips]


def _gather_start(srcs, after, name):
    n = len(srcs)
    me = 4 * lax.axis_index("x") + 2 * lax.axis_index("y") + lax.axis_index("c")
    lands = [lax.dynamic_update_slice(lax.empty((NDEV,) + a.shape, a.dtype), a[None], (me,) + (0,) * a.ndim) for a in srcs]

    def body(*refs):
        src_refs, land_refs = refs[:n], refs[n:2 * n]
        sems, token = refs[2 * n + 1:2 * n + 1 + 2 * n], refs[-1]
        my, sibling, chips = _chip_peers()
        for a in range(n):
            for k, to in enumerate([sibling] + [coords for coords, _ in chips]):
                pltpu.make_async_remote_copy(src_ref=src_refs[a], dst_ref=land_refs[a].at[my], send_sem=sems[2 * a].at[k],
                                             recv_sem=sems[2 * a + 1].at[k], device_id=to, device_id_type=MESH).start()
        token[...] = jnp.zeros_like(token)

    hbm = lambda arrs: tuple(pltpu.HBM(a.shape, a.dtype) for a in arrs)
    res = pl.pallas_call(
        body, name=name,
        out_shape=(pltpu.SemaphoreType.DMA((4,)),) * (2 * n) + hbm(srcs) + hbm(lands) + (jax.ShapeDtypeStruct((8, GRP), F32),),
        in_specs=(_HBM,) * (2 * n) + (_ANY,),
        out_specs=(_SEM,) * (2 * n) + (_HBM,) * (2 * n) + (pl.BlockSpec(memory_space=pltpu.VMEM),),
        input_output_aliases={k: 2 * n + k for k in range(2 * n)}, compiler_params=pltpu.CompilerParams(has_side_effects=_EFFECT),
    )(*[pltpu.with_memory_space_constraint(a, pltpu.HBM) for a in list(srcs) + lands], after)
    return [(res[2 * a], res[2 * a + 1], res[2 * n + a], res[3 * n + a]) for a in range(n)], res[-1]


def _gather_forward(handles, after, name):
    send_sems, recv_sems, src_thru, land_thru = handles

    def body(src_ref, land_ref, send_sems, recv_sems, after_ref, send2, recv2, land_out, token):
        my, sibling, chips = _chip_peers()
        for k, (to, p) in enumerate([(sibling, my ^ 1)] + chips):
            first = pltpu.make_async_remote_copy(src_ref=src_ref, dst_ref=land_ref.at[p], send_sem=send_sems.at[k],
                                                 recv_sem=recv_sems.at[k], device_id=to, device_id_type=MESH)
            first.wait_send()
            first.wait_recv()
        for k, (_, p) in enumerate(chips):
            pltpu.make_async_remote_copy(src_ref=land_ref.at[p], dst_ref=land_ref.at[p], send_sem=send2.at[k],
                                         recv_sem=recv2.at[k], device_id=sibling, device_id_type=MESH).start()
        token[...] = jnp.zeros_like(token)

    res = pl.pallas_call(
        body, name=name,
        out_shape=(pltpu.SemaphoreType.DMA((3,)), pltpu.SemaphoreType.DMA((3,)), pltpu.HBM(land_thru.shape, land_thru.dtype),
                   jax.ShapeDtypeStruct((8, GRP), F32)),
        in_specs=(_HBM, _HBM, _SEM, _SEM, _ANY), out_specs=(_SEM, _SEM, _HBM, pl.BlockSpec(memory_space=pltpu.VMEM)),
        input_output_aliases={1: 2}, compiler_params=pltpu.CompilerParams(has_side_effects=_EFFECT),
    )(src_thru, land_thru, send_sems, recv_sems, after)
    return tuple(res[:3]), res[3]


def _gather_wait(handles, after, name):
    send2, recv2, land_thru = handles

    def body(land_ref, send2, recv2, after_ref, got_ref):
        _, sibling, chips = _chip_peers()
        for k, (_, p) in enumerate(chips):
            cp = pltpu.make_async_remote_copy(src_ref=land_ref.at[p], dst_ref=land_ref.at[p ^ 1], send_sem=send2.at[k],
                                              recv_sem=recv2.at[k], device_id=sibling, device_id_type=MESH)
            cp.wait_send()
            cp.wait_recv()

    return pl.pallas_call(
        body, name=name, out_shape=pltpu.HBM(land_thru.shape, land_thru.dtype),
        in_specs=(_HBM, _SEM, _SEM, _ANY), out_specs=_HBM, input_output_aliases={0: 0},
        compiler_params=pltpu.CompilerParams(has_side_effects=_EFFECT),
    )(land_thru, send2, recv2, after)


NCHIP = NDEV // 2


def _scatter_start1(parts, after, name):
    n = len(parts)
    lands = [lax.empty((NCHIP,) + p.shape[1:], p.dtype) for p in parts]

    def body(*refs):
        part_refs, land_refs = refs[:n], refs[n:2 * n]
        send_sems, recv_sems, token = refs[2 * n + 1], refs[2 * n + 2], refs[-1]
        _, sibling, _ = _chip_peers()
        c = lax.axis_index("c")
        for a in range(n):
            for q in range(NCHIP):
                pltpu.make_async_remote_copy(
                    src_ref=part_refs[a].at[2 * q + 1 - c], dst_ref=land_refs[a].at[q], send_sem=send_sems.at[a * NCHIP + q],
                    recv_sem=recv_sems.at[a * NCHIP + q], device_id=sibling, device_id_type=MESH).start()
        token[...] = jnp.zeros_like(token)

    hbm = lambda arrs: tuple(pltpu.HBM(a.shape, a.dtype) for a in arrs)
    res = pl.pallas_call(
        body, name=name,
        out_shape=(pltpu.SemaphoreType.DMA((n * NCHIP,)), pltpu.SemaphoreType.DMA((n * NCHIP,))) + hbm(parts) + hbm(lands)
        + (jax.ShapeDtypeStruct((8, GRP), F32),),
        in_specs=(_HBM,) * (2 * n) + (_ANY,),
        out_specs=(_SEM, _SEM) + (_HBM,) * (2 * n) + (pl.BlockSpec(memory_space=pltpu.VMEM),),
        input_output_aliases={k: 2 + k for k in range(2 * n)}, compiler_params=pltpu.CompilerParams(has_side_effects=_EFFECT),
    )(*[pltpu.with_memory_space_constraint(a, pltpu.HBM) for a in list(parts) + lands], after)
    return (n,) + tuple(res[:-1]), res[-1]


def _scatter_wait1(handles, after, name):
    n, send_sems, recv_sems = handles[:3]
    thru = handles[3:]

    def body(*refs):
        part_refs, land_refs, send_sems, recv_sems = refs[:n], refs[n:2 * n], refs[2 * n], refs[2 * n + 1]
        _, sibling, _ = _chip_peers()
        for a in range(n):
            for q in range(NCHIP):
                cp = pltpu.make_async_remote_copy(
                    src_ref=part_refs[a].at[q], dst_ref=land_refs[a].at[q], send_sem=send_sems.at[a * NCHIP + q],
                    recv_sem=recv_sems.at[a * NCHIP + q], device_id=sibling, device_id_type=MESH)
                cp.wait_send()
                cp.wait_recv()

    res = pl.pallas_call(
        body, name=name, out_shape=tuple(pltpu.HBM(a.shape, a.dtype) for a in thru),
        in_specs=(_HBM,) * (2 * n) + (_SEM, _SEM, _ANY), out_specs=(_HBM,) * (2 * n),
        input_output_aliases={k: k for k in range(2 * n)}, compiler_params=pltpu.CompilerParams(has_side_effects=_EFFECT),
    )(*thru, send_sems, recv_sems, after)
    return res[:n], res[n:]


def _chip_sum(part, land, name):
    _, r, c = part.shape
    tr = _tile(r, max(16, 2 ** 21 // c), 16)

    def body(core_ref, p_ref, l_ref, o_ref):
        o_ref[...] = (p_ref[...].astype(F32) + l_ref[...].astype(F32)).astype(o_ref.dtype)

    blk = pl.BlockSpec((None, tr, c), lambda q, i, core: (q, i, 0))
    return pl.pallas_call(
        body, name=name, out_shape=jax.ShapeDtypeStruct(land.shape, land.dtype),
        grid_spec=pltpu.PrefetchScalarGridSpec(
            num_scalar_prefetch=1, grid=(NCHIP, r // tr),
            in_specs=[pl.BlockSpec((None, None, tr, c), lambda q, i, core: (q, core[0], i, 0)), blk], out_specs=blk),
        compiler_params=_params(2),
    )(lax.axis_index("c").astype(jnp.int32).reshape(1), part.reshape(NCHIP, 2, r, c), land)


def _scatter_start2(sums, after, name):
    n = len(sums)
    chip = 2 * lax.axis_index("x") + lax.axis_index("y")
    lands = [lax.dynamic_update_slice(lax.empty(s_.shape, s_.dtype), lax.dynamic_index_in_dim(s_, chip, 0, keepdims=True),
                                      (chip,) + (0,) * (s_.ndim - 1)) for s_ in sums]

    def body(*refs):
        sum_refs, land_refs = refs[:n], refs[n:2 * n]
        send_sems, recv_sems, token = refs[2 * n + 1], refs[2 * n + 2], refs[-1]
        my, _, chips = _chip_peers()
        for a in range(n):
            for k, (to, p) in enumerate(chips):
                pltpu.make_async_remote_copy(
                    src_ref=sum_refs[a].at[p // 2], dst_ref=land_refs[a].at[my // 2], send_sem=send_sems.at[a * 3 + k],
                    recv_sem=recv_sems.at[a * 3 + k], device_id=to, device_id_type=MESH).start()
        token[...] = jnp.zeros_like(token)

    hbm = lambda arrs: tuple(pltpu.HBM(a.shape, a.dtype) for a in arrs)
    res = pl.pallas_call(
        body, name=name,
        out_shape=(pltpu.SemaphoreType.DMA((n * 3,)), pltpu.SemaphoreType.DMA((n * 3,))) + hbm(sums) + hbm(lands)
        + (jax.ShapeDtypeStruct((8, GRP), F32),),
        in_specs=(_HBM,) * (2 * n) + (_ANY,),
        out_specs=(_SEM, _SEM) + (_HBM,) * (2 * n) + (pl.BlockSpec(memory_space=pltpu.VMEM),),
        input_output_aliases={k: 2 + k for k in range(2 * n)}, compiler_params=pltpu.CompilerParams(has_side_effects=_EFFECT),
    )(*[pltpu.with_memory_space_constraint(a, pltpu.HBM) for a in list(sums) + lands], after)
    return (n,) + tuple(res[:-1]), res[-1]


def _scatter_wait2(handles, after, name):
    n, send_sems, recv_sems = handles[:3]
    thru = handles[3:]

    def body(*refs):
        sum_refs, land_refs, send_sems, recv_sems = refs[:n], refs[n:2 * n], refs[2 * n], refs[2 * n + 1]
        _, _, chips = _chip_peers()
        for a in range(n):
            for k, (to, p) in enumerate(chips):
                cp = pltpu.make_async_remote_copy(
                    src_ref=sum_refs[a].at[p // 2], dst_ref=land_refs[a].at[p // 2], send_sem=send_sems.at[a * 3 + k],
                    recv_sem=recv_sems.at[a * 3 + k], device_id=to, device_id_type=MESH)
                cp.wait_send()
                cp.wait_recv()

    res = pl.pallas_call(
        body, name=name, out_shape=tuple(pltpu.HBM(a.shape, a.dtype) for a in thru),
        in_specs=(_HBM,) * (2 * n) + (_SEM, _SEM, _ANY), out_specs=(_HBM,) * (2 * n),
        input_output_aliases={k: k for k in range(2 * n)}, compiler_params=pltpu.CompilerParams(has_side_effects=_EFFECT),
    )(*thru, send_sems, recv_sems, after)
    return res[n:]


def _mm(a, b, a_spec, b_spec, out_sds, o_spec, grid, dims, acc, name, after=None):
    extra = () if after is None else (after,)
    assert not acc or out_sds.dtype == F32

    def body(a_ref, b_ref, *rest):
        o_ref = rest[len(extra)]
        if len(b_ref.shape) == 3:
            nkb = b_ref.shape[0]
            wk = a_ref.shape[-1] // nkb
            prod = sum(_dot(a_ref[k] if len(a_ref.shape) == 3 else a_ref[:, k * wk:(k + 1) * wk], b_ref[k], dims)
                       for k in range(nkb))
        else:
            prod = _dot(a_ref[...], b_ref[...], dims)
        if not acc:
            o_ref[...] = prod.astype(o_ref.dtype)
            return
        k = pl.program_id(len(grid) - 1)

        @pl.when(k == 0)
        def _():
            o_ref[...] = prod

        @pl.when(k > 0)
        def _():
            o_ref[...] += prod

    return pl.pallas_call(
        body, name=name, grid=grid, out_shape=out_sds, in_specs=[a_spec, b_spec] + [_ANY] * len(extra), out_specs=o_spec,
        compiler_params=_params(len(grid)),
    )(a, b, *extra)


def _row_spec(tm, d):
    return pl.BlockSpec((tm, d), lambda i: (i, 0))


def _vec_spec(d):
    return pl.BlockSpec((1, d), lambda i: (0, 0))


def _prenorm_fwd(x, g, scale, shift, name):
    s, d = x.shape
    tm = _tile(s, 256, 16)

    def body(x_ref, g_ref, sc_ref, sh_ref, h_ref):
        xv = x_ref[...]
        rstd = lax.rsqrt(jnp.mean(xv * xv, axis=-1, keepdims=True) + EPS)
        h_ref[...] = ((xv * rstd * g_ref[...]) * (1.0 + sc_ref[...]) + sh_ref[...]).astype(BF16)

    return pl.pallas_call(
        body, name=name, grid=(s // tm,), out_shape=jax.ShapeDtypeStruct((s, d), BF16),
        in_specs=[_row_spec(tm, d), _vec_spec(d), _vec_spec(d), _vec_spec(d)], out_specs=_row_spec(tm, d),
        compiler_params=_params(1),
    )(x, g, scale, shift)


def _residual_prenorm(x, gate, p, g, scale, shift, name):
    s, d = x.shape
    tm = _tile(s, 256, 16)

    def body(x_ref, gate_ref, p_ref, g_ref, sc_ref, sh_ref, x1_ref, h_ref):
        xv = x_ref[...] + gate_ref[...] * p_ref[...]
        x1_ref[...] = xv
        rstd = lax.rsqrt(jnp.mean(xv * xv, axis=-1, keepdims=True) + EPS)
        h_ref[...] = ((xv * rstd * g_ref[...]) * (1.0 + sc_ref[...]) + sh_ref[...]).astype(BF16)

    return pl.pallas_call(
        body, name=name, grid=(s // tm,),
        out_shape=(jax.ShapeDtypeStruct((s, d), F32), jax.ShapeDtypeStruct((s, d), BF16)),
        in_specs=[_row_spec(tm, d), _vec_spec(d), _row_spec(tm, d), _vec_spec(d), _vec_spec(d), _vec_spec(d)],
        out_specs=(_row_spec(tm, d), _row_spec(tm, d)), compiler_params=_params(1),
    )(x, gate, p, g, scale, shift)


def _final_loss(x1, f, final_g, target, gate2):
    s, d = x1.shape
    tm = _tile(s, 256, 16)

    def body(x_ref, f_ref, g_ref, t_ref, gate_ref, dx_ref, df_ref, st_ref):
        @pl.when(pl.program_id(0) == 0)
        def _():
            st_ref[...] = jnp.zeros_like(st_ref)

        xv, gf = x_ref[...] + gate_ref[...] * f_ref[...], g_ref[...]
        rstd = lax.rsqrt(jnp.mean(xv * xv, axis=-1, keepdims=True) + EPS)
        xhat = xv * rstd
        err = xhat * gf - t_ref[...]
        dy = err * (1.0 / d)
        gdy = dy * gf
        dx = rstd * (gdy - xhat * jnp.mean(gdy * xhat, axis=-1, keepdims=True))
        dx_ref[...] = dx
        df_ref[...] = (gate_ref[...] * dx).astype(BF16)
        st_ref[0:1, :] += jnp.sum(dy * xhat, axis=0, keepdims=True)
        st_ref[1:2, :] += jnp.sum(err * err, axis=0, keepdims=True)

    return pl.pallas_call(
        body, name="final_loss", grid=(s // tm,),
        out_shape=(jax.ShapeDtypeStruct((s, d), F32), jax.ShapeDtypeStruct((s, d), BF16),
                   jax.ShapeDtypeStruct((8, d), F32)),
        in_specs=[_row_spec(tm, d), _row_spec(tm, d), _vec_spec(d), _row_spec(tm, d), _vec_spec(d)],
        out_specs=(_row_spec(tm, d), _row_spec(tm, d), pl.BlockSpec((8, d), lambda i: (0, 0))),
        compiler_params=_params(1),
    )(x1, f, final_g, target, gate2)


def _prenorm_bwd(xin, dh, dres, pf, g, scale, gate_next, name, after):
    s, d = xin.shape
    tm = _tile(s, 256, 16)

    def body(x_ref, dh_ref, dr_ref, pf_ref, g_ref, sc_ref, gn_ref, after_ref, dx_ref, dn_ref, st_ref):
        @pl.when(pl.program_id(0) == 0)
        def _():
            st_ref[...] = jnp.zeros_like(st_ref)

        xv, dhv, drv, gv = x_ref[...], dh_ref[...], dr_ref[...], g_ref[...]
        one_sc = 1.0 + sc_ref[...]
        rstd = lax.rsqrt(jnp.mean(xv * xv, axis=-1, keepdims=True) + EPS)
        xhat = xv * rstd
        dxhat = dhv * (gv * one_sc)
        dx = drv + rstd * (dxhat - xhat * jnp.mean(dxhat * xhat, axis=-1, keepdims=True))
        dx_ref[...] = dx
        dn_ref[...] = (gn_ref[...] * dx).astype(BF16)
        dhx = dhv * xhat
        st_ref[0:1, :] += jnp.sum(dhv, axis=0, keepdims=True)
        st_ref[1:2, :] += jnp.sum(dhx, axis=0, keepdims=True) * gv
        st_ref[2:3, :] += jnp.sum(dhx, axis=0, keepdims=True) * one_sc
        st_ref[3:4, :] += jnp.sum(drv * pf_ref[...], axis=0, keepdims=True)

    return pl.pallas_call(
        body, name=name, grid=(s // tm,),
        out_shape=(jax.ShapeDtypeStruct((s, d), F32), jax.ShapeDtypeStruct((s, d), BF16),
                   jax.ShapeDtypeStruct((8, d), F32)),
        in_specs=[_row_spec(tm, d)] * 4 + [_vec_spec(d)] * 3 + [_ANY],
        out_specs=(_row_spec(tm, d), _row_spec(tm, d), pl.BlockSpec((8, d), lambda i: (0, 0))),
        compiler_params=_params(1),
    )(xin, dh, dres, pf, g, scale, gate_next, after)


def _ada_exchange(c, w_loc):
    _, d = c.shape
    n = w_loc.shape[1]
    tn = _tile(n, 512, 128) if n % 128 == 0 else n

    def body(c_ref, w_hbm, call_ref, mod_ref, w_ref, rows, send_a, recv_a, send_b, recv_b, local_sems):
        fetch_w = pltpu.make_async_copy(w_hbm, w_ref, local_sems.at[2])
        fetch_w.start()
        x, y, cc = lax.axis_index("x"), lax.axis_index("y"), lax.axis_index("c")
        me = 4 * x + 2 * y + cc
        peers = []
        for k in range(1, NDEV):
            px, py, pc = x ^ (k >> 2), y ^ ((k >> 1) & 1), cc ^ (k & 1)
            peers.append((k - 1, (px, py, pc), 4 * px + 2 * py + pc))

        def exchange(src_of, dst_ref, send_sems, recv_sems, local_sem):
            local = pltpu.make_async_copy(src_of(me), dst_ref.at[me], local_sem)
            local.start()
            sends = [pltpu.make_async_remote_copy(src_ref=src_of(p), dst_ref=dst_ref.at[me], send_sem=send_sems.at[k],
                                                  recv_sem=recv_sems.at[k], device_id=to, device_id_type=MESH)
                     for k, to, p in peers]
            for cp in sends:
                cp.start()
            for k, to, p in peers:
                pltpu.make_async_remote_copy(src_ref=src_of(p), dst_ref=dst_ref.at[p], send_sem=send_sems.at[k],
                                             recv_sem=recv_sems.at[k], device_id=to, device_id_type=MESH).wait_recv()
            for cp in sends:
                cp.wait_send()
            local.wait()

        exchange(lambda p: c_ref, call_ref, send_a, recv_a, local_sems.at[0])
        cv = call_ref[:, 0, :]
        act = cv * jax.nn.sigmoid(cv)
        fetch_w.wait()
        for j in range(n // tn):
            rows[:, 0, j * tn:(j + 1) * tn] = jnp.dot(act, w_ref[:, j * tn:(j + 1) * tn], preferred_element_type=F32,
                                                      precision=lax.Precision.HIGHEST)
        exchange(lambda p: rows.at[p], mod_ref, send_b, recv_b, local_sems.at[1])

    vmem = pl.BlockSpec(memory_space=pltpu.VMEM)
    return pl.pallas_call(
        body, name="ada_exchange",
        out_shape=(jax.ShapeDtypeStruct((NDEV, 1, d), F32), jax.ShapeDtypeStruct((NDEV, 1, n), F32)),
        in_specs=[vmem, _ANY], out_specs=(vmem, vmem),
        scratch_shapes=[pltpu.VMEM((d, n), F32), pltpu.VMEM((NDEV, 1, n), F32)] + [pltpu.SemaphoreType.DMA((NDEV - 1,))] * 4
        + [pltpu.SemaphoreType.DMA((3,))],
        compiler_params=pltpu.CompilerParams(vmem_limit_bytes=VMEM_LIMIT),
    )(c, w_loc)


def _ada_bwd(c_all, dmod_cols):
    nb, d = c_all.shape
    n = dmod_cols.shape[1]
    tn = _tile(n, 512, 128) if n % 128 == 0 else n

    def body(c_ref, dm_ref, o_ref):
        cv = c_ref[...]
        o_ref[...] = lax.dot_general(cv * jax.nn.sigmoid(cv), dm_ref[...], TN, preferred_element_type=F32,
                                     precision=lax.Precision.HIGHEST)

    return pl.pallas_call(
        body, name="ada_bwd", grid=(n // tn,), out_shape=jax.ShapeDtypeStruct((d, n), F32),
        in_specs=[pl.BlockSpec((nb, d), lambda j: (0, 0)), pl.BlockSpec((nb, tn), lambda j: (0, j))],
        out_specs=pl.BlockSpec((d, tn), lambda j: (0, j)), compiler_params=_params(1),
    )(c_all, dmod_cols)


_INV_SQRT2 = 0.7071067811865476
_INV_SQRT2PI = 0.3989422804014327


def _gelu_parts(x):
    cdf = 0.5 * (1.0 + lax.erf(x * _INV_SQRT2))
    return x * cdf, cdf + x * jnp.exp(-0.5 * x * x) * _INV_SQRT2PI


def _gm_group_fwd(up, vp, gv, wt, bcol):
    u = 0.5 * up * (1.0 + lax.erf(up * _INV_SQRT2))
    va = 0.5 * vp * (1.0 + lax.erf(vp * _INV_SQRT2))
    return _gm_group_mix(u, va, gv, wt, bcol)


def _gm_group_mix(u, va, gv, wt, bcol):
    xc = va - jnp.mean(va, axis=-1, keepdims=True)
    rstd_v = lax.rsqrt(jnp.mean(xc * xc, axis=-1, keepdims=True) + EPS)
    yv = xc * rstd_v
    vn = (yv * gv).astype(BF16)
    mixed = _dot(wt, vn, NN) + bcol
    return u, rstd_v, yv, vn, mixed, u * mixed


def _tril_bf16(w):
    row = lax.broadcasted_iota(jnp.int32, w.shape, 0)
    col = lax.broadcasted_iota(jnp.int32, w.shape, 1)
    return jnp.where(col <= row, w, 0.0).astype(BF16)


def _gmlp_fwd(proj, v_norm_g, w_spatial, b_cols, gamma, gm, after):
    s = proj.shape[0]
    ng = gm // GRP

    def body(p_ref, gv_ref, w_ref, b_ref, gam_ref, after_ref, o_ref):
        for g in range(ng):
            lo = g * GRP
            wt = _tril_bf16(w_ref[g])
            *_, o = _gm_group_fwd(p_ref[:, lo:lo + GRP], p_ref[:, gm + lo:gm + lo + GRP], gv_ref[:, lo:lo + GRP],
                                  wt, b_ref[:, g:g + 1])
            rstd_o = lax.rsqrt(jnp.mean(o * o, axis=-1, keepdims=True) + EPS)
            o_ref[:, lo:lo + GRP] = (o * rstd_o * gam_ref[:, lo:lo + GRP]).astype(BF16)

    return pl.pallas_call(
        body, name="gmlp_fwd", grid=(s // GRP,), out_shape=jax.ShapeDtypeStruct((s, gm), BF16),
        in_specs=[pl.BlockSpec((GRP, 2 * gm), lambda n: (n, 0)), _vec_spec(gm),
                  pl.BlockSpec((ng, GRP, GRP), lambda n: (0, 0, 0)), pl.BlockSpec((GRP, GRP), lambda n: (0, 0)),
                  _vec_spec(gm), _ANY],
        out_specs=pl.BlockSpec((GRP, gm), lambda n: (n, 0)), compiler_params=_params(1),
    )(proj, v_norm_g, w_spatial, b_cols, gamma, after)


def _gmlp_bwd(proj, don, v_norm_g, w_spatial, b_cols, gamma, gm, after):
    s = proj.shape[0]
    ng = gm // GRP

    def body(p_ref, don_ref, gv_ref, w_ref, b_ref, gam_ref, after_ref, dp_ref, dw_ref, db_ref, dgv_ref, dgam_ref):
        @pl.when(pl.program_id(0) == 0)
        def _():
            dw_ref[...] = jnp.zeros_like(dw_ref)
            db_ref[...] = jnp.zeros_like(db_ref)
            dgv_ref[...] = jnp.zeros_like(dgv_ref)
            dgam_ref[...] = jnp.zeros_like(dgam_ref)

        lane = lax.broadcasted_iota(jnp.int32, (GRP, GRP), 1)
        row = lax.broadcasted_iota(jnp.int32, (GRP, GRP), 0)
        for g in range(ng):
            lo = g * GRP
            up, vp = p_ref[:, lo:lo + GRP], p_ref[:, gm + lo:gm + lo + GRP]
            gv, gam = gv_ref[:, lo:lo + GRP], gam_ref[:, lo:lo + GRP]
            wt = _tril_bf16(w_ref[g])
            (u, du_dup), (va, dva_dvp) = _gelu_parts(up), _gelu_parts(vp)
            u, rstd_v, yv, vn, mixed, o = _gm_group_mix(u, va, gv, wt, b_ref[:, g:g + 1])
            rstd_o = lax.rsqrt(jnp.mean(o * o, axis=-1, keepdims=True) + EPS)
            ohat = o * rstd_o
            dn = don_ref[:, lo:lo + GRP]
            dgam_ref[:, lo:lo + GRP] += jnp.sum(dn * ohat, axis=0, keepdims=True)
            dohat = dn * gam
            do = rstd_o * (dohat - ohat * jnp.mean(dohat * ohat, axis=-1, keepdims=True))
            du = do * mixed
            dmixed = do * u
            dmb = dmixed.astype(BF16)
            db_ref[...] += jnp.where(lane == g, jnp.sum(dmixed, axis=-1, keepdims=True), 0.0)
            dw_ref[g] += jnp.where(lane <= row, _dot(dmb, vn, NT), 0.0)
            dvn = _dot(wt, dmb, TN)
            dgv_ref[:, lo:lo + GRP] += jnp.sum(dvn * yv, axis=0, keepdims=True)
            dyv = dvn * gv
            dva = rstd_v * (dyv - jnp.mean(dyv, axis=-1, keepdims=True)
                            - yv * jnp.mean(dyv * yv, axis=-1, keepdims=True))
            dp_ref[:, lo:lo + GRP] = (du * du_dup).astype(BF16)
            dp_ref[:, gm + lo:gm + lo + GRP] = (dva * dva_dvp).astype(BF16)

    const2 = lambda n: (0, 0)
    return pl.pallas_call(
        body, name="gmlp_bwd", grid=(s // GRP,),
        out_shape=(jax.ShapeDtypeStruct((s, 2 * gm), BF16), jax.ShapeDtypeStruct((ng, GRP, GRP), F32),
                   jax.ShapeDtypeStruct((GRP, GRP), F32), jax.ShapeDtypeStruct((1, gm), F32),
                   jax.ShapeDtypeStruct((1, gm), F32)),
        in_specs=[pl.BlockSpec((GRP, 2 * gm), lambda n: (n, 0)), pl.BlockSpec((GRP, gm), lambda n: (n, 0)),
                  _vec_spec(gm), pl.BlockSpec((ng, GRP, GRP), lambda n: (0, 0, 0)),
                  pl.BlockSpec((GRP, GRP), const2), _vec_spec(gm), _ANY],
        out_specs=(pl.BlockSpec((GRP, 2 * gm), lambda n: (n, 0)), pl.BlockSpec((ng, GRP, GRP), lambda n: (0, 0, 0)),
                   pl.BlockSpec((GRP, GRP), const2), _vec_spec(gm), _vec_spec(gm)),
        compiler_params=_params(1),
    )(proj, don, v_norm_g, w_spatial, b_cols, gamma, after)


BLK = 512


def _log_sigmoid(z):
    return jnp.minimum(z, 0.0) - jnp.log(1.0 + jnp.exp(-jnp.abs(z)))


def _split_dot(x, tri, passes):
    n = x.shape[0]
    parts, rest = [], x
    for _ in range(passes):
        hi = rest.astype(BF16)
        parts.append(hi)
        rest = rest - hi.astype(F32)
    res = _dot(jnp.concatenate(parts, axis=0), tri, NN)
    out = res[0:n]
    for k in range(1, passes):
        out = out + res[k * n:(k + 1) * n]
    return out


HEADS_PER_STEP = 2


def _sb_fwd(proj, gamma, gm, sb, after):
    s = proj.shape[0]
    hp = min(HEADS_PER_STEP, sb // GRP)
    w = hp * GRP
    nhp, nq = sb // w, s // BLK
    qc, kc, vc, gc = 2 * gm // w, (2 * gm + sb) // w, (2 * gm + 2 * sb) // w, gm // w
    scale = GRP ** -0.5

    def body(q_ref, k_ref, v_ref, gam_ref, after_ref, on_ref, o_ref, a_ref, s_ref, kb, vb):
        i = pl.program_id(1)

        @pl.when(i == 0)
        def _():
            kb[...] = k_ref[...].astype(BF16)
            vb[...] = v_ref[...].astype(BF16)

        qb = q_ref[...].astype(BF16)
        row = lax.broadcasted_iota(jnp.int32, (BLK, BLK), 0)
        col = lax.broadcasted_iota(jnp.int32, (BLK, BLK), 1)
        later = (row > col).astype(BF16)

        def block(j, carry, diag):
            off = pl.multiple_of(j * BLK, BLK)
            kj, vj = kb[pl.ds(off, BLK), :], vb[pl.ds(off, BLK), :]
            out = []
            for h in range(hp):
                tail, acc = carry[h]
                sl = slice(h * GRP, (h + 1) * GRP)
                z = _dot(qb[:, sl], kj[:, sl], NT) * scale
                lb = _log_sigmoid(z)
                l1m = lb - z
                sig = jnp.exp(lb)
                if diag:
                    l1m = jnp.where(col < row, l1m, 0.0)
                    sig = jnp.where(col < row, sig, 0.0)
                after_s = _split_dot(l1m, later, 2)
                a = jnp.exp(lb + after_s + tail)
                if diag:
                    a = jnp.where(col < row, a, 0.0)
                ab = a.astype(BF16)
                a_ref[h, j] = ab
                s_ref[h, j] = sig.astype(BF16)
                out.append((tail + after_s[:, 0:1] + l1m[:, 0:1], acc + _dot(ab, vj[:, sl], NN)))
            return tuple(out)

        init = tuple((jnp.zeros((BLK, 1), F32), jnp.zeros((BLK, GRP), F32)) for _ in range(hp))
        carry = block(i, init, True)
        carry = lax.fori_loop(0, i, lambda jj, c: block(i - 1 - jj, c, False), carry)
        for h in range(hp):
            _, acc = carry[h]
            sl = slice(h * GRP, (h + 1) * GRP)
            rstd = lax.rsqrt(jnp.mean(acc * acc, axis=-1, keepdims=True) + EPS)
            on_ref[:, sl] = (acc * rstd * gam_ref[:, sl]).astype(BF16)
            o_ref[:, sl] = acc

    saved = jax.ShapeDtypeStruct((sb // GRP, nq, nq, BLK, BLK), BF16)
    saved_spec = pl.BlockSpec((hp, None, nq, BLK, BLK), lambda h, i: (h, i, 0, 0, 0))
    return pl.pallas_call(
        body, name="sb_fwd", grid=(nhp, nq),
        out_shape=(jax.ShapeDtypeStruct((s, sb), BF16), jax.ShapeDtypeStruct((s, sb), F32), saved, saved),
        in_specs=[pl.BlockSpec((BLK, w), lambda h, i: (i, qc + h)), pl.BlockSpec((s, w), lambda h, i: (0, kc + h)),
                  pl.BlockSpec((s, w), lambda h, i: (0, vc + h)), pl.BlockSpec((1, w), lambda h, i: (0, gc + h)), _ANY],
        out_specs=(pl.BlockSpec((BLK, w), lambda h, i: (i, h)), pl.BlockSpec((BLK, w), lambda h, i: (i, h)),
                   saved_spec, saved_spec),
        scratch_shapes=[pltpu.VMEM((s, w), BF16), pltpu.VMEM((s, w), BF16)],
        compiler_params=_params(2),
    )(proj, proj, proj, gamma, after)


def _sb_bwd(proj, o_raw, a_sv, s_sv, don, gamma, gm, sb, after):
    s = proj.shape[0]
    hp = min(HEADS_PER_STEP, sb // GRP)
    w = hp * GRP
    nhp, nq = sb // w, s // BLK
    qc, kc, vc, gc = 2 * gm // w, (2 * gm + sb) // w, (2 * gm + 2 * sb) // w, gm // w
    scale = GRP ** -0.5

    def body(q_ref, k_ref, v_ref, o_ref, a_ref, s_ref, don_ref, gam_ref, after_ref, dq_ref, dk_ref, dv_ref, dgam_ref,
             kb, vb, dkt_acc, dvt_acc):
        i = pl.program_id(1)

        @pl.when(i == 0)
        def _():
            kb[...] = k_ref[...].astype(BF16)
            vb[...] = v_ref[...].astype(BF16)
            dkt_acc[...] = jnp.zeros_like(dkt_acc)
            dvt_acc[...] = jnp.zeros_like(dvt_acc)
            dgam_ref[...] = jnp.zeros_like(dgam_ref)

        dobs = []
        for h in range(hp):
            sl = slice(h * GRP, (h + 1) * GRP)
            o, dn = o_ref[:, sl], don_ref[:, sl]
            rstd = lax.rsqrt(jnp.mean(o * o, axis=-1, keepdims=True) + EPS)
            ohat = o * rstd
            dgam_ref[:, sl] += jnp.sum(dn * ohat, axis=0, keepdims=True)
            dohat = dn * gam_ref[:, sl]
            dobs.append((rstd * (dohat - ohat * jnp.mean(dohat * ohat, axis=-1, keepdims=True))).astype(BF16))

        qb = q_ref[...].astype(BF16)
        qts = [qb[:, h * GRP:(h + 1) * GRP].T for h in range(hp)]
        dots = [dob.T for dob in dobs]
        row = lax.broadcasted_iota(jnp.int32, (BLK, BLK), 0)
        col = lax.broadcasted_iota(jnp.int32, (BLK, BLK), 1)
        before = (row < col).astype(BF16)

        def block(j, carry):
            off = pl.multiple_of(j * BLK, BLK)
            kj, vj = kb[pl.ds(off, BLK), :], vb[pl.ds(off, BLK), :]
            out = []
            for h in range(hp):
                e_pre, dq = carry[h]
                sl = slice(h * GRP, (h + 1) * GRP)
                qh, kh, dob = qb[:, sl], kj[:, sl], dobs[h]
                ab = a_ref[h, j]
                sig = s_ref[h, j].astype(F32)
                de = ab.astype(F32) * _dot(dob, vj[:, sl], NT)
                dvt_acc[sl, pl.ds(off, BLK)] += _dot(dots[h], ab, NN)
                before_s = _dot(de.astype(BF16), before, NN)
                dzb = ((de * (1.0 - sig) - (e_pre + before_s) * sig) * scale).astype(BF16)
                dkt_acc[sl, pl.ds(off, BLK)] += _dot(qts[h], dzb, NN)
                out.append((e_pre + before_s[:, BLK - 1:BLK] + de[:, BLK - 1:BLK], dq + _dot(dzb, kh, NN)))
            return tuple(out)

        carry = tuple((jnp.zeros((BLK, 1), F32), jnp.zeros((BLK, GRP), F32)) for _ in range(hp))
        carry = lax.fori_loop(0, i + 1, block, carry)
        for h in range(hp):
            dq_ref[:, h * GRP:(h + 1) * GRP] = carry[h][1].astype(BF16)

        @pl.when(i == nq - 1)
        def _():
            dk_ref[...] = dkt_acc[...].T.astype(BF16)
            dv_ref[...] = dvt_acc[...].T.astype(BF16)

    blk_q = lambda h, i: (i, h)
    whole = lambda h, i: (0, h)
    saved_spec = pl.BlockSpec((hp, None, nq, BLK, BLK), lambda h, i: (h, i, 0, 0, 0))
    return pl.pallas_call(
        body, name="sb_bwd", grid=(nhp, nq),
        out_shape=(jax.ShapeDtypeStruct((s, sb), BF16),) * 3 + (jax.ShapeDtypeStruct((1, sb), F32),),
        in_specs=[pl.BlockSpec((BLK, w), lambda h, i: (i, qc + h)), pl.BlockSpec((s, w), lambda h, i: (0, kc + h)),
                  pl.BlockSpec((s, w), lambda h, i: (0, vc + h)), pl.BlockSpec((BLK, w), blk_q), saved_spec, saved_spec,
                  pl.BlockSpec((BLK, w), lambda h, i: (i, gc + h)),
                  pl.BlockSpec((1, w), lambda h, i: (0, gc + h)), _ANY],
        out_specs=(pl.BlockSpec((BLK, w), blk_q), pl.BlockSpec((s, w), whole), pl.BlockSpec((s, w), whole),
                   pl.BlockSpec((1, w), whole)),
        scratch_shapes=[pltpu.VMEM((s, w), BF16), pltpu.VMEM((s, w), BF16),
                        pltpu.VMEM((w, s), F32), pltpu.VMEM((w, s), F32)],
        compiler_params=_params(2),
    )(proj, proj, proj, o_raw, a_sv, s_sv, don, gamma, after)


FFN_ROW_CHUNKS = 1


def _row_chunks(tm):
    n = FFN_ROW_CHUNKS if tm % (16 * FFN_ROW_CHUNKS) == 0 else 1
    return [slice(k * (tm // n), (k + 1) * (tm // n)) for k in range(n)]


def _ffn_up(h2, wg, wu):
    s, d = h2.shape
    nb, fb, _ = wg.shape
    tm = _tile(s, 1024, 16)

    def body(h_ref, wg_ref, wu_ref, g_ref, u_ref, a_ref):
        for rows_ in _row_chunks(tm):
            hv = h_ref[rows_, :]
            g = _dot(hv, wg_ref[...], NT)
            u = _dot(hv, wu_ref[...], NT)
            g_ref[rows_, :] = g.astype(BF16)
            u_ref[rows_, :] = u.astype(BF16)
            a_ref[rows_, :] = (g * jax.nn.sigmoid(g) * u).astype(BF16)

    rows = pl.BlockSpec((tm, d), lambda i, j: (i, 0))
    wblk = pl.BlockSpec((None, fb, d), lambda i, j: (j, 0, 0))
    hid = pl.BlockSpec((None, tm, fb), lambda i, j: (j, i, 0))
    hid_sds = jax.ShapeDtypeStruct((nb, s, fb), BF16)
    return pl.pallas_call(
        body, name="ffn_up", grid=(s // tm, nb), out_shape=(hid_sds,) * 3, in_specs=[rows, wblk, wblk],
        out_specs=(hid,) * 3, compiler_params=_params(2),
    )(h2, wg, wu)


def _ffn_bwd_act(df, g_pre, u_pre, wd):
    s, d = df.shape
    nb, fb, _ = wd.shape
    tm = _tile(s, 1024, 16)

    def body(df_ref, g_ref, u_ref, wd_ref, dg_ref, du_ref):
        for rows_ in _row_chunks(tm):
            g, u = g_ref[rows_, :].astype(F32), u_ref[rows_, :].astype(F32)
            da = _dot(df_ref[rows_, :], wd_ref[...], NT)
            sg = jax.nn.sigmoid(g)
            dg_ref[rows_, :] = (da * u * (sg * (1.0 + g * (1.0 - sg)))).astype(BF16)
            du_ref[rows_, :] = (da * (g * sg)).astype(BF16)

    rows = pl.BlockSpec((tm, d), lambda i, j: (i, 0))
    wblk = pl.BlockSpec((None, fb, d), lambda i, j: (j, 0, 0))
    hid = pl.BlockSpec((None, tm, fb), lambda i, j: (j, i, 0))
    hid_sds = jax.ShapeDtypeStruct((nb, s, fb), BF16)
    return pl.pallas_call(
        body, name="ffn_bwd_act", grid=(s // tm, nb), out_shape=(hid_sds, hid_sds),
        in_specs=[rows, hid, hid, wblk], out_specs=(hid, hid), compiler_params=_params(2),
    )(df, g_pre, u_pre, wd)


def _ffn_bwd_in(dg, du, wg, wu):
    nb, s, fb = dg.shape
    d = wg.shape[2]
    tm = _tile(s, 1024, 16)

    def body(dg_ref, du_ref, wg_ref, wu_ref, dh_ref):
        j = pl.program_id(1)
        parts = [_dot(dg_ref[rows_, :], wg_ref[...], NN) + _dot(du_ref[rows_, :], wu_ref[...], NN)
                 for rows_ in _row_chunks(tm)]

        @pl.when(j == 0)
        def _():
            for rows_, part in zip(_row_chunks(tm), parts):
                dh_ref[rows_, :] = part

        @pl.when(j > 0)
        def _():
            for rows_, part in zip(_row_chunks(tm), parts):
                dh_ref[rows_, :] += part

    wblk = pl.BlockSpec((None, fb, d), lambda i, j: (j, 0, 0))
    hid = pl.BlockSpec((None, tm, fb), lambda i, j: (j, i, 0))
    return pl.pallas_call(
        body, name="ffn_bwd_in", grid=(s // tm, nb), out_shape=jax.ShapeDtypeStruct((s, d), F32),
        in_specs=[hid, hid, wblk, wblk], out_specs=pl.BlockSpec((tm, d), lambda i, j: (i, 0)),
        compiler_params=_params(2),
    )(dg, du, wg, wu)


def _reduce_adamw(parts, w, m, v, name, after):
    npart, r, c = parts.shape
    tr = _tile(r, max(16, 524288 // c), 16)
    c1, c2 = 1.0 - ADAM_B1 ** ADAM_STEP, 1.0 - ADAM_B2 ** ADAM_STEP

    def body(p_ref, w_ref, m_ref, v_ref, after_ref, g_ref, d_ref, nm_ref, nv_ref):
        g = p_ref[0].astype(F32)
        for k in range(1, npart):
            g = g + p_ref[k].astype(F32)
        nm = ADAM_B1 * m_ref[...] + (1.0 - ADAM_B1) * g
        nv = ADAM_B2 * v_ref[...] + (1.0 - ADAM_B2) * (g * g)
        g_ref[...] = g
        nm_ref[...] = nm
        nv_ref[...] = nv
        d_ref[...] = -ADAM_LR * ((nm / c1) / (jnp.sqrt(nv / c2) + ADAM_EPS) + ADAM_WD * w_ref[...])

    blk = pl.BlockSpec((tr, c), lambda i: (i, 0))
    sds = jax.ShapeDtypeStruct((r, c), F32)
    return pl.pallas_call(
        body, name=name, grid=(r // tr,), out_shape=(sds,) * 4,
        in_specs=[pl.BlockSpec((npart, tr, c), lambda i: (0, i, 0)), blk, blk, blk, _ANY], out_specs=(blk,) * 4,
        compiler_params=_params(1),
    )(parts, w, m, v, after)


def _pack(vecs):
    rows = jnp.concatenate([a.reshape(-1, GRP) for a in vecs], axis=0)
    pad = -rows.shape[0] % 64
    return jnp.pad(rows, ((0, pad), (0, 0)))


def _unpack(rows, shapes):
    out, at = [], 0
    for shp in shapes:
        n = 1
        for k in shp:
            n *= k
        out.append(rows[at:at + n // GRP].reshape(shp))
        at += n // GRP
    return out


def kernel(x, c, w_ada, b_ada, norm1_g, w_in, v_norm_g, w_spatial, b_spatial, out_norm_g, w_out, norm2_g, w_gate, w_up, w_down, final_g, loss_target, m_w_ada, m_b_ada, m_norm1_g, m_w_in, m_v_norm_g, m_w_spatial, m_b_spatial, m_out_norm_g, m_w_out, m_norm2_g, m_w_gate, m_w_up, m_w_down, m_final_g, v_w_ada, v_b_ada, v_norm1_g, v_w_in, v_v_norm_g, v_w_spatial, v_b_spatial, v_out_norm_g, v_w_out, v_norm2_g, v_w_gate, v_w_up, v_w_down, v_final_g):
    s, d = x.shape[1], x.shape[2]
    gm = v_norm_g.shape[1]
    sb = d - gm
    n_in, ffb, ob = w_in.shape[2], w_gate.shape[2], w_out.shape[1]
    xs, tgt = x[0], loss_target[0]
    me = 4 * lax.axis_index("x") + 2 * lax.axis_index("y") + lax.axis_index("c")

    c_all, mod = _ada_exchange(c, w_ada[0])
    c_all, mod = c_all[:, 0, :], mod.reshape(1, 6 * d) + b_ada
    shift1, scale1, gate1, shift2, scale2, gate2 = [mod[:, k * d:(k + 1) * d] for k in range(6)]

    tok, gathers = mod, {}
    tr_ = lambda a: jnp.swapaxes(a, 1, 2)
    w_gate, m_w_gate, v_w_gate, w_up, m_w_up, v_w_up = map(tr_, (w_gate, m_w_gate, v_w_gate, w_up, m_w_up, v_w_up))
    big_w = (("w_in", w_in), ("w_out", w_out), ("w_gate", w_gate), ("w_up", w_up), ("w_down", w_down))
    first, tok = _gather_start([w_in[0].astype(BF16)], tok, "gather_start_w_in")
    rest, tok = _gather_start([w[0].astype(BF16) for _, w in big_w[1:]], tok, "gather_start_weights")
    gathers = {nm: h for (nm, _), h in zip(big_w, first + rest)}
    shift1 = shift1 + tok[0:1, 0:1]

    tm = _tile(s, 512, 16)
    nt = s // tm
    h1 = _prenorm_fwd(xs, norm1_g, scale1, shift1, "prenorm1")
    w_in_all = _gather_wait(_gather_forward(gathers["w_in"], h1, "gather_fwd_w_in")[0], h1, "gather_wait_w_in")
    proj = _mm(h1, w_in_all, pl.BlockSpec((s, d), lambda j: (0, 0)), pl.BlockSpec((None, d, n_in), lambda j: (j, 0, 0)),
               jax.ShapeDtypeStruct((s, NDEV * n_in), F32), pl.BlockSpec((s, n_in), lambda j: (0, j)),
               (NDEV,), NN, False, "proj")
    fwd_w_out, tok = _gather_forward(gathers["w_out"], proj, "gather_fwd_w_out")
    b_cols = jnp.pad(b_spatial[0].T, ((0, 0), (0, GRP - b_spatial.shape[1])))
    on_sb, o_sb, a_sv, s_sv = _sb_fwd(proj, out_norm_g, gm, sb, tok)
    on_gm = _gmlp_fwd(proj, v_norm_g, w_spatial[0], b_cols, out_norm_g, gm, on_sb)
    fwd_w_gate, tok = _gather_forward(gathers["w_gate"], on_gm, "gather_fwd_w_gate")
    o_n = jnp.concatenate([on_gm, on_sb], axis=1)
    w_out_all = _gather_wait(fwd_w_out, tok, "gather_wait_w_out").reshape(d, d)
    rows_1 = pl.BlockSpec((tm, d), lambda i: (i, 0))
    whole_1 = pl.BlockSpec((d, d), lambda i: (0, 0))
    p_out = _mm(o_n, w_out_all, rows_1, whole_1, jax.ShapeDtypeStruct((s, d), F32), rows_1, (nt,), NN, False,
                "out_proj")
    fwd_w_up, tok = _gather_forward(gathers["w_up"], p_out, "gather_fwd_w_up")
    scale2 = scale2 + tok[0:1, 0:1]
    x1, h2 = _residual_prenorm(xs, gate1, p_out, norm2_g, scale2, shift2, "residual1_prenorm2")
    w_gate_all = _gather_wait(fwd_w_gate, h2, "gather_wait_w_gate")
    w_up_all = _gather_wait(fwd_w_up, h2, "gather_wait_w_up")
    g_pre, u_pre, act = _ffn_up(h2, w_gate_all, w_up_all)
    fwd_w_down, tok = _gather_forward(gathers["w_down"], act, "gather_fwd_w_down")
    w_down_all = _gather_wait(fwd_w_down, tok, "gather_wait_w_down")
    tm2 = _tile(s, 1024, 16)
    f_out = _mm(act, w_down_all, pl.BlockSpec((2, tm2, ffb), lambda i, j: (j, i, 0)),
                pl.BlockSpec((2, ffb, d), lambda i, j: (j, 0, 0)), jax.ShapeDtypeStruct((s, d), F32),
                pl.BlockSpec((tm2, d), lambda i, j: (i, 0)), (s // tm2, NDEV // 2), NN, True, "ffn_down")

    dx2, df, st_f = _final_loss(x1, f_out, final_g.reshape(1, d), tgt, gate2)
    dg, du = _ffn_bwd_act(df, g_pre, u_pre, w_down_all)
    dh2 = _ffn_bwd_in(dg, du, w_gate_all, w_up_all)
    hid = pl.BlockSpec((None, s, ffb), lambda j: (j, 0, 0))
    all_d = pl.BlockSpec((s, d), lambda j: (0, 0))

    def second_leg(first, after, tag):
        parts, lands = _scatter_wait1(first, after, "scatter_wait1_" + tag)
        sums = [_chip_sum(p, l, "chip_sum_%s_%d" % (tag, k)) for k, (p, l) in enumerate(zip(parts, lands))]
        return _scatter_start2(sums, after, "scatter_start2_" + tag)

    row_sds = jax.ShapeDtypeStruct((NDEV, ffb, d), BF16)
    row_out = pl.BlockSpec((None, ffb, d), lambda j: (j, 0, 0))
    gw_gate = _mm(dg, h2, hid, all_d, row_sds, row_out, (NDEV,), TN, False, "grad_w_gate")
    gw_up = _mm(du, h2, hid, all_d, row_sds, row_out, (NDEV,), TN, False, "grad_w_up")
    gw_down = _mm(act, df, hid, all_d, row_sds, row_out, (NDEV,), TN, False, "grad_w_down")
    dx1, dp, st2 = _prenorm_bwd(x1, dh2, dx2, f_out, norm2_g, scale2, gate1, "prenorm2_bwd", gw_down)
    first_ffn, tok = _scatter_start1([gw_gate, gw_up, gw_down], dp, "scatter_start1_ffn")
    don = _mm(dp, w_out_all, rows_1, whole_1, jax.ShapeDtypeStruct((s, d), F32), rows_1, (nt,), NT, False,
              "out_proj_bwd", tok)
    gw_out = _mm(o_n, dp, pl.BlockSpec((s, ob), lambda j: (0, j)), all_d,
                 jax.ShapeDtypeStruct((NDEV, ob, d), BF16), pl.BlockSpec((None, ob, d), lambda j: (j, 0, 0)),
                 (NDEV,), TN, False, "grad_w_out", don)
    first_out, tok = _scatter_start1([gw_out], tok, "scatter_start1_out")
    second_ffn, tok = second_leg(first_ffn, tok, "ffn")
    dproj_gm, dw_sp, db_cols, dgv, dgam_gm = _gmlp_bwd(proj, don, v_norm_g, w_spatial[0], b_cols, out_norm_g, gm, tok)
    second_out, tok = second_leg(first_out, dproj_gm, "out")
    dq, dk, dv, dgam_sb = _sb_bwd(proj, o_sb, a_sv, s_sv, don, out_norm_g, gm, sb, tok)
    dproj = jnp.concatenate([dproj_gm, dq, dk, dv], axis=1)
    gw_in = _mm(h1, dproj, all_d, pl.BlockSpec((s, n_in), lambda j: (0, j)),
                jax.ShapeDtypeStruct((NDEV, d, n_in), BF16), pl.BlockSpec((None, d, n_in), lambda j: (j, 0, 0)),
                (NDEV,), TN, False, "grad_w_in")
    first_in, tok = _scatter_start1([gw_in], tok, "scatter_start1_in")
    dh1 = _mm(dproj, w_in_all, pl.BlockSpec((tm2, 2 * n_in), lambda i, j: (i, j)),
              pl.BlockSpec((2, d, n_in), lambda i, j: (j, 0, 0)), jax.ShapeDtypeStruct((s, d), F32),
              pl.BlockSpec((tm2, d), lambda i, j: (i, 0)), (s // tm2, NDEV // 2), NT, True, "in_proj_bwd", tok)
    grad_x, _, st1 = _prenorm_bwd(xs, dh1, dx1, p_out, norm1_g, scale1, gate1, "prenorm1_bwd", tok)

    dmod = jnp.concatenate([st1[0], st1[1], st1[3], st2[0], st2[1], st2[3]])
    loss_row = jnp.pad((0.5 * jnp.sum(st_f[1]) / d).reshape(1, 1), ((0, 0), (0, GRP - 1)))
    small = [st1[2], dgv, dw_sp, db_cols[:, :b_spatial.shape[1]].T, jnp.concatenate([dgam_gm, dgam_sb], axis=1),
             st2[2], st_f[0], dmod, loss_row]
    zero_row = jnp.zeros((1, GRP), F32)
    small_w = [norm1_g, v_norm_g, w_spatial, b_spatial, out_norm_g, norm2_g, final_g, b_ada, zero_row]
    small_m = [m_norm1_g, m_v_norm_g, m_w_spatial, m_b_spatial, m_out_norm_g, m_norm2_g, m_final_g, m_b_ada, zero_row]
    small_v = [v_norm1_g, v_v_norm_g, v_w_spatial, v_b_spatial, v_out_norm_g, v_norm2_g, v_final_g, v_b_ada, zero_row]
    (small_first,), tok = _gather_start([_pack(small)], grad_x, "gather_small_start")
    second_in, tok = second_leg(first_in, tok, "in")

    big = {}
    prev = tok
    for second, group in ((second_ffn, (("w_gate", w_gate, m_w_gate, v_w_gate), ("w_up", w_up, m_w_up, v_w_up),
                                        ("w_down", w_down, m_w_down, v_w_down))),
                          (second_out, (("w_out", w_out, m_w_out, v_w_out),))):
        chip_sums = _scatter_wait2(second, prev, "scatter_wait2_" + group[0][0])
        for (nm, w, m, v), part in zip(group, chip_sums):
            big[nm] = _reduce_adamw(part, w[0], m[0], v[0], "adamw_" + nm, tok)
            prev = big[nm][1]

    small_second, tok2 = _gather_forward(small_first, prev, "gather_small_fwd")
    small_all = _gather_wait(small_second, tok2, "gather_small_wait")
    sm = _reduce_adamw(small_all, _pack(small_w), _pack(small_m), _pack(small_v), "adamw_small", tok)
    shapes = [a.shape for a in small_w]
    sm_g, sm_d, sm_m, sm_v = [_unpack(t, shapes) for t in sm]
    loss = sm_g[8][0, 0]

    at = sum(a.size for a in small_w[:7]) // GRP
    dmod_all = small_all[:, at:at + 6 * d // GRP, :].reshape(NDEV, 6 * d)
    n_ada = w_ada.shape[2]
    dmod_cols = lax.dynamic_slice(dmod_all, (0, me * n_ada), (NDEV, n_ada))
    g_ada = _ada_bwd(c_all, dmod_cols)
    big["w_ada"] = _reduce_adamw(g_ada[None], w_ada[0], m_w_ada[0], v_w_ada[0], "adamw_w_ada", sm[0])
    part_in, = _scatter_wait2(second_in, big["w_ada"][1], "scatter_wait2_w_in")
    big["w_in"] = _reduce_adamw(part_in, w_in[0], m_w_in[0], v_w_in[0], "adamw_w_in", tok)

    names = ["w_ada", "b_ada", "norm1_g", "w_in", "v_norm_g", "w_spatial", "b_spatial", "out_norm_g", "w_out",
             "norm2_g", "w_gate", "w_up", "w_down", "final_g"]
    small_at = {"norm1_g": 0, "v_norm_g": 1, "w_spatial": 2, "b_spatial": 3, "out_norm_g": 4, "norm2_g": 5,
                "final_g": 6, "b_ada": 7}
    outs = [[], [], [], []]
    for nm in names:
        for k in range(4):
            if nm in big:
                res = big[nm][k][None]
                outs[k].append(tr_(res) if nm in ("w_gate", "w_up") else res)
            else:
                outs[k].append((sm_g, sm_d, sm_m, sm_v)[k][small_at[nm]])
    return (loss, grad_x[None], *outs[0], *outs[1], *outs[2], *outs[3])
```
